```python
import jax
import jax.numpy as jnp
from jax import lax
import numpy as np

D_MODEL = 1024
BATCH = 32
SEQ = 2048
DEPTH = 2

GRID_W = 64
CTX_LEN = 256
D_MIX = D_MODEL
EPS = 1e-6
HEAD_DIM = 64
ATT_HEADS = 8
ATT_KV_HEADS = 2
ATT_GROUP = ATT_HEADS // ATT_KV_HEADS
ATT_WIDTH = ATT_HEADS * HEAD_DIM
WINDOW = 128
ATT_BLOCK = 128
ROPE_BASE = 10000.0
GLA_HEADS = 4
GLA_DV = 64
GLA_DK = 32
GLA_WIDTH = GLA_HEADS * GLA_DV
GLA_GATE_RANK = 16
GLA_GATE_NORM = 16.0
GLA_CHUNK = 64
CONV_WIDTH = D_MIX - ATT_WIDTH - GLA_WIDTH
CONV_K = 3
N_EXPERTS = 16
N_GROUPS = 4
EXPERTS_PER_GROUP = N_EXPERTS // N_GROUPS
TOP_K = 2
D_EXPERT = D_MODEL // 2
IN_SIZES = (ATT_WIDTH, ATT_KV_HEADS * HEAD_DIM, ATT_KV_HEADS * HEAD_DIM,
            GLA_HEADS * GLA_DK, GLA_HEADS * GLA_DK, GLA_WIDTH, GLA_WIDTH, GLA_GATE_RANK, GLA_GATE_RANK,
            CONV_WIDTH, CONV_WIDTH, CONV_WIDTH)
N_IN = sum(IN_SIZES)

kernel_name = 'hybrid_dit_swa_gla_shortconv_grouped_moe'


def _rms_norm(x, g):
    xf = x.astype(jnp.float32)
    y = xf * lax.rsqrt(jnp.mean(xf * xf, axis=-1, keepdims=True) + EPS)
    return (y * g.astype(jnp.float32)).astype(x.dtype)


def _modulation(cond, w, b):
    m = (jax.nn.silu(cond) @ w + b)[..., None, :]
    return jnp.split(m, 6, axis=-1)


def _modulate(h, shift, scale):
    return h * (1.0 + scale) + shift


def _split_columns(p):
    parts, start = [], 0
    for size in IN_SIZES:
        parts.append(p[..., start:start + size])
        start += size
    return parts


def _axial_rope_tables(n_tokens):
    rows = n_tokens // GRID_W
    row, col = jnp.meshgrid(jnp.arange(rows), jnp.arange(GRID_W), indexing='ij')
    n_freq = HEAD_DIM // 4
    inv_freq = ROPE_BASE ** (-jnp.arange(n_freq, dtype=jnp.float32) / n_freq)
    ang = jnp.concatenate([row.reshape(-1, 1).astype(jnp.float32) * inv_freq,
                           col.reshape(-1, 1).astype(jnp.float32) * inv_freq], axis=-1)
    return jnp.cos(ang), jnp.sin(ang)


def _rope(x, cos, sin):
    x1, x2 = jnp.split(x, 2, axis=-1)
    cos = cos[None, :, None, :].astype(x.dtype)
    sin = sin[None, :, None, :].astype(x.dtype)
    return jnp.concatenate([x1 * cos - x2 * sin, x1 * sin + x2 * cos], axis=-1)


def _softmax_with_sink(scores, sink):
    sink_col = jnp.broadcast_to(sink[None, :, :, None, None], scores.shape[:-1] + (1,))
    return jax.nn.softmax(jnp.concatenate([scores, sink_col], axis=-1), axis=-1)[..., :-1]


def _attention_group(q_l, k_l, v_l, q_c, k_c, v_c, q_norm_g, k_norm_g, sink, cos, sin, with_ctx_out):
    B, S, _ = q_l.shape
    L = q_c.shape[1]
    scale = HEAD_DIM ** -0.5
    sink = sink.reshape(ATT_KV_HEADS, ATT_GROUP).astype(jnp.float32)
    q_l = (_rope(_rms_norm(q_l.reshape(B, S, ATT_HEADS, HEAD_DIM), q_norm_g), cos, sin) * scale
           ).reshape(B, S, ATT_KV_HEADS, ATT_GROUP, HEAD_DIM)
    k_l = _rope(_rms_norm(k_l.reshape(B, S, ATT_KV_HEADS, HEAD_DIM), k_norm_g), cos, sin)
    v_l = v_l.reshape(B, S, ATT_KV_HEADS, HEAD_DIM)
    k_c = _rms_norm(k_c.reshape(B, L, ATT_KV_HEADS, HEAD_DIM), k_norm_g)
    v_c = v_c.reshape(B, L, ATT_KV_HEADS, HEAD_DIM)

    span = ATT_BLOCK + 2 * WINDOW
    pad = ((0, 0), (WINDOW, WINDOW), (0, 0), (0, 0))
    k_pad, v_pad = jnp.pad(k_l, pad), jnp.pad(v_l, pad)

    def block(i):
        start = i * ATT_BLOCK
        qb = lax.dynamic_slice_in_dim(q_l, start, ATT_BLOCK, axis=1)
        kb = lax.dynamic_slice_in_dim(k_pad, start, span, axis=1)
        vb = lax.dynamic_slice_in_dim(v_pad, start, span, axis=1)
        qpos = start + jnp.arange(ATT_BLOCK)
        kpos = start - WINDOW + jnp.arange(span)
        valid = (jnp.abs(qpos[:, None] - kpos[None, :]) <= WINDOW) & (kpos >= 0)[None, :] & (kpos < S)[None, :]
        s_win = jnp.einsum('bqhgd,bshd->bhgqs', qb, kb).astype(jnp.float32)
        s_win = jnp.where(valid, s_win, -jnp.inf)
        s_ctx = jnp.einsum('bqhgd,bchd->bhgqc', qb, k_c).astype(jnp.float32)
        p = _softmax_with_sink(jnp.concatenate([s_win, s_ctx], axis=-1), sink).astype(v_l.dtype)
        return (jnp.einsum('bhgqs,bshd->bqhgd', p[..., :span], vb)
                + jnp.einsum('bhgqc,bchd->bqhgd', p[..., span:], v_c))

    o = lax.map(block, jnp.arange(S // ATT_BLOCK))
    y_l = jnp.moveaxis(o, 0, 1).reshape(B, S, ATT_WIDTH)
    if not with_ctx_out:
        return y_l, None
    q_c = (_rms_norm(q_c.reshape(B, L, ATT_HEADS, HEAD_DIM), q_norm_g) * scale
           ).reshape(B, L, ATT_KV_HEADS, ATT_GROUP, HEAD_DIM)
    s = jnp.einsum('bqhgd,bchd->bhgqc', q_c, k_c).astype(jnp.float32)
    p = _softmax_with_sink(s, sink).astype(v_c.dtype)
    y_c = jnp.einsum('bhgqc,bchd->bqhgd', p, v_c).reshape(B, L, ATT_WIDTH)
    return y_l, y_c


def _gla_chunk_scan(q, k, v, g, s0, include_diag):
    B, H, T, _ = q.shape
    n = T // GLA_CHUNK

    def to_chunks(a):
        return a.reshape(B, H, n, GLA_CHUNK, a.shape[-1]).transpose(2, 0, 1, 3, 4)

    idx = jnp.arange(GLA_CHUNK)
    mask = (idx[:, None] >= idx[None, :]) if include_diag else (idx[:, None] > idx[None, :])

    def step(state, xs):
        qi, ki, vi, gi = xs
        b = jnp.cumsum(gi, axis=-2)
        o_inter = jnp.einsum('bhcd,bhde->bhce', qi * jnp.exp(b), state)
        diff = b[..., :, None, :] - b[..., None, :, :]
        decay = jnp.exp(jnp.where(mask[:, :, None], diff, -jnp.inf))
        att = jnp.einsum('bhid,bhjd,bhijd->bhij', qi, ki, decay)
        o_intra = jnp.einsum('bhij,bhje->bhie', att, vi)
        b_last = b[..., -1:, :]
        new_state = (jnp.exp(b_last[..., 0, :])[..., None] * state
                     + jnp.einsum('bhcd,bhce->bhde', ki * jnp.exp(b_last - b), vi))
        return new_state, o_inter + o_intra

    s_fin, o = lax.scan(step, s0, (to_chunks(q), to_chunks(k), to_chunks(v), to_chunks(g)))
    return o.transpose(1, 2, 0, 3, 4).reshape(B, H, T, v.shape[-1]), s_fin


def _bidir_gla(q, k, v, g_fwd, g_bwd, s_fwd0, s_bwd0):
    o_f, s_f = _gla_chunk_scan(q, k, v, g_fwd, s_fwd0, True)
    flip = lambda a: jnp.flip(a, axis=2)
    o_b, s_b = _gla_chunk_scan(flip(q), flip(k), flip(v), flip(g_bwd), s_bwd0, False)
    return o_f + flip(o_b), s_f, s_b


def _gla_heads(q, k, v, g_f, g_b, gate_w, gate_b):
    B, T, _ = q.shape

    def heads(a, d):
        return a.reshape(B, T, GLA_HEADS, d).transpose(0, 2, 1, 3)

    def log_decay(g_low, d):
        z = (g_low @ gate_w[d] + gate_b[d]).astype(jnp.float32)
        return heads(jax.nn.log_sigmoid(z) / GLA_GATE_NORM, GLA_DK)

    return (heads(q, GLA_DK) * GLA_DK ** -0.5, heads(k, GLA_DK), heads(v, GLA_DV),
            log_decay(g_f, 0), log_decay(g_b, 1))


def _gla_out(o, r, norm_g):
    B, H, T, DV = o.shape
    o = _rms_norm(o, norm_g).transpose(0, 2, 1, 3).reshape(B, T, H * DV)
    return o.astype(r.dtype) * jax.nn.silu(r)


def _gla_group(parts_l, parts_c, gate_w, gate_b, norm_g, with_ctx_out):
    q_l, k_l, v_l, r_l, gf_l, gb_l = parts_l
    q_c, k_c, v_c, r_c, gf_c, gb_c = parts_c
    ql, kl, vl, dfl, dbl = _gla_heads(q_l, k_l, v_l, gf_l, gb_l, gate_w, gate_b)
    qc, kc, vc, dfc, dbc = _gla_heads(q_c, k_c, v_c, gf_c, gb_c, gate_w, gate_b)
    zeros = jnp.zeros((q_l.shape[0], GLA_HEADS, GLA_DK, GLA_DV), jnp.float32)
    o_c, s_f, s_b = _bidir_gla(qc, kc, vc, dfc, dbc, zeros, zeros)
    o_l, _, _ = _bidir_gla(ql, kl, vl, dfl, dbl, s_f, s_b)
    y_l = _gla_out(o_l, r_l, norm_g)
    y_c = _gla_out(o_c, r_c, norm_g) if with_ctx_out else None
    return y_l, y_c


def _short_conv(u, w):
    kernel = w[:, None, :].astype(u.dtype)
    return lax.conv_general_dilated(u, kernel, window_strides=(1,), padding=((CONV_K // 2, CONV_K // 2),),
                                    dimension_numbers=('NWC', 'WIO', 'NWC'), feature_group_count=u.shape[-1])


def _token_mixers(p_lat, p_ctx, cos, sin, q_norm_g, k_norm_g, sink, gla_gate_w, gla_gate_b, gla_norm_g,
                  conv_w, with_ctx_out):
    aq_l, ak_l, av_l, gq_l, gk_l, gv_l, gr_l, gf_l, gb_l, cb_l, cc_l, ch_l = _split_columns(p_lat)
    aq_c, ak_c, av_c, gq_c, gk_c, gv_c, gr_c, gf_c, gb_c, cb_c, cc_c, ch_c = _split_columns(p_ctx)
    att_l, att_c = _attention_group(aq_l, ak_l, av_l, aq_c, ak_c, av_c, q_norm_g, k_norm_g, sink,
                                    cos, sin, with_ctx_out)
    gla_l, gla_c = _gla_group((gq_l, gk_l, gv_l, gr_l, gf_l, gb_l), (gq_c, gk_c, gv_c, gr_c, gf_c, gb_c),
                              gla_gate_w, gla_gate_b, gla_norm_g, with_ctx_out)
    conv_l = cb_l * _short_conv(cc_l * ch_l, conv_w)
    y_lat = jnp.concatenate([att_l, gla_l, conv_l], axis=-1)
    if not with_ctx_out:
        return y_lat, None
    conv_c = cb_c * _short_conv(cc_c * ch_c, conv_w)
    y_ctx = jnp.concatenate([att_c, gla_c, conv_c], axis=-1)
    return y_lat, y_ctx


def _moe(h, w_router, b_router, w_gate, w_up, w_down):
    logits = jnp.einsum('btd,de->bte', h, w_router).astype(jnp.float32) + b_router.astype(jnp.float32)
    probs = jax.nn.softmax(logits, axis=-1)
    grouped = probs.reshape(probs.shape[:-1] + (N_GROUPS, EXPERTS_PER_GROUP))
    group_score = lax.top_k(grouped, TOP_K)[0].sum(axis=-1)
    g_sel = jnp.argmax(group_score, axis=-1)
    in_group = jnp.take_along_axis(grouped, g_sel[..., None, None], axis=-2)[..., 0, :]
    top_p, top_i = lax.top_k(in_group, TOP_K)
    weights = top_p / jnp.sum(top_p, axis=-1, keepdims=True)
    expert_idx = g_sel[..., None] * EXPERTS_PER_GROUP + top_i
    gates = jnp.sum(jax.nn.one_hot(expert_idx, N_EXPERTS, dtype=jnp.float32) * weights[..., None],
                    axis=-2).astype(h.dtype)
    out = jnp.zeros_like(h)
    for e in range(N_EXPERTS):
        a = jax.nn.silu(h @ w_gate[e]) * (h @ w_up[e])
        out = out + gates[..., e:e + 1] * (a @ w_down[e])
    return out


def setup_inputs(seed: int = 0) -> dict:
    key = jax.random.key(seed)
    ks = jax.random.split(key, 22)

    def nrm(k, shape, scale):
        return jax.random.normal(k, shape, jnp.float32) * scale

    return {
        'x': nrm(ks[0], (BATCH, SEQ, D_MODEL), 1.0),
        'c': nrm(ks[1], (BATCH, D_MODEL), 1.0),
        'ctx': nrm(ks[2], (BATCH, CTX_LEN, D_MODEL), 1.0),
        'c_ctx': nrm(ks[3], (D_MODEL,), 1.0),
        'w_ada': nrm(ks[4], (DEPTH, D_MODEL, 6 * D_MODEL), 0.3 * D_MODEL ** -0.5),
        'b_ada': nrm(ks[5], (DEPTH, 6 * D_MODEL), 0.05),
        'norm_mix_g': 1.0 + nrm(ks[6], (DEPTH, D_MODEL), 0.05),
        'norm_ffn_g': 1.0 + nrm(ks[7], (DEPTH, D_MODEL), 0.05),
        'w_in': nrm(ks[8], (DEPTH, D_MODEL, N_IN), D_MODEL ** -0.5),
        'q_norm_g': 1.0 + nrm(ks[9], (DEPTH, HEAD_DIM), 0.05),
        'k_norm_g': 1.0 + nrm(ks[10], (DEPTH, HEAD_DIM), 0.05),
        'attn_sink': nrm(ks[11], (DEPTH, ATT_HEADS), 0.5),
        'gla_gate_w': nrm(ks[12], (DEPTH, 2, GLA_GATE_RANK, GLA_HEADS * GLA_DK), GLA_GATE_RANK ** -0.5),
        'gla_gate_b': nrm(ks[13], (DEPTH, 2, GLA_HEADS * GLA_DK), 0.1),
        'gla_norm_g': 1.0 + nrm(ks[14], (DEPTH, GLA_DV), 0.05),
        'conv_w': nrm(ks[15], (DEPTH, CONV_K, CONV_WIDTH), CONV_K ** -0.5),
        'w_out': nrm(ks[16], (DEPTH, D_MIX, D_MODEL), D_MIX ** -0.5),
        'w_router': nrm(ks[17], (D_MODEL, N_EXPERTS), D_MODEL ** -0.5),
        'b_router': nrm(ks[18], (N_EXPERTS,), 0.01),
        'w_gate_e': nrm(ks[19], (DEPTH, N_EXPERTS, D_MODEL, D_EXPERT), D_MODEL ** -0.5),
        'w_up_e': nrm(ks[20], (DEPTH, N_EXPERTS, D_MODEL, D_EXPERT), D_MODEL ** -0.5),
        'w_down_e': nrm(ks[21], (DEPTH, N_EXPERTS, D_EXPERT, D_MODEL), D_EXPERT ** -0.5),
    }


def reference(x, c, ctx, c_ctx, w_ada, b_ada, norm_mix_g, norm_ffn_g, w_in, q_norm_g, k_norm_g, attn_sink,
              gla_gate_w, gla_gate_b, gla_norm_g, conv_w, w_out, w_router, b_router, w_gate_e, w_up_e, w_down_e):
    cos, sin = _axial_rope_tables(x.shape[1])
    ctx_len = ctx.shape[1]
    for l in range(DEPTH):
        last = l == DEPTH - 1
        m_lat = _modulation(c, w_ada[l], b_ada[l])
        m_ctx = _modulation(c_ctx, w_ada[l], b_ada[l])
        h_lat = _modulate(_rms_norm(x, norm_mix_g[l]), m_lat[0], m_lat[1])
        h_ctx = _modulate(_rms_norm(ctx, norm_mix_g[l]), m_ctx[0], m_ctx[1])
        y_lat, y_ctx = _token_mixers(h_lat @ w_in[l], h_ctx @ w_in[l], cos, sin, q_norm_g[l], k_norm_g[l],
                                     attn_sink[l], gla_gate_w[l], gla_gate_b[l], gla_norm_g[l], conv_w[l],
                                     not last)
        x = x + m_lat[2] * (y_lat @ w_out[l])
        h_lat = _modulate(_rms_norm(x, norm_ffn_g[l]), m_lat[3], m_lat[4])
        if last:
            x = x + m_lat[5] * _moe(h_lat, w_router, b_router, w_gate_e[l], w_up_e[l], w_down_e[l])
        else:
            ctx = ctx + m_ctx[2] * (y_ctx @ w_out[l])
            h_ctx = _modulate(_rms_norm(ctx, norm_ffn_g[l]), m_ctx[3], m_ctx[4])
            f = _moe(jnp.concatenate([h_ctx, h_lat], axis=1), w_router, b_router,
                     w_gate_e[l], w_up_e[l], w_down_e[l])
            ctx = ctx + m_ctx[5] * f[:, :ctx_len]
            x = x + m_lat[5] * f[:, ctx_len:]
    return x
```

```python
import functools

import numpy as np
import jax
import jax.numpy as jnp
from jax import lax
from jax.experimental import pallas as pl
from jax.experimental.pallas import tpu as pltpu

D_MODEL = 1024
DEPTH = 2
GRID_W = 64
EPS = 1e-6
HEAD_DIM = 64
ATT_HEADS = 8
ATT_KV_HEADS = 2
ATT_GROUP = ATT_HEADS // ATT_KV_HEADS
ATT_WIDTH = ATT_HEADS * HEAD_DIM
WINDOW = 128
ROPE_BASE = 10000.0
GLA_HEADS = 4
GLA_DV = 64
GLA_DK = 32
GLA_WIDTH = GLA_HEADS * GLA_DV
GLA_GATE_RANK = 16
GLA_GATE_NORM = 16.0
GLA_CHUNK = 64
CONV_WIDTH = 256
N_EXPERTS = 16
N_GROUPS = 4
EXPERTS_PER_GROUP = 4
D_EXPERT = D_MODEL // 2

LANES = 128
KV_WIDTH = ATT_KV_HEADS * HEAD_DIM
GLA_QK_WIDTH = GLA_HEADS * GLA_DK
COL_AQ, COL_AK, COL_AV = 0, 512, 640
COL_GQ, COL_GK, COL_GV, COL_GR = 768, 896, 1024, 1280
COL_CB, COL_CC, COL_CH, COL_GT = 1536, 1792, 2048, 2304
N_PROJ = 2432
QK_COLS = COL_AV
N_PAIRS = 6
N_CLASSES = N_GROUPS * N_PAIRS
PAIR_LO = (0, 0, 0, 1, 1, 2)
PAIR_HI = (1, 2, 3, 2, 3, 3)
ROW_W = D_MODEL + LANES
NEG = -1e30

TOKEN_TILE = 768
LAT_TILE = 1024
MOE_TILE = 512
VMEM_LIMIT = 56 * 1024 * 1024

F32 = jnp.float32
BF16 = jnp.bfloat16


def _cparams(n_axes):
    return pltpu.CompilerParams(dimension_semantics=("arbitrary",) * n_axes, vmem_limit_bytes=VMEM_LIMIT)


def _silu(x):
    return x * jax.nn.sigmoid(x)


def _mod_body(c_ref, w_ref, b_ref, o_ref):
    c = c_ref[...]
    a = _silu(c).astype(BF16)
    o_ref[0] = jnp.dot(a, w_ref[0].astype(BF16), preferred_element_type=F32) + b_ref[0]


def _modulation(cond, w_ada, b_ada):
    rows = cond.shape[0]
    nblk = w_ada.shape[2] // D_MODEL
    return pl.pallas_call(
        _mod_body,
        grid=(DEPTH, nblk),
        in_specs=[pl.BlockSpec((rows, D_MODEL), lambda l, n: (0, 0)),
                  pl.BlockSpec((1, D_MODEL, D_MODEL), lambda l, n: (l, 0, n)),
                  pl.BlockSpec((1, 1, D_MODEL), lambda l, n: (l, 0, n))],
        out_specs=pl.BlockSpec((1, rows, D_MODEL), lambda l, n: (l, 0, n)),
        out_shape=jax.ShapeDtypeStruct((DEPTH, rows, w_ada.shape[2]), F32),
        compiler_params=_cparams(2),
        name="modulation",
    )(cond, w_ada, b_ada.reshape(DEPTH, 1, -1))


def _row_mod(mod, is_ctx, i):
    return jnp.where(is_ctx, mod[6 + i:7 + i], mod[i:i + 1])


def _norm_modulate(x, gain, mod, is_ctx, i_shift, i_scale):
    ms = jnp.mean(x * x, axis=-1, keepdims=True)
    xn = x * lax.rsqrt(ms + EPS) * gain
    return xn * (1.0 + _row_mod(mod, is_ctx, i_scale)) + _row_mod(mod, is_ctx, i_shift)


def _in_body(x_ref, mod_ref, g_ref, w_ref, cos_ref, sin_ref, qkg_ref, bd_ref, o_ref, *, n_lat, tm):
    j = pl.program_id(0)
    row = j * tm + lax.broadcasted_iota(jnp.int32, (tm, 1), 0)
    is_ctx = row >= n_lat
    h = _norm_modulate(x_ref[0], g_ref[...], mod_ref[0], is_ctx, 0, 1).astype(BF16)
    qk = jnp.dot(h, w_ref[:, :QK_COLS], preferred_element_type=F32)
    cos = cos_ref[...]
    sin = sin_ref[...]
    lane = lax.broadcasted_iota(jnp.int32, (1, LANES), 1)
    first_half = (lane % HEAD_DIM) < (HEAD_DIM // 2)
    for c in range(QK_COLS // LANES):
        xc = qk[:, c * LANES:(c + 1) * LANES]
        ss = jnp.dot((xc * xc).astype(BF16), bd_ref[...], preferred_element_type=F32) * (1.0 / HEAD_DIM)
        xc = xc * lax.rsqrt(ss + EPS) * qkg_ref[c:c + 1, :]
        rot = jnp.where(first_half, pltpu.roll(xc, LANES - HEAD_DIM // 2, 1), pltpu.roll(xc, HEAD_DIM // 2, 1))
        o_ref[0, :, c * LANES:(c + 1) * LANES] = (xc * cos + rot * sin).astype(BF16)
    o_ref[0, :, QK_COLS:] = jnp.dot(h, w_ref[:, QK_COLS:], preferred_element_type=F32).astype(BF16)


def _in_proj(xx, modv, gain, w, cos, sin, qkg, bd, n_lat):
    B, T, _ = xx.shape
    tm = TOKEN_TILE
    return pl.pallas_call(
        functools.partial(_in_body, n_lat=n_lat, tm=tm),
        grid=(T // tm, B),
        in_specs=[pl.BlockSpec((1, tm, D_MODEL), lambda j, b: (b, j, 0)),
                  pl.BlockSpec((1, 16, D_MODEL), lambda j, b: (b, 0, 0)),
                  pl.BlockSpec((1, D_MODEL), lambda j, b: (0, 0)),
                  pl.BlockSpec((D_MODEL, N_PROJ), lambda j, b: (0, 0)),
                  pl.BlockSpec((tm, LANES), lambda j, b: (j, 0)),
                  pl.BlockSpec((tm, LANES), lambda j, b: (j, 0)),
                  pl.BlockSpec((8, LANES), lambda j, b: (0, 0)),
                  pl.BlockSpec((LANES, LANES), lambda j, b: (0, 0))],
        out_specs=pl.BlockSpec((1, tm, N_PROJ), lambda j, b: (b, j, 0)),
        out_shape=jax.ShapeDtypeStruct((B, T, N_PROJ), BF16),
        compiler_params=_cparams(2),
        name="in_proj",
    )(xx, modv, gain, w, cos, sin, qkg, bd)


ATT_BLOCK = 128
ATT_SPAN = ATT_BLOCK + 2 * WINDOW


def _attend(qblk, pieces, sinks):
    outs = []
    rowi = lax.broadcasted_iota(jnp.int32, (ATT_GROUP * ATT_BLOCK, 1), 0)
    for h in range(ATT_KV_HEADS):
        qs = jnp.concatenate(
            [qblk[:, HEAD_DIM * (ATT_GROUP * h + g):HEAD_DIM * (ATT_GROUP * h + g + 1)] for g in range(ATT_GROUP)],
            axis=0)
        sink = jnp.full((ATT_GROUP * ATT_BLOCK, 1), sinks[ATT_GROUP * h + ATT_GROUP - 1], F32)
        for g in range(ATT_GROUP - 2, -1, -1):
            sink = jnp.where(rowi < (g + 1) * ATT_BLOCK, sinks[ATT_GROUP * h + g], sink)
        scores = []
        m = sink
        for k, _, mask in pieces:
            s = lax.dot_general(qs, k[:, HEAD_DIM * h:HEAD_DIM * (h + 1)], (((1,), (1,)), ((), ())),
                                preferred_element_type=F32)
            if mask is not None:
                s = jnp.where(mask, s, NEG)
            scores.append(s)
            m = jnp.maximum(m, jnp.max(s, axis=-1, keepdims=True))
        den = jnp.exp(sink - m)
        acc = jnp.zeros((ATT_GROUP * ATT_BLOCK, HEAD_DIM), F32)
        for s, (_, v, _) in zip(scores, pieces):
            p = jnp.exp(s - m)
            den = den + jnp.sum(p, axis=-1, keepdims=True)
            acc = acc + jnp.dot(p.astype(BF16), v[:, HEAD_DIM * h:HEAD_DIM * (h + 1)], preferred_element_type=F32)
        o = acc / den
        outs += [o[g * ATT_BLOCK:(g + 1) * ATT_BLOCK] for g in range(ATT_GROUP)]
    return jnp.concatenate(outs, axis=1)


def _att_body(sink_ref, q_ref, k_ref, v_ref, o_ref, *, n_lat, n_ctx, with_ctx_out):
    sinks = [sink_ref[i] for i in range(ATT_HEADS)]
    k_ctx = k_ref[0, n_lat:n_lat + n_ctx, :]
    v_ctx = v_ref[0, n_lat:n_lat + n_ctx, :]
    qi = lax.broadcasted_iota(jnp.int32, (ATT_GROUP * ATT_BLOCK, ATT_SPAN), 0) % ATT_BLOCK
    ki = lax.broadcasted_iota(jnp.int32, (ATT_GROUP * ATT_BLOCK, ATT_SPAN), 1)

    def lat_block(i, carry):
        q0 = pl.multiple_of(i * ATT_BLOCK, ATT_BLOCK)
        k0 = pl.multiple_of(jnp.clip(q0 - WINDOW, 0, n_lat - ATT_SPAN), ATT_BLOCK)
        delta = (q0 + qi) - (k0 + ki)
        mask = jnp.abs(delta) <= WINDOW
        pieces = [(k_ref[0, pl.ds(k0, ATT_SPAN), :], v_ref[0, pl.ds(k0, ATT_SPAN), :], mask),
                  (k_ctx, v_ctx, None)]
        o_ref[0, pl.ds(q0, ATT_BLOCK), :] = _attend(q_ref[0, pl.ds(q0, ATT_BLOCK), :], pieces, sinks).astype(BF16)
        return carry

    lax.fori_loop(0, n_lat // ATT_BLOCK, lat_block, 0)
    if with_ctx_out:
        for c in range(n_ctx // ATT_BLOCK):
            r0 = n_lat + c * ATT_BLOCK
            o_ref[0, r0:r0 + ATT_BLOCK, :] = _attend(q_ref[0, r0:r0 + ATT_BLOCK, :], [(k_ctx, v_ctx, None)],
                                                      sinks).astype(BF16)
    else:
        o_ref[0, n_lat:, :] = jnp.zeros((n_ctx, ATT_WIDTH), BF16)


def _attention(p, sinks, n_lat, with_ctx_out):
    B, T, _ = p.shape
    return pl.pallas_call(
        functools.partial(_att_body, n_lat=n_lat, n_ctx=T - n_lat, with_ctx_out=with_ctx_out),
        grid=(B,),
        in_specs=[pl.BlockSpec(memory_space=pltpu.SMEM),
                  pl.BlockSpec((1, T, ATT_WIDTH), lambda b: (b, 0, COL_AQ // ATT_WIDTH)),
                  pl.BlockSpec((1, T, KV_WIDTH), lambda b: (b, 0, COL_AK // KV_WIDTH)),
                  pl.BlockSpec((1, T, KV_WIDTH), lambda b: (b, 0, COL_AV // KV_WIDTH))],
        out_specs=pl.BlockSpec((1, T, ATT_WIDTH), lambda b: (b, 0, 0)),
        out_shape=jax.ShapeDtypeStruct((B, T, ATT_WIDTH), BF16),
        compiler_params=_cparams(1),
        name="attention",
    )(sinks, p, p, p)


CONV_PAD = 8
EPI_ROWS = 256


def _log_sigmoid(z):
    return jnp.minimum(z, 0.0) - jnp.log1p(jnp.exp(-jnp.abs(z)))


def _gla_body(gq_ref, gk_ref, gv_ref, gr_ref, cb_ref, cc_ref, ch_ref, gt_ref, wgf_ref, wgb_ref, gbias_ref, ng_ref,
              cw_ref, bd_ref, o_ref, gf_s, gb_s, of_s, ob_s, stf_s, stb_s, u_s, *, n_lat, n_ctx):
    T = n_lat + n_ctx
    C = GLA_CHUNK
    gt = gt_ref[0]
    zf = jnp.dot(gt, wgf_ref[...], preferred_element_type=F32) + gbias_ref[0:1, :]
    zb = jnp.dot(gt, wgb_ref[...], preferred_element_type=F32) + gbias_ref[1:2, :]
    gf_s[...] = _log_sigmoid(zf) * (1.0 / GLA_GATE_NORM)
    gb_s[...] = _log_sigmoid(zb) * (1.0 / GLA_GATE_NORM)
    stf_s[...] = jnp.zeros_like(stf_s)
    stb_s[...] = jnp.zeros_like(stb_s)

    ri = lax.broadcasted_iota(jnp.int32, (C, C), 0)
    ci = lax.broadcasted_iota(jnp.int32, (C, C), 1)
    tri_f = jnp.where(ci <= ri, 1.0, 0.0).astype(BF16)
    tri_b = jnp.where(ci >= ri, 1.0, 0.0).astype(BF16)
    r4 = lax.broadcasted_iota(jnp.int32, (GLA_HEADS * C, C), 0) % C
    c4 = lax.broadcasted_iota(jnp.int32, (GLA_HEADS * C, C), 1)
    causal_f = c4 <= r4
    causal_b = c4 > r4
    hr = lax.broadcasted_iota(jnp.int32, (GLA_HEADS * C, GLA_QK_WIDTH), 0) // C
    hl = lax.broadcasted_iota(jnp.int32, (GLA_HEADS * C, GLA_QK_WIDTH), 1) // GLA_DK
    head_rows = hr == hl
    sr = lax.broadcasted_iota(jnp.int32, (GLA_WIDTH, GLA_QK_WIDTH), 0) // GLA_DV
    sl = lax.broadcasted_iota(jnp.int32, (GLA_WIDTH, GLA_QK_WIDTH), 1) // GLA_DK
    state_mask = sr == sl
    ol = lax.broadcasted_iota(jnp.int32, (C, GLA_WIDTH), 1) // GLA_DV

    def chunk(r0, g_s, st_s, o_s, tri, causal, end_row):
        g = g_s[pl.ds(r0, C), :]
        g_hi = g.astype(BF16)
        g_lo = (g - g_hi.astype(F32)).astype(BF16)
        b = jnp.dot(tri, g_hi, preferred_element_type=F32) + jnp.dot(tri, g_lo, preferred_element_type=F32)
        q = gq_ref[0, pl.ds(r0, C), :].astype(F32) * (GLA_DK ** -0.5)
        k = gk_ref[0, pl.ds(r0, C), :].astype(F32)
        v = gv_ref[0, pl.ds(r0, C), :]
        b_end = b[end_row:end_row + 1, :]
        qe = (q * jnp.exp(b)).astype(BF16)
        ke = (k * jnp.exp(-b)).astype(BF16)
        kl = (k * jnp.exp(b_end - b)).astype(BF16)
        qe4 = jnp.where(head_rows, jnp.concatenate([qe] * GLA_HEADS, axis=0), jnp.zeros((), BF16))
        att = lax.dot_general(qe4, ke, (((1,), (1,)), ((), ())), preferred_element_type=F32)
        att = jnp.where(causal, att, 0.0).astype(BF16)
        full = jnp.dot(att, v, preferred_element_type=F32)
        o = full[(GLA_HEADS - 1) * C:]
        for h in range(GLA_HEADS - 2, -1, -1):
            o = jnp.where(ol == h, full[h * C:(h + 1) * C], o)
        st = st_s[...]
        o = o + lax.dot_general(qe, st.astype(BF16), (((1,), (1,)), ((), ())), preferred_element_type=F32)
        upd = lax.dot_general(v, kl, (((0,), (0,)), ((), ())), preferred_element_type=F32)
        st_s[...] = st * jnp.exp(b_end) + jnp.where(state_mask, upd, 0.0)
        o_s[pl.ds(r0, C), :] = o

    nc_lat = n_lat // C
    nc_ctx = n_ctx // C

    def step(i, carry):
        in_ctx = i < nc_ctx
        cf = jnp.where(in_ctx, nc_lat + i, i - nc_ctx)
        cb = jnp.where(in_ctx, nc_lat + nc_ctx - 1 - i, nc_lat - 1 - (i - nc_ctx))
        chunk(pl.multiple_of(cf * C, C), gf_s, stf_s, of_s, tri_f, causal_f, C - 1)
        chunk(pl.multiple_of(cb * C, C), gb_s, stb_s, ob_s, tri_b, causal_b, 0)
        return carry

    lax.fori_loop(0, nc_lat + nc_ctx, step, 0)

    u_s[0:CONV_PAD, :] = jnp.zeros((CONV_PAD, CONV_WIDTH), F32)
    u_s[CONV_PAD + T:, :] = jnp.zeros((CONV_PAD, CONV_WIDTH), F32)
    u_s[CONV_PAD:CONV_PAD + T, :] = cc_ref[0].astype(F32) * ch_ref[0].astype(F32)
    w0 = cw_ref[0:1, :]
    w1 = cw_ref[1:2, :]
    w2 = cw_ref[2:3, :]
    for e in range(T // EPI_ROWS):
        r0 = e * EPI_ROWS
        o = of_s[r0:r0 + EPI_ROWS, :] + ob_s[r0:r0 + EPI_ROWS, :]
        ss = jnp.dot((o * o).astype(BF16), bd_ref[...], preferred_element_type=F32) * (1.0 / GLA_DV)
        on = o * lax.rsqrt(ss + EPS) * ng_ref[...]
        r = gr_ref[0, r0:r0 + EPI_ROWS, :].astype(F32)
        o_ref[0, r0:r0 + EPI_ROWS, 0:GLA_WIDTH] = (on * _silu(r)).astype(BF16)
        t = r0 + lax.broadcasted_iota(jnp.int32, (EPI_ROWS, 1), 0)
        up = u_s[CONV_PAD + r0 - 1:CONV_PAD + r0 - 1 + EPI_ROWS, :]
        mid = u_s[CONV_PAD + r0:CONV_PAD + r0 + EPI_ROWS, :]
        dn = u_s[CONV_PAD + r0 + 1:CONV_PAD + r0 + 1 + EPI_ROWS, :]
        up = jnp.where(t == n_lat, 0.0, up)
        dn = jnp.where(t == n_lat - 1, 0.0, dn)
        conv = w0 * up + w1 * mid + w2 * dn
        o_ref[0, r0:r0 + EPI_ROWS, GLA_WIDTH:] = (cb_ref[0, r0:r0 + EPI_ROWS, :].astype(F32) * conv).astype(BF16)


def _gla_conv(p, wgf, wgb, gbias, ng, cw, bd, n_lat):
    B, T, _ = p.shape

    def col(width, start):
        return pl.BlockSpec((1, T, width), lambda b: (b, 0, start // width))

    def const(shape):
        return pl.BlockSpec(shape, lambda b: (0,) * len(shape))

    return pl.pallas_call(
        functools.partial(_gla_body, n_lat=n_lat, n_ctx=T - n_lat),
        grid=(B,),
        in_specs=[col(GLA_QK_WIDTH, COL_GQ), col(GLA_QK_WIDTH, COL_GK), col(GLA_WIDTH, COL_GV), col(GLA_WIDTH, COL_GR),
                  col(CONV_WIDTH, COL_CB), col(CONV_WIDTH, COL_CC), col(CONV_WIDTH, COL_CH), col(LANES, COL_GT),
                  const((LANES, GLA_QK_WIDTH)), const((LANES, GLA_QK_WIDTH)), const((8, GLA_QK_WIDTH)),
                  const((1, GLA_WIDTH)), const((8, CONV_WIDTH)), const((GLA_WIDTH, GLA_WIDTH))],
        out_specs=pl.BlockSpec((1, T, GLA_WIDTH + CONV_WIDTH), lambda b: (b, 0, 0)),
        out_shape=jax.ShapeDtypeStruct((B, T, GLA_WIDTH + CONV_WIDTH), BF16),
        scratch_shapes=[pltpu.VMEM((T, GLA_QK_WIDTH), F32), pltpu.VMEM((T, GLA_QK_WIDTH), F32),
                        pltpu.VMEM((T, GLA_WIDTH), F32), pltpu.VMEM((T, GLA_WIDTH), F32),
                        pltpu.VMEM((GLA_WIDTH, GLA_QK_WIDTH), F32), pltpu.VMEM((GLA_WIDTH, GLA_QK_WIDTH), F32),
                        pltpu.VMEM((T + 2 * CONV_PAD, CONV_WIDTH), F32)],
        compiler_params=_cparams(1),
        name="gla_conv",
    )(p, p, p, p, p, p, p, p, wgf, wgb, gbias, ng, cw, bd)


def _route(logits_t):
    mx = jnp.max(logits_t, axis=0, keepdims=True)
    ex = jnp.exp(logits_t - mx)
    probs = ex / jnp.sum(ex, axis=0, keepdims=True)
    P = [probs[e:e + 1] for e in range(N_EXPERTS)]
    scores = []
    for g in range(N_GROUPS):
        a, b, c, d = P[4 * g:4 * g + 4]
        scores.append(jnp.maximum(jnp.maximum(jnp.maximum(a + b, a + c), jnp.maximum(a + d, b + c)),
                                  jnp.maximum(b + d, c + d)))
    best = jnp.maximum(jnp.maximum(scores[0], scores[1]), jnp.maximum(scores[2], scores[3]))
    taken = jnp.zeros_like(best, dtype=jnp.bool_)
    sel = []
    for g in range(N_GROUPS):
        s = (scores[g] == best) & jnp.logical_not(taken)
        sel.append(s)
        taken = taken | s
    gsel = jnp.where(sel[1], 1.0, 0.0) + jnp.where(sel[2], 2.0, 0.0) + jnp.where(sel[3], 3.0, 0.0)
    ig = [jnp.where(sel[0], P[j], jnp.where(sel[1], P[4 + j], jnp.where(sel[2], P[8 + j], P[12 + j])))
          for j in range(EXPERTS_PER_GROUP)]

    def first_max(vals):
        v = jnp.maximum(jnp.maximum(vals[0], vals[1]), jnp.maximum(vals[2], vals[3]))
        tk = jnp.zeros_like(v, dtype=jnp.bool_)
        hot = []
        for x in vals:
            s = (x == v) & jnp.logical_not(tk)
            hot.append(s)
            tk = tk | s
        idx = jnp.where(hot[1], 1.0, 0.0) + jnp.where(hot[2], 2.0, 0.0) + jnp.where(hot[3], 3.0, 0.0)
        return v, hot, idx

    v1, hot1, i1 = first_max(ig)
    v2, _, i2 = first_max([jnp.where(hot1[j], -1.0, ig[j]) for j in range(EXPERTS_PER_GROUP)])
    den = v1 + v2
    w1 = v1 / den
    w2 = v2 / den
    lo = jnp.minimum(i1, i2)
    hi = jnp.maximum(i1, i2)
    w_lo = jnp.where(i1 < i2, w1, w2)
    w_hi = jnp.where(i1 < i2, w2, w1)
    pair = jnp.where(lo == 0.0, hi - 1.0, jnp.where(lo == 1.0, hi + 1.0, 5.0))
    return gsel * N_PAIRS + pair, w_lo, w_hi


def _out_body(ya_ref, yg_ref, x_ref, mod_ref, wo_ref, g_ref, wr_ref, br_ref, xo_ref, rt_ref, wrow_ref, *, n_lat, tm):
    j = pl.program_id(1)
    row = j * tm + lax.broadcasted_iota(jnp.int32, (tm, 1), 0)
    is_ctx = row >= n_lat
    mod = mod_ref[0]
    y = (jnp.dot(ya_ref[0], wo_ref[0:ATT_WIDTH, :], preferred_element_type=F32)
         + jnp.dot(yg_ref[0], wo_ref[ATT_WIDTH:, :], preferred_element_type=F32))
    xn = x_ref[0] + _row_mod(mod, is_ctx, 2) * y
    xo_ref[0] = xn
    h2 = _norm_modulate(xn, g_ref[...], mod, is_ctx, 3, 4).astype(BF16)
    logits = jnp.dot(h2, wr_ref[...], preferred_element_type=F32) + br_ref[0:1, :]
    cls, w_lo, w_hi = _route(logits.T[0:N_EXPERTS, :])
    rt_ref[0] = jnp.concatenate([cls, w_lo, w_hi, jnp.zeros((5, tm), F32)], axis=0)
    wrow_ref[0] = jnp.concatenate([w_lo, w_hi, jnp.zeros((LANES - 2, tm), F32)], axis=0).T


def _out_proj(ya, yg, xx, modv, wo, gain, wr, br, n_lat, rows, tm):
    B, T, _ = xx.shape
    nj = rows // tm
    return pl.pallas_call(
        functools.partial(_out_body, n_lat=n_lat, tm=tm),
        grid=(B, nj),
        in_specs=[pl.BlockSpec((1, tm, ATT_WIDTH), lambda b, j: (b, j, 0)),
                  pl.BlockSpec((1, tm, GLA_WIDTH + CONV_WIDTH), lambda b, j: (b, j, 0)),
                  pl.BlockSpec((1, tm, D_MODEL), lambda b, j: (b, j, 0)),
                  pl.BlockSpec((1, 16, D_MODEL), lambda b, j: (b, 0, 0)),
                  pl.BlockSpec((D_MODEL, D_MODEL), lambda b, j: (0, 0)),
                  pl.BlockSpec((1, D_MODEL), lambda b, j: (0, 0)),
                  pl.BlockSpec((D_MODEL, LANES), lambda b, j: (0, 0)),
                  pl.BlockSpec((8, LANES), lambda b, j: (0, 0))],
        out_specs=[pl.BlockSpec((1, tm, D_MODEL), lambda b, j: (b, j, 0)),
                   pl.BlockSpec((1, 8, tm), lambda b, j: (b * nj + j, 0, 0)),
                   pl.BlockSpec((1, tm, LANES), lambda b, j: (b, j, 0))],
        out_shape=[jax.ShapeDtypeStruct((B, rows, D_MODEL), F32),
                   jax.ShapeDtypeStruct((B * nj, 8, tm), F32),
                   jax.ShapeDtypeStruct((B, rows, LANES), F32)],
        compiler_params=_cparams(2),
        name="out_proj_router",
    )(ya, yg, xx, modv, wo, gain, wr, br)


def _row_copy_loop(n, make_copy):
    def issue(r, c):
        make_copy(r).start()
        return c

    lax.fori_loop(0, n, issue, 0)

    def drain(r, c):
        make_copy(r).wait()
        return c

    lax.fori_loop(0, n, drain, 0)


def _disp_body(x_ref, mod_ref, g_ref, wrow_ref, dest_ref, hs_in_ref, hs_ref, buf, idx, sem_i, sem_d, *, n_lat, tm, nj):
    del hs_in_ref
    b = pl.program_id(0)
    j = pl.program_id(1)
    fetch = pltpu.make_async_copy(dest_ref.at[b * nj + j], idx, sem_i)
    fetch.start()
    row = j * tm + lax.broadcasted_iota(jnp.int32, (tm, 1), 0)
    is_ctx = row >= n_lat
    buf[:, 0:D_MODEL] = _norm_modulate(x_ref[0], g_ref[...], mod_ref[0], is_ctx, 3, 4)
    buf[:, D_MODEL:] = wrow_ref[0]
    fetch.wait()
    _row_copy_loop(tm, lambda r: pltpu.make_async_copy(buf.at[pl.ds(r, 1), :], hs_ref.at[pl.ds(idx[0, r], 1), :],
                                                       sem_d))


def _dispatch(xx, modv, gain, wrow, dest, hs0, n_lat, tm):
    B, rows, _ = xx.shape
    nj = rows // tm
    return pl.pallas_call(
        functools.partial(_disp_body, n_lat=n_lat, tm=tm, nj=nj),
        grid=(B, nj),
        in_specs=[pl.BlockSpec((1, tm, D_MODEL), lambda b, j: (b, j, 0)),
                  pl.BlockSpec((1, 16, D_MODEL), lambda b, j: (b, 0, 0)),
                  pl.BlockSpec((1, D_MODEL), lambda b, j: (0, 0)),
                  pl.BlockSpec((1, tm, LANES), lambda b, j: (b, j, 0)),
                  pl.BlockSpec(memory_space=pl.ANY),
                  pl.BlockSpec(memory_space=pl.ANY)],
        out_specs=pl.BlockSpec(memory_space=pl.ANY),
        out_shape=jax.ShapeDtypeStruct(hs0.shape, F32),
        scratch_shapes=[pltpu.VMEM((tm, ROW_W), F32), pltpu.SMEM((1, tm), jnp.int32),
                        pltpu.SemaphoreType.DMA, pltpu.SemaphoreType.DMA],
        input_output_aliases={5: 0},
        compiler_params=_cparams(2),
        name="dispatch",
    )(xx, modv, gain, wrow, dest, hs0)


def _moe_body(tile_ref, e_lo_ref, e_hi_ref, valid_ref, hs_ref, wg1, wu1, wd1, wg2, wu2, wd2, ys_ref):
    g = pl.program_id(0)

    @pl.when(valid_ref[g] == 1)
    def _():
        blk = hs_ref[...]
        h = blk[:, 0:D_MODEL].astype(BF16)
        w_lo = blk[:, D_MODEL:D_MODEL + 1]
        w_hi = blk[:, D_MODEL + 1:D_MODEL + 2]

        def act(wg, wu, w):
            a = _silu(jnp.dot(h, wg[0], preferred_element_type=F32)) * jnp.dot(h, wu[0], preferred_element_type=F32)
            return (a * w).astype(BF16)

        ys_ref[...] = (jnp.dot(act(wg1, wu1, w_lo), wd1[0], preferred_element_type=F32)
                       + jnp.dot(act(wg2, wu2, w_hi), wd2[0], preferred_element_type=F32))

    @pl.when(valid_ref[g] == 0)
    def _():
        ys_ref[...] = jnp.zeros_like(ys_ref)


def _moe(hs, tile, e_lo, e_hi, valid, wg, wu, wd):
    n_tiles = tile.shape[0]
    tm = MOE_TILE

    def w_in(sel):
        return pl.BlockSpec((1, D_MODEL, D_EXPERT), lambda g, t, lo, hi, v: ((lo, hi)[sel][g], 0, 0))

    def w_out(sel):
        return pl.BlockSpec((1, D_EXPERT, D_MODEL), lambda g, t, lo, hi, v: ((lo, hi)[sel][g], 0, 0))

    return pl.pallas_call(
        _moe_body,
        grid_spec=pltpu.PrefetchScalarGridSpec(
            num_scalar_prefetch=4,
            grid=(n_tiles,),
            in_specs=[pl.BlockSpec((tm, ROW_W), lambda g, t, lo, hi, v: (t[g], 0)),
                      w_in(0), w_in(0), w_out(0), w_in(1), w_in(1), w_out(1)],
            out_specs=pl.BlockSpec((tm, D_MODEL), lambda g, t, lo, hi, v: (g, 0))),
        out_shape=jax.ShapeDtypeStruct((n_tiles * tm, D_MODEL), F32),
        compiler_params=_cparams(1),
        name="moe_pairs",
    )(tile, e_lo, e_hi, valid, hs, wg, wu, wd, wg, wu, wd)


def _fin_body(x_ref, mod_ref, dest_ref, ys_ref, xo_ref, buf, idx, sem_i, sem_d, *, n_lat, tm, nj):
    b = pl.program_id(0)
    j = pl.program_id(1)
    fetch = pltpu.make_async_copy(dest_ref.at[b * nj + j], idx, sem_i)
    fetch.start()
    fetch.wait()
    _row_copy_loop(tm, lambda r: pltpu.make_async_copy(ys_ref.at[pl.ds(idx[0, r], 1), :], buf.at[pl.ds(r, 1), :],
                                                       sem_d))
    row = j * tm + lax.broadcasted_iota(jnp.int32, (tm, 1), 0)
    is_ctx = row >= n_lat
    xo_ref[0] = x_ref[0] + _row_mod(mod_ref[0], is_ctx, 5) * buf[...]


def _combine(xx, modv, dest, ys, n_lat, tm):
    B, rows, _ = xx.shape
    nj = rows // tm
    return pl.pallas_call(
        functools.partial(_fin_body, n_lat=n_lat, tm=tm, nj=nj),
        grid=(B, nj),
        in_specs=[pl.BlockSpec((1, tm, D_MODEL), lambda b, j: (b, j, 0)),
                  pl.BlockSpec((1, 16, D_MODEL), lambda b, j: (b, 0, 0)),
                  pl.BlockSpec(memory_space=pl.ANY),
                  pl.BlockSpec(memory_space=pl.ANY)],
        out_specs=pl.BlockSpec((1, tm, D_MODEL), lambda b, j: (b, j, 0)),
        out_shape=jax.ShapeDtypeStruct((B, rows, D_MODEL), F32),
        scratch_shapes=[pltpu.VMEM((tm, D_MODEL), F32), pltpu.SMEM((1, tm), jnp.int32),
                        pltpu.SemaphoreType.DMA, pltpu.SemaphoreType.DMA],
        compiler_params=_cparams(2),
        name="combine",
    )(xx, modv, dest, ys)


def _rope_tables(n_lat, n_ctx):
    rows = n_lat // GRID_W
    row, col = jnp.meshgrid(jnp.arange(rows), jnp.arange(GRID_W), indexing="ij")
    n_freq = HEAD_DIM // 4
    inv_freq = ROPE_BASE ** (-jnp.arange(n_freq, dtype=F32) / n_freq)
    ang = jnp.concatenate([row.reshape(-1, 1).astype(F32) * inv_freq, col.reshape(-1, 1).astype(F32) * inv_freq],
                          axis=-1)
    cos = jnp.tile(jnp.cos(ang), (1, LANES // (HEAD_DIM // 2)))
    sin = jnp.tile(jnp.sin(ang), (1, LANES // (HEAD_DIM // 2)))
    sign = jnp.where((jnp.arange(LANES) % HEAD_DIM) < HEAD_DIM // 2, -1.0, 1.0).astype(F32)
    cos = jnp.concatenate([cos, jnp.ones((n_ctx, LANES), F32)], axis=0)
    sin = jnp.concatenate([sin * sign, jnp.zeros((n_ctx, LANES), F32)], axis=0)
    return cos, sin


def _block_diag_ones(n, blk):
    i = np.arange(n) // blk
    return jnp.asarray((i[:, None] == i[None, :]).astype(np.float32), dtype=BF16)


def _routing_tables(cls, n_tiles):
    onehot = (cls[:, None] == jnp.arange(N_CLASSES, dtype=jnp.int32)[None, :]).astype(jnp.int32)
    csum = jnp.cumsum(onehot, axis=0)
    rank = jnp.sum(onehot * (csum - 1), axis=1)
    counts = csum[-1]
    tiles_c = (counts + MOE_TILE - 1) // MOE_TILE
    tile_end = jnp.cumsum(tiles_c)
    tile_start = tile_end - tiles_c
    dest = jnp.sum(onehot * (tile_start * MOE_TILE)[None, :], axis=1) + rank
    total = tile_end[-1]
    g = jnp.arange(n_tiles, dtype=jnp.int32)
    valid = (g < total).astype(jnp.int32)
    g_eff = jnp.minimum(g, total - 1)
    c_of = jnp.sum((g_eff[:, None] >= tile_end[None, :]).astype(jnp.int32), axis=1)
    group = c_of // N_PAIRS
    pair = c_of % N_PAIRS
    lo = jnp.asarray(PAIR_LO, jnp.int32)
    hi = jnp.asarray(PAIR_HI, jnp.int32)
    e_lo = group * EXPERTS_PER_GROUP + jnp.sum((pair[:, None] == jnp.arange(N_PAIRS)[None, :]) * lo[None, :], axis=1)
    e_hi = group * EXPERTS_PER_GROUP + jnp.sum((pair[:, None] == jnp.arange(N_PAIRS)[None, :]) * hi[None, :], axis=1)
    return dest.astype(jnp.int32), g_eff, e_lo.astype(jnp.int32), e_hi.astype(jnp.int32), valid


def kernel(x, c, ctx, c_ctx, w_ada, b_ada, norm_mix_g, norm_ffn_g, w_in, q_norm_g, k_norm_g, attn_sink, gla_gate_w,
           gla_gate_b, gla_norm_g, conv_w, w_out, w_router, b_router, w_gate_e, w_up_e, w_down_e):
    B, S, D = x.shape
    L = ctx.shape[1]
    T = S + L
    assert D == D_MODEL and T % TOKEN_TILE == 0 and S % LAT_TILE == 0 and S % GRID_W == 0
    assert S % ATT_BLOCK == 0 and L % ATT_BLOCK == 0 and S >= ATT_SPAN and T % EPI_ROWS == 0

    cond_rows = -(-(B + 1) // 8) * 8
    cond = jnp.zeros((cond_rows, D), F32).at[:B].set(c).at[B].set(c_ctx)
    mod_all = _modulation(cond, w_ada, b_ada)

    cos, sin = _rope_tables(S, L)
    bd_head = _block_diag_ones(LANES, HEAD_DIM)
    bd_gla = _block_diag_ones(GLA_WIDTH, GLA_DV)
    xx = jnp.concatenate([x, ctx], axis=1)

    for l in range(DEPTH):
        last = l == DEPTH - 1
        m_lat = mod_all[l, :B].reshape(B, 6, D)
        m_ctx = jnp.broadcast_to(mod_all[l, B].reshape(1, 6, D), (B, 6, D))
        modv = jnp.concatenate([m_lat, m_ctx, jnp.zeros((B, 4, D), F32)], axis=1)

        wl = w_in[l]
        w_perm = jnp.concatenate([wl[:, :1536], wl[:, 1568:], wl[:, 1536:1568],
                                  jnp.zeros((D, N_PROJ - wl.shape[1]), F32)], axis=1).astype(BF16)
        qg = jnp.tile(q_norm_g[l], LANES // HEAD_DIM) * (HEAD_DIM ** -0.5)
        kg = jnp.tile(k_norm_g[l], LANES // HEAD_DIM)
        qkg = jnp.stack([qg] * (ATT_WIDTH // LANES) + [kg] + [jnp.zeros_like(kg)] * 3)
        p = _in_proj(xx, modv, norm_mix_g[l].reshape(1, D), w_perm, cos, sin, qkg, bd_head, S)

        y_att = _attention(p, attn_sink[l], S, not last)

        pad_rows = jnp.zeros((LANES - 2 * GLA_GATE_RANK, GLA_QK_WIDTH), F32)
        zero_rank = jnp.zeros((GLA_GATE_RANK, GLA_QK_WIDTH), F32)
        wgf = jnp.concatenate([gla_gate_w[l, 0], zero_rank, pad_rows], axis=0).astype(BF16)
        wgb = jnp.concatenate([zero_rank, gla_gate_w[l, 1], pad_rows], axis=0).astype(BF16)
        gbias = jnp.concatenate([gla_gate_b[l], jnp.zeros((6, GLA_QK_WIDTH), F32)], axis=0)
        ng = jnp.tile(gla_norm_g[l], GLA_HEADS).reshape(1, GLA_WIDTH)
        cw = jnp.concatenate([conv_w[l], jnp.zeros((5, CONV_WIDTH), F32)], axis=0)
        y_gc = _gla_conv(p, wgf, wgb, gbias, ng, cw, bd_gla, S)

        rows, tm = (S, LAT_TILE) if last else (T, TOKEN_TILE)
        wr = jnp.concatenate([w_router, jnp.zeros((D, LANES - N_EXPERTS), F32)], axis=1).astype(BF16)
        br = jnp.zeros((8, LANES), F32).at[0, :N_EXPERTS].set(b_router)
        ffn_g = norm_ffn_g[l].reshape(1, D)
        xx_mid, route, wrow = _out_proj(y_att, y_gc, xx, modv, w_out[l].astype(BF16), ffn_g, wr, br, S, rows, tm)

        n_tok = B * rows
        n_tiles = -(-n_tok // MOE_TILE) + N_CLASSES
        cls = route[:, 0, :].reshape(n_tok).astype(jnp.int32)
        dest, tile, e_lo, e_hi, valid = _routing_tables(cls, n_tiles)
        dest = dest.reshape(n_tok // tm, 1, tm)

        hs = _dispatch(xx_mid, modv, ffn_g, wrow, dest, jnp.zeros((n_tiles * MOE_TILE, ROW_W), F32), S, tm)
        ys = _moe(hs, tile, e_lo, e_hi, valid, w_gate_e[l].astype(BF16), w_up_e[l].astype(BF16),
                  w_down_e[l].astype(BF16))
        xx = _combine(xx_mid, modv, dest, ys, S, tm)
    return xx
```

```python
import functools

import numpy as np
import jax
import jax.numpy as jnp
from jax import lax
from jax.experimental import pallas as pl
from jax.experimental.pallas import tpu as pltpu

D_MODEL = 1024
DEPTH = 2
GRID_W = 64
EPS = 1e-6
HEAD_DIM = 64
ATT_HEADS = 8
ATT_KV_HEADS = 2
ATT_GROUP = ATT_HEADS // ATT_KV_HEADS
ATT_WIDTH = ATT_HEADS * HEAD_DIM
WINDOW = 128
ROPE_BASE = 10000.0
GLA_HEADS = 4
GLA_DV = 64
GLA_DK = 32
GLA_WIDTH = GLA_HEADS * GLA_DV
GLA_GATE_RANK = 16
GLA_GATE_NORM = 16.0
GLA_CHUNK = 64
CONV_WIDTH = 256
N_EXPERTS = 16
N_GROUPS = 4
EXPERTS_PER_GROUP = 4
D_EXPERT = D_MODEL // 2

LANES = 128
KV_WIDTH = ATT_KV_HEADS * HEAD_DIM
GLA_QK_WIDTH = GLA_HEADS * GLA_DK
COL_AQ, COL_AK, COL_AV = 0, 512, 640
COL_GQ, COL_GK, COL_GV, COL_GR = 768, 896, 1024, 1280
COL_CB, COL_CC, COL_CH, COL_GT = 1536, 1792, 2048, 2304
N_PROJ = 2432
QK_COLS = COL_AV
N_PAIRS = 6
N_CLASSES = N_GROUPS * N_PAIRS
PAIR_LO = (0, 0, 0, 1, 1, 2)
PAIR_HI = (1, 2, 3, 2, 3, 3)
CLS_ROWS = 32
NEG = -1e30

TOKEN_TILE = 768
LAT_TILE = 1024
MOE_TILE = 512
VMEM_LIMIT = 56 * 1024 * 1024

F32 = jnp.float32
BF16 = jnp.bfloat16


def _cparams(n_axes):
    return pltpu.CompilerParams(dimension_semantics=("arbitrary",) * n_axes, vmem_limit_bytes=VMEM_LIMIT)


def _silu(x):
    return x * jax.nn.sigmoid(x)


def _mod_body(c_ref, w_ref, b_ref, o_ref):
    c = c_ref[...]
    a = _silu(c).astype(BF16)
    o_ref[0] = jnp.dot(a, w_ref[0].astype(BF16), preferred_element_type=F32) + b_ref[0]


def _modulation(cond, w_ada, b_ada):
    rows = cond.shape[0]
    nblk = w_ada.shape[2] // D_MODEL
    return pl.pallas_call(
        _mod_body,
        grid=(DEPTH, nblk),
        in_specs=[pl.BlockSpec((rows, D_MODEL), lambda l, n: (0, 0)),
                  pl.BlockSpec((1, D_MODEL, D_MODEL), lambda l, n: (l, 0, n)),
                  pl.BlockSpec((1, 1, D_MODEL), lambda l, n: (l, 0, n))],
        out_specs=pl.BlockSpec((1, rows, D_MODEL), lambda l, n: (l, 0, n)),
        out_shape=jax.ShapeDtypeStruct((DEPTH, rows, w_ada.shape[2]), F32),
        compiler_params=_cparams(2),
        name="modulation",
    )(cond, w_ada, b_ada.reshape(DEPTH, 1, -1))


def _row_mod(mod, is_ctx, i):
    return jnp.where(is_ctx, mod[6 + i:7 + i], mod[i:i + 1])


def _norm_modulate(x, gain, mod, is_ctx, i_shift, i_scale):
    ms = jnp.mean(x * x, axis=-1, keepdims=True)
    xn = x * lax.rsqrt(ms + EPS) * gain
    return xn * (1.0 + _row_mod(mod, is_ctx, i_scale)) + _row_mod(mod, is_ctx, i_shift)


def _in_body(x_ref, mod_ref, g_ref, w_ref, cos_ref, sin_ref, qkg_ref, bd_ref, o_ref, *, n_lat, tm):
    j = pl.program_id(0)
    row = j * tm + lax.broadcasted_iota(jnp.int32, (tm, 1), 0)
    is_ctx = row >= n_lat
    h = _norm_modulate(x_ref[0], g_ref[...], mod_ref[0], is_ctx, 0, 1).astype(BF16)
    qk = jnp.dot(h, w_ref[:, :QK_COLS], preferred_element_type=F32)
    cos = cos_ref[...]
    sin = sin_ref[...]
    lane = lax.broadcasted_iota(jnp.int32, (1, LANES), 1)
    first_half = (lane % HEAD_DIM) < (HEAD_DIM // 2)
    for c in range(QK_COLS // LANES):
        xc = qk[:, c * LANES:(c + 1) * LANES]
        ss = jnp.dot((xc * xc).astype(BF16), bd_ref[...], preferred_element_type=F32) * (1.0 / HEAD_DIM)
        xc = xc * lax.rsqrt(ss + EPS) * qkg_ref[c:c + 1, :]
        rot = jnp.where(first_half, pltpu.roll(xc, LANES - HEAD_DIM // 2, 1), pltpu.roll(xc, HEAD_DIM // 2, 1))
        o_ref[0, :, c * LANES:(c + 1) * LANES] = (xc * cos + rot * sin).astype(BF16)
    o_ref[0, :, QK_COLS:] = jnp.dot(h, w_ref[:, QK_COLS:], preferred_element_type=F32).astype(BF16)


def _in_proj(xx, modv, gain, w, cos, sin, qkg, bd, n_lat):
    B, T, _ = xx.shape
    tm = TOKEN_TILE
    return pl.pallas_call(
        functools.partial(_in_body, n_lat=n_lat, tm=tm),
        grid=(T // tm, B),
        in_specs=[pl.BlockSpec((1, tm, D_MODEL), lambda j, b: (b, j, 0)),
                  pl.BlockSpec((1, 16, D_MODEL), lambda j, b: (b, 0, 0)),
                  pl.BlockSpec((1, D_MODEL), lambda j, b: (0, 0)),
                  pl.BlockSpec((D_MODEL, N_PROJ), lambda j, b: (0, 0)),
                  pl.BlockSpec((tm, LANES), lambda j, b: (j, 0)),
                  pl.BlockSpec((tm, LANES), lambda j, b: (j, 0)),
                  pl.BlockSpec((8, LANES), lambda j, b: (0, 0)),
                  pl.BlockSpec((LANES, LANES), lambda j, b: (0, 0))],
        out_specs=pl.BlockSpec((1, tm, N_PROJ), lambda j, b: (b, j, 0)),
        out_shape=jax.ShapeDtypeStruct((B, T, N_PROJ), BF16),
        compiler_params=_cparams(2),
        name="in_proj",
    )(xx, modv, gain, w, cos, sin, qkg, bd)


ATT_BLOCK = 128
ATT_SPAN = ATT_BLOCK + 2 * WINDOW


def _attend(qblk, pieces, sinks):
    outs = []
    rowi = lax.broadcasted_iota(jnp.int32, (ATT_GROUP * ATT_BLOCK, 1), 0)
    for h in range(ATT_KV_HEADS):
        qs = jnp.concatenate(
            [qblk[:, HEAD_DIM * (ATT_GROUP * h + g):HEAD_DIM * (ATT_GROUP * h + g + 1)] for g in range(ATT_GROUP)],
            axis=0)
        sink = jnp.full((ATT_GROUP * ATT_BLOCK, 1), sinks[ATT_GROUP * h + ATT_GROUP - 1], F32)
        for g in range(ATT_GROUP - 2, -1, -1):
            sink = jnp.where(rowi < (g + 1) * ATT_BLOCK, sinks[ATT_GROUP * h + g], sink)
        scores = []
        m = sink
        for k, _, mask in pieces:
            s = lax.dot_general(qs, k[:, HEAD_DIM * h:HEAD_DIM * (h + 1)], (((1,), (1,)), ((), ())),
                                preferred_element_type=F32)
            if mask is not None:
                s = jnp.where(mask, s, NEG)
            scores.append(s)
            m = jnp.maximum(m, jnp.max(s, axis=-1, keepdims=True))
        den = jnp.exp(sink - m)
        acc = jnp.zeros((ATT_GROUP * ATT_BLOCK, HEAD_DIM), F32)
        for s, (_, v, _) in zip(scores, pieces):
            p = jnp.exp(s - m)
            den = den + jnp.sum(p, axis=-1, keepdims=True)
            acc = acc + jnp.dot(p.astype(BF16), v[:, HEAD_DIM * h:HEAD_DIM * (h + 1)], preferred_element_type=F32)
        o = acc / den
        outs += [o[g * ATT_BLOCK:(g + 1) * ATT_BLOCK] for g in range(ATT_GROUP)]
    return jnp.concatenate(outs, axis=1)


def _att_body(sink_ref, q_ref, k_ref, v_ref, o_ref, *, n_lat, n_ctx, with_ctx_out):
    sinks = [sink_ref[i] for i in range(ATT_HEADS)]
    k_ctx = k_ref[0, n_lat:n_lat + n_ctx, :]
    v_ctx = v_ref[0, n_lat:n_lat + n_ctx, :]
    qi = lax.broadcasted_iota(jnp.int32, (ATT_GROUP * ATT_BLOCK, ATT_SPAN), 0) % ATT_BLOCK
    ki = lax.broadcasted_iota(jnp.int32, (ATT_GROUP * ATT_BLOCK, ATT_SPAN), 1)

    def lat_block(i, carry):
        q0 = pl.multiple_of(i * ATT_BLOCK, ATT_BLOCK)
        k0 = pl.multiple_of(jnp.clip(q0 - WINDOW, 0, n_lat - ATT_SPAN), ATT_BLOCK)
        delta = (q0 + qi) - (k0 + ki)
        mask = jnp.abs(delta) <= WINDOW
        pieces = [(k_ref[0, pl.ds(k0, ATT_SPAN), :], v_ref[0, pl.ds(k0, ATT_SPAN), :], mask),
                  (k_ctx, v_ctx, None)]
        o_ref[0, pl.ds(q0, ATT_BLOCK), :] = _attend(q_ref[0, pl.ds(q0, ATT_BLOCK), :], pieces, sinks).astype(BF16)
        return carry

    lax.fori_loop(0, n_lat // ATT_BLOCK, lat_block, 0)
    if with_ctx_out:
        for c in range(n_ctx // ATT_BLOCK):
            r0 = n_lat + c * ATT_BLOCK
            o_ref[0, r0:r0 + ATT_BLOCK, :] = _attend(q_ref[0, r0:r0 + ATT_BLOCK, :], [(k_ctx, v_ctx, None)],
                                                      sinks).astype(BF16)
    else:
        o_ref[0, n_lat:, :] = jnp.zeros((n_ctx, ATT_WIDTH), BF16)


def _attention(p, sinks, n_lat, with_ctx_out):
    B, T, _ = p.shape
    return pl.pallas_call(
        functools.partial(_att_body, n_lat=n_lat, n_ctx=T - n_lat, with_ctx_out=with_ctx_out),
        grid=(B,),
        in_specs=[pl.BlockSpec(memory_space=pltpu.SMEM),
                  pl.BlockSpec((1, T, ATT_WIDTH), lambda b: (b, 0, COL_AQ // ATT_WIDTH)),
                  pl.BlockSpec((1, T, KV_WIDTH), lambda b: (b, 0, COL_AK // KV_WIDTH)),
                  pl.BlockSpec((1, T, KV_WIDTH), lambda b: (b, 0, COL_AV // KV_WIDTH))],
        out_specs=pl.BlockSpec((1, T, ATT_WIDTH), lambda b: (b, 0, 0)),
        out_shape=jax.ShapeDtypeStruct((B, T, ATT_WIDTH), BF16),
        compiler_params=_cparams(1),
        name="attention",
    )(sinks, p, p, p)


CONV_PAD = 8
EPI_ROWS = 256


def _log_sigmoid(z):
    return jnp.minimum(z, 0.0) - jnp.log1p(jnp.exp(-jnp.abs(z)))


def _gla_body(gq_ref, gk_ref, gv_ref, gr_ref, cb_ref, cc_ref, ch_ref, gt_ref, wgf_ref, wgb_ref, gbias_ref, ng_ref,
              cw_ref, bd_ref, o_ref, gf_s, gb_s, of_s, ob_s, stf_s, stb_s, u_s, *, n_lat, n_ctx):
    T = n_lat + n_ctx
    C = GLA_CHUNK
    gt = gt_ref[0]
    zf = jnp.dot(gt, wgf_ref[...], preferred_element_type=F32) + gbias_ref[0:1, :]
    zb = jnp.dot(gt, wgb_ref[...], preferred_element_type=F32) + gbias_ref[1:2, :]
    gf_s[...] = _log_sigmoid(zf) * (1.0 / GLA_GATE_NORM)
    gb_s[...] = _log_sigmoid(zb) * (1.0 / GLA_GATE_NORM)
    stf_s[...] = jnp.zeros_like(stf_s)
    stb_s[...] = jnp.zeros_like(stb_s)

    ri = lax.broadcasted_iota(jnp.int32, (C, C), 0)
    ci = lax.broadcasted_iota(jnp.int32, (C, C), 1)
    tri_f = jnp.where(ci <= ri, 1.0, 0.0).astype(BF16)
    tri_b = jnp.where(ci >= ri, 1.0, 0.0).astype(BF16)
    r4 = lax.broadcasted_iota(jnp.int32, (GLA_HEADS * C, C), 0) % C
    c4 = lax.broadcasted_iota(jnp.int32, (GLA_HEADS * C, C), 1)
    causal_f = c4 <= r4
    causal_b = c4 > r4
    hr = lax.broadcasted_iota(jnp.int32, (GLA_HEADS * C, GLA_QK_WIDTH), 0) // C
    hl = lax.broadcasted_iota(jnp.int32, (GLA_HEADS * C, GLA_QK_WIDTH), 1) // GLA_DK
    head_rows = hr == hl
    sr = lax.broadcasted_iota(jnp.int32, (GLA_WIDTH, GLA_QK_WIDTH), 0) // GLA_DV
    sl = lax.broadcasted_iota(jnp.int32, (GLA_WIDTH, GLA_QK_WIDTH), 1) // GLA_DK
    state_mask = sr == sl
    ol = lax.broadcasted_iota(jnp.int32, (C, GLA_WIDTH), 1) // GLA_DV

    def chunk(r0, g_s, st_s, o_s, tri, causal, end_row):
        g = g_s[pl.ds(r0, C), :]
        g_hi = g.astype(BF16)
        g_lo = (g - g_hi.astype(F32)).astype(BF16)
        b = jnp.dot(tri, g_hi, preferred_element_type=F32) + jnp.dot(tri, g_lo, preferred_element_type=F32)
        q = gq_ref[0, pl.ds(r0, C), :].astype(F32) * (GLA_DK ** -0.5)
        k = gk_ref[0, pl.ds(r0, C), :].astype(F32)
        v = gv_ref[0, pl.ds(r0, C), :]
        b_end = b[end_row:end_row + 1, :]
        qe = (q * jnp.exp(b)).astype(BF16)
        ke = (k * jnp.exp(-b)).astype(BF16)
        kl = (k * jnp.exp(b_end - b)).astype(BF16)
        qe4 = jnp.where(head_rows, jnp.concatenate([qe] * GLA_HEADS, axis=0), jnp.zeros((), BF16))
        att = lax.dot_general(qe4, ke, (((1,), (1,)), ((), ())), preferred_element_type=F32)
        att = jnp.where(causal, att, 0.0).astype(BF16)
        full = jnp.dot(att, v, preferred_element_type=F32)
        o = full[(GLA_HEADS - 1) * C:]
        for h in range(GLA_HEADS - 2, -1, -1):
            o = jnp.where(ol == h, full[h * C:(h + 1) * C], o)
        st = st_s[...]
        o = o + lax.dot_general(qe, st.astype(BF16), (((1,), (1,)), ((), ())), preferred_element_type=F32)
        upd = lax.dot_general(v, kl, (((0,), (0,)), ((), ())), preferred_element_type=F32)
        st_s[...] = st * jnp.exp(b_end) + jnp.where(state_mask, upd, 0.0)
        o_s[pl.ds(r0, C), :] = o

    nc_lat = n_lat // C
    nc_ctx = n_ctx // C

    def step(i, carry):
        in_ctx = i < nc_ctx
        cf = jnp.where(in_ctx, nc_lat + i, i - nc_ctx)
        cb = jnp.where(in_ctx, nc_lat + nc_ctx - 1 - i, nc_lat - 1 - (i - nc_ctx))
        chunk(pl.multiple_of(cf * C, C), gf_s, stf_s, of_s, tri_f, causal_f, C - 1)
        chunk(pl.multiple_of(cb * C, C), gb_s, stb_s, ob_s, tri_b, causal_b, 0)
        return carry

    lax.fori_loop(0, nc_lat + nc_ctx, step, 0)

    u_s[0:CONV_PAD, :] = jnp.zeros((CONV_PAD, CONV_WIDTH), F32)
    u_s[CONV_PAD + T:, :] = jnp.zeros((CONV_PAD, CONV_WIDTH), F32)
    u_s[CONV_PAD:CONV_PAD + T, :] = cc_ref[0].astype(F32) * ch_ref[0].astype(F32)
    w0 = cw_ref[0:1, :]
    w1 = cw_ref[1:2, :]
    w2 = cw_ref[2:3, :]
    for e in range(T // EPI_ROWS):
        r0 = e * EPI_ROWS
        o = of_s[r0:r0 + EPI_ROWS, :] + ob_s[r0:r0 + EPI_ROWS, :]
        ss = jnp.dot((o * o).astype(BF16), bd_ref[...], preferred_element_type=F32) * (1.0 / GLA_DV)
        on = o * lax.rsqrt(ss + EPS) * ng_ref[...]
        r = gr_ref[0, r0:r0 + EPI_ROWS, :].astype(F32)
        o_ref[0, r0:r0 + EPI_ROWS, 0:GLA_WIDTH] = (on * _silu(r)).astype(BF16)
        t = r0 + lax.broadcasted_iota(jnp.int32, (EPI_ROWS, 1), 0)
        up = u_s[CONV_PAD + r0 - 1:CONV_PAD + r0 - 1 + EPI_ROWS, :]
        mid = u_s[CONV_PAD + r0:CONV_PAD + r0 + EPI_ROWS, :]
        dn = u_s[CONV_PAD + r0 + 1:CONV_PAD + r0 + 1 + EPI_ROWS, :]
        up = jnp.where(t == n_lat, 0.0, up)
        dn = jnp.where(t == n_lat - 1, 0.0, dn)
        conv = w0 * up + w1 * mid + w2 * dn
        o_ref[0, r0:r0 + EPI_ROWS, GLA_WIDTH:] = (cb_ref[0, r0:r0 + EPI_ROWS, :].astype(F32) * conv).astype(BF16)


def _gla_conv(p, wgf, wgb, gbias, ng, cw, bd, n_lat):
    B, T, _ = p.shape

    def col(width, start):
        return pl.BlockSpec((1, T, width), lambda b: (b, 0, start // width))

    def const(shape):
        return pl.BlockSpec(shape, lambda b: (0,) * len(shape))

    return pl.pallas_call(
        functools.partial(_gla_body, n_lat=n_lat, n_ctx=T - n_lat),
        grid=(B,),
        in_specs=[col(GLA_QK_WIDTH, COL_GQ), col(GLA_QK_WIDTH, COL_GK), col(GLA_WIDTH, COL_GV), col(GLA_WIDTH, COL_GR),
                  col(CONV_WIDTH, COL_CB), col(CONV_WIDTH, COL_CC), col(CONV_WIDTH, COL_CH), col(LANES, COL_GT),
                  const((LANES, GLA_QK_WIDTH)), const((LANES, GLA_QK_WIDTH)), const((8, GLA_QK_WIDTH)),
                  const((1, GLA_WIDTH)), const((8, CONV_WIDTH)), const((GLA_WIDTH, GLA_WIDTH))],
        out_specs=pl.BlockSpec((1, T, GLA_WIDTH + CONV_WIDTH), lambda b: (b, 0, 0)),
        out_shape=jax.ShapeDtypeStruct((B, T, GLA_WIDTH + CONV_WIDTH), BF16),
        scratch_shapes=[pltpu.VMEM((T, GLA_QK_WIDTH), F32), pltpu.VMEM((T, GLA_QK_WIDTH), F32),
                        pltpu.VMEM((T, GLA_WIDTH), F32), pltpu.VMEM((T, GLA_WIDTH), F32),
                        pltpu.VMEM((GLA_WIDTH, GLA_QK_WIDTH), F32), pltpu.VMEM((GLA_WIDTH, GLA_QK_WIDTH), F32),
                        pltpu.VMEM((T + 2 * CONV_PAD, CONV_WIDTH), F32)],
        compiler_params=_cparams(1),
        name="gla_conv",
    )(p, p, p, p, p, p, p, p, wgf, wgb, gbias, ng, cw, bd)


def _route(logits_t):
    mx = jnp.max(logits_t, axis=0, keepdims=True)
    ex = jnp.exp(logits_t - mx)
    probs = ex / jnp.sum(ex, axis=0, keepdims=True)
    P = [probs[e:e + 1] for e in range(N_EXPERTS)]
    scores = []
    for g in range(N_GROUPS):
        a, b, c, d = P[4 * g:4 * g + 4]
        scores.append(jnp.maximum(jnp.maximum(jnp.maximum(a + b, a + c), jnp.maximum(a + d, b + c)),
                                  jnp.maximum(b + d, c + d)))
    best = jnp.maximum(jnp.maximum(scores[0], scores[1]), jnp.maximum(scores[2], scores[3]))
    taken = jnp.zeros_like(best, dtype=jnp.bool_)
    sel = []
    for g in range(N_GROUPS):
        s = (scores[g] == best) & jnp.logical_not(taken)
        sel.append(s)
        taken = taken | s
    gsel = jnp.where(sel[1], 1.0, 0.0) + jnp.where(sel[2], 2.0, 0.0) + jnp.where(sel[3], 3.0, 0.0)
    ig = [jnp.where(sel[0], P[j], jnp.where(sel[1], P[4 + j], jnp.where(sel[2], P[8 + j], P[12 + j])))
          for j in range(EXPERTS_PER_GROUP)]

    def first_max(vals):
        v = jnp.maximum(jnp.maximum(vals[0], vals[1]), jnp.maximum(vals[2], vals[3]))
        tk = jnp.zeros_like(v, dtype=jnp.bool_)
        hot = []
        for x in vals:
            s = (x == v) & jnp.logical_not(tk)
            hot.append(s)
            tk = tk | s
        idx = jnp.where(hot[1], 1.0, 0.0) + jnp.where(hot[2], 2.0, 0.0) + jnp.where(hot[3], 3.0, 0.0)
        return v, hot, idx

    _, hot1, i1 = first_max(ig)
    _, _, i2 = first_max([jnp.where(hot1[j], -1.0, ig[j]) for j in range(EXPERTS_PER_GROUP)])
    lo = jnp.minimum(i1, i2)
    hi = jnp.maximum(i1, i2)
    pair = jnp.where(lo == 0.0, hi - 1.0, jnp.where(lo == 1.0, hi + 1.0, 5.0))
    return gsel * N_PAIRS + pair


def _class_rank(cls, tri_ref, cnt_s):
    n = cls.shape[1]
    cid = lax.broadcasted_iota(jnp.int32, (CLS_ROWS, n), 0).astype(F32)
    onehot = jnp.where(cls == cid, 1.0, 0.0)
    segs = [onehot[:, k * LANES:(k + 1) * LANES] for k in range(n // LANES)]
    before = jnp.dot(jnp.concatenate(segs, axis=0).astype(BF16), tri_ref[...], preferred_element_type=F32)
    base = cnt_s[...]
    ranks = []
    for k, seg in enumerate(segs):
        ranks.append(jnp.sum(seg * (before[k * CLS_ROWS:(k + 1) * CLS_ROWS] + base), axis=0, keepdims=True))
        base = base + jnp.sum(seg, axis=1, keepdims=True)
    cnt_s[...] = base
    return jnp.concatenate(ranks, axis=1)


def _out_body(ya_ref, yg_ref, x_ref, mod_ref, wo_ref, g_ref, wr_ref, br_ref, tri_ref, xo_ref, rt_ref, cnt_ref, cnt_s,
              *, n_lat, tm):
    j = pl.program_id(1)

    @pl.when((pl.program_id(0) == 0) & (j == 0))
    def _():
        cnt_s[...] = jnp.zeros_like(cnt_s)

    row = j * tm + lax.broadcasted_iota(jnp.int32, (tm, 1), 0)
    is_ctx = row >= n_lat
    mod = mod_ref[0]
    y = (jnp.dot(ya_ref[0], wo_ref[0:ATT_WIDTH, :], preferred_element_type=F32)
         + jnp.dot(yg_ref[0], wo_ref[ATT_WIDTH:, :], preferred_element_type=F32))
    xn = x_ref[0] + _row_mod(mod, is_ctx, 2) * y
    xo_ref[0] = xn
    h2 = _norm_modulate(xn, g_ref[...], mod, is_ctx, 3, 4).astype(BF16)
    logits = jnp.dot(h2, wr_ref[...], preferred_element_type=F32) + br_ref[0:1, :]
    cls = _route(logits.T[0:N_EXPERTS, :])
    rank = _class_rank(cls, tri_ref, cnt_s)
    rt_ref[0] = jnp.concatenate([cls, rank, jnp.zeros((6, tm), F32)], axis=0).astype(jnp.int32)
    cnt_ref[...] = jnp.broadcast_to(cnt_s[...], (CLS_ROWS, LANES)).astype(jnp.int32)


def _out_proj(ya, yg, xx, modv, wo, gain, wr, br, tri, n_lat, rows, tm):
    B, T, _ = xx.shape
    nj = rows // tm
    return pl.pallas_call(
        functools.partial(_out_body, n_lat=n_lat, tm=tm),
        grid=(B, nj),
        in_specs=[pl.BlockSpec((1, tm, ATT_WIDTH), lambda b, j: (b, j, 0)),
                  pl.BlockSpec((1, tm, GLA_WIDTH + CONV_WIDTH), lambda b, j: (b, j, 0)),
                  pl.BlockSpec((1, tm, D_MODEL), lambda b, j: (b, j, 0)),
                  pl.BlockSpec((1, 16, D_MODEL), lambda b, j: (b, 0, 0)),
                  pl.BlockSpec((D_MODEL, D_MODEL), lambda b, j: (0, 0)),
                  pl.BlockSpec((1, D_MODEL), lambda b, j: (0, 0)),
                  pl.BlockSpec((D_MODEL, LANES), lambda b, j: (0, 0)),
                  pl.BlockSpec((8, LANES), lambda b, j: (0, 0)),
                  pl.BlockSpec((LANES, LANES), lambda b, j: (0, 0))],
        out_specs=[pl.BlockSpec((1, tm, D_MODEL), lambda b, j: (b, j, 0)),
                   pl.BlockSpec((1, 8, tm), lambda b, j: (b * nj + j, 0, 0)),
                   pl.BlockSpec((CLS_ROWS, LANES), lambda b, j: (0, 0))],
        out_shape=[jax.ShapeDtypeStruct((B, rows, D_MODEL), F32),
                   jax.ShapeDtypeStruct((B * nj, 8, tm), jnp.int32),
                   jax.ShapeDtypeStruct((CLS_ROWS, LANES), jnp.int32)],
        scratch_shapes=[pltpu.VMEM((CLS_ROWS, 1), F32)],
        compiler_params=_cparams(2),
        name="out_proj_router",
    )(ya, yg, xx, modv, wo, gain, wr, br, tri)


ROW_UNROLL = 8


def _issue_rows(tm, idx, slot, off_ref, make_copy):
    def trip(i, c):
        for u in range(ROW_UNROLL):
            r = i * ROW_UNROLL + u
            make_copy(r, off_ref[idx[slot, 0, r]] + idx[slot, 1, r]).start(priority=u % 2)
        return c

    lax.fori_loop(0, tm // ROW_UNROLL, trip, 0)


def _disp_body(off_ref, x_ref, mod_ref, g_ref, rt_ref, hs_in_ref, hs_ref, buf, idx, sem_i, sem_d, *, n_lat, tm, nj,
               n_steps):
    del hs_in_ref
    j = pl.program_id(1)
    s = pl.program_id(0) * nj + j
    last = n_steps - 1
    slot = s % 2

    def idx_fetch(step, sl):
        return pltpu.make_async_copy(rt_ref.at[step], idx.at[sl], sem_i.at[sl])

    def drain(sl):
        pltpu.make_async_copy(buf.at[sl], hs_ref.at[pl.ds(0, tm)], sem_d.at[sl]).wait()

    @pl.when(s == 0)
    def _():
        idx_fetch(s, slot).start()

    row = j * tm + lax.broadcasted_iota(jnp.int32, (tm, 1), 0)
    is_ctx = row >= n_lat
    h2 = _norm_modulate(x_ref[0], g_ref[...], mod_ref[0], is_ctx, 3, 4)
    buf[slot] = h2.reshape(tm, 8, LANES)
    idx_fetch(s, slot).wait()
    _issue_rows(tm, idx, slot, off_ref, lambda r, d: pltpu.make_async_copy(buf.at[slot, r], hs_ref.at[d],
                                                                           sem_d.at[slot]))

    @pl.when(s > 0)
    def _():
        drain(1 - slot)

    @pl.when(s < last)
    def _():
        idx_fetch(s + 1, 1 - slot).start()

    @pl.when(s == last)
    def _():
        drain(slot)


def _dispatch(off, xx, modv, gain, route, hs0, n_lat, tm):
    B, rows, _ = xx.shape
    nj = rows // tm
    return pl.pallas_call(
        functools.partial(_disp_body, n_lat=n_lat, tm=tm, nj=nj, n_steps=B * nj),
        grid_spec=pltpu.PrefetchScalarGridSpec(
            num_scalar_prefetch=1,
            grid=(B, nj),
            in_specs=[pl.BlockSpec((1, tm, D_MODEL), lambda b, j, o: (b, j, 0)),
                      pl.BlockSpec((1, 16, D_MODEL), lambda b, j, o: (b, 0, 0)),
                      pl.BlockSpec((1, D_MODEL), lambda b, j, o: (0, 0)),
                      pl.BlockSpec(memory_space=pl.ANY),
                      pl.BlockSpec(memory_space=pl.ANY)],
            out_specs=pl.BlockSpec(memory_space=pl.ANY),
            scratch_shapes=[pltpu.VMEM((2, tm, 8, LANES), F32), pltpu.SMEM((2, 8, tm), jnp.int32),
                            pltpu.SemaphoreType.DMA((2,)), pltpu.SemaphoreType.DMA((2,))]),
        out_shape=jax.ShapeDtypeStruct(hs0.shape, F32),
        input_output_aliases={5: 0},
        compiler_params=_cparams(2),
        name="dispatch",
    )(off, xx, modv, gain, route, hs0)


def _moe_body(tile_ref, e_lo_ref, e_hi_ref, valid_ref, hs_ref, wg1, wu1, wd1, wg2, wu2, wd2, wr_ref, br_ref, ys_ref):
    g = pl.program_id(0)
    tm = hs_ref.shape[0]

    @pl.when(valid_ref[g] == 1)
    def _():
        x = hs_ref[...].reshape(tm, D_MODEL)
        e_lo = e_lo_ref[g]
        e_hi = e_hi_ref[g]
        dw = wr_ref[pl.ds(e_lo, 1), :] - wr_ref[pl.ds(e_hi, 1), :]
        d = jnp.sum(x * dw, axis=-1, keepdims=True) + (br_ref[e_lo] - br_ref[e_hi])
        w_lo = jax.nn.sigmoid(d)
        w_hi = jax.nn.sigmoid(-d)
        h = x.astype(BF16)

        def act(wg, wu, w):
            a = _silu(jnp.dot(h, wg[0], preferred_element_type=F32)) * jnp.dot(h, wu[0], preferred_element_type=F32)
            return (a * w).astype(BF16)

        y = (jnp.dot(act(wg1, wu1, w_lo), wd1[0], preferred_element_type=F32)
             + jnp.dot(act(wg2, wu2, w_hi), wd2[0], preferred_element_type=F32))
        ys_ref[...] = y.reshape(tm, 8, LANES)

    @pl.when(valid_ref[g] == 0)
    def _():
        ys_ref[...] = jnp.zeros_like(ys_ref)


def _moe(hs, tile, e_lo, e_hi, valid, wg, wu, wd, wr_t, br):
    n_tiles = tile.shape[0]
    tm = MOE_TILE

    def w_in(sel):
        return pl.BlockSpec((1, D_MODEL, D_EXPERT), lambda g, t, lo, hi, v: ((lo, hi)[sel][g], 0, 0))

    def w_out(sel):
        return pl.BlockSpec((1, D_EXPERT, D_MODEL), lambda g, t, lo, hi, v: ((lo, hi)[sel][g], 0, 0))

    return pl.pallas_call(
        _moe_body,
        grid_spec=pltpu.PrefetchScalarGridSpec(
            num_scalar_prefetch=4,
            grid=(n_tiles,),
            in_specs=[pl.BlockSpec((tm, 8, LANES), lambda g, t, lo, hi, v: (t[g], 0, 0)),
                      w_in(0), w_in(0), w_out(0), w_in(1), w_in(1), w_out(1),
                      pl.BlockSpec((N_EXPERTS, D_MODEL), lambda g, t, lo, hi, v: (0, 0)),
                      pl.BlockSpec(memory_space=pltpu.SMEM)],
            out_specs=pl.BlockSpec((tm, 8, LANES), lambda g, t, lo, hi, v: (g, 0, 0))),
        out_shape=jax.ShapeDtypeStruct((n_tiles * tm, 8, LANES), F32),
        compiler_params=_cparams(1),
        name="moe_pairs",
    )(tile, e_lo, e_hi, valid, hs, wg, wu, wd, wg, wu, wd, wr_t, br)


def _fin_body(off_ref, x_ref, mod_ref, rt_ref, ys_ref, xo_ref, buf, idx, sem_i, sem_d, *, n_lat, tm, nj, n_steps):
    j = pl.program_id(1)
    s = pl.program_id(0) * nj + j
    last = n_steps - 1
    slot = s % 2

    def idx_fetch(step, sl):
        return pltpu.make_async_copy(rt_ref.at[step], idx.at[sl], sem_i.at[sl])

    def gather(sl):
        _issue_rows(tm, idx, sl, off_ref, lambda r, d: pltpu.make_async_copy(ys_ref.at[d], buf.at[sl, r],
                                                                             sem_d.at[sl]))

    @pl.when(s == 0)
    def _():
        idx_fetch(0, 0).start()
        idx_fetch(0, 0).wait()
        gather(0)
        if last > 0:
            idx_fetch(1, 1).start()

    @pl.when(s < last)
    def _():
        idx_fetch(s + 1, 1 - slot).wait()
        gather(1 - slot)

    @pl.when(s + 2 <= last)
    def _():
        idx_fetch(s + 2, slot).start()

    pltpu.make_async_copy(ys_ref.at[pl.ds(0, tm)], buf.at[slot], sem_d.at[slot]).wait()
    row = j * tm + lax.broadcasted_iota(jnp.int32, (tm, 1), 0)
    is_ctx = row >= n_lat
    xo_ref[0] = x_ref[0] + _row_mod(mod_ref[0], is_ctx, 5) * buf[slot].reshape(tm, D_MODEL)


def _combine(off, xx, modv, route, ys, n_lat, tm):
    B, rows, _ = xx.shape
    nj = rows // tm
    return pl.pallas_call(
        functools.partial(_fin_body, n_lat=n_lat, tm=tm, nj=nj, n_steps=B * nj),
        grid_spec=pltpu.PrefetchScalarGridSpec(
            num_scalar_prefetch=1,
            grid=(B, nj),
            in_specs=[pl.BlockSpec((1, tm, D_MODEL), lambda b, j, o: (b, j, 0)),
                      pl.BlockSpec((1, 16, D_MODEL), lambda b, j, o: (b, 0, 0)),
                      pl.BlockSpec(memory_space=pl.ANY),
                      pl.BlockSpec(memory_space=pl.ANY)],
            out_specs=pl.BlockSpec((1, tm, D_MODEL), lambda b, j, o: (b, j, 0)),
            scratch_shapes=[pltpu.VMEM((2, tm, 8, LANES), F32), pltpu.SMEM((2, 8, tm), jnp.int32),
                            pltpu.SemaphoreType.DMA((2,)), pltpu.SemaphoreType.DMA((2,))]),
        out_shape=jax.ShapeDtypeStruct((B, rows, D_MODEL), F32),
        compiler_params=_cparams(2),
        name="combine",
    )(off, xx, modv, route, ys)


def _rope_tables(n_lat, n_ctx):
    rows = n_lat // GRID_W
    row, col = jnp.meshgrid(jnp.arange(rows), jnp.arange(GRID_W), indexing="ij")
    n_freq = HEAD_DIM // 4
    inv_freq = ROPE_BASE ** (-jnp.arange(n_freq, dtype=F32) / n_freq)
    ang = jnp.concatenate([row.reshape(-1, 1).astype(F32) * inv_freq, col.reshape(-1, 1).astype(F32) * inv_freq],
                          axis=-1)
    cos = jnp.tile(jnp.cos(ang), (1, LANES // (HEAD_DIM // 2)))
    sin = jnp.tile(jnp.sin(ang), (1, LANES // (HEAD_DIM // 2)))
    sign = jnp.where((jnp.arange(LANES) % HEAD_DIM) < HEAD_DIM // 2, -1.0, 1.0).astype(F32)
    cos = jnp.concatenate([cos, jnp.ones((n_ctx, LANES), F32)], axis=0)
    sin = jnp.concatenate([sin * sign, jnp.zeros((n_ctx, LANES), F32)], axis=0)
    return cos, sin


def _block_diag_ones(n, blk):
    i = np.arange(n) // blk
    return jnp.asarray((i[:, None] == i[None, :]).astype(np.float32), dtype=BF16)


def _routing_tables(counts, n_tiles):
    tiles_c = (counts + MOE_TILE - 1) // MOE_TILE
    tile_end = jnp.cumsum(tiles_c)
    tile_start = tile_end - tiles_c
    off = tile_start * MOE_TILE
    total = tile_end[-1]
    g = jnp.arange(n_tiles, dtype=jnp.int32)
    valid = (g < total).astype(jnp.int32)
    g_eff = jnp.minimum(g, total - 1)
    c_of = jnp.sum((g_eff[:, None] >= tile_end[None, :]).astype(jnp.int32), axis=1)
    group = c_of // N_PAIRS
    pair = c_of % N_PAIRS
    lo = jnp.asarray(PAIR_LO, jnp.int32)
    hi = jnp.asarray(PAIR_HI, jnp.int32)
    e_lo = group * EXPERTS_PER_GROUP + jnp.sum((pair[:, None] == jnp.arange(N_PAIRS)[None, :]) * lo[None, :], axis=1)
    e_hi = group * EXPERTS_PER_GROUP + jnp.sum((pair[:, None] == jnp.arange(N_PAIRS)[None, :]) * hi[None, :], axis=1)
    return off.astype(jnp.int32), g_eff, e_lo.astype(jnp.int32), e_hi.astype(jnp.int32), valid


def kernel(x, c, ctx, c_ctx, w_ada, b_ada, norm_mix_g, norm_ffn_g, w_in, q_norm_g, k_norm_g, attn_sink, gla_gate_w,
           gla_gate_b, gla_norm_g, conv_w, w_out, w_router, b_router, w_gate_e, w_up_e, w_down_e):
    B, S, D = x.shape
    L = ctx.shape[1]
    T = S + L
    assert D == D_MODEL and T % TOKEN_TILE == 0 and S % LAT_TILE == 0 and S % GRID_W == 0
    assert S % ATT_BLOCK == 0 and L % ATT_BLOCK == 0 and S >= ATT_SPAN and T % EPI_ROWS == 0

    cond_rows = -(-(B + 1) // 8) * 8
    cond = jnp.zeros((cond_rows, D), F32).at[:B].set(c).at[B].set(c_ctx)
    mod_all = _modulation(cond, w_ada, b_ada)

    cos, sin = _rope_tables(S, L)
    bd_head = _block_diag_ones(LANES, HEAD_DIM)
    bd_gla = _block_diag_ones(GLA_WIDTH, GLA_DV)
    tri = jnp.asarray(np.triu(np.ones((LANES, LANES), np.float32), 1), dtype=BF16)
    xx = jnp.concatenate([x, ctx], axis=1)

    for l in range(DEPTH):
        last = l == DEPTH - 1
        m_lat = mod_all[l, :B].reshape(B, 6, D)
        m_ctx = jnp.broadcast_to(mod_all[l, B].reshape(1, 6, D), (B, 6, D))
        modv = jnp.concatenate([m_lat, m_ctx, jnp.zeros((B, 4, D), F32)], axis=1)

        wl = w_in[l]
        w_perm = jnp.concatenate([wl[:, :1536], wl[:, 1568:], wl[:, 1536:1568],
                                  jnp.zeros((D, N_PROJ - wl.shape[1]), F32)], axis=1).astype(BF16)
        qg = jnp.tile(q_norm_g[l], LANES // HEAD_DIM) * (HEAD_DIM ** -0.5)
        kg = jnp.tile(k_norm_g[l], LANES // HEAD_DIM)
        qkg = jnp.stack([qg] * (ATT_WIDTH // LANES) + [kg] + [jnp.zeros_like(kg)] * 3)
        p = _in_proj(xx, modv, norm_mix_g[l].reshape(1, D), w_perm, cos, sin, qkg, bd_head, S)

        y_att = _attention(p, attn_sink[l], S, not last)

        pad_rows = jnp.zeros((LANES - 2 * GLA_GATE_RANK, GLA_QK_WIDTH), F32)
        zero_rank = jnp.zeros((GLA_GATE_RANK, GLA_QK_WIDTH), F32)
        wgf = jnp.concatenate([gla_gate_w[l, 0], zero_rank, pad_rows], axis=0).astype(BF16)
        wgb = jnp.concatenate([zero_rank, gla_gate_w[l, 1], pad_rows], axis=0).astype(BF16)
        gbias = jnp.concatenate([gla_gate_b[l], jnp.zeros((6, GLA_QK_WIDTH), F32)], axis=0)
        ng = jnp.tile(gla_norm_g[l], GLA_HEADS).reshape(1, GLA_WIDTH)
        cw = jnp.concatenate([conv_w[l], jnp.zeros((5, CONV_WIDTH), F32)], axis=0)
        y_gc = _gla_conv(p, wgf, wgb, gbias, ng, cw, bd_gla, S)

        rows, tm = (S, LAT_TILE) if last else (T, TOKEN_TILE)
        wr = jnp.concatenate([w_router, jnp.zeros((D, LANES - N_EXPERTS), F32)], axis=1).astype(BF16)
        br = jnp.zeros((8, LANES), F32).at[0, :N_EXPERTS].set(b_router)
        ffn_g = norm_ffn_g[l].reshape(1, D)
        xx_mid, route, counts = _out_proj(y_att, y_gc, xx, modv, w_out[l].astype(BF16), ffn_g, wr, br, tri, S, rows,
                                          tm)

        n_tiles = -(-(B * rows) // MOE_TILE) + N_CLASSES
        off, tile, e_lo, e_hi, valid = _routing_tables(counts[:N_CLASSES, 0], n_tiles)

        hs = _dispatch(off, xx_mid, modv, ffn_g, route, jnp.zeros((n_tiles * MOE_TILE, 8, LANES), F32), S, tm)
        ys = _moe(hs, tile, e_lo, e_hi, valid, w_gate_e[l].astype(BF16), w_up_e[l].astype(BF16),
                  w_down_e[l].astype(BF16), w_router.T, b_router)
        xx = _combine(off, xx_mid, modv, route, ys, S, tm)
    return xx
```

```python
import functools

import numpy as np
import jax
import jax.numpy as jnp
from jax import lax
from jax.experimental import pallas as pl
from jax.experimental.pallas import tpu as pltpu

D_MODEL = 1024
DEPTH = 2
GRID_W = 64
EPS = 1e-6
HEAD_DIM = 64
ATT_HEADS = 8
ATT_KV_HEADS = 2
ATT_GROUP = ATT_HEADS // ATT_KV_HEADS
ATT_WIDTH = ATT_HEADS * HEAD_DIM
WINDOW = 128
ROPE_BASE = 10000.0
GLA_HEADS = 4
GLA_DV = 64
GLA_DK = 32
GLA_WIDTH = GLA_HEADS * GLA_DV
GLA_GATE_RANK = 16
GLA_GATE_NORM = 16.0
GLA_CHUNK = 64
CONV_WIDTH = 256
N_EXPERTS = 16
N_GROUPS = 4
EXPERTS_PER_GROUP = 4
D_EXPERT = D_MODEL // 2

LANES = 128
KV_WIDTH = ATT_KV_HEADS * HEAD_DIM
GLA_QK_WIDTH = GLA_HEADS * GLA_DK
COL_AQ, COL_AK, COL_AV = 0, 512, 640
COL_GQ, COL_GK, COL_GV, COL_GR = 768, 896, 1024, 1280
COL_CB, COL_CC, COL_CH, COL_GT = 1536, 1792, 2048, 2304
N_PROJ = 2432
QK_COLS = COL_AV
N_PAIRS = 6
N_CLASSES = N_GROUPS * N_PAIRS
PAIR_LO = (0, 0, 0, 1, 1, 2)
PAIR_HI = (1, 2, 3, 2, 3, 3)
CLS_ROWS = 32
NEG = -1e30

TOKEN_TILE = 768
LAT_TILE = 1024
MOE_TILE = 512
VMEM_LIMIT = 56 * 1024 * 1024

F32 = jnp.float32
BF16 = jnp.bfloat16


def _cparams(n_axes):
    return pltpu.CompilerParams(dimension_semantics=("arbitrary",) * n_axes, vmem_limit_bytes=VMEM_LIMIT)


def _silu(x):
    return x * jax.nn.sigmoid(x)


def _mod_body(c_ref, w_ref, b_ref, o_ref):
    c = c_ref[...]
    a = _silu(c).astype(BF16)
    o_ref[0] = jnp.dot(a, w_ref[0].astype(BF16), preferred_element_type=F32) + b_ref[0]


def _modulation(cond, w_ada, b_ada):
    rows = cond.shape[0]
    nblk = w_ada.shape[2] // D_MODEL
    return pl.pallas_call(
        _mod_body,
        grid=(DEPTH, nblk),
        in_specs=[pl.BlockSpec((rows, D_MODEL), lambda l, n: (0, 0)),
                  pl.BlockSpec((1, D_MODEL, D_MODEL), lambda l, n: (l, 0, n)),
                  pl.BlockSpec((1, 1, D_MODEL), lambda l, n: (l, 0, n))],
        out_specs=pl.BlockSpec((1, rows, D_MODEL), lambda l, n: (l, 0, n)),
        out_shape=jax.ShapeDtypeStruct((DEPTH, rows, w_ada.shape[2]), F32),
        compiler_params=_cparams(2),
        name="modulation",
    )(cond, w_ada, b_ada.reshape(DEPTH, 1, -1))


def _row_mod(mod, is_ctx, i):
    return jnp.where(is_ctx, mod[6 + i:7 + i], mod[i:i + 1])


def _norm_modulate(x, gain, mod, is_ctx, i_shift, i_scale):
    ms = jnp.mean(x * x, axis=-1, keepdims=True)
    xn = x * lax.rsqrt(ms + EPS) * gain
    return xn * (1.0 + _row_mod(mod, is_ctx, i_scale)) + _row_mod(mod, is_ctx, i_shift)


def _in_body(x_ref, mod_ref, g_ref, w_ref, cos_ref, sin_ref, qkg_ref, bd_ref, o_ref, *, n_lat, tm):
    j = pl.program_id(0)
    row = j * tm + lax.broadcasted_iota(jnp.int32, (tm, 1), 0)
    is_ctx = row >= n_lat
    h = _norm_modulate(x_ref[0], g_ref[...], mod_ref[0], is_ctx, 0, 1).astype(BF16)
    qk = jnp.dot(h, w_ref[:, :QK_COLS], preferred_element_type=F32)
    cos = cos_ref[...]
    sin = sin_ref[...]
    lane = lax.broadcasted_iota(jnp.int32, (1, LANES), 1)
    first_half = (lane % HEAD_DIM) < (HEAD_DIM // 2)
    for c in range(QK_COLS // LANES):
        xc = qk[:, c * LANES:(c + 1) * LANES]
        ss = jnp.dot((xc * xc).astype(BF16), bd_ref[...], preferred_element_type=F32) * (1.0 / HEAD_DIM)
        xc = xc * lax.rsqrt(ss + EPS) * qkg_ref[c:c + 1, :]
        rot = jnp.where(first_half, pltpu.roll(xc, LANES - HEAD_DIM // 2, 1), pltpu.roll(xc, HEAD_DIM // 2, 1))
        o_ref[0, :, c * LANES:(c + 1) * LANES] = (xc * cos + rot * sin).astype(BF16)
    o_ref[0, :, QK_COLS:] = jnp.dot(h, w_ref[:, QK_COLS:], preferred_element_type=F32).astype(BF16)


def _in_proj(xx, modv, gain, w, cos, sin, qkg, bd, n_lat):
    B, T, _ = xx.shape
    tm = TOKEN_TILE
    return pl.pallas_call(
        functools.partial(_in_body, n_lat=n_lat, tm=tm),
        grid=(T // tm, B),
        in_specs=[pl.BlockSpec((1, tm, D_MODEL), lambda j, b: (b, j, 0)),
                  pl.BlockSpec((1, 16, D_MODEL), lambda j, b: (b, 0, 0)),
                  pl.BlockSpec((1, D_MODEL), lambda j, b: (0, 0)),
                  pl.BlockSpec((D_MODEL, N_PROJ), lambda j, b: (0, 0)),
                  pl.BlockSpec((tm, LANES), lambda j, b: (j, 0)),
                  pl.BlockSpec((tm, LANES), lambda j, b: (j, 0)),
                  pl.BlockSpec((8, LANES), lambda j, b: (0, 0)),
                  pl.BlockSpec((LANES, LANES), lambda j, b: (0, 0))],
        out_specs=pl.BlockSpec((1, tm, N_PROJ), lambda j, b: (b, j, 0)),
        out_shape=jax.ShapeDtypeStruct((B, T, N_PROJ), BF16),
        compiler_params=_cparams(2),
        name="in_proj",
    )(xx, modv, gain, w, cos, sin, qkg, bd)


ATT_BLOCK = 128
ATT_SPAN = ATT_BLOCK + 2 * WINDOW


ATT_HEAD_ORDER = (0, 4, 1, 5, 2, 6, 3, 7)


def _attend(qblk, k_parts, v_parts, biases, sinks):
    rows = ATT_GROUP * ATT_BLOCK
    rowi = lax.broadcasted_iota(jnp.int32, (rows, 1), 0)
    lane = lax.broadcasted_iota(jnp.int32, (1, LANES), 1)
    lower = lane < HEAD_DIM
    nt = (((1,), (1,)), ((), ()))
    normed = []
    for h in range(ATT_KV_HEADS):
        keep = lower if h == 0 else jnp.logical_not(lower)
        qs = jnp.concatenate([jnp.where(keep, qblk[:, g * LANES:(g + 1) * LANES], jnp.zeros((), BF16))
                              for g in range(ATT_GROUP)], axis=0)
        sink = jnp.full((rows, 1), sinks[ATT_GROUP * h + ATT_GROUP - 1], F32)
        for g in range(ATT_GROUP - 2, -1, -1):
            sink = jnp.where(rowi < (g + 1) * ATT_BLOCK, sinks[ATT_GROUP * h + g], sink)
        scores = []
        for k in k_parts:
            s = lax.dot_general(qs, k, nt, preferred_element_type=F32)
            scores += [s[:, c * LANES:(c + 1) * LANES] for c in range(k.shape[0] // LANES)]
        scores = [s if b is None else s + b for s, b in zip(scores, biases)]
        m = scores[0]
        for s in scores[1:]:
            m = jnp.maximum(m, s)
        m = jnp.maximum(jnp.max(m, axis=-1, keepdims=True), sink)
        probs = [jnp.exp((s - m).astype(BF16)) for s in scores]
        acc = jnp.where(keep, 0.0, jnp.exp(sink - m))
        c0 = 0
        for v in v_parts[h]:
            n = v.shape[0] // LANES
            acc = acc + jnp.dot(jnp.concatenate(probs[c0:c0 + n], axis=1), v, preferred_element_type=F32)
            c0 += n
        normed.append(acc / pltpu.roll(acc, HEAD_DIM, 1))
    return jnp.concatenate([jnp.where(lower, normed[0][g * ATT_BLOCK:(g + 1) * ATT_BLOCK],
                                      normed[1][g * ATT_BLOCK:(g + 1) * ATT_BLOCK]) for g in range(ATT_GROUP)], axis=1)


def _att_body(sink_ref, q_ref, k_ref, v_ref, o_ref, v1_s, *, n_lat, n_ctx, with_ctx_out):
    sinks = [sink_ref[i] for i in range(ATT_HEADS)]
    lane = lax.broadcasted_iota(jnp.int32, (1, LANES), 1)
    vv = v_ref[0]
    v1_s[0] = jnp.where(lane < HEAD_DIM, vv, jnp.ones((), BF16))
    v1_s[1] = jnp.where(lane < HEAD_DIM, jnp.ones((), BF16), vv)
    k_ctx = k_ref[0, n_lat:n_lat + n_ctx, :]
    v_ctx = [v1_s[h, n_lat:n_lat + n_ctx, :] for h in range(ATT_KV_HEADS)]
    no_bias = [None] * (n_ctx // LANES)
    qi = lax.broadcasted_iota(jnp.int32, (ATT_GROUP * ATT_BLOCK, LANES), 0) % ATT_BLOCK
    ki = lax.broadcasted_iota(jnp.int32, (ATT_GROUP * ATT_BLOCK, LANES), 1)
    past_ok = jnp.where(ki >= qi, 0.0, NEG)
    ahead_ok = jnp.where(ki <= qi, 0.0, NEG)

    def block(q0, k0, n_keys, biases):
        k_parts = [k_ref[0, pl.ds(k0, n_keys), :], k_ctx]
        v_parts = [[v1_s[h, pl.ds(k0, n_keys), :], v_ctx[h]] for h in range(ATT_KV_HEADS)]
        out = _attend(q_ref[0, pl.ds(q0, ATT_BLOCK), :], k_parts, v_parts, biases + no_bias, sinks)
        o_ref[0, pl.ds(q0, ATT_BLOCK), :] = out.astype(BF16)

    def interior(i, carry):
        q0 = pl.multiple_of(i * ATT_BLOCK, ATT_BLOCK)
        block(q0, pl.multiple_of(q0 - WINDOW, ATT_BLOCK), ATT_SPAN, [past_ok, None, ahead_ok])
        return carry

    nq = n_lat // ATT_BLOCK
    block(0, 0, 2 * ATT_BLOCK, [None, ahead_ok])
    lax.fori_loop(1, nq - 1, interior, 0)
    block(n_lat - ATT_BLOCK, n_lat - 2 * ATT_BLOCK, 2 * ATT_BLOCK, [past_ok, None])
    if with_ctx_out:
        for c in range(n_ctx // ATT_BLOCK):
            r0 = n_lat + c * ATT_BLOCK
            out = _attend(q_ref[0, r0:r0 + ATT_BLOCK, :], [k_ctx], [[v_ctx[h]] for h in range(ATT_KV_HEADS)],
                          no_bias, sinks)
            o_ref[0, r0:r0 + ATT_BLOCK, :] = out.astype(BF16)
    else:
        o_ref[0, n_lat:, :] = jnp.zeros((n_ctx, ATT_WIDTH), BF16)


def _attention(p, sinks, n_lat, with_ctx_out):
    B, T, _ = p.shape
    return pl.pallas_call(
        functools.partial(_att_body, n_lat=n_lat, n_ctx=T - n_lat, with_ctx_out=with_ctx_out),
        grid=(B,),
        in_specs=[pl.BlockSpec(memory_space=pltpu.SMEM),
                  pl.BlockSpec((1, T, ATT_WIDTH), lambda b: (b, 0, COL_AQ // ATT_WIDTH)),
                  pl.BlockSpec((1, T, KV_WIDTH), lambda b: (b, 0, COL_AK // KV_WIDTH)),
                  pl.BlockSpec((1, T, KV_WIDTH), lambda b: (b, 0, COL_AV // KV_WIDTH))],
        out_specs=pl.BlockSpec((1, T, ATT_WIDTH), lambda b: (b, 0, 0)),
        out_shape=jax.ShapeDtypeStruct((B, T, ATT_WIDTH), BF16),
        scratch_shapes=[pltpu.VMEM((ATT_KV_HEADS, T, KV_WIDTH), BF16)],
        compiler_params=_cparams(1),
        name="attention",
    )(sinks, p, p, p)


GLA_BLOCK_A = 256
GLA_BLOCK_B = 128
CONV_PAD = 8
EPI_ROWS = 256


def _log_sigmoid(z):
    return jnp.minimum(z, 0.0) - jnp.log1p(jnp.exp(-jnp.abs(z)))


def _gla_body(gq_ref, gk_ref, gv_ref, gr_ref, cb_ref, cc_ref, ch_ref, gt_ref, wgf_ref, wgb_ref, gbias_ref, ng_ref,
              cw_ref, bd_ref, cum_ref, o_ref, qe_s, ke_s, dec_s, upd_s, prev_s, st_s, o_s, u_s, *, n_lat, n_ctx):
    T = n_lat + n_ctx
    C = GLA_CHUNK
    nt = (((1,), (1,)), ((), ()))
    tn = (((0,), (0,)), ((), ()))

    sr = lax.broadcasted_iota(jnp.int32, (GLA_WIDTH, GLA_QK_WIDTH), 0) // GLA_DV
    sl = lax.broadcasted_iota(jnp.int32, (GLA_WIDTH, GLA_QK_WIDTH), 1) // GLA_DK
    state_mask = sr == sl

    def factors(i, carry):
        r0 = pl.multiple_of(i * GLA_BLOCK_A, GLA_BLOCK_A)
        rows = pl.ds(r0, GLA_BLOCK_A)
        gt = gt_ref[0, rows, :]
        q = gq_ref[0, rows, :].astype(F32) * (GLA_DK ** -0.5)
        k = gk_ref[0, rows, :].astype(F32)
        v = gv_ref[0, rows, :]
        for d, wg_ref in enumerate((wgf_ref, wgb_ref)):
            z = jnp.dot(gt, wg_ref[...], preferred_element_type=F32) + gbias_ref[d:d + 1, :]
            g = _log_sigmoid(z) * (1.0 / GLA_GATE_NORM)
            g_hi = g.astype(BF16)
            g_lo = (g - g_hi.astype(F32)).astype(BF16)
            cum = jnp.dot(cum_ref[d], jnp.concatenate([g_hi, g_lo], axis=1), preferred_element_type=F32)
            b = cum[:GLA_BLOCK_A, :GLA_QK_WIDTH] + cum[:GLA_BLOCK_A, GLA_QK_WIDTH:]
            tot = cum[GLA_BLOCK_A:, :GLA_QK_WIDTH] + cum[GLA_BLOCK_A:, GLA_QK_WIDTH:]
            qe_s[d, rows, :] = (q * jnp.exp(b)).astype(BF16)
            ke_s[d, rows, :] = (k * jnp.exp(-b)).astype(BF16)
            dec_s[d, rows, :] = jnp.exp(tot)
            kl = (k * jnp.exp(tot - b)).astype(BF16)
            for cc in range(GLA_BLOCK_A // C):
                upd = lax.dot_general(v[cc * C:(cc + 1) * C], kl[cc * C:(cc + 1) * C], tn, preferred_element_type=F32)
                upd_s[d, i * (GLA_BLOCK_A // C) + cc] = jnp.where(state_mask, upd, 0.0)
        return carry

    lax.fori_loop(0, T // GLA_BLOCK_A, factors, 0)

    nc_lat = n_lat // C
    nc_ctx = n_ctx // C
    st_s[...] = jnp.zeros_like(st_s)

    def scan(i, carry):
        in_ctx = i < nc_ctx
        cf = jnp.where(in_ctx, nc_lat + i, i - nc_ctx)
        cb = jnp.where(in_ctx, nc_lat + nc_ctx - 1 - i, nc_lat - 1 - (i - nc_ctx))
        for d, cid in enumerate((cf, cb)):
            st = st_s[d]
            prev_s[d, cid] = st.astype(BF16)
            st_s[d] = st * dec_s[d, pl.ds(pl.multiple_of(cid * C, C), 1), :] + upd_s[d, cid]
        return carry

    lax.fori_loop(0, nc_lat + nc_ctx, scan, 0)

    RB = GLA_BLOCK_B
    hr = lax.broadcasted_iota(jnp.int32, (GLA_HEADS * RB, GLA_QK_WIDTH), 0) // RB
    hl = lax.broadcasted_iota(jnp.int32, (GLA_HEADS * RB, GLA_QK_WIDTH), 1) // GLA_DK
    head_rows = hr == hl
    qr = lax.broadcasted_iota(jnp.int32, (GLA_HEADS * RB, RB), 0) % RB
    kc = lax.broadcasted_iota(jnp.int32, (GLA_HEADS * RB, RB), 1)
    same_chunk = (qr // C) == (kc // C)
    visible = (same_chunk & (kc <= qr), same_chunk & (kc > qr))
    ol = lax.broadcasted_iota(jnp.int32, (RB, GLA_WIDTH), 1) // GLA_DV

    def outputs(i, carry):
        r0 = pl.multiple_of(i * RB, RB)
        rows = pl.ds(r0, RB)
        v = gv_ref[0, rows, :]
        total = jnp.zeros((RB, GLA_WIDTH), F32)
        for d in range(2):
            qe = qe_s[d, rows, :]
            qe4 = jnp.where(head_rows, jnp.concatenate([qe] * GLA_HEADS, axis=0), jnp.zeros((), BF16))
            att = lax.dot_general(qe4, ke_s[d, rows, :], nt, preferred_element_type=F32)
            att = jnp.where(visible[d], att, 0.0).astype(BF16)
            full = jnp.dot(att, v, preferred_element_type=F32)
            o = full[(GLA_HEADS - 1) * RB:]
            for h in range(GLA_HEADS - 2, -1, -1):
                o = jnp.where(ol == h, full[h * RB:(h + 1) * RB], o)
            inter = [lax.dot_general(qe[cc * C:(cc + 1) * C], prev_s[d, i * (RB // C) + cc], nt,
                                     preferred_element_type=F32) for cc in range(RB // C)]
            total = total + o + jnp.concatenate(inter, axis=0)
        o_s[rows, :] = total
        return carry

    lax.fori_loop(0, T // RB, outputs, 0, unroll=2)

    u_s[0:CONV_PAD, :] = jnp.zeros((CONV_PAD, CONV_WIDTH), F32)
    u_s[CONV_PAD + T:, :] = jnp.zeros((CONV_PAD, CONV_WIDTH), F32)
    u_s[CONV_PAD:CONV_PAD + T, :] = cc_ref[0].astype(F32) * ch_ref[0].astype(F32)
    w0 = cw_ref[0:1, :]
    w1 = cw_ref[1:2, :]
    w2 = cw_ref[2:3, :]
    for e in range(T // EPI_ROWS):
        r0 = e * EPI_ROWS
        o = o_s[r0:r0 + EPI_ROWS, :]
        ss = jnp.dot((o * o).astype(BF16), bd_ref[...], preferred_element_type=F32) * (1.0 / GLA_DV)
        on = o * lax.rsqrt(ss + EPS) * ng_ref[...]
        r = gr_ref[0, r0:r0 + EPI_ROWS, :].astype(F32)
        o_ref[0, r0:r0 + EPI_ROWS, 0:GLA_WIDTH] = (on * _silu(r)).astype(BF16)
        t = r0 + lax.broadcasted_iota(jnp.int32, (EPI_ROWS, 1), 0)
        up = u_s[CONV_PAD + r0 - 1:CONV_PAD + r0 - 1 + EPI_ROWS, :]
        mid = u_s[CONV_PAD + r0:CONV_PAD + r0 + EPI_ROWS, :]
        dn = u_s[CONV_PAD + r0 + 1:CONV_PAD + r0 + 1 + EPI_ROWS, :]
        up = jnp.where(t == n_lat, 0.0, up)
        dn = jnp.where(t == n_lat - 1, 0.0, dn)
        conv = w0 * up + w1 * mid + w2 * dn
        o_ref[0, r0:r0 + EPI_ROWS, GLA_WIDTH:] = (cb_ref[0, r0:r0 + EPI_ROWS, :].astype(F32) * conv).astype(BF16)


def _gla_cum_matrices():
    i = np.arange(GLA_BLOCK_A)
    same = (i[:, None] // GLA_CHUNK) == (i[None, :] // GLA_CHUNK)
    fwd = same & (i[None, :] <= i[:, None])
    bwd = same & (i[None, :] >= i[:, None])
    mats = np.stack([np.concatenate([fwd, same], axis=0), np.concatenate([bwd, same], axis=0)])
    return jnp.asarray(mats.astype(np.float32), dtype=BF16)


def _gla_conv(p, wgf, wgb, gbias, ng, cw, bd, cum, n_lat):
    B, T, _ = p.shape
    nc = T // GLA_CHUNK

    def col(width, start):
        return pl.BlockSpec((1, T, width), lambda b: (b, 0, start // width))

    def const(shape):
        return pl.BlockSpec(shape, lambda b: (0,) * len(shape))

    return pl.pallas_call(
        functools.partial(_gla_body, n_lat=n_lat, n_ctx=T - n_lat),
        grid=(B,),
        in_specs=[col(GLA_QK_WIDTH, COL_GQ), col(GLA_QK_WIDTH, COL_GK), col(GLA_WIDTH, COL_GV), col(GLA_WIDTH, COL_GR),
                  col(CONV_WIDTH, COL_CB), col(CONV_WIDTH, COL_CC), col(CONV_WIDTH, COL_CH), col(LANES, COL_GT),
                  const((LANES, GLA_QK_WIDTH)), const((LANES, GLA_QK_WIDTH)), const((8, GLA_QK_WIDTH)),
                  const((1, GLA_WIDTH)), const((8, CONV_WIDTH)), const((GLA_WIDTH, GLA_WIDTH)),
                  const((2, 2 * GLA_BLOCK_A, GLA_BLOCK_A))],
        out_specs=pl.BlockSpec((1, T, GLA_WIDTH + CONV_WIDTH), lambda b: (b, 0, 0)),
        out_shape=jax.ShapeDtypeStruct((B, T, GLA_WIDTH + CONV_WIDTH), BF16),
        scratch_shapes=[pltpu.VMEM((2, T, GLA_QK_WIDTH), BF16), pltpu.VMEM((2, T, GLA_QK_WIDTH), BF16),
                        pltpu.VMEM((2, T, GLA_QK_WIDTH), F32),
                        pltpu.VMEM((2, nc, GLA_WIDTH, GLA_QK_WIDTH), F32),
                        pltpu.VMEM((2, nc, GLA_WIDTH, GLA_QK_WIDTH), BF16),
                        pltpu.VMEM((2, GLA_WIDTH, GLA_QK_WIDTH), F32),
                        pltpu.VMEM((T, GLA_WIDTH), F32),
                        pltpu.VMEM((T + 2 * CONV_PAD, CONV_WIDTH), F32)],
        compiler_params=_cparams(1),
        name="gla_conv",
    )(p, p, p, p, p, p, p, p, wgf, wgb, gbias, ng, cw, bd, cum)


def _route(logits_t):
    mx = jnp.max(logits_t, axis=0, keepdims=True)
    ex = jnp.exp(logits_t - mx)
    probs = ex / jnp.sum(ex, axis=0, keepdims=True)
    P = [probs[e:e + 1] for e in range(N_EXPERTS)]
    scores = []
    for g in range(N_GROUPS):
        a, b, c, d = P[4 * g:4 * g + 4]
        scores.append(jnp.maximum(jnp.maximum(jnp.maximum(a + b, a + c), jnp.maximum(a + d, b + c)),
                                  jnp.maximum(b + d, c + d)))
    best = jnp.maximum(jnp.maximum(scores[0], scores[1]), jnp.maximum(scores[2], scores[3]))
    taken = jnp.zeros_like(best, dtype=jnp.bool_)
    sel = []
    for g in range(N_GROUPS):
        s = (scores[g] == best) & jnp.logical_not(taken)
        sel.append(s)
        taken = taken | s
    gsel = jnp.where(sel[1], 1.0, 0.0) + jnp.where(sel[2], 2.0, 0.0) + jnp.where(sel[3], 3.0, 0.0)
    ig = [jnp.where(sel[0], P[j], jnp.where(sel[1], P[4 + j], jnp.where(sel[2], P[8 + j], P[12 + j])))
          for j in range(EXPERTS_PER_GROUP)]

    def first_max(vals):
        v = jnp.maximum(jnp.maximum(vals[0], vals[1]), jnp.maximum(vals[2], vals[3]))
        tk = jnp.zeros_like(v, dtype=jnp.bool_)
        hot = []
        for x in vals:
            s = (x == v) & jnp.logical_not(tk)
            hot.append(s)
            tk = tk | s
        idx = jnp.where(hot[1], 1.0, 0.0) + jnp.where(hot[2], 2.0, 0.0) + jnp.where(hot[3], 3.0, 0.0)
        return v, hot, idx

    _, hot1, i1 = first_max(ig)
    _, _, i2 = first_max([jnp.where(hot1[j], -1.0, ig[j]) for j in range(EXPERTS_PER_GROUP)])
    lo = jnp.minimum(i1, i2)
    hi = jnp.maximum(i1, i2)
    pair = jnp.where(lo == 0.0, hi - 1.0, jnp.where(lo == 1.0, hi + 1.0, 5.0))
    return gsel * N_PAIRS + pair


def _class_rank(cls, tri_ref, cnt_s):
    n = cls.shape[1]
    cid = lax.broadcasted_iota(jnp.int32, (CLS_ROWS, n), 0).astype(F32)
    onehot = jnp.where(cls == cid, 1.0, 0.0)
    segs = [onehot[:, k * LANES:(k + 1) * LANES] for k in range(n // LANES)]
    before = jnp.dot(jnp.concatenate(segs, axis=0).astype(BF16), tri_ref[...], preferred_element_type=F32)
    base = cnt_s[...]
    ranks = []
    for k, seg in enumerate(segs):
        ranks.append(jnp.sum(seg * (before[k * CLS_ROWS:(k + 1) * CLS_ROWS] + base), axis=0, keepdims=True))
        base = base + jnp.sum(seg, axis=1, keepdims=True)
    cnt_s[...] = base
    return jnp.concatenate(ranks, axis=1)


def _out_body(ya_ref, yg_ref, x_ref, mod_ref, wo_ref, g_ref, wr_ref, br_ref, tri_ref, xo_ref, rt_ref, cnt_ref, cnt_s,
              *, n_lat, tm):
    j = pl.program_id(1)

    @pl.when((pl.program_id(0) == 0) & (j == 0))
    def _():
        cnt_s[...] = jnp.zeros_like(cnt_s)

    row = j * tm + lax.broadcasted_iota(jnp.int32, (tm, 1), 0)
    is_ctx = row >= n_lat
    mod = mod_ref[0]
    y = (jnp.dot(ya_ref[0], wo_ref[0:ATT_WIDTH, :], preferred_element_type=F32)
         + jnp.dot(yg_ref[0], wo_ref[ATT_WIDTH:, :], preferred_element_type=F32))
    xn = x_ref[0] + _row_mod(mod, is_ctx, 2) * y
    xo_ref[0] = xn
    h2 = _norm_modulate(xn, g_ref[...], mod, is_ctx, 3, 4).astype(BF16)
    logits = jnp.dot(h2, wr_ref[...], preferred_element_type=F32) + br_ref[0:1, :]
    cls = _route(logits.T[0:N_EXPERTS, :])
    rank = _class_rank(cls, tri_ref, cnt_s)
    rt_ref[0] = jnp.concatenate([cls, rank, jnp.zeros((6, tm), F32)], axis=0).astype(jnp.int32)
    cnt_ref[...] = jnp.broadcast_to(cnt_s[...], (CLS_ROWS, LANES)).astype(jnp.int32)


def _out_proj(ya, yg, xx, modv, wo, gain, wr, br, tri, n_lat, rows, tm):
    B, T, _ = xx.shape
    nj = rows // tm
    return pl.pallas_call(
        functools.partial(_out_body, n_lat=n_lat, tm=tm),
        grid=(B, nj),
        in_specs=[pl.BlockSpec((1, tm, ATT_WIDTH), lambda b, j: (b, j, 0)),
                  pl.BlockSpec((1, tm, GLA_WIDTH + CONV_WIDTH), lambda b, j: (b, j, 0)),
                  pl.BlockSpec((1, tm, D_MODEL), lambda b, j: (b, j, 0)),
                  pl.BlockSpec((1, 16, D_MODEL), lambda b, j: (b, 0, 0)),
                  pl.BlockSpec((D_MODEL, D_MODEL), lambda b, j: (0, 0)),
                  pl.BlockSpec((1, D_MODEL), lambda b, j: (0, 0)),
                  pl.BlockSpec((D_MODEL, LANES), lambda b, j: (0, 0)),
                  pl.BlockSpec((8, LANES), lambda b, j: (0, 0)),
                  pl.BlockSpec((LANES, LANES), lambda b, j: (0, 0))],
        out_specs=[pl.BlockSpec((1, tm, D_MODEL), lambda b, j: (b, j, 0)),
                   pl.BlockSpec((1, 8, tm), lambda b, j: (b * nj + j, 0, 0)),
                   pl.BlockSpec((CLS_ROWS, LANES), lambda b, j: (0, 0))],
        out_shape=[jax.ShapeDtypeStruct((B, rows, D_MODEL), F32),
                   jax.ShapeDtypeStruct((B * nj, 8, tm), jnp.int32),
                   jax.ShapeDtypeStruct((CLS_ROWS, LANES), jnp.int32)],
        scratch_shapes=[pltpu.VMEM((CLS_ROWS, 1), F32)],
        compiler_params=_cparams(2),
        name="out_proj_router",
    )(ya, yg, xx, modv, wo, gain, wr, br, tri)


ROW_UNROLL = 8


def _issue_rows(tm, idx, slot, off_ref, make_copy):
    def trip(i, c):
        for u in range(ROW_UNROLL):
            r = i * ROW_UNROLL + u
            make_copy(r, off_ref[idx[slot, 0, r]] + idx[slot, 1, r]).start(priority=u % 2)
        return c

    lax.fori_loop(0, tm // ROW_UNROLL, trip, 0)


def _disp_body(off_ref, x_ref, mod_ref, g_ref, rt_ref, hs_in_ref, hs_ref, buf, idx, sem_i, sem_d, *, n_lat, tm, nj,
               n_steps):
    del hs_in_ref
    j = pl.program_id(1)
    s = pl.program_id(0) * nj + j
    last = n_steps - 1
    slot = s % 2

    def idx_fetch(step, sl):
        return pltpu.make_async_copy(rt_ref.at[step], idx.at[sl], sem_i.at[sl])

    def drain(sl):
        pltpu.make_async_copy(buf.at[sl], hs_ref.at[pl.ds(0, tm)], sem_d.at[sl]).wait()

    @pl.when(s == 0)
    def _():
        idx_fetch(s, slot).start()

    row = j * tm + lax.broadcasted_iota(jnp.int32, (tm, 1), 0)
    is_ctx = row >= n_lat
    h2 = _norm_modulate(x_ref[0], g_ref[...], mod_ref[0], is_ctx, 3, 4)
    buf[slot] = h2.reshape(tm, 8, LANES)
    idx_fetch(s, slot).wait()
    _issue_rows(tm, idx, slot, off_ref, lambda r, d: pltpu.make_async_copy(buf.at[slot, r], hs_ref.at[d],
                                                                           sem_d.at[slot]))

    @pl.when(s > 0)
    def _():
        drain(1 - slot)

    @pl.when(s < last)
    def _():
        idx_fetch(s + 1, 1 - slot).start()

    @pl.when(s == last)
    def _():
        drain(slot)


def _dispatch(off, xx, modv, gain, route, hs0, n_lat, tm):
    B, rows, _ = xx.shape
    nj = rows // tm
    return pl.pallas_call(
        functools.partial(_disp_body, n_lat=n_lat, tm=tm, nj=nj, n_steps=B * nj),
        grid_spec=pltpu.PrefetchScalarGridSpec(
            num_scalar_prefetch=1,
            grid=(B, nj),
            in_specs=[pl.BlockSpec((1, tm, D_MODEL), lambda b, j, o: (b, j, 0)),
                      pl.BlockSpec((1, 16, D_MODEL), lambda b, j, o: (b, 0, 0)),
                      pl.BlockSpec((1, D_MODEL), lambda b, j, o: (0, 0)),
                      pl.BlockSpec(memory_space=pl.ANY),
                      pl.BlockSpec(memory_space=pl.ANY)],
            out_specs=pl.BlockSpec(memory_space=pl.ANY),
            scratch_shapes=[pltpu.VMEM((2, tm, 8, LANES), F32), pltpu.SMEM((2, 8, tm), jnp.int32),
                            pltpu.SemaphoreType.DMA((2,)), pltpu.SemaphoreType.DMA((2,))]),
        out_shape=jax.ShapeDtypeStruct(hs0.shape, F32),
        input_output_aliases={5: 0},
        compiler_params=_cparams(2),
        name="dispatch",
    )(off, xx, modv, gain, route, hs0)


def _moe_body(tile_ref, e_lo_ref, e_hi_ref, valid_ref, hs_ref, wg1, wu1, wd1, wg2, wu2, wd2, wr_ref, br_ref, ys_ref):
    g = pl.program_id(0)
    tm = hs_ref.shape[0]

    @pl.when(valid_ref[g] == 1)
    def _():
        x = hs_ref[...].reshape(tm, D_MODEL)
        e_lo = e_lo_ref[g]
        e_hi = e_hi_ref[g]
        dw = wr_ref[pl.ds(e_lo, 1), :] - wr_ref[pl.ds(e_hi, 1), :]
        d = jnp.sum(x * dw, axis=-1, keepdims=True) + (br_ref[e_lo] - br_ref[e_hi])
        w_lo = jax.nn.sigmoid(d)
        w_hi = jax.nn.sigmoid(-d)
        h = x.astype(BF16)

        def act(wg, wu, w):
            a = _silu(jnp.dot(h, wg[0], preferred_element_type=F32)) * jnp.dot(h, wu[0], preferred_element_type=F32)
            return (a * w).astype(BF16)

        y = (jnp.dot(act(wg1, wu1, w_lo), wd1[0], preferred_element_type=F32)
             + jnp.dot(act(wg2, wu2, w_hi), wd2[0], preferred_element_type=F32))
        ys_ref[...] = y.reshape(tm, 8, LANES)

    @pl.when(valid_ref[g] == 0)
    def _():
        ys_ref[...] = jnp.zeros_like(ys_ref)


def _moe(hs, tile, e_lo, e_hi, valid, wg, wu, wd, wr_t, br):
    n_tiles = tile.shape[0]
    tm = MOE_TILE

    def w_in(sel):
        return pl.BlockSpec((1, D_MODEL, D_EXPERT), lambda g, t, lo, hi, v: ((lo, hi)[sel][g], 0, 0))

    def w_out(sel):
        return pl.BlockSpec((1, D_EXPERT, D_MODEL), lambda g, t, lo, hi, v: ((lo, hi)[sel][g], 0, 0))

    return pl.pallas_call(
        _moe_body,
        grid_spec=pltpu.PrefetchScalarGridSpec(
            num_scalar_prefetch=4,
            grid=(n_tiles,),
            in_specs=[pl.BlockSpec((tm, 8, LANES), lambda g, t, lo, hi, v: (t[g], 0, 0)),
                      w_in(0), w_in(0), w_out(0), w_in(1), w_in(1), w_out(1),
                      pl.BlockSpec((N_EXPERTS, D_MODEL), lambda g, t, lo, hi, v: (0, 0)),
                      pl.BlockSpec(memory_space=pltpu.SMEM)],
            out_specs=pl.BlockSpec((tm, 8, LANES), lambda g, t, lo, hi, v: (g, 0, 0))),
        out_shape=jax.ShapeDtypeStruct((n_tiles * tm, 8, LANES), F32),
        compiler_params=_cparams(1),
        name="moe_pairs",
    )(tile, e_lo, e_hi, valid, hs, wg, wu, wd, wg, wu, wd, wr_t, br)


def _fin_body(off_ref, x_ref, mod_ref, rt_ref, ys_ref, xo_ref, buf, idx, sem_i, sem_d, *, n_lat, tm, nj, n_steps):
    j = pl.program_id(1)
    s = pl.program_id(0) * nj + j
    last = n_steps - 1
    slot = s % 2

    def idx_fetch(step, sl):
        return pltpu.make_async_copy(rt_ref.at[step], idx.at[sl], sem_i.at[sl])

    def gather(sl):
        _issue_rows(tm, idx, sl, off_ref, lambda r, d: pltpu.make_async_copy(ys_ref.at[d], buf.at[sl, r],
                                                                             sem_d.at[sl]))

    @pl.when(s == 0)
    def _():
        idx_fetch(0, 0).start()
        idx_fetch(0, 0).wait()
        gather(0)
        if last > 0:
            idx_fetch(1, 1).start()

    @pl.when(s < last)
    def _():
        idx_fetch(s + 1, 1 - slot).wait()
        gather(1 - slot)

    @pl.when(s + 2 <= last)
    def _():
        idx_fetch(s + 2, slot).start()

    pltpu.make_async_copy(ys_ref.at[pl.ds(0, tm)], buf.at[slot], sem_d.at[slot]).wait()
    row = j * tm + lax.broadcasted_iota(jnp.int32, (tm, 1), 0)
    is_ctx = row >= n_lat
    xo_ref[0] = x_ref[0] + _row_mod(mod_ref[0], is_ctx, 5) * buf[slot].reshape(tm, D_MODEL)


def _combine(off, xx, modv, route, ys, n_lat, tm):
    B, rows, _ = xx.shape
    nj = rows // tm
    return pl.pallas_call(
        functools.partial(_fin_body, n_lat=n_lat, tm=tm, nj=nj, n_steps=B * nj),
        grid_spec=pltpu.PrefetchScalarGridSpec(
            num_scalar_prefetch=1,
            grid=(B, nj),
            in_specs=[pl.BlockSpec((1, tm, D_MODEL), lambda b, j, o: (b, j, 0)),
                      pl.BlockSpec((1, 16, D_MODEL), lambda b, j, o: (b, 0, 0)),
                      pl.BlockSpec(memory_space=pl.ANY),
                      pl.BlockSpec(memory_space=pl.ANY)],
            out_specs=pl.BlockSpec((1, tm, D_MODEL), lambda b, j, o: (b, j, 0)),
            scratch_shapes=[pltpu.VMEM((2, tm, 8, LANES), F32), pltpu.SMEM((2, 8, tm), jnp.int32),
                            pltpu.SemaphoreType.DMA((2,)), pltpu.SemaphoreType.DMA((2,))]),
        out_shape=jax.ShapeDtypeStruct((B, rows, D_MODEL), F32),
        compiler_params=_cparams(2),
        name="combine",
    )(off, xx, modv, route, ys)


def _rope_tables(n_lat, n_ctx):
    rows = n_lat // GRID_W
    row, col = jnp.meshgrid(jnp.arange(rows), jnp.arange(GRID_W), indexing="ij")
    n_freq = HEAD_DIM // 4
    inv_freq = ROPE_BASE ** (-jnp.arange(n_freq, dtype=F32) / n_freq)
    ang = jnp.concatenate([row.reshape(-1, 1).astype(F32) * inv_freq, col.reshape(-1, 1).astype(F32) * inv_freq],
                          axis=-1)
    cos = jnp.tile(jnp.cos(ang), (1, LANES // (HEAD_DIM // 2)))
    sin = jnp.tile(jnp.sin(ang), (1, LANES // (HEAD_DIM // 2)))
    sign = jnp.where((jnp.arange(LANES) % HEAD_DIM) < HEAD_DIM // 2, -1.0, 1.0).astype(F32)
    cos = jnp.concatenate([cos, jnp.ones((n_ctx, LANES), F32)], axis=0)
    sin = jnp.concatenate([sin * sign, jnp.zeros((n_ctx, LANES), F32)], axis=0)
    return cos, sin


def _block_diag_ones(n, blk):
    i = np.arange(n) // blk
    return jnp.asarray((i[:, None] == i[None, :]).astype(np.float32), dtype=BF16)


def _routing_tables(counts, n_tiles):
    tiles_c = (counts + MOE_TILE - 1) // MOE_TILE
    tile_end = jnp.cumsum(tiles_c)
    tile_start = tile_end - tiles_c
    off = tile_start * MOE_TILE
    total = tile_end[-1]
    g = jnp.arange(n_tiles, dtype=jnp.int32)
    valid = (g < total).astype(jnp.int32)
    g_eff = jnp.minimum(g, total - 1)
    c_of = jnp.sum((g_eff[:, None] >= tile_end[None, :]).astype(jnp.int32), axis=1)
    group = c_of // N_PAIRS
    pair = c_of % N_PAIRS
    lo = jnp.asarray(PAIR_LO, jnp.int32)
    hi = jnp.asarray(PAIR_HI, jnp.int32)
    e_lo = group * EXPERTS_PER_GROUP + jnp.sum((pair[:, None] == jnp.arange(N_PAIRS)[None, :]) * lo[None, :], axis=1)
    e_hi = group * EXPERTS_PER_GROUP + jnp.sum((pair[:, None] == jnp.arange(N_PAIRS)[None, :]) * hi[None, :], axis=1)
    return off.astype(jnp.int32), g_eff, e_lo.astype(jnp.int32), e_hi.astype(jnp.int32), valid


def kernel(x, c, ctx, c_ctx, w_ada, b_ada, norm_mix_g, norm_ffn_g, w_in, q_norm_g, k_norm_g, attn_sink, gla_gate_w,
           gla_gate_b, gla_norm_g, conv_w, w_out, w_router, b_router, w_gate_e, w_up_e, w_down_e):
    B, S, D = x.shape
    L = ctx.shape[1]
    T = S + L
    assert D == D_MODEL and T % TOKEN_TILE == 0 and S % LAT_TILE == 0 and S % GRID_W == 0
    assert S % ATT_BLOCK == 0 and L % ATT_BLOCK == 0 and S >= ATT_SPAN and T % EPI_ROWS == 0

    cond_rows = -(-(B + 1) // 8) * 8
    cond = jnp.zeros((cond_rows, D), F32).at[:B].set(c).at[B].set(c_ctx)
    mod_all = _modulation(cond, w_ada, b_ada)

    cos, sin = _rope_tables(S, L)
    bd_head = _block_diag_ones(LANES, HEAD_DIM)
    bd_gla = _block_diag_ones(GLA_WIDTH, GLA_DV)
    cum = _gla_cum_matrices()
    tri = jnp.asarray(np.triu(np.ones((LANES, LANES), np.float32), 1), dtype=BF16)
    xx = jnp.concatenate([x, ctx], axis=1)

    for l in range(DEPTH):
        last = l == DEPTH - 1
        m_lat = mod_all[l, :B].reshape(B, 6, D)
        m_ctx = jnp.broadcast_to(mod_all[l, B].reshape(1, 6, D), (B, 6, D))
        modv = jnp.concatenate([m_lat, m_ctx, jnp.zeros((B, 4, D), F32)], axis=1)

        wl = w_in[l]
        order = jnp.asarray(ATT_HEAD_ORDER)
        wq = wl[:, :ATT_WIDTH].reshape(D, ATT_HEADS, HEAD_DIM)[:, order, :].reshape(D, ATT_WIDTH)
        w_perm = jnp.concatenate([wq, wl[:, ATT_WIDTH:1536], wl[:, 1568:], wl[:, 1536:1568],
                                  jnp.zeros((D, N_PROJ - wl.shape[1]), F32)], axis=1).astype(BF16)
        qg = jnp.tile(q_norm_g[l], LANES // HEAD_DIM) * (HEAD_DIM ** -0.5)
        kg = jnp.tile(k_norm_g[l], LANES // HEAD_DIM)
        qkg = jnp.stack([qg] * (ATT_WIDTH // LANES) + [kg] + [jnp.zeros_like(kg)] * 3)
        p = _in_proj(xx, modv, norm_mix_g[l].reshape(1, D), w_perm, cos, sin, qkg, bd_head, S)

        y_att = _attention(p, attn_sink[l], S, not last)

        pad_rows = jnp.zeros((LANES - 2 * GLA_GATE_RANK, GLA_QK_WIDTH), F32)
        zero_rank = jnp.zeros((GLA_GATE_RANK, GLA_QK_WIDTH), F32)
        wgf = jnp.concatenate([gla_gate_w[l, 0], zero_rank, pad_rows], axis=0).astype(BF16)
        wgb = jnp.concatenate([zero_rank, gla_gate_w[l, 1], pad_rows], axis=0).astype(BF16)
        gbias = jnp.concatenate([gla_gate_b[l], jnp.zeros((6, GLA_QK_WIDTH), F32)], axis=0)
        ng = jnp.tile(gla_norm_g[l], GLA_HEADS).reshape(1, GLA_WIDTH)
        cw = jnp.concatenate([conv_w[l], jnp.zeros((5, CONV_WIDTH), F32)], axis=0)
        y_gc = _gla_conv(p, wgf, wgb, gbias, ng, cw, bd_gla, cum, S)

        rows, tm = (S, LAT_TILE) if last else (T, TOKEN_TILE)
        wr = jnp.concatenate([w_router, jnp.zeros((D, LANES - N_EXPERTS), F32)], axis=1).astype(BF16)
        br = jnp.zeros((8, LANES), F32).at[0, :N_EXPERTS].set(b_router)
        ffn_g = norm_ffn_g[l].reshape(1, D)
        wo_att = w_out[l, :ATT_WIDTH].reshape(ATT_HEADS, HEAD_DIM, D)[order].reshape(ATT_WIDTH, D)
        wo = jnp.concatenate([wo_att, w_out[l, ATT_WIDTH:]], axis=0).astype(BF16)
        xx_mid, route, counts = _out_proj(y_att, y_gc, xx, modv, wo, ffn_g, wr, br, tri, S, rows, tm)

        n_tiles = -(-(B * rows) // MOE_TILE) + N_CLASSES
        off, tile, e_lo, e_hi, valid = _routing_tables(counts[:N_CLASSES, 0], n_tiles)

        hs = _dispatch(off, xx_mid, modv, ffn_g, route, jnp.zeros((n_tiles * MOE_TILE, 8, LANES), F32), S, tm)
        ys = _moe(hs, tile, e_lo, e_hi, valid, w_gate_e[l].astype(BF16), w_up_e[l].astype(BF16),
                  w_down_e[l].astype(BF16), w_router.T, b_router)
        xx = _combine(off, xx_mid, modv, route, ys, S, tm)
    return xx
```

```python
import functools

import numpy as np
import jax
import jax.numpy as jnp
from jax import lax
from jax.experimental import pallas as pl
from jax.experimental.pallas import tpu as pltpu

D_MODEL = 1024
DEPTH = 2
GRID_W = 64
EPS = 1e-6
HEAD_DIM = 64
ATT_HEADS = 8
ATT_KV_HEADS = 2
ATT_GROUP = ATT_HEADS // ATT_KV_HEADS
ATT_WIDTH = ATT_HEADS * HEAD_DIM
WINDOW = 128
ROPE_BASE = 10000.0
GLA_HEADS = 4
GLA_DV = 64
GLA_DK = 32
GLA_WIDTH = GLA_HEADS * GLA_DV
GLA_GATE_RANK = 16
GLA_GATE_NORM = 16.0
GLA_CHUNK = 64
CONV_WIDTH = 256
N_EXPERTS = 16
N_GROUPS = 4
EXPERTS_PER_GROUP = 4
D_EXPERT = D_MODEL // 2

LANES = 128
KV_WIDTH = ATT_KV_HEADS * HEAD_DIM
GLA_QK_WIDTH = GLA_HEADS * GLA_DK
COL_AQ, COL_AK, COL_AV = 0, 512, 640
COL_GQ, COL_GK, COL_GV, COL_GR = 768, 896, 1024, 1280
COL_CB, COL_CC, COL_CH, COL_GT = 1536, 1792, 2048, 2304
N_PROJ = 2432
QK_COLS = COL_AV
N_PAIRS = 6
N_CLASSES = N_GROUPS * N_PAIRS
PAIR_LO = (0, 0, 0, 1, 1, 2)
PAIR_HI = (1, 2, 3, 2, 3, 3)
CLS_ROWS = 32
NEG = -1e30

TOKEN_TILE = 768
LAT_TILE = 1024
MOE_TILE = 512
SUB_ROWS = 256
VMEM_LIMIT = 56 * 1024 * 1024

F32 = jnp.float32
BF16 = jnp.bfloat16


def _cparams(n_axes):
    return pltpu.CompilerParams(dimension_semantics=("arbitrary",) * n_axes, vmem_limit_bytes=VMEM_LIMIT)


def _silu(x):
    return x * jax.nn.sigmoid(x)


def _mod_body(c_ref, w_ref, b_ref, o_ref):
    c = c_ref[...]
    a = _silu(c).astype(BF16)
    o_ref[0] = jnp.dot(a, w_ref[0].astype(BF16), preferred_element_type=F32) + b_ref[0]


def _modulation(cond, w_ada, b_ada):
    rows = cond.shape[0]
    nblk = w_ada.shape[2] // D_MODEL
    return pl.pallas_call(
        _mod_body,
        grid=(DEPTH, nblk),
        in_specs=[pl.BlockSpec((rows, D_MODEL), lambda l, n: (0, 0)),
                  pl.BlockSpec((1, D_MODEL, D_MODEL), lambda l, n: (l, 0, n)),
                  pl.BlockSpec((1, 1, D_MODEL), lambda l, n: (l, 0, n))],
        out_specs=pl.BlockSpec((1, rows, D_MODEL), lambda l, n: (l, 0, n)),
        out_shape=jax.ShapeDtypeStruct((DEPTH, rows, w_ada.shape[2]), F32),
        compiler_params=_cparams(2),
        name="modulation",
    )(cond, w_ada, b_ada.reshape(DEPTH, 1, -1))


def _row_mod(mod, is_ctx, i):
    return jnp.where(is_ctx, mod[6 + i:7 + i], mod[i:i + 1])


def _norm_modulate(x, gain, mod, is_ctx, i_shift, i_scale):
    ms = jnp.mean(x * x, axis=-1, keepdims=True)
    xn = x * lax.rsqrt(ms + EPS) * gain
    return xn * (1.0 + _row_mod(mod, is_ctx, i_scale)) + _row_mod(mod, is_ctx, i_shift)


def _in_body(x_ref, mod_ref, g_ref, w_ref, cos_ref, sin_ref, qkg_ref, bd_ref, o_ref, *, n_lat, tm):
    j = pl.program_id(0)
    lane = lax.broadcasted_iota(jnp.int32, (1, LANES), 1)
    first_half = (lane % HEAD_DIM) < (HEAD_DIM // 2)
    for r0 in range(0, tm, SUB_ROWS):
        rows = slice(r0, r0 + SUB_ROWS)
        row = j * tm + r0 + lax.broadcasted_iota(jnp.int32, (SUB_ROWS, 1), 0)
        is_ctx = row >= n_lat
        h = _norm_modulate(x_ref[0, rows, :], g_ref[...], mod_ref[0], is_ctx, 0, 1).astype(BF16)
        qk = jnp.dot(h, w_ref[:, :QK_COLS], preferred_element_type=F32)
        cos = cos_ref[rows, :]
        sin = sin_ref[rows, :]
        for c in range(QK_COLS // LANES):
            xc = qk[:, c * LANES:(c + 1) * LANES]
            ss = jnp.dot((xc * xc).astype(BF16), bd_ref[...], preferred_element_type=F32) * (1.0 / HEAD_DIM)
            xc = xc * lax.rsqrt(ss + EPS) * qkg_ref[c:c + 1, :]
            rot = jnp.where(first_half, pltpu.roll(xc, LANES - HEAD_DIM // 2, 1), pltpu.roll(xc, HEAD_DIM // 2, 1))
            o_ref[0, rows, c * LANES:(c + 1) * LANES] = (xc * cos + rot * sin).astype(BF16)
        o_ref[0, rows, QK_COLS:] = jnp.dot(h, w_ref[:, QK_COLS:], preferred_element_type=F32).astype(BF16)


def _in_proj(xx, modv, gain, w, cos, sin, qkg, bd, n_lat):
    B, T, _ = xx.shape
    tm = TOKEN_TILE
    return pl.pallas_call(
        functools.partial(_in_body, n_lat=n_lat, tm=tm),
        grid=(T // tm, B),
        in_specs=[pl.BlockSpec((1, tm, D_MODEL), lambda j, b: (b, j, 0)),
                  pl.BlockSpec((1, 16, D_MODEL), lambda j, b: (b, 0, 0)),
                  pl.BlockSpec((1, D_MODEL), lambda j, b: (0, 0)),
                  pl.BlockSpec((D_MODEL, N_PROJ), lambda j, b: (0, 0)),
                  pl.BlockSpec((tm, LANES), lambda j, b: (j, 0)),
                  pl.BlockSpec((tm, LANES), lambda j, b: (j, 0)),
                  pl.BlockSpec((8, LANES), lambda j, b: (0, 0)),
                  pl.BlockSpec((LANES, LANES), lambda j, b: (0, 0))],
        out_specs=pl.BlockSpec((1, tm, N_PROJ), lambda j, b: (b, j, 0)),
        out_shape=jax.ShapeDtypeStruct((B, T, N_PROJ), BF16),
        compiler_params=_cparams(2),
        name="in_proj",
    )(xx, modv, gain, w, cos, sin, qkg, bd)


ATT_BLOCK = 128
ATT_SPAN = ATT_BLOCK + 2 * WINDOW


ATT_HEAD_ORDER = (0, 4, 1, 5, 2, 6, 3, 7)


def _attend(qblk, k_parts, v_parts, biases, sinks):
    rows = ATT_GROUP * ATT_BLOCK
    rowi = lax.broadcasted_iota(jnp.int32, (rows, 1), 0)
    lane = lax.broadcasted_iota(jnp.int32, (1, LANES), 1)
    lower = lane < HEAD_DIM
    nt = (((1,), (1,)), ((), ()))
    normed = []
    for h in range(ATT_KV_HEADS):
        keep = lower if h == 0 else jnp.logical_not(lower)
        qs = jnp.concatenate([jnp.where(keep, qblk[:, g * LANES:(g + 1) * LANES], jnp.zeros((), BF16))
                              for g in range(ATT_GROUP)], axis=0)
        sink = jnp.full((rows, 1), sinks[ATT_GROUP * h + ATT_GROUP - 1], F32)
        for g in range(ATT_GROUP - 2, -1, -1):
            sink = jnp.where(rowi < (g + 1) * ATT_BLOCK, sinks[ATT_GROUP * h + g], sink)
        scores = []
        for k in k_parts:
            s = lax.dot_general(qs, k, nt, preferred_element_type=F32)
            scores += [s[:, c * LANES:(c + 1) * LANES] for c in range(k.shape[0] // LANES)]
        scores = [s if b is None else s + b for s, b in zip(scores, biases)]
        m = scores[0]
        for s in scores[1:]:
            m = jnp.maximum(m, s)
        m = jnp.maximum(jnp.max(m, axis=-1, keepdims=True), sink)
        probs = [jnp.exp((s - m).astype(BF16)) for s in scores]
        acc = jnp.where(keep, 0.0, jnp.exp(sink - m))
        c0 = 0
        for v in v_parts[h]:
            n = v.shape[0] // LANES
            acc = acc + jnp.dot(jnp.concatenate(probs[c0:c0 + n], axis=1), v, preferred_element_type=F32)
            c0 += n
        normed.append(acc / pltpu.roll(acc, HEAD_DIM, 1))
    return jnp.concatenate([jnp.where(lower, normed[0][g * ATT_BLOCK:(g + 1) * ATT_BLOCK],
                                      normed[1][g * ATT_BLOCK:(g + 1) * ATT_BLOCK]) for g in range(ATT_GROUP)], axis=1)


def _att_body(sink_ref, q_ref, k_ref, v_ref, o_ref, v1_s, *, n_lat, n_ctx, with_ctx_out):
    sinks = [sink_ref[i] for i in range(ATT_HEADS)]
    lane = lax.broadcasted_iota(jnp.int32, (1, LANES), 1)
    vv = v_ref[0]
    v1_s[0] = jnp.where(lane < HEAD_DIM, vv, jnp.ones((), BF16))
    v1_s[1] = jnp.where(lane < HEAD_DIM, jnp.ones((), BF16), vv)
    k_ctx = k_ref[0, n_lat:n_lat + n_ctx, :]
    v_ctx = [v1_s[h, n_lat:n_lat + n_ctx, :] for h in range(ATT_KV_HEADS)]
    no_bias = [None] * (n_ctx // LANES)
    qi = lax.broadcasted_iota(jnp.int32, (ATT_GROUP * ATT_BLOCK, LANES), 0) % ATT_BLOCK
    ki = lax.broadcasted_iota(jnp.int32, (ATT_GROUP * ATT_BLOCK, LANES), 1)
    past_ok = jnp.where(ki >= qi, 0.0, NEG)
    ahead_ok = jnp.where(ki <= qi, 0.0, NEG)

    def block(q0, k0, n_keys, biases):
        k_parts = [k_ref[0, pl.ds(k0, n_keys), :], k_ctx]
        v_parts = [[v1_s[h, pl.ds(k0, n_keys), :], v_ctx[h]] for h in range(ATT_KV_HEADS)]
        out = _attend(q_ref[0, pl.ds(q0, ATT_BLOCK), :], k_parts, v_parts, biases + no_bias, sinks)
        o_ref[0, pl.ds(q0, ATT_BLOCK), :] = out.astype(BF16)

    def interior(i, carry):
        q0 = pl.multiple_of(i * ATT_BLOCK, ATT_BLOCK)
        block(q0, pl.multiple_of(q0 - WINDOW, ATT_BLOCK), ATT_SPAN, [past_ok, None, ahead_ok])
        return carry

    nq = n_lat // ATT_BLOCK
    block(0, 0, 2 * ATT_BLOCK, [None, ahead_ok])
    lax.fori_loop(1, nq - 1, interior, 0)
    block(n_lat - ATT_BLOCK, n_lat - 2 * ATT_BLOCK, 2 * ATT_BLOCK, [past_ok, None])
    if with_ctx_out:
        for c in range(n_ctx // ATT_BLOCK):
            r0 = n_lat + c * ATT_BLOCK
            out = _attend(q_ref[0, r0:r0 + ATT_BLOCK, :], [k_ctx], [[v_ctx[h]] for h in range(ATT_KV_HEADS)],
                          no_bias, sinks)
            o_ref[0, r0:r0 + ATT_BLOCK, :] = out.astype(BF16)
    else:
        o_ref[0, n_lat:, :] = jnp.zeros((n_ctx, ATT_WIDTH), BF16)


def _attention(p, sinks, n_lat, with_ctx_out):
    B, T, _ = p.shape
    return pl.pallas_call(
        functools.partial(_att_body, n_lat=n_lat, n_ctx=T - n_lat, with_ctx_out=with_ctx_out),
        grid=(B,),
        in_specs=[pl.BlockSpec(memory_space=pltpu.SMEM),
                  pl.BlockSpec((1, T, ATT_WIDTH), lambda b: (b, 0, COL_AQ // ATT_WIDTH)),
                  pl.BlockSpec((1, T, KV_WIDTH), lambda b: (b, 0, COL_AK // KV_WIDTH)),
                  pl.BlockSpec((1, T, KV_WIDTH), lambda b: (b, 0, COL_AV // KV_WIDTH))],
        out_specs=pl.BlockSpec((1, T, ATT_WIDTH), lambda b: (b, 0, 0)),
        out_shape=jax.ShapeDtypeStruct((B, T, ATT_WIDTH), BF16),
        scratch_shapes=[pltpu.VMEM((ATT_KV_HEADS, T, KV_WIDTH), BF16)],
        compiler_params=_cparams(1),
        name="attention",
    )(sinks, p, p, p)


GLA_BLOCK_A = 256
GLA_BLOCK_B = 128
CONV_PAD = 8
EPI_ROWS = 256


def _log_sigmoid(z):
    return jnp.minimum(z, 0.0) - jnp.log1p(jnp.exp(-jnp.abs(z)))


def _gla_body(gq_ref, gk_ref, gv_ref, gr_ref, cb_ref, cc_ref, ch_ref, gt_ref, wgf_ref, wgb_ref, gbias_ref, ng_ref,
              cw_ref, bd_ref, cum_ref, o_ref, qe_s, ke_s, dec_s, upd_s, prev_s, st_s, o_s, u_s, *, n_lat, n_ctx):
    T = n_lat + n_ctx
    C = GLA_CHUNK
    nt = (((1,), (1,)), ((), ()))
    tn = (((0,), (0,)), ((), ()))

    sr = lax.broadcasted_iota(jnp.int32, (GLA_WIDTH, GLA_QK_WIDTH), 0) // GLA_DV
    sl = lax.broadcasted_iota(jnp.int32, (GLA_WIDTH, GLA_QK_WIDTH), 1) // GLA_DK
    state_mask = sr == sl

    def factors(i, carry):
        r0 = pl.multiple_of(i * GLA_BLOCK_A, GLA_BLOCK_A)
        rows = pl.ds(r0, GLA_BLOCK_A)
        gt = gt_ref[0, rows, :]
        q = gq_ref[0, rows, :].astype(F32) * (GLA_DK ** -0.5)
        k = gk_ref[0, rows, :].astype(F32)
        v = gv_ref[0, rows, :]
        for d, wg_ref in enumerate((wgf_ref, wgb_ref)):
            z = jnp.dot(gt, wg_ref[...], preferred_element_type=F32) + gbias_ref[d:d + 1, :]
            g = _log_sigmoid(z) * (1.0 / GLA_GATE_NORM)
            g_hi = g.astype(BF16)
            g_lo = (g - g_hi.astype(F32)).astype(BF16)
            cum = jnp.dot(cum_ref[d], jnp.concatenate([g_hi, g_lo], axis=1), preferred_element_type=F32)
            b = cum[:GLA_BLOCK_A, :GLA_QK_WIDTH] + cum[:GLA_BLOCK_A, GLA_QK_WIDTH:]
            tot = cum[GLA_BLOCK_A:, :GLA_QK_WIDTH] + cum[GLA_BLOCK_A:, GLA_QK_WIDTH:]
            qe_s[d, rows, :] = (q * jnp.exp(b)).astype(BF16)
            ke_s[d, rows, :] = (k * jnp.exp(-b)).astype(BF16)
            dec_s[d, rows, :] = jnp.exp(tot)
            kl = (k * jnp.exp(tot - b)).astype(BF16)
            for cc in range(GLA_BLOCK_A // C):
                upd = lax.dot_general(v[cc * C:(cc + 1) * C], kl[cc * C:(cc + 1) * C], tn, preferred_element_type=F32)
                upd_s[d, i * (GLA_BLOCK_A // C) + cc] = jnp.where(state_mask, upd, 0.0)
        return carry

    lax.fori_loop(0, T // GLA_BLOCK_A, factors, 0)

    nc_lat = n_lat // C
    nc_ctx = n_ctx // C
    st_s[...] = jnp.zeros_like(st_s)

    def scan(i, carry):
        in_ctx = i < nc_ctx
        cf = jnp.where(in_ctx, nc_lat + i, i - nc_ctx)
        cb = jnp.where(in_ctx, nc_lat + nc_ctx - 1 - i, nc_lat - 1 - (i - nc_ctx))
        for d, cid in enumerate((cf, cb)):
            st = st_s[d]
            prev_s[d, cid] = st.astype(BF16)
            st_s[d] = st * dec_s[d, pl.ds(pl.multiple_of(cid * C, C), 1), :] + upd_s[d, cid]
        return carry

    lax.fori_loop(0, nc_lat + nc_ctx, scan, 0)

    RB = GLA_BLOCK_B
    hr = lax.broadcasted_iota(jnp.int32, (GLA_HEADS * RB, GLA_QK_WIDTH), 0) // RB
    hl = lax.broadcasted_iota(jnp.int32, (GLA_HEADS * RB, GLA_QK_WIDTH), 1) // GLA_DK
    head_rows = hr == hl
    qr = lax.broadcasted_iota(jnp.int32, (GLA_HEADS * RB, RB), 0) % RB
    kc = lax.broadcasted_iota(jnp.int32, (GLA_HEADS * RB, RB), 1)
    same_chunk = (qr // C) == (kc // C)
    visible = (same_chunk & (kc <= qr), same_chunk & (kc > qr))
    ol = lax.broadcasted_iota(jnp.int32, (RB, GLA_WIDTH), 1) // GLA_DV

    def outputs(i, carry):
        r0 = pl.multiple_of(i * RB, RB)
        rows = pl.ds(r0, RB)
        v = gv_ref[0, rows, :]
        total = jnp.zeros((RB, GLA_WIDTH), F32)
        for d in range(2):
            qe = qe_s[d, rows, :]
            qe4 = jnp.where(head_rows, jnp.concatenate([qe] * GLA_HEADS, axis=0), jnp.zeros((), BF16))
            att = lax.dot_general(qe4, ke_s[d, rows, :], nt, preferred_element_type=F32)
            att = jnp.where(visible[d], att, 0.0).astype(BF16)
            full = jnp.dot(att, v, preferred_element_type=F32)
            o = full[(GLA_HEADS - 1) * RB:]
            for h in range(GLA_HEADS - 2, -1, -1):
                o = jnp.where(ol == h, full[h * RB:(h + 1) * RB], o)
            inter = [lax.dot_general(qe[cc * C:(cc + 1) * C], prev_s[d, i * (RB // C) + cc], nt,
                                     preferred_element_type=F32) for cc in range(RB // C)]
            total = total + o + jnp.concatenate(inter, axis=0)
        o_s[rows, :] = total
        return carry

    lax.fori_loop(0, T // RB, outputs, 0, unroll=2)

    u_s[0:CONV_PAD, :] = jnp.zeros((CONV_PAD, CONV_WIDTH), F32)
    u_s[CONV_PAD + T:, :] = jnp.zeros((CONV_PAD, CONV_WIDTH), F32)
    u_s[CONV_PAD:CONV_PAD + T, :] = cc_ref[0].astype(F32) * ch_ref[0].astype(F32)
    w0 = cw_ref[0:1, :]
    w1 = cw_ref[1:2, :]
    w2 = cw_ref[2:3, :]
    for e in range(T // EPI_ROWS):
        r0 = e * EPI_ROWS
        o = o_s[r0:r0 + EPI_ROWS, :]
        ss = jnp.dot((o * o).astype(BF16), bd_ref[...], preferred_element_type=F32) * (1.0 / GLA_DV)
        on = o * lax.rsqrt(ss + EPS) * ng_ref[...]
        r = gr_ref[0, r0:r0 + EPI_ROWS, :].astype(F32)
        o_ref[0, r0:r0 + EPI_ROWS, 0:GLA_WIDTH] = (on * _silu(r)).astype(BF16)
        t = r0 + lax.broadcasted_iota(jnp.int32, (EPI_ROWS, 1), 0)
        up = u_s[CONV_PAD + r0 - 1:CONV_PAD + r0 - 1 + EPI_ROWS, :]
        mid = u_s[CONV_PAD + r0:CONV_PAD + r0 + EPI_ROWS, :]
        dn = u_s[CONV_PAD + r0 + 1:CONV_PAD + r0 + 1 + EPI_ROWS, :]
        up = jnp.where(t == n_lat, 0.0, up)
        dn = jnp.where(t == n_lat - 1, 0.0, dn)
        conv = w0 * up + w1 * mid + w2 * dn
        o_ref[0, r0:r0 + EPI_ROWS, GLA_WIDTH:] = (cb_ref[0, r0:r0 + EPI_ROWS, :].astype(F32) * conv).astype(BF16)


def _gla_cum_matrices():
    i = np.arange(GLA_BLOCK_A)
    same = (i[:, None] // GLA_CHUNK) == (i[None, :] // GLA_CHUNK)
    fwd = same & (i[None, :] <= i[:, None])
    bwd = same & (i[None, :] >= i[:, None])
    mats = np.stack([np.concatenate([fwd, same], axis=0), np.concatenate([bwd, same], axis=0)])
    return jnp.asarray(mats.astype(np.float32), dtype=BF16)


def _gla_conv(p, wgf, wgb, gbias, ng, cw, bd, cum, n_lat):
    B, T, _ = p.shape
    nc = T // GLA_CHUNK

    def col(width, start):
        return pl.BlockSpec((1, T, width), lambda b: (b, 0, start // width))

    def const(shape):
        return pl.BlockSpec(shape, lambda b: (0,) * len(shape))

    return pl.pallas_call(
        functools.partial(_gla_body, n_lat=n_lat, n_ctx=T - n_lat),
        grid=(B,),
        in_specs=[col(GLA_QK_WIDTH, COL_GQ), col(GLA_QK_WIDTH, COL_GK), col(GLA_WIDTH, COL_GV), col(GLA_WIDTH, COL_GR),
                  col(CONV_WIDTH, COL_CB), col(CONV_WIDTH, COL_CC), col(CONV_WIDTH, COL_CH), col(LANES, COL_GT),
                  const((LANES, GLA_QK_WIDTH)), const((LANES, GLA_QK_WIDTH)), const((8, GLA_QK_WIDTH)),
                  const((1, GLA_WIDTH)), const((8, CONV_WIDTH)), const((GLA_WIDTH, GLA_WIDTH)),
                  const((2, 2 * GLA_BLOCK_A, GLA_BLOCK_A))],
        out_specs=pl.BlockSpec((1, T, GLA_WIDTH + CONV_WIDTH), lambda b: (b, 0, 0)),
        out_shape=jax.ShapeDtypeStruct((B, T, GLA_WIDTH + CONV_WIDTH), BF16),
        scratch_shapes=[pltpu.VMEM((2, T, GLA_QK_WIDTH), BF16), pltpu.VMEM((2, T, GLA_QK_WIDTH), BF16),
                        pltpu.VMEM((2, T, GLA_QK_WIDTH), F32),
                        pltpu.VMEM((2, nc, GLA_WIDTH, GLA_QK_WIDTH), F32),
                        pltpu.VMEM((2, nc, GLA_WIDTH, GLA_QK_WIDTH), BF16),
                        pltpu.VMEM((2, GLA_WIDTH, GLA_QK_WIDTH), F32),
                        pltpu.VMEM((T, GLA_WIDTH), F32),
                        pltpu.VMEM((T + 2 * CONV_PAD, CONV_WIDTH), F32)],
        compiler_params=_cparams(1),
        name="gla_conv",
    )(p, p, p, p, p, p, p, p, wgf, wgb, gbias, ng, cw, bd, cum)


def _route(logits_t):
    mx = jnp.max(logits_t, axis=0, keepdims=True)
    ex = jnp.exp(logits_t - mx)
    probs = ex / jnp.sum(ex, axis=0, keepdims=True)
    P = [probs[e:e + 1] for e in range(N_EXPERTS)]
    scores = []
    for g in range(N_GROUPS):
        a, b, c, d = P[4 * g:4 * g + 4]
        scores.append(jnp.maximum(jnp.maximum(jnp.maximum(a + b, a + c), jnp.maximum(a + d, b + c)),
                                  jnp.maximum(b + d, c + d)))
    best = jnp.maximum(jnp.maximum(scores[0], scores[1]), jnp.maximum(scores[2], scores[3]))
    taken = jnp.zeros_like(best, dtype=jnp.bool_)
    sel = []
    for g in range(N_GROUPS):
        s = (scores[g] == best) & jnp.logical_not(taken)
        sel.append(s)
        taken = taken | s
    gsel = jnp.where(sel[1], 1.0, 0.0) + jnp.where(sel[2], 2.0, 0.0) + jnp.where(sel[3], 3.0, 0.0)
    ig = [jnp.where(sel[0], P[j], jnp.where(sel[1], P[4 + j], jnp.where(sel[2], P[8 + j], P[12 + j])))
          for j in range(EXPERTS_PER_GROUP)]

    def first_max(vals):
        v = jnp.maximum(jnp.maximum(vals[0], vals[1]), jnp.maximum(vals[2], vals[3]))
        tk = jnp.zeros_like(v, dtype=jnp.bool_)
        hot = []
        for x in vals:
            s = (x == v) & jnp.logical_not(tk)
            hot.append(s)
            tk = tk | s
        idx = jnp.where(hot[1], 1.0, 0.0) + jnp.where(hot[2], 2.0, 0.0) + jnp.where(hot[3], 3.0, 0.0)
        return v, hot, idx

    _, hot1, i1 = first_max(ig)
    _, _, i2 = first_max([jnp.where(hot1[j], -1.0, ig[j]) for j in range(EXPERTS_PER_GROUP)])
    lo = jnp.minimum(i1, i2)
    hi = jnp.maximum(i1, i2)
    pair = jnp.where(lo == 0.0, hi - 1.0, jnp.where(lo == 1.0, hi + 1.0, 5.0))
    return gsel * N_PAIRS + pair


def _class_rank(cls, tri_ref, cnt_s):
    n = cls.shape[1]
    cid = lax.broadcasted_iota(jnp.int32, (CLS_ROWS, n), 0).astype(F32)
    onehot = jnp.where(cls == cid, 1.0, 0.0)
    segs = [onehot[:, k * LANES:(k + 1) * LANES] for k in range(n // LANES)]
    before = jnp.dot(jnp.concatenate(segs, axis=0).astype(BF16), tri_ref[...], preferred_element_type=F32)
    base = cnt_s[...]
    ranks = []
    for k, seg in enumerate(segs):
        ranks.append(jnp.sum(seg * (before[k * CLS_ROWS:(k + 1) * CLS_ROWS] + base), axis=0, keepdims=True))
        base = base + jnp.sum(seg, axis=1, keepdims=True)
    cnt_s[...] = base
    return jnp.concatenate(ranks, axis=1)


def _out_body(ya_ref, yg_ref, x_ref, mod_ref, wo_ref, g_ref, wr_ref, br_ref, tri_ref, xo_ref, rt_ref, cnt_ref, cnt_s,
              logit_s, *, n_lat, tm):
    j = pl.program_id(1)

    @pl.when((pl.program_id(0) == 0) & (j == 0))
    def _():
        cnt_s[...] = jnp.zeros_like(cnt_s)

    mod = mod_ref[0]
    for r0 in range(0, tm, SUB_ROWS):
        rows = slice(r0, r0 + SUB_ROWS)
        row = j * tm + r0 + lax.broadcasted_iota(jnp.int32, (SUB_ROWS, 1), 0)
        is_ctx = row >= n_lat
        y = (jnp.dot(ya_ref[0, rows, :], wo_ref[0:ATT_WIDTH, :], preferred_element_type=F32)
             + jnp.dot(yg_ref[0, rows, :], wo_ref[ATT_WIDTH:, :], preferred_element_type=F32))
        xn = x_ref[0, rows, :] + _row_mod(mod, is_ctx, 2) * y
        xo_ref[0, rows, :] = xn
        h2 = _norm_modulate(xn, g_ref[...], mod, is_ctx, 3, 4).astype(BF16)
        logit_s[rows, :] = jnp.dot(h2, wr_ref[...], preferred_element_type=F32) + br_ref[0:1, :]
    cls = _route(logit_s[...].T[0:N_EXPERTS, :])
    rank = _class_rank(cls, tri_ref, cnt_s)
    rt_ref[0] = jnp.concatenate([cls, rank, jnp.zeros((6, tm), F32)], axis=0).astype(jnp.int32)
    cnt_ref[...] = jnp.broadcast_to(cnt_s[...], (CLS_ROWS, LANES)).astype(jnp.int32)


def _out_proj(ya, yg, xx, modv, wo, gain, wr, br, tri, n_lat, rows, tm):
    B, T, _ = xx.shape
    nj = rows // tm
    return pl.pallas_call(
        functools.partial(_out_body, n_lat=n_lat, tm=tm),
        grid=(B, nj),
        in_specs=[pl.BlockSpec((1, tm, ATT_WIDTH), lambda b, j: (b, j, 0)),
                  pl.BlockSpec((1, tm, GLA_WIDTH + CONV_WIDTH), lambda b, j: (b, j, 0)),
                  pl.BlockSpec((1, tm, D_MODEL), lambda b, j: (b, j, 0)),
                  pl.BlockSpec((1, 16, D_MODEL), lambda b, j: (b, 0, 0)),
                  pl.BlockSpec((D_MODEL, D_MODEL), lambda b, j: (0, 0)),
                  pl.BlockSpec((1, D_MODEL), lambda b, j: (0, 0)),
                  pl.BlockSpec((D_MODEL, LANES), lambda b, j: (0, 0)),
                  pl.BlockSpec((8, LANES), lambda b, j: (0, 0)),
                  pl.BlockSpec((LANES, LANES), lambda b, j: (0, 0))],
        out_specs=[pl.BlockSpec((1, tm, D_MODEL), lambda b, j: (b, j, 0)),
                   pl.BlockSpec((1, 8, tm), lambda b, j: (b * nj + j, 0, 0)),
                   pl.BlockSpec((CLS_ROWS, LANES), lambda b, j: (0, 0))],
        out_shape=[jax.ShapeDtypeStruct((B, rows, D_MODEL), F32),
                   jax.ShapeDtypeStruct((B * nj, 8, tm), jnp.int32),
                   jax.ShapeDtypeStruct((CLS_ROWS, LANES), jnp.int32)],
        scratch_shapes=[pltpu.VMEM((CLS_ROWS, 1), F32), pltpu.VMEM((tm, LANES), F32)],
        compiler_params=_cparams(2),
        name="out_proj_router",
    )(ya, yg, xx, modv, wo, gain, wr, br, tri)


ROW_UNROLL = 8
IDX_STRIDE = 1024


def _idx_slot(idx, slot, tm):
    return idx.at[pl.ds(pl.multiple_of(slot * IDX_STRIDE, IDX_STRIDE), tm)]


def _issue_rows(tm, idx, slot, make_copy):
    base = slot * IDX_STRIDE

    def trip(i, c):
        for u in range(ROW_UNROLL):
            r = i * ROW_UNROLL + u
            make_copy(r, idx[base + r]).start(priority=u % 2)
        return c

    lax.fori_loop(0, tm // ROW_UNROLL, trip, 0)


def _disp_body(x_ref, mod_ref, g_ref, dest_ref, hs_in_ref, hs_ref, buf, idx, sem_i, sem_d, *, n_lat, tm, nj,
               n_steps):
    del hs_in_ref
    j = pl.program_id(1)
    s = pl.program_id(0) * nj + j
    last = n_steps - 1
    slot = s % 2

    def idx_fetch(step, sl):
        return pltpu.make_async_copy(dest_ref.at[step], _idx_slot(idx, sl, tm), sem_i.at[sl])

    def drain(sl):
        pltpu.make_async_copy(buf.at[sl], hs_ref.at[pl.ds(0, tm)], sem_d.at[sl]).wait()

    @pl.when(s == 0)
    def _():
        idx_fetch(s, slot).start()

    row = j * tm + lax.broadcasted_iota(jnp.int32, (tm, 1), 0)
    is_ctx = row >= n_lat
    h2 = _norm_modulate(x_ref[0], g_ref[...], mod_ref[0], is_ctx, 3, 4)
    buf[slot] = h2.reshape(tm, 8, LANES)
    idx_fetch(s, slot).wait()
    _issue_rows(tm, idx, slot, lambda r, d: pltpu.make_async_copy(buf.at[slot, r], hs_ref.at[d], sem_d.at[slot]))

    @pl.when(s > 0)
    def _():
        drain(1 - slot)

    @pl.when(s < last)
    def _():
        idx_fetch(s + 1, 1 - slot).start()

    @pl.when(s == last)
    def _():
        drain(slot)


def _dispatch(xx, modv, gain, dest, hs0, n_lat, tm):
    B, rows, _ = xx.shape
    nj = rows // tm
    return pl.pallas_call(
        functools.partial(_disp_body, n_lat=n_lat, tm=tm, nj=nj, n_steps=B * nj),
        grid=(B, nj),
        in_specs=[pl.BlockSpec((1, tm, D_MODEL), lambda b, j: (b, j, 0)),
                  pl.BlockSpec((1, 16, D_MODEL), lambda b, j: (b, 0, 0)),
                  pl.BlockSpec((1, D_MODEL), lambda b, j: (0, 0)),
                  pl.BlockSpec(memory_space=pl.ANY),
                  pl.BlockSpec(memory_space=pl.ANY)],
        out_specs=pl.BlockSpec(memory_space=pl.ANY),
        scratch_shapes=[pltpu.VMEM((2, tm, 8, LANES), F32), pltpu.SMEM((2 * IDX_STRIDE,), jnp.int32),
                        pltpu.SemaphoreType.DMA((2,)), pltpu.SemaphoreType.DMA((2,))],
        out_shape=jax.ShapeDtypeStruct(hs0.shape, F32),
        input_output_aliases={4: 0},
        compiler_params=_cparams(2),
        name="dispatch",
    )(xx, modv, gain, dest, hs0)


def _moe_body(tile_ref, e_lo_ref, e_hi_ref, valid_ref, hs_ref, wg1, wu1, wd1, wg2, wu2, wd2, wr_ref, br_ref, ys_ref):
    g = pl.program_id(0)
    tm = hs_ref.shape[0]

    @pl.when(valid_ref[g] == 1)
    def _():
        x = hs_ref[...].reshape(tm, D_MODEL)
        e_lo = e_lo_ref[g]
        e_hi = e_hi_ref[g]
        dw = wr_ref[pl.ds(e_lo, 1), :] - wr_ref[pl.ds(e_hi, 1), :]
        d = jnp.sum(x * dw, axis=-1, keepdims=True) + (br_ref[e_lo] - br_ref[e_hi])
        w_lo = jax.nn.sigmoid(d)
        w_hi = jax.nn.sigmoid(-d)
        h = x.astype(BF16)

        def act(wg, wu, w):
            a = _silu(jnp.dot(h, wg[0], preferred_element_type=F32)) * jnp.dot(h, wu[0], preferred_element_type=F32)
            return (a * w).astype(BF16)

        y = (jnp.dot(act(wg1, wu1, w_lo), wd1[0], preferred_element_type=F32)
             + jnp.dot(act(wg2, wu2, w_hi), wd2[0], preferred_element_type=F32))
        ys_ref[...] = y.reshape(tm, 8, LANES)

    @pl.when(valid_ref[g] == 0)
    def _():
        ys_ref[...] = jnp.zeros_like(ys_ref)


def _moe(hs, tile, e_lo, e_hi, valid, wg, wu, wd, wr_t, br):
    n_tiles = tile.shape[0]
    tm = MOE_TILE

    def w_in(sel):
        return pl.BlockSpec((1, D_MODEL, D_EXPERT), lambda g, t, lo, hi, v: ((lo, hi)[sel][g], 0, 0))

    def w_out(sel):
        return pl.BlockSpec((1, D_EXPERT, D_MODEL), lambda g, t, lo, hi, v: ((lo, hi)[sel][g], 0, 0))

    return pl.pallas_call(
        _moe_body,
        grid_spec=pltpu.PrefetchScalarGridSpec(
            num_scalar_prefetch=4,
            grid=(n_tiles,),
            in_specs=[pl.BlockSpec((tm, 8, LANES), lambda g, t, lo, hi, v: (t[g], 0, 0)),
                      w_in(0), w_in(0), w_out(0), w_in(1), w_in(1), w_out(1),
                      pl.BlockSpec((N_EXPERTS, D_MODEL), lambda g, t, lo, hi, v: (0, 0)),
                      pl.BlockSpec(memory_space=pltpu.SMEM)],
            out_specs=pl.BlockSpec((tm, 8, LANES), lambda g, t, lo, hi, v: (g, 0, 0))),
        out_shape=jax.ShapeDtypeStruct((n_tiles * tm, 8, LANES), F32),
        compiler_params=_cparams(1),
        name="moe_pairs",
    )(tile, e_lo, e_hi, valid, hs, wg, wu, wd, wg, wu, wd, wr_t, br)


def _fin_body(x_ref, mod_ref, dest_ref, ys_ref, xo_ref, buf, idx, sem_i, sem_d, *, n_lat, tm, nj, n_steps):
    j = pl.program_id(1)
    s = pl.program_id(0) * nj + j
    last = n_steps - 1
    slot = s % 2

    def idx_fetch(step, sl):
        return pltpu.make_async_copy(dest_ref.at[step], _idx_slot(idx, sl, tm), sem_i.at[sl])

    def gather(sl):
        _issue_rows(tm, idx, sl, lambda r, d: pltpu.make_async_copy(ys_ref.at[d], buf.at[sl, r], sem_d.at[sl]))

    @pl.when(s == 0)
    def _():
        idx_fetch(0, 0).start()
        idx_fetch(0, 0).wait()
        gather(0)
        if last > 0:
            idx_fetch(1, 1).start()

    @pl.when(s < last)
    def _():
        idx_fetch(s + 1, 1 - slot).wait()
        gather(1 - slot)

    @pl.when(s + 2 <= last)
    def _():
        idx_fetch(s + 2, slot).start()

    pltpu.make_async_copy(ys_ref.at[pl.ds(0, tm)], buf.at[slot], sem_d.at[slot]).wait()
    row = j * tm + lax.broadcasted_iota(jnp.int32, (tm, 1), 0)
    is_ctx = row >= n_lat
    xo_ref[0] = x_ref[0] + _row_mod(mod_ref[0], is_ctx, 5) * buf[slot].reshape(tm, D_MODEL)


def _combine(xx, modv, dest, ys, n_lat, tm):
    B, rows, _ = xx.shape
    nj = rows // tm
    return pl.pallas_call(
        functools.partial(_fin_body, n_lat=n_lat, tm=tm, nj=nj, n_steps=B * nj),
        grid=(B, nj),
        in_specs=[pl.BlockSpec((1, tm, D_MODEL), lambda b, j: (b, j, 0)),
                  pl.BlockSpec((1, 16, D_MODEL), lambda b, j: (b, 0, 0)),
                  pl.BlockSpec(memory_space=pl.ANY),
                  pl.BlockSpec(memory_space=pl.ANY)],
        out_specs=pl.BlockSpec((1, tm, D_MODEL), lambda b, j: (b, j, 0)),
        scratch_shapes=[pltpu.VMEM((2, tm, 8, LANES), F32), pltpu.SMEM((2 * IDX_STRIDE,), jnp.int32),
                        pltpu.SemaphoreType.DMA((2,)), pltpu.SemaphoreType.DMA((2,))],
        out_shape=jax.ShapeDtypeStruct((B, rows, D_MODEL), F32),
        compiler_params=_cparams(2),
        name="combine",
    )(xx, modv, dest, ys)


def _rope_tables(n_lat, n_ctx):
    rows = n_lat // GRID_W
    row, col = jnp.meshgrid(jnp.arange(rows), jnp.arange(GRID_W), indexing="ij")
    n_freq = HEAD_DIM // 4
    inv_freq = ROPE_BASE ** (-jnp.arange(n_freq, dtype=F32) / n_freq)
    ang = jnp.concatenate([row.reshape(-1, 1).astype(F32) * inv_freq, col.reshape(-1, 1).astype(F32) * inv_freq],
                          axis=-1)
    cos = jnp.tile(jnp.cos(ang), (1, LANES // (HEAD_DIM // 2)))
    sin = jnp.tile(jnp.sin(ang), (1, LANES // (HEAD_DIM // 2)))
    sign = jnp.where((jnp.arange(LANES) % HEAD_DIM) < HEAD_DIM // 2, -1.0, 1.0).astype(F32)
    cos = jnp.concatenate([cos, jnp.ones((n_ctx, LANES), F32)], axis=0)
    sin = jnp.concatenate([sin * sign, jnp.zeros((n_ctx, LANES), F32)], axis=0)
    return cos, sin


def _block_diag_ones(n, blk):
    i = np.arange(n) // blk
    return jnp.asarray((i[:, None] == i[None, :]).astype(np.float32), dtype=BF16)


def _routing_tables(counts, n_tiles):
    tiles_c = (counts + MOE_TILE - 1) // MOE_TILE
    tile_end = jnp.cumsum(tiles_c)
    tile_start = tile_end - tiles_c
    off = tile_start * MOE_TILE
    total = tile_end[-1]
    g = jnp.arange(n_tiles, dtype=jnp.int32)
    valid = (g < total).astype(jnp.int32)
    g_eff = jnp.minimum(g, total - 1)
    c_of = jnp.sum((g_eff[:, None] >= tile_end[None, :]).astype(jnp.int32), axis=1)
    group = c_of // N_PAIRS
    pair = c_of % N_PAIRS
    lo = jnp.asarray(PAIR_LO, jnp.int32)
    hi = jnp.asarray(PAIR_HI, jnp.int32)
    e_lo = group * EXPERTS_PER_GROUP + jnp.sum((pair[:, None] == jnp.arange(N_PAIRS)[None, :]) * lo[None, :], axis=1)
    e_hi = group * EXPERTS_PER_GROUP + jnp.sum((pair[:, None] == jnp.arange(N_PAIRS)[None, :]) * hi[None, :], axis=1)
    return off.astype(jnp.int32), g_eff, e_lo.astype(jnp.int32), e_hi.astype(jnp.int32), valid


def kernel(x, c, ctx, c_ctx, w_ada, b_ada, norm_mix_g, norm_ffn_g, w_in, q_norm_g, k_norm_g, attn_sink, gla_gate_w,
           gla_gate_b, gla_norm_g, conv_w, w_out, w_router, b_router, w_gate_e, w_up_e, w_down_e):
    B, S, D = x.shape
    L = ctx.shape[1]
    T = S + L
    assert D == D_MODEL and T % TOKEN_TILE == 0 and S % LAT_TILE == 0 and S % GRID_W == 0
    assert S % ATT_BLOCK == 0 and L % ATT_BLOCK == 0 and S >= ATT_SPAN and T % EPI_ROWS == 0

    cond_rows = -(-(B + 1) // 8) * 8
    cond = jnp.zeros((cond_rows, D), F32).at[:B].set(c).at[B].set(c_ctx)
    mod_all = _modulation(cond, w_ada, b_ada)

    cos, sin = _rope_tables(S, L)
    bd_head = _block_diag_ones(LANES, HEAD_DIM)
    bd_gla = _block_diag_ones(GLA_WIDTH, GLA_DV)
    cum = _gla_cum_matrices()
    tri = jnp.asarray(np.triu(np.ones((LANES, LANES), np.float32), 1), dtype=BF16)
    xx = jnp.concatenate([x, ctx], axis=1)

    for l in range(DEPTH):
        last = l == DEPTH - 1
        m_lat = mod_all[l, :B].reshape(B, 6, D)
        m_ctx = jnp.broadcast_to(mod_all[l, B].reshape(1, 6, D), (B, 6, D))
        modv = jnp.concatenate([m_lat, m_ctx, jnp.zeros((B, 4, D), F32)], axis=1)

        wl = w_in[l]
        order = jnp.asarray(ATT_HEAD_ORDER)
        wq = wl[:, :ATT_WIDTH].reshape(D, ATT_HEADS, HEAD_DIM)[:, order, :].reshape(D, ATT_WIDTH)
        w_perm = jnp.concatenate([wq, wl[:, ATT_WIDTH:1536], wl[:, 1568:], wl[:, 1536:1568],
                                  jnp.zeros((D, N_PROJ - wl.shape[1]), F32)], axis=1).astype(BF16)
        qg = jnp.tile(q_norm_g[l], LANES // HEAD_DIM) * (HEAD_DIM ** -0.5)
        kg = jnp.tile(k_norm_g[l], LANES // HEAD_DIM)
        qkg = jnp.stack([qg] * (ATT_WIDTH // LANES) + [kg] + [jnp.zeros_like(kg)] * 3)
        p = _in_proj(xx, modv, norm_mix_g[l].reshape(1, D), w_perm, cos, sin, qkg, bd_head, S)

        y_att = _attention(p, attn_sink[l], S, not last)

        pad_rows = jnp.zeros((LANES - 2 * GLA_GATE_RANK, GLA_QK_WIDTH), F32)
        zero_rank = jnp.zeros((GLA_GATE_RANK, GLA_QK_WIDTH), F32)
        wgf = jnp.concatenate([gla_gate_w[l, 0], zero_rank, pad_rows], axis=0).astype(BF16)
        wgb = jnp.concatenate([zero_rank, gla_gate_w[l, 1], pad_rows], axis=0).astype(BF16)
        gbias = jnp.concatenate([gla_gate_b[l], jnp.zeros((6, GLA_QK_WIDTH), F32)], axis=0)
        ng = jnp.tile(gla_norm_g[l], GLA_HEADS).reshape(1, GLA_WIDTH)
        cw = jnp.concatenate([conv_w[l], jnp.zeros((5, CONV_WIDTH), F32)], axis=0)
        y_gc = _gla_conv(p, wgf, wgb, gbias, ng, cw, bd_gla, cum, S)

        rows, tm = (S, LAT_TILE) if last else (T, TOKEN_TILE)
        wr = jnp.concatenate([w_router, jnp.zeros((D, LANES - N_EXPERTS), F32)], axis=1).astype(BF16)
        br = jnp.zeros((8, LANES), F32).at[0, :N_EXPERTS].set(b_router)
        ffn_g = norm_ffn_g[l].reshape(1, D)
        wo_att = w_out[l, :ATT_WIDTH].reshape(ATT_HEADS, HEAD_DIM, D)[order].reshape(ATT_WIDTH, D)
        wo = jnp.concatenate([wo_att, w_out[l, ATT_WIDTH:]], axis=0).astype(BF16)
        xx_mid, route, counts = _out_proj(y_att, y_gc, xx, modv, wo, ffn_g, wr, br, tri, S, rows, tm)

        n_tiles = -(-(B * rows) // MOE_TILE) + N_CLASSES
        off, tile, e_lo, e_hi, valid = _routing_tables(counts[:N_CLASSES, 0], n_tiles)

        dest = jnp.take(off, route[:, 0, :]) + route[:, 1, :]
        hs = _dispatch(xx_mid, modv, ffn_g, dest, jnp.zeros((n_tiles * MOE_TILE, 8, LANES), F32), S, tm)
        ys = _moe(hs, tile, e_lo, e_hi, valid, w_gate_e[l].astype(BF16), w_up_e[l].astype(BF16),
                  w_down_e[l].astype(BF16), w_router.T, b_router)
        xx = _combine(xx_mid, modv, dest, ys, S, tm)
    return xx
```

```python
import functools

import numpy as np
import jax
import jax.numpy as jnp
from jax import lax
from jax.experimental import pallas as pl
from jax.experimental.pallas import tpu as pltpu

D_MODEL = 1024
DEPTH = 2
GRID_W = 64
EPS = 1e-6
HEAD_DIM = 64
ATT_HEADS = 8
ATT_KV_HEADS = 2
ATT_GROUP = ATT_HEADS // ATT_KV_HEADS
ATT_WIDTH = ATT_HEADS * HEAD_DIM
WINDOW = 128
ROPE_BASE = 10000.0
GLA_HEADS = 4
GLA_DV = 64
GLA_DK = 32
GLA_WIDTH = GLA_HEADS * GLA_DV
GLA_GATE_RANK = 16
GLA_GATE_NORM = 16.0
GLA_CHUNK = 64
CONV_WIDTH = 256
N_EXPERTS = 16
N_GROUPS = 4
EXPERTS_PER_GROUP = 4
D_EXPERT = D_MODEL // 2

LANES = 128
KV_WIDTH = ATT_KV_HEADS * HEAD_DIM
GLA_QK_WIDTH = GLA_HEADS * GLA_DK
COL_AQ, COL_AK, COL_AV = 0, 512, 640
COL_GQ, COL_GK, COL_GV, COL_GR = 768, 896, 1024, 1280
COL_CB, COL_CC, COL_CH, COL_GT = 1536, 1792, 2048, 2304
N_PROJ = 2432
QK_COLS = COL_AV
N_PAIRS = 6
N_CLASSES = N_GROUPS * N_PAIRS
PAIR_LO = (0, 0, 0, 1, 1, 2)
PAIR_HI = (1, 2, 3, 2, 3, 3)
CLS_ROWS = 32
NEG = -1e30

TOKEN_TILE = 768
LAT_TILE = 1024
MOE_TILE = 512
SUB_ROWS = 256
VMEM_LIMIT = 56 * 1024 * 1024

F32 = jnp.float32
BF16 = jnp.bfloat16


def _cparams(n_axes):
    return pltpu.CompilerParams(dimension_semantics=("arbitrary",) * n_axes, vmem_limit_bytes=VMEM_LIMIT)


def _silu(x):
    return x * jax.nn.sigmoid(x)


def _mod_body(c_ref, w_ref, b_ref, o_ref):
    c = c_ref[...]
    a = _silu(c).astype(BF16)
    o_ref[0] = jnp.dot(a, w_ref[0].astype(BF16), preferred_element_type=F32) + b_ref[0]


def _modulation(cond, w_ada, b_ada):
    rows = cond.shape[0]
    nblk = w_ada.shape[2] // D_MODEL
    return pl.pallas_call(
        _mod_body,
        grid=(DEPTH, nblk),
        in_specs=[pl.BlockSpec((rows, D_MODEL), lambda l, n: (0, 0)),
                  pl.BlockSpec((1, D_MODEL, D_MODEL), lambda l, n: (l, 0, n)),
                  pl.BlockSpec((1, 1, D_MODEL), lambda l, n: (l, 0, n))],
        out_specs=pl.BlockSpec((1, rows, D_MODEL), lambda l, n: (l, 0, n)),
        out_shape=jax.ShapeDtypeStruct((DEPTH, rows, w_ada.shape[2]), F32),
        compiler_params=_cparams(2),
        name="modulation",
    )(cond, w_ada, b_ada.reshape(DEPTH, 1, -1))


def _row_mod(mod, is_ctx, i):
    return jnp.where(is_ctx, mod[6 + i:7 + i], mod[i:i + 1])


def _norm_modulate(x, gain, mod, is_ctx, i_shift, i_scale):
    ms = jnp.mean(x * x, axis=-1, keepdims=True)
    xn = x * lax.rsqrt(ms + EPS) * gain
    return xn * (1.0 + _row_mod(mod, is_ctx, i_scale)) + _row_mod(mod, is_ctx, i_shift)


def _in_body(x_ref, mod_ref, g_ref, w_ref, cos_ref, sin_ref, qkg_ref, bd_ref, o_ref, *, n_lat, tm):
    j = pl.program_id(0)
    lane = lax.broadcasted_iota(jnp.int32, (1, LANES), 1)
    first_half = (lane % HEAD_DIM) < (HEAD_DIM // 2)
    for r0 in range(0, tm, SUB_ROWS):
        rows = slice(r0, r0 + SUB_ROWS)
        row = j * tm + r0 + lax.broadcasted_iota(jnp.int32, (SUB_ROWS, 1), 0)
        is_ctx = row >= n_lat
        h = _norm_modulate(x_ref[0, rows, :], g_ref[...], mod_ref[0], is_ctx, 0, 1).astype(BF16)
        qk = jnp.dot(h, w_ref[:, :QK_COLS], preferred_element_type=F32)
        cos = cos_ref[rows, :]
        sin = sin_ref[rows, :]
        for c in range(QK_COLS // LANES):
            xc = qk[:, c * LANES:(c + 1) * LANES]
            ss = jnp.dot((xc * xc).astype(BF16), bd_ref[...], preferred_element_type=F32) * (1.0 / HEAD_DIM)
            xc = xc * lax.rsqrt(ss + EPS) * qkg_ref[c:c + 1, :]
            rot = jnp.where(first_half, pltpu.roll(xc, LANES - HEAD_DIM // 2, 1), pltpu.roll(xc, HEAD_DIM // 2, 1))
            o_ref[0, rows, c * LANES:(c + 1) * LANES] = (xc * cos + rot * sin).astype(BF16)
        o_ref[0, rows, QK_COLS:] = jnp.dot(h, w_ref[:, QK_COLS:], preferred_element_type=F32).astype(BF16)


def _in_proj(xx, modv, gain, w, cos, sin, qkg, bd, n_lat):
    B, T, _ = xx.shape
    tm = TOKEN_TILE
    return pl.pallas_call(
        functools.partial(_in_body, n_lat=n_lat, tm=tm),
        grid=(T // tm, B),
        in_specs=[pl.BlockSpec((1, tm, D_MODEL), lambda j, b: (b, j, 0)),
                  pl.BlockSpec((1, 16, D_MODEL), lambda j, b: (b, 0, 0)),
                  pl.BlockSpec((1, D_MODEL), lambda j, b: (0, 0)),
                  pl.BlockSpec((D_MODEL, N_PROJ), lambda j, b: (0, 0)),
                  pl.BlockSpec((tm, LANES), lambda j, b: (j, 0)),
                  pl.BlockSpec((tm, LANES), lambda j, b: (j, 0)),
                  pl.BlockSpec((8, LANES), lambda j, b: (0, 0)),
                  pl.BlockSpec((LANES, LANES), lambda j, b: (0, 0))],
        out_specs=pl.BlockSpec((1, tm, N_PROJ), lambda j, b: (b, j, 0)),
        out_shape=jax.ShapeDtypeStruct((B, T, N_PROJ), BF16),
        compiler_params=_cparams(2),
        name="in_proj",
    )(xx, modv, gain, w, cos, sin, qkg, bd)


ATT_BLOCK = 128
ATT_SPAN = ATT_BLOCK + 2 * WINDOW


ATT_HEAD_ORDER = (0, 4, 1, 5, 2, 6, 3, 7)


def _attend(qblk, k_parts, v_parts, biases, sinks):
    rows = ATT_GROUP * ATT_BLOCK
    rowi = lax.broadcasted_iota(jnp.int32, (rows, 1), 0)
    lane = lax.broadcasted_iota(jnp.int32, (1, LANES), 1)
    lower = lane < HEAD_DIM
    nt = (((1,), (1,)), ((), ()))
    normed = []
    for h in range(ATT_KV_HEADS):
        keep = lower if h == 0 else jnp.logical_not(lower)
        qs = jnp.concatenate([jnp.where(keep, qblk[:, g * LANES:(g + 1) * LANES], jnp.zeros((), BF16))
                              for g in range(ATT_GROUP)], axis=0)
        sink = jnp.full((rows, 1), sinks[ATT_GROUP * h + ATT_GROUP - 1], F32)
        for g in range(ATT_GROUP - 2, -1, -1):
            sink = jnp.where(rowi < (g + 1) * ATT_BLOCK, sinks[ATT_GROUP * h + g], sink)
        scores = []
        for k in k_parts:
            s = lax.dot_general(qs, k, nt, preferred_element_type=F32)
            scores += [s[:, c * LANES:(c + 1) * LANES] for c in range(k.shape[0] // LANES)]
        scores = [s if b is None else s + b for s, b in zip(scores, biases)]
        m = scores[0]
        for s in scores[1:]:
            m = jnp.maximum(m, s)
        m = jnp.maximum(jnp.max(m, axis=-1, keepdims=True), sink)
        probs = [jnp.exp((s - m).astype(BF16)) for s in scores]
        acc = jnp.where(keep, 0.0, jnp.exp(sink - m))
        c0 = 0
        for v in v_parts[h]:
            n = v.shape[0] // LANES
            acc = acc + jnp.dot(jnp.concatenate(probs[c0:c0 + n], axis=1), v, preferred_element_type=F32)
            c0 += n
        normed.append(acc / pltpu.roll(acc, HEAD_DIM, 1))
    return jnp.concatenate([jnp.where(lower, normed[0][g * ATT_BLOCK:(g + 1) * ATT_BLOCK],
                                      normed[1][g * ATT_BLOCK:(g + 1) * ATT_BLOCK]) for g in range(ATT_GROUP)], axis=1)


def _att_body(sink_ref, q_ref, k_ref, v_ref, o_ref, v1_s, *, n_lat, n_ctx, with_ctx_out):
    sinks = [sink_ref[i] for i in range(ATT_HEADS)]
    lane = lax.broadcasted_iota(jnp.int32, (1, LANES), 1)
    vv = v_ref[0]
    v1_s[0] = jnp.where(lane < HEAD_DIM, vv, jnp.ones((), BF16))
    v1_s[1] = jnp.where(lane < HEAD_DIM, jnp.ones((), BF16), vv)
    k_ctx = k_ref[0, n_lat:n_lat + n_ctx, :]
    v_ctx = [v1_s[h, n_lat:n_lat + n_ctx, :] for h in range(ATT_KV_HEADS)]
    no_bias = [None] * (n_ctx // LANES)
    qi = lax.broadcasted_iota(jnp.int32, (ATT_GROUP * ATT_BLOCK, LANES), 0) % ATT_BLOCK
    ki = lax.broadcasted_iota(jnp.int32, (ATT_GROUP * ATT_BLOCK, LANES), 1)
    past_ok = jnp.where(ki >= qi, 0.0, NEG)
    ahead_ok = jnp.where(ki <= qi, 0.0, NEG)

    def block(q0, k0, n_keys, biases):
        k_parts = [k_ref[0, pl.ds(k0, n_keys), :], k_ctx]
        v_parts = [[v1_s[h, pl.ds(k0, n_keys), :], v_ctx[h]] for h in range(ATT_KV_HEADS)]
        out = _attend(q_ref[0, pl.ds(q0, ATT_BLOCK), :], k_parts, v_parts, biases + no_bias, sinks)
        o_ref[0, pl.ds(q0, ATT_BLOCK), :] = out.astype(BF16)

    def interior(i, carry):
        q0 = pl.multiple_of(i * ATT_BLOCK, ATT_BLOCK)
        block(q0, pl.multiple_of(q0 - WINDOW, ATT_BLOCK), ATT_SPAN, [past_ok, None, ahead_ok])
        return carry

    nq = n_lat // ATT_BLOCK
    block(0, 0, 2 * ATT_BLOCK, [None, ahead_ok])
    lax.fori_loop(1, nq - 1, interior, 0, unroll=2)
    block(n_lat - ATT_BLOCK, n_lat - 2 * ATT_BLOCK, 2 * ATT_BLOCK, [past_ok, None])
    if with_ctx_out:
        for c in range(n_ctx // ATT_BLOCK):
            r0 = n_lat + c * ATT_BLOCK
            out = _attend(q_ref[0, r0:r0 + ATT_BLOCK, :], [k_ctx], [[v_ctx[h]] for h in range(ATT_KV_HEADS)],
                          no_bias, sinks)
            o_ref[0, r0:r0 + ATT_BLOCK, :] = out.astype(BF16)
    else:
        o_ref[0, n_lat:, :] = jnp.zeros((n_ctx, ATT_WIDTH), BF16)


def _attention(p, sinks, n_lat, with_ctx_out):
    B, T, _ = p.shape
    return pl.pallas_call(
        functools.partial(_att_body, n_lat=n_lat, n_ctx=T - n_lat, with_ctx_out=with_ctx_out),
        grid=(B,),
        in_specs=[pl.BlockSpec(memory_space=pltpu.SMEM),
                  pl.BlockSpec((1, T, ATT_WIDTH), lambda b: (b, 0, COL_AQ // ATT_WIDTH)),
                  pl.BlockSpec((1, T, KV_WIDTH), lambda b: (b, 0, COL_AK // KV_WIDTH)),
                  pl.BlockSpec((1, T, KV_WIDTH), lambda b: (b, 0, COL_AV // KV_WIDTH))],
        out_specs=pl.BlockSpec((1, T, ATT_WIDTH), lambda b: (b, 0, 0)),
        out_shape=jax.ShapeDtypeStruct((B, T, ATT_WIDTH), BF16),
        scratch_shapes=[pltpu.VMEM((ATT_KV_HEADS, T, KV_WIDTH), BF16)],
        compiler_params=_cparams(1),
        name="attention",
    )(sinks, p, p, p)


GLA_BLOCK_A = 256
GLA_BLOCK_B = 128
CONV_PAD = 8
EPI_ROWS = 256


def _log_sigmoid(z):
    return jnp.minimum(z, 0.0) - jnp.log1p(jnp.exp(-jnp.abs(z)))


def _gla_body(gq_ref, gk_ref, gv_ref, gr_ref, cb_ref, cc_ref, ch_ref, gt_ref, wgf_ref, wgb_ref, gbias_ref, ng_ref,
              cw_ref, bd_ref, cum_ref, o_ref, qe_s, ke_s, dec_s, upd_s, prev_s, st_s, o_s, u_s, *, n_lat, n_ctx):
    T = n_lat + n_ctx
    C = GLA_CHUNK
    nt = (((1,), (1,)), ((), ()))
    tn = (((0,), (0,)), ((), ()))

    sr = lax.broadcasted_iota(jnp.int32, (GLA_WIDTH, GLA_QK_WIDTH), 0) // GLA_DV
    sl = lax.broadcasted_iota(jnp.int32, (GLA_WIDTH, GLA_QK_WIDTH), 1) // GLA_DK
    state_mask = sr == sl

    def factors(i, carry):
        r0 = pl.multiple_of(i * GLA_BLOCK_A, GLA_BLOCK_A)
        rows = pl.ds(r0, GLA_BLOCK_A)
        gt = gt_ref[0, rows, :]
        q = gq_ref[0, rows, :].astype(F32) * (GLA_DK ** -0.5)
        k = gk_ref[0, rows, :].astype(F32)
        v = gv_ref[0, rows, :]
        for d, wg_ref in enumerate((wgf_ref, wgb_ref)):
            z = jnp.dot(gt, wg_ref[...], preferred_element_type=F32) + gbias_ref[d:d + 1, :]
            g = _log_sigmoid(z) * (1.0 / GLA_GATE_NORM)
            g_hi = g.astype(BF16)
            g_lo = (g - g_hi.astype(F32)).astype(BF16)
            cum = jnp.dot(cum_ref[d], jnp.concatenate([g_hi, g_lo], axis=1), preferred_element_type=F32)
            b = cum[:GLA_BLOCK_A, :GLA_QK_WIDTH] + cum[:GLA_BLOCK_A, GLA_QK_WIDTH:]
            tot = cum[GLA_BLOCK_A:, :GLA_QK_WIDTH] + cum[GLA_BLOCK_A:, GLA_QK_WIDTH:]
            qe_s[d, rows, :] = (q * jnp.exp(b)).astype(BF16)
            ke_s[d, rows, :] = (k * jnp.exp(-b)).astype(BF16)
            dec_s[d, rows, :] = jnp.exp(tot)
            kl = (k * jnp.exp(tot - b)).astype(BF16)
            for cc in range(GLA_BLOCK_A // C):
                upd = lax.dot_general(v[cc * C:(cc + 1) * C], kl[cc * C:(cc + 1) * C], tn, preferred_element_type=F32)
                upd_s[d, i * (GLA_BLOCK_A // C) + cc] = jnp.where(state_mask, upd, 0.0)
        return carry

    lax.fori_loop(0, T // GLA_BLOCK_A, factors, 0, unroll=3)

    nc_lat = n_lat // C
    nc_ctx = n_ctx // C
    st_s[...] = jnp.zeros_like(st_s)

    def scan(i, carry):
        in_ctx = i < nc_ctx
        cf = jnp.where(in_ctx, nc_lat + i, i - nc_ctx)
        cb = jnp.where(in_ctx, nc_lat + nc_ctx - 1 - i, nc_lat - 1 - (i - nc_ctx))
        for d, cid in enumerate((cf, cb)):
            st = st_s[d]
            prev_s[d, cid] = st.astype(BF16)
            st_s[d] = st * dec_s[d, pl.ds(pl.multiple_of(cid * C, C), 1), :] + upd_s[d, cid]
        return carry

    lax.fori_loop(0, nc_lat + nc_ctx, scan, 0)

    RB = GLA_BLOCK_B
    hr = lax.broadcasted_iota(jnp.int32, (GLA_HEADS * RB, GLA_QK_WIDTH), 0) // RB
    hl = lax.broadcasted_iota(jnp.int32, (GLA_HEADS * RB, GLA_QK_WIDTH), 1) // GLA_DK
    head_rows = hr == hl
    qr = lax.broadcasted_iota(jnp.int32, (GLA_HEADS * RB, RB), 0) % RB
    kc = lax.broadcasted_iota(jnp.int32, (GLA_HEADS * RB, RB), 1)
    same_chunk = (qr // C) == (kc // C)
    visible = (same_chunk & (kc <= qr), same_chunk & (kc > qr))
    ol = lax.broadcasted_iota(jnp.int32, (RB, GLA_WIDTH), 1) // GLA_DV

    def outputs(i, carry):
        r0 = pl.multiple_of(i * RB, RB)
        rows = pl.ds(r0, RB)
        v = gv_ref[0, rows, :]
        total = jnp.zeros((RB, GLA_WIDTH), F32)
        for d in range(2):
            qe = qe_s[d, rows, :]
            qe4 = jnp.where(head_rows, jnp.concatenate([qe] * GLA_HEADS, axis=0), jnp.zeros((), BF16))
            att = lax.dot_general(qe4, ke_s[d, rows, :], nt, preferred_element_type=F32)
            att = jnp.where(visible[d], att, 0.0).astype(BF16)
            full = jnp.dot(att, v, preferred_element_type=F32)
            o = full[(GLA_HEADS - 1) * RB:]
            for h in range(GLA_HEADS - 2, -1, -1):
                o = jnp.where(ol == h, full[h * RB:(h + 1) * RB], o)
            inter = [lax.dot_general(qe[cc * C:(cc + 1) * C], prev_s[d, i * (RB // C) + cc], nt,
                                     preferred_element_type=F32) for cc in range(RB // C)]
            total = total + o + jnp.concatenate(inter, axis=0)
        o_s[rows, :] = total
        return carry

    lax.fori_loop(0, T // RB, outputs, 0, unroll=2)

    u_s[0:CONV_PAD, :] = jnp.zeros((CONV_PAD, CONV_WIDTH), F32)
    u_s[CONV_PAD + T:, :] = jnp.zeros((CONV_PAD, CONV_WIDTH), F32)
    u_s[CONV_PAD:CONV_PAD + T, :] = cc_ref[0].astype(F32) * ch_ref[0].astype(F32)
    w0 = cw_ref[0:1, :]
    w1 = cw_ref[1:2, :]
    w2 = cw_ref[2:3, :]
    for e in range(T // EPI_ROWS):
        r0 = e * EPI_ROWS
        o = o_s[r0:r0 + EPI_ROWS, :]
        ss = jnp.dot((o * o).astype(BF16), bd_ref[...], preferred_element_type=F32) * (1.0 / GLA_DV)
        on = o * lax.rsqrt(ss + EPS) * ng_ref[...]
        r = gr_ref[0, r0:r0 + EPI_ROWS, :].astype(F32)
        o_ref[0, r0:r0 + EPI_ROWS, 0:GLA_WIDTH] = (on * _silu(r)).astype(BF16)
        t = r0 + lax.broadcasted_iota(jnp.int32, (EPI_ROWS, 1), 0)
        up = u_s[CONV_PAD + r0 - 1:CONV_PAD + r0 - 1 + EPI_ROWS, :]
        mid = u_s[CONV_PAD + r0:CONV_PAD + r0 + EPI_ROWS, :]
        dn = u_s[CONV_PAD + r0 + 1:CONV_PAD + r0 + 1 + EPI_ROWS, :]
        up = jnp.where(t == n_lat, 0.0, up)
        dn = jnp.where(t == n_lat - 1, 0.0, dn)
        conv = w0 * up + w1 * mid + w2 * dn
        o_ref[0, r0:r0 + EPI_ROWS, GLA_WIDTH:] = (cb_ref[0, r0:r0 + EPI_ROWS, :].astype(F32) * conv).astype(BF16)


def _gla_cum_matrices():
    i = np.arange(GLA_BLOCK_A)
    same = (i[:, None] // GLA_CHUNK) == (i[None, :] // GLA_CHUNK)
    fwd = same & (i[None, :] <= i[:, None])
    bwd = same & (i[None, :] >= i[:, None])
    mats = np.stack([np.concatenate([fwd, same], axis=0), np.concatenate([bwd, same], axis=0)])
    return jnp.asarray(mats.astype(np.float32), dtype=BF16)


def _gla_conv(p, wgf, wgb, gbias, ng, cw, bd, cum, n_lat):
    B, T, _ = p.shape
    nc = T // GLA_CHUNK

    def col(width, start):
        return pl.BlockSpec((1, T, width), lambda b: (b, 0, start // width))

    def const(shape):
        return pl.BlockSpec(shape, lambda b: (0,) * len(shape))

    return pl.pallas_call(
        functools.partial(_gla_body, n_lat=n_lat, n_ctx=T - n_lat),
        grid=(B,),
        in_specs=[col(GLA_QK_WIDTH, COL_GQ), col(GLA_QK_WIDTH, COL_GK), col(GLA_WIDTH, COL_GV), col(GLA_WIDTH, COL_GR),
                  col(CONV_WIDTH, COL_CB), col(CONV_WIDTH, COL_CC), col(CONV_WIDTH, COL_CH), col(LANES, COL_GT),
                  const((LANES, GLA_QK_WIDTH)), const((LANES, GLA_QK_WIDTH)), const((8, GLA_QK_WIDTH)),
                  const((1, GLA_WIDTH)), const((8, CONV_WIDTH)), const((GLA_WIDTH, GLA_WIDTH)),
                  const((2, 2 * GLA_BLOCK_A, GLA_BLOCK_A))],
        out_specs=pl.BlockSpec((1, T, GLA_WIDTH + CONV_WIDTH), lambda b: (b, 0, 0)),
        out_shape=jax.ShapeDtypeStruct((B, T, GLA_WIDTH + CONV_WIDTH), BF16),
        scratch_shapes=[pltpu.VMEM((2, T, GLA_QK_WIDTH), BF16), pltpu.VMEM((2, T, GLA_QK_WIDTH), BF16),
                        pltpu.VMEM((2, T, GLA_QK_WIDTH), F32),
                        pltpu.VMEM((2, nc, GLA_WIDTH, GLA_QK_WIDTH), F32),
                        pltpu.VMEM((2, nc, GLA_WIDTH, GLA_QK_WIDTH), BF16),
                        pltpu.VMEM((2, GLA_WIDTH, GLA_QK_WIDTH), F32),
                        pltpu.VMEM((T, GLA_WIDTH), F32),
                        pltpu.VMEM((T + 2 * CONV_PAD, CONV_WIDTH), F32)],
        compiler_params=_cparams(1),
        name="gla_conv",
    )(p, p, p, p, p, p, p, p, wgf, wgb, gbias, ng, cw, bd, cum)


def _route(logits_t):
    mx = jnp.max(logits_t, axis=0, keepdims=True)
    ex = jnp.exp(logits_t - mx)
    probs = ex / jnp.sum(ex, axis=0, keepdims=True)
    P = [probs[e:e + 1] for e in range(N_EXPERTS)]
    scores = []
    for g in range(N_GROUPS):
        a, b, c, d = P[4 * g:4 * g + 4]
        scores.append(jnp.maximum(jnp.maximum(jnp.maximum(a + b, a + c), jnp.maximum(a + d, b + c)),
                                  jnp.maximum(b + d, c + d)))
    best = jnp.maximum(jnp.maximum(scores[0], scores[1]), jnp.maximum(scores[2], scores[3]))
    taken = jnp.zeros_like(best, dtype=jnp.bool_)
    sel = []
    for g in range(N_GROUPS):
        s = (scores[g] == best) & jnp.logical_not(taken)
        sel.append(s)
        taken = taken | s
    gsel = jnp.where(sel[1], 1.0, 0.0) + jnp.where(sel[2], 2.0, 0.0) + jnp.where(sel[3], 3.0, 0.0)
    ig = [jnp.where(sel[0], P[j], jnp.where(sel[1], P[4 + j], jnp.where(sel[2], P[8 + j], P[12 + j])))
          for j in range(EXPERTS_PER_GROUP)]

    def first_max(vals):
        v = jnp.maximum(jnp.maximum(vals[0], vals[1]), jnp.maximum(vals[2], vals[3]))
        tk = jnp.zeros_like(v, dtype=jnp.bool_)
        hot = []
        for x in vals:
            s = (x == v) & jnp.logical_not(tk)
            hot.append(s)
            tk = tk | s
        idx = jnp.where(hot[1], 1.0, 0.0) + jnp.where(hot[2], 2.0, 0.0) + jnp.where(hot[3], 3.0, 0.0)
        return v, hot, idx

    _, hot1, i1 = first_max(ig)
    _, _, i2 = first_max([jnp.where(hot1[j], -1.0, ig[j]) for j in range(EXPERTS_PER_GROUP)])
    lo = jnp.minimum(i1, i2)
    hi = jnp.maximum(i1, i2)
    pair = jnp.where(lo == 0.0, hi - 1.0, jnp.where(lo == 1.0, hi + 1.0, 5.0))
    return gsel * N_PAIRS + pair


def _class_rank(cls, tri_ref, cnt_s):
    n = cls.shape[1]
    cid = lax.broadcasted_iota(jnp.int32, (CLS_ROWS, n), 0).astype(F32)
    onehot = jnp.where(cls == cid, 1.0, 0.0)
    segs = [onehot[:, k * LANES:(k + 1) * LANES] for k in range(n // LANES)]
    before = jnp.dot(jnp.concatenate(segs, axis=0).astype(BF16), tri_ref[...], preferred_element_type=F32)
    base = cnt_s[...]
    ranks = []
    for k, seg in enumerate(segs):
        ranks.append(jnp.sum(seg * (before[k * CLS_ROWS:(k + 1) * CLS_ROWS] + base), axis=0, keepdims=True))
        base = base + jnp.sum(seg, axis=1, keepdims=True)
    cnt_s[...] = base
    return jnp.concatenate(ranks, axis=1)


def _out_body(ya_ref, yg_ref, x_ref, mod_ref, wo_ref, g_ref, wr_ref, br_ref, tri_ref, xo_ref, rt_ref, cnt_ref, cnt_s,
              logit_s, *, n_lat, tm):
    j = pl.program_id(1)

    @pl.when((pl.program_id(0) == 0) & (j == 0))
    def _():
        cnt_s[...] = jnp.zeros_like(cnt_s)

    mod = mod_ref[0]
    for r0 in range(0, tm, SUB_ROWS):
        rows = slice(r0, r0 + SUB_ROWS)
        row = j * tm + r0 + lax.broadcasted_iota(jnp.int32, (SUB_ROWS, 1), 0)
        is_ctx = row >= n_lat
        y = (jnp.dot(ya_ref[0, rows, :], wo_ref[0:ATT_WIDTH, :], preferred_element_type=F32)
             + jnp.dot(yg_ref[0, rows, :], wo_ref[ATT_WIDTH:, :], preferred_element_type=F32))
        xn = x_ref[0, rows, :] + _row_mod(mod, is_ctx, 2) * y
        xo_ref[0, rows, :] = xn
        h2 = _norm_modulate(xn, g_ref[...], mod, is_ctx, 3, 4).astype(BF16)
        logit_s[rows, :] = jnp.dot(h2, wr_ref[...], preferred_element_type=F32) + br_ref[0:1, :]
    cls = _route(logit_s[...].T[0:N_EXPERTS, :])
    rank = _class_rank(cls, tri_ref, cnt_s)
    rt_ref[0] = jnp.concatenate([cls, rank, jnp.zeros((6, tm), F32)], axis=0).astype(jnp.int32)
    cnt_ref[...] = jnp.broadcast_to(cnt_s[...], (CLS_ROWS, LANES)).astype(jnp.int32)


def _out_proj(ya, yg, xx, modv, wo, gain, wr, br, tri, n_lat, rows, tm):
    B, T, _ = xx.shape
    nj = rows // tm
    return pl.pallas_call(
        functools.partial(_out_body, n_lat=n_lat, tm=tm),
        grid=(B, nj),
        in_specs=[pl.BlockSpec((1, tm, ATT_WIDTH), lambda b, j: (b, j, 0)),
                  pl.BlockSpec((1, tm, GLA_WIDTH + CONV_WIDTH), lambda b, j: (b, j, 0)),
                  pl.BlockSpec((1, tm, D_MODEL), lambda b, j: (b, j, 0)),
                  pl.BlockSpec((1, 16, D_MODEL), lambda b, j: (b, 0, 0)),
                  pl.BlockSpec((D_MODEL, D_MODEL), lambda b, j: (0, 0)),
                  pl.BlockSpec((1, D_MODEL), lambda b, j: (0, 0)),
                  pl.BlockSpec((D_MODEL, LANES), lambda b, j: (0, 0)),
                  pl.BlockSpec((8, LANES), lambda b, j: (0, 0)),
                  pl.BlockSpec((LANES, LANES), lambda b, j: (0, 0))],
        out_specs=[pl.BlockSpec((1, tm, D_MODEL), lambda b, j: (b, j, 0)),
                   pl.BlockSpec((1, 8, tm), lambda b, j: (b * nj + j, 0, 0)),
                   pl.BlockSpec((CLS_ROWS, LANES), lambda b, j: (0, 0))],
        out_shape=[jax.ShapeDtypeStruct((B, rows, D_MODEL), F32),
                   jax.ShapeDtypeStruct((B * nj, 8, tm), jnp.int32),
                   jax.ShapeDtypeStruct((CLS_ROWS, LANES), jnp.int32)],
        scratch_shapes=[pltpu.VMEM((CLS_ROWS, 1), F32), pltpu.VMEM((tm, LANES), F32)],
        compiler_params=_cparams(2),
        name="out_proj_router",
    )(ya, yg, xx, modv, wo, gain, wr, br, tri)


ROW_UNROLL = 8
IDX_STRIDE = 1024


def _idx_slot(idx, slot, tm):
    return idx.at[pl.ds(pl.multiple_of(slot * IDX_STRIDE, IDX_STRIDE), tm)]


def _issue_rows(tm, idx, slot, make_copy):
    base = slot * IDX_STRIDE

    def trip(i, c):
        for u in range(ROW_UNROLL):
            r = i * ROW_UNROLL + u
            make_copy(r, idx[base + r]).start(priority=u % 2)
        return c

    lax.fori_loop(0, tm // ROW_UNROLL, trip, 0)


def _disp_body(x_ref, mod_ref, g_ref, dest_ref, hs_in_ref, hs_ref, buf, idx, sem_i, sem_d, *, n_lat, tm, nj,
               n_steps):
    del hs_in_ref
    j = pl.program_id(1)
    s = pl.program_id(0) * nj + j
    last = n_steps - 1
    slot = s % 2

    def idx_fetch(step, sl):
        return pltpu.make_async_copy(dest_ref.at[step], _idx_slot(idx, sl, tm), sem_i.at[sl])

    def drain(sl):
        pltpu.make_async_copy(buf.at[sl], hs_ref.at[pl.ds(0, tm)], sem_d.at[sl]).wait()

    @pl.when(s == 0)
    def _():
        idx_fetch(s, slot).start()

    row = j * tm + lax.broadcasted_iota(jnp.int32, (tm, 1), 0)
    is_ctx = row >= n_lat
    h2 = _norm_modulate(x_ref[0], g_ref[...], mod_ref[0], is_ctx, 3, 4)
    buf[slot] = h2.reshape(tm, 8, LANES)
    idx_fetch(s, slot).wait()
    _issue_rows(tm, idx, slot, lambda r, d: pltpu.make_async_copy(buf.at[slot, r], hs_ref.at[d], sem_d.at[slot]))

    @pl.when(s > 0)
    def _():
        drain(1 - slot)

    @pl.when(s < last)
    def _():
        idx_fetch(s + 1, 1 - slot).start()

    @pl.when(s == last)
    def _():
        drain(slot)


def _dispatch(xx, modv, gain, dest, hs0, n_lat, tm):
    B, rows, _ = xx.shape
    nj = rows // tm
    return pl.pallas_call(
        functools.partial(_disp_body, n_lat=n_lat, tm=tm, nj=nj, n_steps=B * nj),
        grid=(B, nj),
        in_specs=[pl.BlockSpec((1, tm, D_MODEL), lambda b, j: (b, j, 0)),
                  pl.BlockSpec((1, 16, D_MODEL), lambda b, j: (b, 0, 0)),
                  pl.BlockSpec((1, D_MODEL), lambda b, j: (0, 0)),
                  pl.BlockSpec(memory_space=pl.ANY),
                  pl.BlockSpec(memory_space=pl.ANY)],
        out_specs=pl.BlockSpec(memory_space=pl.ANY),
        scratch_shapes=[pltpu.VMEM((2, tm, 8, LANES), F32), pltpu.SMEM((2 * IDX_STRIDE,), jnp.int32),
                        pltpu.SemaphoreType.DMA((2,)), pltpu.SemaphoreType.DMA((2,))],
        out_shape=jax.ShapeDtypeStruct(hs0.shape, F32),
        input_output_aliases={4: 0},
        compiler_params=_cparams(2),
        name="dispatch",
    )(xx, modv, gain, dest, hs0)


def _moe_body(tile_ref, e_lo_ref, e_hi_ref, valid_ref, hs_ref, wg1, wu1, wd1, wg2, wu2, wd2, wr_ref, br_ref, ys_ref):
    g = pl.program_id(0)
    tm = hs_ref.shape[0]

    @pl.when(valid_ref[g] == 1)
    def _():
        x = hs_ref[...].reshape(tm, D_MODEL)
        e_lo = e_lo_ref[g]
        e_hi = e_hi_ref[g]
        dw = wr_ref[pl.ds(e_lo, 1), :] - wr_ref[pl.ds(e_hi, 1), :]
        d = jnp.sum(x * dw, axis=-1, keepdims=True) + (br_ref[e_lo] - br_ref[e_hi])
        w_lo = jax.nn.sigmoid(d)
        w_hi = jax.nn.sigmoid(-d)
        h = x.astype(BF16)

        def act(wg, wu, w):
            a = _silu(jnp.dot(h, wg[0], preferred_element_type=F32)) * jnp.dot(h, wu[0], preferred_element_type=F32)
            return (a * w).astype(BF16)

        y = (jnp.dot(act(wg1, wu1, w_lo), wd1[0], preferred_element_type=F32)
             + jnp.dot(act(wg2, wu2, w_hi), wd2[0], preferred_element_type=F32))
        ys_ref[...] = y.reshape(tm, 8, LANES)

    @pl.when(valid_ref[g] == 0)
    def _():
        ys_ref[...] = jnp.zeros_like(ys_ref)


def _moe(hs, tile, e_lo, e_hi, valid, wg, wu, wd, wr_t, br):
    n_tiles = tile.shape[0]
    tm = MOE_TILE

    def w_in(sel):
        return pl.BlockSpec((1, D_MODEL, D_EXPERT), lambda g, t, lo, hi, v: ((lo, hi)[sel][g], 0, 0))

    def w_out(sel):
        return pl.BlockSpec((1, D_EXPERT, D_MODEL), lambda g, t, lo, hi, v: ((lo, hi)[sel][g], 0, 0))

    return pl.pallas_call(
        _moe_body,
        grid_spec=pltpu.PrefetchScalarGridSpec(
            num_scalar_prefetch=4,
            grid=(n_tiles,),
            in_specs=[pl.BlockSpec((tm, 8, LANES), lambda g, t, lo, hi, v: (t[g], 0, 0)),
                      w_in(0), w_in(0), w_out(0), w_in(1), w_in(1), w_out(1),
                      pl.BlockSpec((N_EXPERTS, D_MODEL), lambda g, t, lo, hi, v: (0, 0)),
                      pl.BlockSpec(memory_space=pltpu.SMEM)],
            out_specs=pl.BlockSpec((tm, 8, LANES), lambda g, t, lo, hi, v: (g, 0, 0))),
        out_shape=jax.ShapeDtypeStruct((n_tiles * tm, 8, LANES), F32),
        compiler_params=_cparams(1),
        name="moe_pairs",
    )(tile, e_lo, e_hi, valid, hs, wg, wu, wd, wg, wu, wd, wr_t, br)


def _fin_body(x_ref, mod_ref, dest_ref, ys_ref, xo_ref, buf, idx, sem_i, sem_d, *, n_lat, tm, nj, n_steps):
    j = pl.program_id(1)
    s = pl.program_id(0) * nj + j
    last = n_steps - 1
    slot = s % 2

    def idx_fetch(step, sl):
        return pltpu.make_async_copy(dest_ref.at[step], _idx_slot(idx, sl, tm), sem_i.at[sl])

    def gather(sl):
        _issue_rows(tm, idx, sl, lambda r, d: pltpu.make_async_copy(ys_ref.at[d], buf.at[sl, r], sem_d.at[sl]))

    @pl.when(s == 0)
    def _():
        idx_fetch(0, 0).start()
        idx_fetch(0, 0).wait()
        gather(0)
        if last > 0:
            idx_fetch(1, 1).start()

    @pl.when(s < last)
    def _():
        idx_fetch(s + 1, 1 - slot).wait()
        gather(1 - slot)

    @pl.when(s + 2 <= last)
    def _():
        idx_fetch(s + 2, slot).start()

    pltpu.make_async_copy(ys_ref.at[pl.ds(0, tm)], buf.at[slot], sem_d.at[slot]).wait()
    row = j * tm + lax.broadcasted_iota(jnp.int32, (tm, 1), 0)
    is_ctx = row >= n_lat
    xo_ref[0] = x_ref[0] + _row_mod(mod_ref[0], is_ctx, 5) * buf[slot].reshape(tm, D_MODEL)


def _combine(xx, modv, dest, ys, n_lat, tm):
    B, rows, _ = xx.shape
    nj = rows // tm
    return pl.pallas_call(
        functools.partial(_fin_body, n_lat=n_lat, tm=tm, nj=nj, n_steps=B * nj),
        grid=(B, nj),
        in_specs=[pl.BlockSpec((1, tm, D_MODEL), lambda b, j: (b, j, 0)),
                  pl.BlockSpec((1, 16, D_MODEL), lambda b, j: (b, 0, 0)),
                  pl.BlockSpec(memory_space=pl.ANY),
                  pl.BlockSpec(memory_space=pl.ANY)],
        out_specs=pl.BlockSpec((1, tm, D_MODEL), lambda b, j: (b, j, 0)),
        scratch_shapes=[pltpu.VMEM((2, tm, 8, LANES), F32), pltpu.SMEM((2 * IDX_STRIDE,), jnp.int32),
                        pltpu.SemaphoreType.DMA((2,)), pltpu.SemaphoreType.DMA((2,))],
        out_shape=jax.ShapeDtypeStruct((B, rows, D_MODEL), F32),
        compiler_params=_cparams(2),
        name="combine",
    )(xx, modv, dest, ys)


def _rope_tables(n_lat, n_ctx):
    rows = n_lat // GRID_W
    row, col = jnp.meshgrid(jnp.arange(rows), jnp.arange(GRID_W), indexing="ij")
    n_freq = HEAD_DIM // 4
    inv_freq = ROPE_BASE ** (-jnp.arange(n_freq, dtype=F32) / n_freq)
    ang = jnp.concatenate([row.reshape(-1, 1).astype(F32) * inv_freq, col.reshape(-1, 1).astype(F32) * inv_freq],
                          axis=-1)
    cos = jnp.tile(jnp.cos(ang), (1, LANES // (HEAD_DIM // 2)))
    sin = jnp.tile(jnp.sin(ang), (1, LANES // (HEAD_DIM // 2)))
    sign = jnp.where((jnp.arange(LANES) % HEAD_DIM) < HEAD_DIM // 2, -1.0, 1.0).astype(F32)
    cos = jnp.concatenate([cos, jnp.ones((n_ctx, LANES), F32)], axis=0)
    sin = jnp.concatenate([sin * sign, jnp.zeros((n_ctx, LANES), F32)], axis=0)
    return cos, sin


def _block_diag_ones(n, blk):
    i = np.arange(n) // blk
    return jnp.asarray((i[:, None] == i[None, :]).astype(np.float32), dtype=BF16)


def _routing_tables(counts, n_tiles):
    tiles_c = (counts + MOE_TILE - 1) // MOE_TILE
    tile_end = jnp.cumsum(tiles_c)
    tile_start = tile_end - tiles_c
    off = tile_start * MOE_TILE
    total = tile_end[-1]
    g = jnp.arange(n_tiles, dtype=jnp.int32)
    valid = (g < total).astype(jnp.int32)
    g_eff = jnp.minimum(g, total - 1)
    c_of = jnp.sum((g_eff[:, None] >= tile_end[None, :]).astype(jnp.int32), axis=1)
    group = c_of // N_PAIRS
    pair = c_of % N_PAIRS
    lo = jnp.asarray(PAIR_LO, jnp.int32)
    hi = jnp.asarray(PAIR_HI, jnp.int32)
    e_lo = group * EXPERTS_PER_GROUP + jnp.sum((pair[:, None] == jnp.arange(N_PAIRS)[None, :]) * lo[None, :], axis=1)
    e_hi = group * EXPERTS_PER_GROUP + jnp.sum((pair[:, None] == jnp.arange(N_PAIRS)[None, :]) * hi[None, :], axis=1)
    return off.astype(jnp.int32), g_eff, e_lo.astype(jnp.int32), e_hi.astype(jnp.int32), valid


def kernel(x, c, ctx, c_ctx, w_ada, b_ada, norm_mix_g, norm_ffn_g, w_in, q_norm_g, k_norm_g, attn_sink, gla_gate_w,
           gla_gate_b, gla_norm_g, conv_w, w_out, w_router, b_router, w_gate_e, w_up_e, w_down_e):
    B, S, D = x.shape
    L = ctx.shape[1]
    T = S + L
    assert D == D_MODEL and T % TOKEN_TILE == 0 and S % LAT_TILE == 0 and S % GRID_W == 0
    assert S % ATT_BLOCK == 0 and L % ATT_BLOCK == 0 and S >= ATT_SPAN and T % EPI_ROWS == 0

    cond_rows = -(-(B + 1) // 8) * 8
    cond = jnp.zeros((cond_rows, D), F32).at[:B].set(c).at[B].set(c_ctx)
    mod_all = _modulation(cond, w_ada, b_ada)

    cos, sin = _rope_tables(S, L)
    bd_head = _block_diag_ones(LANES, HEAD_DIM)
    bd_gla = _block_diag_ones(GLA_WIDTH, GLA_DV)
    cum = _gla_cum_matrices()
    tri = jnp.asarray(np.triu(np.ones((LANES, LANES), np.float32), 1), dtype=BF16)
    xx = jnp.concatenate([x, ctx], axis=1)

    for l in range(DEPTH):
        last = l == DEPTH - 1
        m_lat = mod_all[l, :B].reshape(B, 6, D)
        m_ctx = jnp.broadcast_to(mod_all[l, B].reshape(1, 6, D), (B, 6, D))
        modv = jnp.concatenate([m_lat, m_ctx, jnp.zeros((B, 4, D), F32)], axis=1)

        wl = w_in[l]
        order = jnp.asarray(ATT_HEAD_ORDER)
        wq = wl[:, :ATT_WIDTH].reshape(D, ATT_HEADS, HEAD_DIM)[:, order, :].reshape(D, ATT_WIDTH)
        w_perm = jnp.concatenate([wq, wl[:, ATT_WIDTH:1536], wl[:, 1568:], wl[:, 1536:1568],
                                  jnp.zeros((D, N_PROJ - wl.shape[1]), F32)], axis=1).astype(BF16)
        qg = jnp.tile(q_norm_g[l], LANES // HEAD_DIM) * (HEAD_DIM ** -0.5)
        kg = jnp.tile(k_norm_g[l], LANES // HEAD_DIM)
        qkg = jnp.stack([qg] * (ATT_WIDTH // LANES) + [kg] + [jnp.zeros_like(kg)] * 3)
        p = _in_proj(xx, modv, norm_mix_g[l].reshape(1, D), w_perm, cos, sin, qkg, bd_head, S)

        y_att = _attention(p, attn_sink[l], S, not last)

        pad_rows = jnp.zeros((LANES - 2 * GLA_GATE_RANK, GLA_QK_WIDTH), F32)
        zero_rank = jnp.zeros((GLA_GATE_RANK, GLA_QK_WIDTH), F32)
        wgf = jnp.concatenate([gla_gate_w[l, 0], zero_rank, pad_rows], axis=0).astype(BF16)
        wgb = jnp.concatenate([zero_rank, gla_gate_w[l, 1], pad_rows], axis=0).astype(BF16)
        gbias = jnp.concatenate([gla_gate_b[l], jnp.zeros((6, GLA_QK_WIDTH), F32)], axis=0)
        ng = jnp.tile(gla_norm_g[l], GLA_HEADS).reshape(1, GLA_WIDTH)
        cw = jnp.concatenate([conv_w[l], jnp.zeros((5, CONV_WIDTH), F32)], axis=0)
        y_gc = _gla_conv(p, wgf, wgb, gbias, ng, cw, bd_gla, cum, S)

        rows, tm = (S, LAT_TILE) if last else (T, TOKEN_TILE)
        wr = jnp.concatenate([w_router, jnp.zeros((D, LANES - N_EXPERTS), F32)], axis=1).astype(BF16)
        br = jnp.zeros((8, LANES), F32).at[0, :N_EXPERTS].set(b_router)
        ffn_g = norm_ffn_g[l].reshape(1, D)
        wo_att = w_out[l, :ATT_WIDTH].reshape(ATT_HEADS, HEAD_DIM, D)[order].reshape(ATT_WIDTH, D)
        wo = jnp.concatenate([wo_att, w_out[l, ATT_WIDTH:]], axis=0).astype(BF16)
        xx_mid, route, counts = _out_proj(y_att, y_gc, xx, modv, wo, ffn_g, wr, br, tri, S, rows, tm)

        n_tiles = -(-(B * rows) // MOE_TILE) + N_CLASSES
        off, tile, e_lo, e_hi, valid = _routing_tables(counts[:N_CLASSES, 0], n_tiles)

        dest = route[:, 1, :]
        for cls_id in range(N_CLASSES):
            dest = dest + jnp.where(route[:, 0, :] == cls_id, off[cls_id], 0)
        hs = _dispatch(xx_mid, modv, ffn_g, dest, jnp.zeros((n_tiles * MOE_TILE, 8, LANES), F32), S, tm)
        ys = _moe(hs, tile, e_lo, e_hi, valid, w_gate_e[l].astype(BF16), w_up_e[l].astype(BF16),
                  w_down_e[l].astype(BF16), w_router.T, b_router)
        xx = _combine(xx_mid, modv, dest, ys, S, tm)
    return xx
```

```python
import functools

import numpy as np
import jax
import jax.numpy as jnp
from jax import lax
from jax.experimental import pallas as pl
from jax.experimental.pallas import tpu as pltpu

D_MODEL = 1024
DEPTH = 2
GRID_W = 64
EPS = 1e-6
HEAD_DIM = 64
ATT_HEADS = 8
ATT_KV_HEADS = 2
ATT_GROUP = ATT_HEADS // ATT_KV_HEADS
ATT_WIDTH = ATT_HEADS * HEAD_DIM
WINDOW = 128
ROPE_BASE = 10000.0
GLA_HEADS = 4
GLA_DV = 64
GLA_DK = 32
GLA_WIDTH = GLA_HEADS * GLA_DV
GLA_GATE_RANK = 16
GLA_GATE_NORM = 16.0
GLA_CHUNK = 64
CONV_WIDTH = 256
N_EXPERTS = 16
N_GROUPS = 4
EXPERTS_PER_GROUP = 4
D_EXPERT = D_MODEL // 2

LANES = 128
KV_WIDTH = ATT_KV_HEADS * HEAD_DIM
GLA_QK_WIDTH = GLA_HEADS * GLA_DK
COL_AQ, COL_AK, COL_AV = 0, 512, 640
COL_GQ, COL_GK, COL_GV, COL_GR = 768, 896, 1024, 1280
COL_CB, COL_CC, COL_CH, COL_GT = 1536, 1792, 2048, 2304
N_PROJ = 2432
QK_COLS = COL_AV
N_PAIRS = 6
N_CLASSES = N_GROUPS * N_PAIRS
PAIR_LO = (0, 0, 0, 1, 1, 2)
PAIR_HI = (1, 2, 3, 2, 3, 3)
CLS_ROWS = 32
NEG = -1e30

TOKEN_TILE = 768
LAT_TILE = 1024
MOE_TILE = 512
SUB_ROWS = 256
VMEM_LIMIT = 56 * 1024 * 1024

F32 = jnp.float32
BF16 = jnp.bfloat16


def _cparams(n_axes):
    return pltpu.CompilerParams(dimension_semantics=("arbitrary",) * n_axes, vmem_limit_bytes=VMEM_LIMIT)


def _silu(x):
    return x * jax.nn.sigmoid(x)


def _mod_body(c_ref, w_ref, b_ref, o_ref):
    c = c_ref[...]
    a = _silu(c).astype(BF16)
    o_ref[0] = jnp.dot(a, w_ref[0].astype(BF16), preferred_element_type=F32) + b_ref[0]


def _modulation(cond, w_ada, b_ada):
    rows = cond.shape[0]
    nblk = w_ada.shape[2] // D_MODEL
    return pl.pallas_call(
        _mod_body,
        grid=(DEPTH, nblk),
        in_specs=[pl.BlockSpec((rows, D_MODEL), lambda l, n: (0, 0)),
                  pl.BlockSpec((1, D_MODEL, D_MODEL), lambda l, n: (l, 0, n)),
                  pl.BlockSpec((1, 1, D_MODEL), lambda l, n: (l, 0, n))],
        out_specs=pl.BlockSpec((1, rows, D_MODEL), lambda l, n: (l, 0, n)),
        out_shape=jax.ShapeDtypeStruct((DEPTH, rows, w_ada.shape[2]), F32),
        compiler_params=_cparams(2),
        name="modulation",
    )(cond, w_ada, b_ada.reshape(DEPTH, 1, -1))


def _row_mod(mod, is_ctx, i):
    return jnp.where(is_ctx, mod[6 + i:7 + i], mod[i:i + 1])


def _norm_modulate(x, gain, mod, is_ctx, i_shift, i_scale):
    ms = jnp.mean(x * x, axis=-1, keepdims=True)
    xn = x * lax.rsqrt(ms + EPS) * gain
    return xn * (1.0 + _row_mod(mod, is_ctx, i_scale)) + _row_mod(mod, is_ctx, i_shift)


def _stream_specs(stream, ctx, tm, b_of, j_of):
    n_sub = tm // SUB_ROWS
    last_piece = stream.shape[1] // SUB_ROWS - 1
    specs = [pl.BlockSpec((1, SUB_ROWS, D_MODEL),
                          lambda *g, k=k: (b_of(g), jnp.minimum(j_of(g) * n_sub + k, last_piece), 0))
             for k in range(n_sub)]
    args = [stream] * n_sub
    if ctx is not None:
        assert ctx.shape[1] == SUB_ROWS and (stream.shape[1] + SUB_ROWS) % tm == 0
        specs.append(pl.BlockSpec((1, SUB_ROWS, D_MODEL), lambda *g: (b_of(g), 0, 0)))
        args.append(ctx)
    return specs, args


def _stream_piece(x_refs, k, n_sub, is_last_tile):
    x = x_refs[k][0]
    if len(x_refs) > n_sub and k == n_sub - 1:
        x = jnp.where(is_last_tile, x_refs[n_sub][0], x)
    return x


def _in_body(*refs, n_lat, tm, n_x):
    x_refs = refs[:n_x]
    mod_ref, g_ref, w_ref, cos_ref, sin_ref, qkg_ref, bd_ref, o_ref = refs[n_x:]
    j = pl.program_id(0)
    is_last_tile = j == pl.num_programs(0) - 1
    lane = lax.broadcasted_iota(jnp.int32, (1, LANES), 1)
    first_half = (lane % HEAD_DIM) < (HEAD_DIM // 2)
    for k, r0 in enumerate(range(0, tm, SUB_ROWS)):
        rows = slice(r0, r0 + SUB_ROWS)
        row = j * tm + r0 + lax.broadcasted_iota(jnp.int32, (SUB_ROWS, 1), 0)
        is_ctx = row >= n_lat
        x = _stream_piece(x_refs, k, tm // SUB_ROWS, is_last_tile)
        h = _norm_modulate(x, g_ref[...], mod_ref[0], is_ctx, 0, 1).astype(BF16)
        qk = jnp.dot(h, w_ref[:, :QK_COLS], preferred_element_type=F32)
        cos = cos_ref[rows, :]
        sin = sin_ref[rows, :]
        for c in range(QK_COLS // LANES):
            xc = qk[:, c * LANES:(c + 1) * LANES]
            ss = jnp.dot((xc * xc).astype(BF16), bd_ref[...], preferred_element_type=F32) * (1.0 / HEAD_DIM)
            xc = xc * lax.rsqrt(ss + EPS) * qkg_ref[c:c + 1, :]
            rot = jnp.where(first_half, pltpu.roll(xc, LANES - HEAD_DIM // 2, 1), pltpu.roll(xc, HEAD_DIM // 2, 1))
            o_ref[0, rows, c * LANES:(c + 1) * LANES] = (xc * cos + rot * sin).astype(BF16)
        o_ref[0, rows, QK_COLS:] = jnp.dot(h, w_ref[:, QK_COLS:], preferred_element_type=F32).astype(BF16)


def _in_proj(stream, ctx, modv, gain, w, cos, sin, qkg, bd, n_lat):
    B = stream.shape[0]
    T = stream.shape[1] + (0 if ctx is None else ctx.shape[1])
    tm = TOKEN_TILE
    x_specs, x_args = _stream_specs(stream, ctx, tm, lambda g: g[1], lambda g: g[0])
    return pl.pallas_call(
        functools.partial(_in_body, n_lat=n_lat, tm=tm, n_x=len(x_args)),
        grid=(T // tm, B),
        in_specs=x_specs + [
                  pl.BlockSpec((1, 16, D_MODEL), lambda j, b: (b, 0, 0)),
                  pl.BlockSpec((1, D_MODEL), lambda j, b: (0, 0)),
                  pl.BlockSpec((D_MODEL, N_PROJ), lambda j, b: (0, 0)),
                  pl.BlockSpec((tm, LANES), lambda j, b: (j, 0)),
                  pl.BlockSpec((tm, LANES), lambda j, b: (j, 0)),
                  pl.BlockSpec((8, LANES), lambda j, b: (0, 0)),
                  pl.BlockSpec((LANES, LANES), lambda j, b: (0, 0))],
        out_specs=pl.BlockSpec((1, tm, N_PROJ), lambda j, b: (b, j, 0)),
        out_shape=jax.ShapeDtypeStruct((B, T, N_PROJ), BF16),
        compiler_params=_cparams(2),
        name="in_proj",
    )(*x_args, modv, gain, w, cos, sin, qkg, bd)


ATT_BLOCK = 128
ATT_SPAN = ATT_BLOCK + 2 * WINDOW


ATT_HEAD_ORDER = (0, 4, 1, 5, 2, 6, 3, 7)


def _attend(qblk, k_parts, vt_parts, biases, sinks):
    cols = ATT_GROUP * ATT_BLOCK
    coli = lax.broadcasted_iota(jnp.int32, (1, cols), 1)
    lane = lax.broadcasted_iota(jnp.int32, (1, LANES), 1)
    lower = lane < HEAD_DIM
    rowd = lax.broadcasted_iota(jnp.int32, (LANES, 1), 0)
    nt = (((1,), (1,)), ((), ()))
    normed = []
    for h in range(ATT_KV_HEADS):
        keep = lower if h == 0 else jnp.logical_not(lower)
        qs = jnp.concatenate([jnp.where(keep, qblk[:, g * LANES:(g + 1) * LANES], jnp.zeros((), BF16))
                              for g in range(ATT_GROUP)], axis=0)
        sink = jnp.full((1, cols), sinks[ATT_GROUP * h + ATT_GROUP - 1], F32)
        for g in range(ATT_GROUP - 2, -1, -1):
            sink = jnp.where(coli < (g + 1) * ATT_BLOCK, sinks[ATT_GROUP * h + g], sink)
        scores = []
        for k in k_parts:
            s = lax.dot_general(k, qs, nt, preferred_element_type=F32)
            scores += [s[c * LANES:(c + 1) * LANES] for c in range(k.shape[0] // LANES)]
        scores = [s if b is None else s + b for s, b in zip(scores, biases)]
        m = scores[0]
        for s in scores[1:]:
            m = jnp.maximum(m, s)
        m = jnp.maximum(jnp.max(m, axis=0, keepdims=True), sink)
        probs = [jnp.exp((s - m).astype(BF16)) for s in scores]
        den_rows = (rowd >= HEAD_DIM) if h == 0 else (rowd < HEAD_DIM)
        acc = jnp.where(den_rows, jnp.exp(sink - m), 0.0)
        c0 = 0
        for vt in vt_parts[h]:
            n = vt.shape[1] // LANES
            acc = acc + jnp.dot(vt, jnp.concatenate(probs[c0:c0 + n], axis=0), preferred_element_type=F32)
            c0 += n
        val, den = (acc[:HEAD_DIM], acc[HEAD_DIM:]) if h == 0 else (acc[HEAD_DIM:], acc[:HEAD_DIM])
        normed.append(val / den)
    return jnp.concatenate(
        [jnp.concatenate([normed[0][:, g * ATT_BLOCK:(g + 1) * ATT_BLOCK],
                          normed[1][:, g * ATT_BLOCK:(g + 1) * ATT_BLOCK]], axis=0).T for g in range(ATT_GROUP)], axis=1)


def _att_body(sink_ref, q_ref, k_ref, v_ref, o_ref, vt_s, *, n_lat, n_ctx, with_ctx_out):
    sinks = [sink_ref[i] for i in range(ATT_HEADS)]
    lane = lax.broadcasted_iota(jnp.int32, (1, LANES), 1)
    for t0 in range(0, n_lat + n_ctx, ATT_BLOCK):
        vv = v_ref[0, t0:t0 + ATT_BLOCK, :].astype(F32)
        vt_s[0, :, t0:t0 + ATT_BLOCK] = jnp.where(lane < HEAD_DIM, vv, 1.0).T.astype(BF16)
        vt_s[1, :, t0:t0 + ATT_BLOCK] = jnp.where(lane < HEAD_DIM, 1.0, vv).T.astype(BF16)
    k_ctx = k_ref[0, n_lat:n_lat + n_ctx, :]
    vt_ctx = [vt_s[h, :, n_lat:n_lat + n_ctx] for h in range(ATT_KV_HEADS)]
    no_bias = [None] * (n_ctx // LANES)
    ki = lax.broadcasted_iota(jnp.int32, (LANES, ATT_GROUP * ATT_BLOCK), 0)
    qi = lax.broadcasted_iota(jnp.int32, (LANES, ATT_GROUP * ATT_BLOCK), 1) % ATT_BLOCK
    past_ok = jnp.where(ki >= qi, 0.0, NEG)
    ahead_ok = jnp.where(ki <= qi, 0.0, NEG)

    def block(q0, k0, n_keys, biases):
        k_parts = [k_ref[0, pl.ds(k0, n_keys), :], k_ctx]
        vt_parts = [[vt_s[h, :, pl.ds(k0, n_keys)], vt_ctx[h]] for h in range(ATT_KV_HEADS)]
        out = _attend(q_ref[0, pl.ds(q0, ATT_BLOCK), :], k_parts, vt_parts, biases + no_bias, sinks)
        o_ref[0, pl.ds(q0, ATT_BLOCK), :] = out.astype(BF16)

    def interior(i, carry):
        q0 = pl.multiple_of(i * ATT_BLOCK, ATT_BLOCK)
        block(q0, pl.multiple_of(q0 - WINDOW, ATT_BLOCK), ATT_SPAN, [past_ok, None, ahead_ok])
        return carry

    nq = n_lat // ATT_BLOCK
    block(0, 0, 2 * ATT_BLOCK, [None, ahead_ok])
    lax.fori_loop(1, nq - 1, interior, 0, unroll=2)
    block(n_lat - ATT_BLOCK, n_lat - 2 * ATT_BLOCK, 2 * ATT_BLOCK, [past_ok, None])
    if with_ctx_out:
        for c in range(n_ctx // ATT_BLOCK):
            r0 = n_lat + c * ATT_BLOCK
            out = _attend(q_ref[0, r0:r0 + ATT_BLOCK, :], [k_ctx], [[vt_ctx[h]] for h in range(ATT_KV_HEADS)],
                          no_bias, sinks)
            o_ref[0, r0:r0 + ATT_BLOCK, :] = out.astype(BF16)
    else:
        o_ref[0, n_lat:, :] = jnp.zeros((n_ctx, ATT_WIDTH), BF16)


def _attention(p, sinks, n_lat, with_ctx_out):
    B, T, _ = p.shape
    return pl.pallas_call(
        functools.partial(_att_body, n_lat=n_lat, n_ctx=T - n_lat, with_ctx_out=with_ctx_out),
        grid=(B,),
        in_specs=[pl.BlockSpec(memory_space=pltpu.SMEM),
                  pl.BlockSpec((1, T, ATT_WIDTH), lambda b: (b, 0, COL_AQ // ATT_WIDTH)),
                  pl.BlockSpec((1, T, KV_WIDTH), lambda b: (b, 0, COL_AK // KV_WIDTH)),
                  pl.BlockSpec((1, T, KV_WIDTH), lambda b: (b, 0, COL_AV // KV_WIDTH))],
        out_specs=pl.BlockSpec((1, T, ATT_WIDTH), lambda b: (b, 0, 0)),
        out_shape=jax.ShapeDtypeStruct((B, T, ATT_WIDTH), BF16),
        scratch_shapes=[pltpu.VMEM((ATT_KV_HEADS, KV_WIDTH, T), BF16)],
        compiler_params=_cparams(1),
        name="attention",
    )(sinks, p, p, p)


GLA_BLOCK_A = 256
GLA_BLOCK_B = 128
CONV_PAD = 8
EPI_ROWS = 256


def _log_sigmoid(z):
    return jnp.minimum(z, 0.0) - jnp.log1p(jnp.exp(-jnp.abs(z)))


def _gla_body(gq_ref, gk_ref, gv_ref, gr_ref, cb_ref, cc_ref, ch_ref, gt_ref, wgf_ref, wgb_ref, gbias_ref, ng_ref,
              cw_ref, bd_ref, cum_ref, o_ref, qe_s, ke_s, dec_s, upd_s, prev_s, st_s, o_s, u_s, *, n_lat, n_ctx):
    T = n_lat + n_ctx
    C = GLA_CHUNK
    nt = (((1,), (1,)), ((), ()))
    tn = (((0,), (0,)), ((), ()))

    sr = lax.broadcasted_iota(jnp.int32, (GLA_WIDTH, GLA_QK_WIDTH), 0) // GLA_DV
    sl = lax.broadcasted_iota(jnp.int32, (GLA_WIDTH, GLA_QK_WIDTH), 1) // GLA_DK
    state_mask = sr == sl

    def factors(i, carry):
        r0 = pl.multiple_of(i * GLA_BLOCK_A, GLA_BLOCK_A)
        rows = pl.ds(r0, GLA_BLOCK_A)
        gt = gt_ref[0, rows, :]
        q = gq_ref[0, rows, :].astype(F32) * (GLA_DK ** -0.5)
        k = gk_ref[0, rows, :].astype(F32)
        v = gv_ref[0, rows, :]
        for d, wg_ref in enumerate((wgf_ref, wgb_ref)):
            z = jnp.dot(gt, wg_ref[...], preferred_element_type=F32) + gbias_ref[d:d + 1, :]
            g = _log_sigmoid(z) * (1.0 / GLA_GATE_NORM)
            g_hi = g.astype(BF16)
            g_lo = (g - g_hi.astype(F32)).astype(BF16)
            cum = jnp.dot(cum_ref[d], jnp.concatenate([g_hi, g_lo], axis=1), preferred_element_type=F32)
            b = cum[:GLA_BLOCK_A, :GLA_QK_WIDTH] + cum[:GLA_BLOCK_A, GLA_QK_WIDTH:]
            tot = cum[GLA_BLOCK_A:, :GLA_QK_WIDTH] + cum[GLA_BLOCK_A:, GLA_QK_WIDTH:]
            qe_s[d, rows, :] = (q * jnp.exp(b)).astype(BF16)
            ke_s[d, rows, :] = (k * jnp.exp(-b)).astype(BF16)
            dec_s[d, rows, :] = jnp.exp(tot)
            kl = (k * jnp.exp(tot - b)).astype(BF16)
            for cc in range(GLA_BLOCK_A // C):
                upd = lax.dot_general(v[cc * C:(cc + 1) * C], kl[cc * C:(cc + 1) * C], tn, preferred_element_type=F32)
                upd_s[d, i * (GLA_BLOCK_A // C) + cc] = jnp.where(state_mask, upd, 0.0)
        return carry

    lax.fori_loop(0, T // GLA_BLOCK_A, factors, 0, unroll=3)

    nc_lat = n_lat // C
    nc_ctx = n_ctx // C
    st_s[...] = jnp.zeros_like(st_s)

    def scan(i, carry):
        in_ctx = i < nc_ctx
        cf = jnp.where(in_ctx, nc_lat + i, i - nc_ctx)
        cb = jnp.where(in_ctx, nc_lat + nc_ctx - 1 - i, nc_lat - 1 - (i - nc_ctx))
        for d, cid in enumerate((cf, cb)):
            st = st_s[d]
            prev_s[d, cid] = st.astype(BF16)
            st_s[d] = st * dec_s[d, pl.ds(pl.multiple_of(cid * C, C), 1), :] + upd_s[d, cid]
        return carry

    lax.fori_loop(0, nc_lat + nc_ctx, scan, 0)

    RB = GLA_BLOCK_B
    hr = lax.broadcasted_iota(jnp.int32, (GLA_HEADS * RB, GLA_QK_WIDTH), 0) // RB
    hl = lax.broadcasted_iota(jnp.int32, (GLA_HEADS * RB, GLA_QK_WIDTH), 1) // GLA_DK
    head_rows = hr == hl
    qr = lax.broadcasted_iota(jnp.int32, (GLA_HEADS * RB, RB), 0) % RB
    kc = lax.broadcasted_iota(jnp.int32, (GLA_HEADS * RB, RB), 1)
    same_chunk = (qr // C) == (kc // C)
    visible = (same_chunk & (kc <= qr), same_chunk & (kc > qr))
    ol = lax.broadcasted_iota(jnp.int32, (RB, GLA_WIDTH), 1) // GLA_DV

    def outputs(i, carry):
        r0 = pl.multiple_of(i * RB, RB)
        rows = pl.ds(r0, RB)
        v = gv_ref[0, rows, :]
        total = jnp.zeros((RB, GLA_WIDTH), F32)
        for d in range(2):
            qe = qe_s[d, rows, :]
            qe4 = jnp.where(head_rows, jnp.concatenate([qe] * GLA_HEADS, axis=0), jnp.zeros((), BF16))
            att = lax.dot_general(qe4, ke_s[d, rows, :], nt, preferred_element_type=F32)
            att = jnp.where(visible[d], att, 0.0).astype(BF16)
            full = jnp.dot(att, v, preferred_element_type=F32)
            o = full[(GLA_HEADS - 1) * RB:]
            for h in range(GLA_HEADS - 2, -1, -1):
                o = jnp.where(ol == h, full[h * RB:(h + 1) * RB], o)
            inter = [lax.dot_general(qe[cc * C:(cc + 1) * C], prev_s[d, i * (RB // C) + cc], nt,
                                     preferred_element_type=F32) for cc in range(RB // C)]
            total = total + o + jnp.concatenate(inter, axis=0)
        o_s[rows, :] = total
        return carry

    lax.fori_loop(0, T // RB, outputs, 0, unroll=2)

    u_s[0:CONV_PAD, :] = jnp.zeros((CONV_PAD, CONV_WIDTH), F32)
    u_s[CONV_PAD + T:, :] = jnp.zeros((CONV_PAD, CONV_WIDTH), F32)
    u_s[CONV_PAD:CONV_PAD + T, :] = cc_ref[0].astype(F32) * ch_ref[0].astype(F32)
    w0 = cw_ref[0:1, :]
    w1 = cw_ref[1:2, :]
    w2 = cw_ref[2:3, :]
    for e in range(T // EPI_ROWS):
        r0 = e * EPI_ROWS
        o = o_s[r0:r0 + EPI_ROWS, :]
        ss = jnp.dot((o * o).astype(BF16), bd_ref[...], preferred_element_type=F32) * (1.0 / GLA_DV)
        on = o * lax.rsqrt(ss + EPS) * ng_ref[...]
        r = gr_ref[0, r0:r0 + EPI_ROWS, :].astype(F32)
        o_ref[0, r0:r0 + EPI_ROWS, 0:GLA_WIDTH] = (on * _silu(r)).astype(BF16)
        t = r0 + lax.broadcasted_iota(jnp.int32, (EPI_ROWS, 1), 0)
        up = u_s[CONV_PAD + r0 - 1:CONV_PAD + r0 - 1 + EPI_ROWS, :]
        mid = u_s[CONV_PAD + r0:CONV_PAD + r0 + EPI_ROWS, :]
        dn = u_s[CONV_PAD + r0 + 1:CONV_PAD + r0 + 1 + EPI_ROWS, :]
        up = jnp.where(t == n_lat, 0.0, up)
        dn = jnp.where(t == n_lat - 1, 0.0, dn)
        conv = w0 * up + w1 * mid + w2 * dn
        o_ref[0, r0:r0 + EPI_ROWS, GLA_WIDTH:] = (cb_ref[0, r0:r0 + EPI_ROWS, :].astype(F32) * conv).astype(BF16)


def _gla_cum_matrices():
    i = np.arange(GLA_BLOCK_A)
    same = (i[:, None] // GLA_CHUNK) == (i[None, :] // GLA_CHUNK)
    fwd = same & (i[None, :] <= i[:, None])
    bwd = same & (i[None, :] >= i[:, None])
    mats = np.stack([np.concatenate([fwd, same], axis=0), np.concatenate([bwd, same], axis=0)])
    return jnp.asarray(mats.astype(np.float32), dtype=BF16)


def _gla_conv(p, wgf, wgb, gbias, ng, cw, bd, cum, n_lat):
    B, T, _ = p.shape
    nc = T // GLA_CHUNK

    def col(width, start):
        return pl.BlockSpec((1, T, width), lambda b: (b, 0, start // width))

    def const(shape):
        return pl.BlockSpec(shape, lambda b: (0,) * len(shape))

    return pl.pallas_call(
        functools.partial(_gla_body, n_lat=n_lat, n_ctx=T - n_lat),
        grid=(B,),
        in_specs=[col(GLA_QK_WIDTH, COL_GQ), col(GLA_QK_WIDTH, COL_GK), col(GLA_WIDTH, COL_GV), col(GLA_WIDTH, COL_GR),
                  col(CONV_WIDTH, COL_CB), col(CONV_WIDTH, COL_CC), col(CONV_WIDTH, COL_CH), col(LANES, COL_GT),
                  const((LANES, GLA_QK_WIDTH)), const((LANES, GLA_QK_WIDTH)), const((8, GLA_QK_WIDTH)),
                  const((1, GLA_WIDTH)), const((8, CONV_WIDTH)), const((GLA_WIDTH, GLA_WIDTH)),
                  const((2, 2 * GLA_BLOCK_A, GLA_BLOCK_A))],
        out_specs=pl.BlockSpec((1, T, GLA_WIDTH + CONV_WIDTH), lambda b: (b, 0, 0)),
        out_shape=jax.ShapeDtypeStruct((B, T, GLA_WIDTH + CONV_WIDTH), BF16),
        scratch_shapes=[pltpu.VMEM((2, T, GLA_QK_WIDTH), BF16), pltpu.VMEM((2, T, GLA_QK_WIDTH), BF16),
                        pltpu.VMEM((2, T, GLA_QK_WIDTH), F32),
                        pltpu.VMEM((2, nc, GLA_WIDTH, GLA_QK_WIDTH), F32),
                        pltpu.VMEM((2, nc, GLA_WIDTH, GLA_QK_WIDTH), BF16),
                        pltpu.VMEM((2, GLA_WIDTH, GLA_QK_WIDTH), F32),
                        pltpu.VMEM((T, GLA_WIDTH), F32),
                        pltpu.VMEM((T + 2 * CONV_PAD, CONV_WIDTH), F32)],
        compiler_params=_cparams(1),
        name="gla_conv",
    )(p, p, p, p, p, p, p, p, wgf, wgb, gbias, ng, cw, bd, cum)


def _route(logits_t):
    mx = jnp.max(logits_t, axis=0, keepdims=True)
    ex = jnp.exp(logits_t - mx)
    probs = ex / jnp.sum(ex, axis=0, keepdims=True)
    P = [probs[e:e + 1] for e in range(N_EXPERTS)]
    scores = []
    for g in range(N_GROUPS):
        a, b, c, d = P[4 * g:4 * g + 4]
        scores.append(jnp.maximum(jnp.maximum(jnp.maximum(a + b, a + c), jnp.maximum(a + d, b + c)),
                                  jnp.maximum(b + d, c + d)))
    best = jnp.maximum(jnp.maximum(scores[0], scores[1]), jnp.maximum(scores[2], scores[3]))
    taken = jnp.zeros_like(best, dtype=jnp.bool_)
    sel = []
    for g in range(N_GROUPS):
        s = (scores[g] == best) & jnp.logical_not(taken)
        sel.append(s)
        taken = taken | s
    gsel = jnp.where(sel[1], 1.0, 0.0) + jnp.where(sel[2], 2.0, 0.0) + jnp.where(sel[3], 3.0, 0.0)
    ig = [jnp.where(sel[0], P[j], jnp.where(sel[1], P[4 + j], jnp.where(sel[2], P[8 + j], P[12 + j])))
          for j in range(EXPERTS_PER_GROUP)]

    def first_max(vals):
        v = jnp.maximum(jnp.maximum(vals[0], vals[1]), jnp.maximum(vals[2], vals[3]))
        tk = jnp.zeros_like(v, dtype=jnp.bool_)
        hot = []
        for x in vals:
            s = (x == v) & jnp.logical_not(tk)
            hot.append(s)
            tk = tk | s
        idx = jnp.where(hot[1], 1.0, 0.0) + jnp.where(hot[2], 2.0, 0.0) + jnp.where(hot[3], 3.0, 0.0)
        return v, hot, idx

    _, hot1, i1 = first_max(ig)
    _, _, i2 = first_max([jnp.where(hot1[j], -1.0, ig[j]) for j in range(EXPERTS_PER_GROUP)])
    lo = jnp.minimum(i1, i2)
    hi = jnp.maximum(i1, i2)
    pair = jnp.where(lo == 0.0, hi - 1.0, jnp.where(lo == 1.0, hi + 1.0, 5.0))
    return gsel * N_PAIRS + pair


def _class_rank(cls, tri_ref, cnt_s):
    n = cls.shape[1]
    cid = lax.broadcasted_iota(jnp.int32, (CLS_ROWS, n), 0).astype(F32)
    onehot = jnp.where(cls == cid, 1.0, 0.0)
    segs = [onehot[:, k * LANES:(k + 1) * LANES] for k in range(n // LANES)]
    before = jnp.dot(jnp.concatenate(segs, axis=0).astype(BF16), tri_ref[...], preferred_element_type=F32)
    base = cnt_s[...]
    ranks = []
    for k, seg in enumerate(segs):
        ranks.append(jnp.sum(seg * (before[k * CLS_ROWS:(k + 1) * CLS_ROWS] + base), axis=0, keepdims=True))
        base = base + jnp.sum(seg, axis=1, keepdims=True)
    cnt_s[...] = base
    return jnp.concatenate(ranks, axis=1)


def _out_body(*refs, n_lat, tm, n_x):
    x_refs = refs[:n_x]
    ya_ref, yg_ref, mod_ref, wo_ref, g_ref, wr_ref, br_ref, tri_ref, xo_ref, rt_ref, cnt_ref, cnt_s, logit_s = refs[n_x:]
    j = pl.program_id(1)
    is_last_tile = j == pl.num_programs(1) - 1

    @pl.when((pl.program_id(0) == 0) & (j == 0))
    def _():
        cnt_s[...] = jnp.zeros_like(cnt_s)

    mod = mod_ref[0]
    for k, r0 in enumerate(range(0, tm, SUB_ROWS)):
        rows = slice(r0, r0 + SUB_ROWS)
        row = j * tm + r0 + lax.broadcasted_iota(jnp.int32, (SUB_ROWS, 1), 0)
        is_ctx = row >= n_lat
        y = (jnp.dot(ya_ref[0, rows, :], wo_ref[0:ATT_WIDTH, :], preferred_element_type=F32)
             + jnp.dot(yg_ref[0, rows, :], wo_ref[ATT_WIDTH:, :], preferred_element_type=F32))
        xn = _stream_piece(x_refs, k, tm // SUB_ROWS, is_last_tile) + _row_mod(mod, is_ctx, 2) * y
        xo_ref[0, rows, :] = xn
        h2 = _norm_modulate(xn, g_ref[...], mod, is_ctx, 3, 4).astype(BF16)
        logit_s[rows, :] = jnp.dot(h2, wr_ref[...], preferred_element_type=F32) + br_ref[0:1, :]
    cls = _route(logit_s[...].T[0:N_EXPERTS, :])
    rank = _class_rank(cls, tri_ref, cnt_s)
    rt_ref[0] = jnp.concatenate([cls, rank, jnp.zeros((6, tm), F32)], axis=0).astype(jnp.int32)
    cnt_ref[...] = jnp.broadcast_to(cnt_s[...], (CLS_ROWS, LANES)).astype(jnp.int32)


def _out_proj(ya, yg, stream, ctx, modv, wo, gain, wr, br, tri, n_lat, rows, tm):
    B = stream.shape[0]
    nj = rows // tm
    x_specs, x_args = _stream_specs(stream, ctx, tm, lambda g: g[0], lambda g: g[1])
    return pl.pallas_call(
        functools.partial(_out_body, n_lat=n_lat, tm=tm, n_x=len(x_args)),
        grid=(B, nj),
        in_specs=x_specs + [
                  pl.BlockSpec((1, tm, ATT_WIDTH), lambda b, j: (b, j, 0)),
                  pl.BlockSpec((1, tm, GLA_WIDTH + CONV_WIDTH), lambda b, j: (b, j, 0)),
                  pl.BlockSpec((1, 16, D_MODEL), lambda b, j: (b, 0, 0)),
                  pl.BlockSpec((D_MODEL, D_MODEL), lambda b, j: (0, 0)),
                  pl.BlockSpec((1, D_MODEL), lambda b, j: (0, 0)),
                  pl.BlockSpec((D_MODEL, LANES), lambda b, j: (0, 0)),
                  pl.BlockSpec((8, LANES), lambda b, j: (0, 0)),
                  pl.BlockSpec((LANES, LANES), lambda b, j: (0, 0))],
        out_specs=[pl.BlockSpec((1, tm, D_MODEL), lambda b, j: (b, j, 0)),
                   pl.BlockSpec((1, 8, tm), lambda b, j: (b * nj + j, 0, 0)),
                   pl.BlockSpec((CLS_ROWS, LANES), lambda b, j: (0, 0))],
        out_shape=[jax.ShapeDtypeStruct((B, rows, D_MODEL), F32),
                   jax.ShapeDtypeStruct((B * nj, 8, tm), jnp.int32),
                   jax.ShapeDtypeStruct((CLS_ROWS, LANES), jnp.int32)],
        scratch_shapes=[pltpu.VMEM((CLS_ROWS, 1), F32), pltpu.VMEM((tm, LANES), F32)],
        compiler_params=_cparams(2),
        name="out_proj_router",
    )(*x_args, ya, yg, modv, wo, gain, wr, br, tri)


ROW_UNROLL = 8
IDX_STRIDE = 1024


def _idx_slot(idx, slot, tm):
    return idx.at[pl.ds(pl.multiple_of(slot * IDX_STRIDE, IDX_STRIDE), tm)]


def _issue_rows(tm, idx, slot, make_copy):
    base = slot * IDX_STRIDE

    def trip(i, c):
        for u in range(ROW_UNROLL):
            r = i * ROW_UNROLL + u
            make_copy(r, idx[base + r]).start(priority=u % 2)
        return c

    lax.fori_loop(0, tm // ROW_UNROLL, trip, 0)


PAD_SLOTS = 32
PAD_BITS = 9


def _zero_pad_rows(pad_ref, zeros_ref, hs_ref, sem):
    def pieces(c, fn):
        start = pad_ref[c]
        n = pad_ref[PAD_SLOTS + c]
        for bit in range(PAD_BITS):
            size = 1 << bit
            below = n & (size - 1)

            @pl.when((n & size) != 0)
            def _():
                fn(pltpu.make_async_copy(zeros_ref.at[pl.ds(0, size)], hs_ref.at[pl.ds(start + below, size)], sem))

        def block(i, carry):
            fn(pltpu.make_async_copy(zeros_ref, hs_ref.at[pl.ds(start + n + i * MOE_TILE, MOE_TILE)], sem))
            return carry

        lax.fori_loop(0, pad_ref[2 * PAD_SLOTS + c], block, 0)

    def start_all(c, carry):
        pieces(c, lambda cp: cp.start())
        return carry

    def wait_all(c, carry):
        pieces(c, lambda cp: cp.wait())
        return carry

    lax.fori_loop(0, N_CLASSES + 1, start_all, 0)
    lax.fori_loop(0, N_CLASSES + 1, wait_all, 0)


def _disp_body(pad_ref, x_ref, mod_ref, g_ref, dest_ref, hs_ref, buf, idx, sem_i, sem_d, sem_z, *, n_lat, tm, nj,
               n_steps):
    j = pl.program_id(1)
    s = pl.program_id(0) * nj + j
    last = n_steps - 1
    slot = s % 2

    def idx_fetch(step, sl):
        return pltpu.make_async_copy(dest_ref.at[step], _idx_slot(idx, sl, tm), sem_i.at[sl])

    def drain(sl):
        pltpu.make_async_copy(buf.at[sl], hs_ref.at[pl.ds(0, tm)], sem_d.at[sl]).wait()

    @pl.when(s == 0)
    def _():
        idx_fetch(s, slot).start()
        buf[1] = jnp.zeros((tm, 8, LANES), F32)
        _zero_pad_rows(pad_ref, buf.at[1, pl.ds(0, MOE_TILE)], hs_ref, sem_z)

    row = j * tm + lax.broadcasted_iota(jnp.int32, (tm, 1), 0)
    is_ctx = row >= n_lat
    h2 = _norm_modulate(x_ref[0], g_ref[...], mod_ref[0], is_ctx, 3, 4)
    buf[slot] = h2.reshape(tm, 8, LANES)
    idx_fetch(s, slot).wait()
    _issue_rows(tm, idx, slot, lambda r, d: pltpu.make_async_copy(buf.at[slot, r], hs_ref.at[d], sem_d.at[slot]))

    @pl.when(s > 0)
    def _():
        drain(1 - slot)

    @pl.when(s < last)
    def _():
        idx_fetch(s + 1, 1 - slot).start()

    @pl.when(s == last)
    def _():
        drain(slot)


def _dispatch(pad, xx, modv, gain, dest, n_rows, n_lat, tm):
    B, rows, _ = xx.shape
    nj = rows // tm
    assert tm >= MOE_TILE
    return pl.pallas_call(
        functools.partial(_disp_body, n_lat=n_lat, tm=tm, nj=nj, n_steps=B * nj),
        grid=(B, nj),
        in_specs=[pl.BlockSpec(memory_space=pltpu.SMEM),
                  pl.BlockSpec((1, tm, D_MODEL), lambda b, j: (b, j, 0)),
                  pl.BlockSpec((1, 16, D_MODEL), lambda b, j: (b, 0, 0)),
                  pl.BlockSpec((1, D_MODEL), lambda b, j: (0, 0)),
                  pl.BlockSpec(memory_space=pl.ANY)],
        out_specs=pl.BlockSpec(memory_space=pl.ANY),
        scratch_shapes=[pltpu.VMEM((2, tm, 8, LANES), F32), pltpu.SMEM((2 * IDX_STRIDE,), jnp.int32),
                        pltpu.SemaphoreType.DMA((2,)), pltpu.SemaphoreType.DMA((2,)), pltpu.SemaphoreType.DMA],
        out_shape=jax.ShapeDtypeStruct((n_rows, 8, LANES), F32),
        compiler_params=_cparams(2),
        name="dispatch",
    )(pad, xx, modv, gain, dest)


def _moe_body(tile_ref, e_lo_ref, e_hi_ref, valid_ref, hs_ref, wg1, wu1, wd1, wg2, wu2, wd2, wr_ref, br_ref, ys_ref):
    g = pl.program_id(0)
    tm = hs_ref.shape[0]

    @pl.when(valid_ref[g] == 1)
    def _():
        x = hs_ref[...].reshape(tm, D_MODEL)
        e_lo = e_lo_ref[g]
        e_hi = e_hi_ref[g]
        dw = wr_ref[pl.ds(e_lo, 1), :] - wr_ref[pl.ds(e_hi, 1), :]
        d = jnp.sum(x * dw, axis=-1, keepdims=True) + (br_ref[e_lo] - br_ref[e_hi])
        w_lo = jax.nn.sigmoid(d)
        w_hi = jax.nn.sigmoid(-d)
        h = x.astype(BF16)

        def act(wg, wu, w):
            a = _silu(jnp.dot(h, wg[0], preferred_element_type=F32)) * jnp.dot(h, wu[0], preferred_element_type=F32)
            return (a * w).astype(BF16)

        y = (jnp.dot(act(wg1, wu1, w_lo), wd1[0], preferred_element_type=F32)
             + jnp.dot(act(wg2, wu2, w_hi), wd2[0], preferred_element_type=F32))
        ys_ref[...] = y.reshape(tm, 8, LANES)

    @pl.when(valid_ref[g] == 0)
    def _():
        ys_ref[...] = jnp.zeros_like(ys_ref)


def _moe(hs, tile, e_lo, e_hi, valid, wg, wu, wd, wr_t, br):
    n_tiles = tile.shape[0]
    tm = MOE_TILE

    def w_in(sel):
        return pl.BlockSpec((1, D_MODEL, D_EXPERT), lambda g, t, lo, hi, v: ((lo, hi)[sel][g], 0, 0))

    def w_out(sel):
        return pl.BlockSpec((1, D_EXPERT, D_MODEL), lambda g, t, lo, hi, v: ((lo, hi)[sel][g], 0, 0))

    return pl.pallas_call(
        _moe_body,
        grid_spec=pltpu.PrefetchScalarGridSpec(
            num_scalar_prefetch=4,
            grid=(n_tiles,),
            in_specs=[pl.BlockSpec((tm, 8, LANES), lambda g, t, lo, hi, v: (t[g], 0, 0)),
                      w_in(0), w_in(0), w_out(0), w_in(1), w_in(1), w_out(1),
                      pl.BlockSpec((N_EXPERTS, D_MODEL), lambda g, t, lo, hi, v: (0, 0)),
                      pl.BlockSpec(memory_space=pltpu.SMEM)],
            out_specs=pl.BlockSpec((tm, 8, LANES), lambda g, t, lo, hi, v: (g, 0, 0))),
        out_shape=jax.ShapeDtypeStruct((n_tiles * tm, 8, LANES), F32),
        compiler_params=_cparams(1),
        name="moe_pairs",
    )(tile, e_lo, e_hi, valid, hs, wg, wu, wd, wg, wu, wd, wr_t, br)


def _fin_body(x_ref, mod_ref, dest_ref, ys_ref, xo_ref, buf, idx, sem_i, sem_d, *, n_lat, tm, nj, n_steps):
    j = pl.program_id(1)
    s = pl.program_id(0) * nj + j
    last = n_steps - 1
    slot = s % 2

    def idx_fetch(step, sl):
        return pltpu.make_async_copy(dest_ref.at[step], _idx_slot(idx, sl, tm), sem_i.at[sl])

    def gather(sl):
        _issue_rows(tm, idx, sl, lambda r, d: pltpu.make_async_copy(ys_ref.at[d], buf.at[sl, r], sem_d.at[sl]))

    @pl.when(s == 0)
    def _():
        idx_fetch(0, 0).start()
        idx_fetch(0, 0).wait()
        gather(0)
        if last > 0:
            idx_fetch(1, 1).start()

    @pl.when(s < last)
    def _():
        idx_fetch(s + 1, 1 - slot).wait()
        gather(1 - slot)

    @pl.when(s + 2 <= last)
    def _():
        idx_fetch(s + 2, slot).start()

    pltpu.make_async_copy(ys_ref.at[pl.ds(0, tm)], buf.at[slot], sem_d.at[slot]).wait()
    row = j * tm + lax.broadcasted_iota(jnp.int32, (tm, 1), 0)
    is_ctx = row >= n_lat
    xo_ref[0] = x_ref[0] + _row_mod(mod_ref[0], is_ctx, 5) * buf[slot].reshape(tm, D_MODEL)


def _combine(xx, modv, dest, ys, n_lat, tm):
    B, rows, _ = xx.shape
    nj = rows // tm
    return pl.pallas_call(
        functools.partial(_fin_body, n_lat=n_lat, tm=tm, nj=nj, n_steps=B * nj),
        grid=(B, nj),
        in_specs=[pl.BlockSpec((1, tm, D_MODEL), lambda b, j: (b, j, 0)),
                  pl.BlockSpec((1, 16, D_MODEL), lambda b, j: (b, 0, 0)),
                  pl.BlockSpec(memory_space=pl.ANY),
                  pl.BlockSpec(memory_space=pl.ANY)],
        out_specs=pl.BlockSpec((1, tm, D_MODEL), lambda b, j: (b, j, 0)),
        scratch_shapes=[pltpu.VMEM((2, tm, 8, LANES), F32), pltpu.SMEM((2 * IDX_STRIDE,), jnp.int32),
                        pltpu.SemaphoreType.DMA((2,)), pltpu.SemaphoreType.DMA((2,))],
        out_shape=jax.ShapeDtypeStruct((B, rows, D_MODEL), F32),
        compiler_params=_cparams(2),
        name="combine",
    )(xx, modv, dest, ys)


def _rope_tables(n_lat, n_ctx):
    rows = n_lat // GRID_W
    row, col = jnp.meshgrid(jnp.arange(rows), jnp.arange(GRID_W), indexing="ij")
    n_freq = HEAD_DIM // 4
    inv_freq = ROPE_BASE ** (-jnp.arange(n_freq, dtype=F32) / n_freq)
    ang = jnp.concatenate([row.reshape(-1, 1).astype(F32) * inv_freq, col.reshape(-1, 1).astype(F32) * inv_freq],
                          axis=-1)
    cos = jnp.tile(jnp.cos(ang), (1, LANES // (HEAD_DIM // 2)))
    sin = jnp.tile(jnp.sin(ang), (1, LANES // (HEAD_DIM // 2)))
    sign = jnp.where((jnp.arange(LANES) % HEAD_DIM) < HEAD_DIM // 2, -1.0, 1.0).astype(F32)
    cos = jnp.concatenate([cos, jnp.ones((n_ctx, LANES), F32)], axis=0)
    sin = jnp.concatenate([sin * sign, jnp.zeros((n_ctx, LANES), F32)], axis=0)
    return cos, sin


def _block_diag_ones(n, blk):
    i = np.arange(n) // blk
    return jnp.asarray((i[:, None] == i[None, :]).astype(np.float32), dtype=BF16)


def _routing_tables(counts, n_tiles):
    tiles_c = (counts + MOE_TILE - 1) // MOE_TILE
    tile_end = jnp.cumsum(tiles_c)
    tile_start = tile_end - tiles_c
    off = tile_start * MOE_TILE
    total = tile_end[-1]
    g = jnp.arange(n_tiles, dtype=jnp.int32)
    valid = (g < total).astype(jnp.int32)
    g_eff = jnp.minimum(g, total - 1)
    c_of = jnp.sum((g_eff[:, None] >= tile_end[None, :]).astype(jnp.int32), axis=1)
    group = c_of // N_PAIRS
    pair = c_of % N_PAIRS
    lo = jnp.asarray(PAIR_LO, jnp.int32)
    hi = jnp.asarray(PAIR_HI, jnp.int32)
    e_lo = group * EXPERTS_PER_GROUP + jnp.sum((pair[:, None] == jnp.arange(N_PAIRS)[None, :]) * lo[None, :], axis=1)
    e_hi = group * EXPERTS_PER_GROUP + jnp.sum((pair[:, None] == jnp.arange(N_PAIRS)[None, :]) * hi[None, :], axis=1)
    fill = jnp.zeros((PAD_SLOTS - N_CLASSES - 1,), jnp.int32)
    pad = jnp.concatenate([off + counts, (total * MOE_TILE)[None], fill,
                           tiles_c * MOE_TILE - counts, jnp.zeros((1,), jnp.int32), fill,
                           jnp.zeros((N_CLASSES,), jnp.int32), (n_tiles - total)[None], fill])
    return off.astype(jnp.int32), pad.astype(jnp.int32), g_eff, e_lo.astype(jnp.int32), e_hi.astype(jnp.int32), valid


def kernel(x, c, ctx, c_ctx, w_ada, b_ada, norm_mix_g, norm_ffn_g, w_in, q_norm_g, k_norm_g, attn_sink, gla_gate_w,
           gla_gate_b, gla_norm_g, conv_w, w_out, w_router, b_router, w_gate_e, w_up_e, w_down_e):
    B, S, D = x.shape
    L = ctx.shape[1]
    T = S + L
    assert D == D_MODEL and T % TOKEN_TILE == 0 and S % LAT_TILE == 0 and S % GRID_W == 0
    assert S % ATT_BLOCK == 0 and L % ATT_BLOCK == 0 and S >= ATT_SPAN and T % EPI_ROWS == 0

    cond_rows = -(-(B + 1) // 8) * 8
    cond = jnp.zeros((cond_rows, D), F32).at[:B].set(c).at[B].set(c_ctx)
    mod_all = _modulation(cond, w_ada, b_ada)

    cos, sin = _rope_tables(S, L)
    bd_head = _block_diag_ones(LANES, HEAD_DIM)
    bd_gla = _block_diag_ones(GLA_WIDTH, GLA_DV)
    cum = _gla_cum_matrices()
    tri = jnp.asarray(np.triu(np.ones((LANES, LANES), np.float32), 1), dtype=BF16)
    stream, stream_ctx = x, ctx

    for l in range(DEPTH):
        last = l == DEPTH - 1
        m_lat = mod_all[l, :B].reshape(B, 6, D)
        m_ctx = jnp.broadcast_to(mod_all[l, B].reshape(1, 6, D), (B, 6, D))
        modv = jnp.concatenate([m_lat, m_ctx, jnp.zeros((B, 4, D), F32)], axis=1)

        wl = w_in[l]
        order = jnp.asarray(ATT_HEAD_ORDER)
        wq = wl[:, :ATT_WIDTH].reshape(D, ATT_HEADS, HEAD_DIM)[:, order, :].reshape(D, ATT_WIDTH)
        w_perm = jnp.concatenate([wq, wl[:, ATT_WIDTH:1536], wl[:, 1568:], wl[:, 1536:1568],
                                  jnp.zeros((D, N_PROJ - wl.shape[1]), F32)], axis=1).astype(BF16)
        qg = jnp.tile(q_norm_g[l], LANES // HEAD_DIM) * (HEAD_DIM ** -0.5)
        kg = jnp.tile(k_norm_g[l], LANES // HEAD_DIM)
        qkg = jnp.stack([qg] * (ATT_WIDTH // LANES) + [kg] + [jnp.zeros_like(kg)] * 3)
        p = _in_proj(stream, stream_ctx, modv, norm_mix_g[l].reshape(1, D), w_perm, cos, sin, qkg, bd_head, S)

        y_att = _attention(p, attn_sink[l], S, not last)

        pad_rows = jnp.zeros((LANES - 2 * GLA_GATE_RANK, GLA_QK_WIDTH), F32)
        zero_rank = jnp.zeros((GLA_GATE_RANK, GLA_QK_WIDTH), F32)
        wgf = jnp.concatenate([gla_gate_w[l, 0], zero_rank, pad_rows], axis=0).astype(BF16)
        wgb = jnp.concatenate([zero_rank, gla_gate_w[l, 1], pad_rows], axis=0).astype(BF16)
        gbias = jnp.concatenate([gla_gate_b[l], jnp.zeros((6, GLA_QK_WIDTH), F32)], axis=0)
        ng = jnp.tile(gla_norm_g[l], GLA_HEADS).reshape(1, GLA_WIDTH)
        cw = jnp.concatenate([conv_w[l], jnp.zeros((5, CONV_WIDTH), F32)], axis=0)
        y_gc = _gla_conv(p, wgf, wgb, gbias, ng, cw, bd_gla, cum, S)

        rows, tm = (S, LAT_TILE) if last else (T, TOKEN_TILE)
        wr = jnp.concatenate([w_router, jnp.zeros((D, LANES - N_EXPERTS), F32)], axis=1).astype(BF16)
        br = jnp.zeros((8, LANES), F32).at[0, :N_EXPERTS].set(b_router)
        ffn_g = norm_ffn_g[l].reshape(1, D)
        wo_att = w_out[l, :ATT_WIDTH].reshape(ATT_HEADS, HEAD_DIM, D)[order].reshape(ATT_WIDTH, D)
        wo = jnp.concatenate([wo_att, w_out[l, ATT_WIDTH:]], axis=0).astype(BF16)
        xx_mid, route, counts = _out_proj(y_att, y_gc, stream, stream_ctx, modv, wo, ffn_g, wr, br, tri, S, rows, tm)

        n_tiles = -(-(B * rows) // MOE_TILE) + N_CLASSES
        off, pad, tile, e_lo, e_hi, valid = _routing_tables(counts[:N_CLASSES, 0], n_tiles)

        dest = route[:, 1, :]
        for cls_id in range(N_CLASSES):
            dest = dest + jnp.where(route[:, 0, :] == cls_id, off[cls_id], 0)
        hs = _dispatch(pad, xx_mid, modv, ffn_g, dest, n_tiles * MOE_TILE, S, tm)
        ys = _moe(hs, tile, e_lo, e_hi, valid, w_gate_e[l].astype(BF16), w_up_e[l].astype(BF16),
                  w_down_e[l].astype(BF16), w_router.T, b_router)
        stream, stream_ctx = _combine(xx_mid, modv, dest, ys, S, tm), None
    return stream
```

```python
import functools

import numpy as np
import jax
import jax.numpy as jnp
from jax import lax
from jax.experimental import pallas as pl
from jax.experimental.pallas import tpu as pltpu

D_MODEL = 1024
DEPTH = 2
GRID_W = 64
EPS = 1e-6
HEAD_DIM = 64
ATT_HEADS = 8
ATT_KV_HEADS = 2
ATT_GROUP = ATT_HEADS // ATT_KV_HEADS
ATT_WIDTH = ATT_HEADS * HEAD_DIM
WINDOW = 128
ROPE_BASE = 10000.0
GLA_HEADS = 4
GLA_DV = 64
GLA_DK = 32
GLA_WIDTH = GLA_HEADS * GLA_DV
GLA_GATE_RANK = 16
GLA_GATE_NORM = 16.0
GLA_CHUNK = 64
CONV_WIDTH = 256
N_EXPERTS = 16
N_GROUPS = 4
EXPERTS_PER_GROUP = 4
D_EXPERT = D_MODEL // 2

LANES = 128
KV_WIDTH = ATT_KV_HEADS * HEAD_DIM
GLA_QK_WIDTH = GLA_HEADS * GLA_DK
COL_AQ, COL_AK, COL_AV = 0, 512, 640
COL_GQ, COL_GK, COL_GV, COL_GR = 768, 896, 1024, 1280
COL_CB, COL_CC, COL_CH, COL_GT = 1536, 1792, 2048, 2304
N_PROJ = 2432
QK_COLS = COL_AV
N_PAIRS = 6
N_CLASSES = N_GROUPS * N_PAIRS
PAIR_LO = (0, 0, 0, 1, 1, 2)
PAIR_HI = (1, 2, 3, 2, 3, 3)
CLS_ROWS = 32
NEG = -1e30

TOKEN_TILE = 768
LAT_TILE = 1024
MOE_TILE = 512
SUB_ROWS = 256
VMEM_LIMIT = 56 * 1024 * 1024

F32 = jnp.float32
BF16 = jnp.bfloat16


def _cparams(n_axes):
    return pltpu.CompilerParams(dimension_semantics=("arbitrary",) * n_axes, vmem_limit_bytes=VMEM_LIMIT)


def _silu(x):
    return x * jax.nn.sigmoid(x)


def _mod_body(c_ref, w_ref, b_ref, o_ref):
    c = c_ref[...]
    a = _silu(c).astype(BF16)
    o_ref[0] = jnp.dot(a, w_ref[0].astype(BF16), preferred_element_type=F32) + b_ref[0]


def _modulation(cond, w_ada, b_ada):
    rows = cond.shape[0]
    nblk = w_ada.shape[2] // D_MODEL
    return pl.pallas_call(
        _mod_body,
        grid=(DEPTH, nblk),
        in_specs=[pl.BlockSpec((rows, D_MODEL), lambda l, n: (0, 0)),
                  pl.BlockSpec((1, D_MODEL, D_MODEL), lambda l, n: (l, 0, n)),
                  pl.BlockSpec((1, 1, D_MODEL), lambda l, n: (l, 0, n))],
        out_specs=pl.BlockSpec((1, rows, D_MODEL), lambda l, n: (l, 0, n)),
        out_shape=jax.ShapeDtypeStruct((DEPTH, rows, w_ada.shape[2]), F32),
        compiler_params=_cparams(2),
        name="modulation",
    )(cond, w_ada, b_ada.reshape(DEPTH, 1, -1))


def _row_mod(mod, is_ctx, i):
    return jnp.where(is_ctx, mod[6 + i:7 + i], mod[i:i + 1])


def _norm_modulate(x, gain, mod, is_ctx, i_shift, i_scale):
    ms = jnp.mean(x * x, axis=-1, keepdims=True)
    xn = x * lax.rsqrt(ms + EPS) * gain
    return xn * (1.0 + _row_mod(mod, is_ctx, i_scale)) + _row_mod(mod, is_ctx, i_shift)


def _stream_specs(stream, ctx, tm, b_of, j_of):
    n_sub = tm // SUB_ROWS
    last_piece = stream.shape[1] // SUB_ROWS - 1
    specs = [pl.BlockSpec((1, SUB_ROWS, D_MODEL),
                          lambda *g, k=k: (b_of(g), jnp.minimum(j_of(g) * n_sub + k, last_piece), 0))
             for k in range(n_sub)]
    args = [stream] * n_sub
    if ctx is not None:
        assert ctx.shape[1] == SUB_ROWS and (stream.shape[1] + SUB_ROWS) % tm == 0
        specs.append(pl.BlockSpec((1, SUB_ROWS, D_MODEL), lambda *g: (b_of(g), 0, 0)))
        args.append(ctx)
    return specs, args


def _stream_piece(x_refs, k, n_sub, is_last_tile):
    x = x_refs[k][0]
    if len(x_refs) > n_sub and k == n_sub - 1:
        x = jnp.where(is_last_tile, x_refs[n_sub][0], x)
    return x


def _in_body(*refs, n_lat, tm, n_x):
    x_refs = refs[:n_x]
    mod_ref, g_ref, w_ref, cos_ref, sin_ref, qkg_ref, bd_ref, o_ref = refs[n_x:]
    j = pl.program_id(0)
    is_last_tile = j == pl.num_programs(0) - 1
    lane = lax.broadcasted_iota(jnp.int32, (1, LANES), 1)
    first_half = (lane % HEAD_DIM) < (HEAD_DIM // 2)
    for k, r0 in enumerate(range(0, tm, SUB_ROWS)):
        rows = slice(r0, r0 + SUB_ROWS)
        row = j * tm + r0 + lax.broadcasted_iota(jnp.int32, (SUB_ROWS, 1), 0)
        is_ctx = row >= n_lat
        x = _stream_piece(x_refs, k, tm // SUB_ROWS, is_last_tile)
        h = _norm_modulate(x, g_ref[...], mod_ref[0], is_ctx, 0, 1).astype(BF16)
        qk = jnp.dot(h, w_ref[:, :QK_COLS], preferred_element_type=F32)
        cos = cos_ref[rows, :]
        sin = sin_ref[rows, :]
        for c in range(QK_COLS // LANES):
            xc = qk[:, c * LANES:(c + 1) * LANES]
            ss = jnp.dot((xc * xc).astype(BF16), bd_ref[...], preferred_element_type=F32) * (1.0 / HEAD_DIM)
            xc = xc * lax.rsqrt(ss + EPS) * qkg_ref[c:c + 1, :]
            rot = jnp.where(first_half, pltpu.roll(xc, LANES - HEAD_DIM // 2, 1), pltpu.roll(xc, HEAD_DIM // 2, 1))
            o_ref[0, rows, c * LANES:(c + 1) * LANES] = (xc * cos + rot * sin).astype(BF16)
        o_ref[0, rows, QK_COLS:] = jnp.dot(h, w_ref[:, QK_COLS:], preferred_element_type=F32).astype(BF16)


def _in_proj(stream, ctx, modv, gain, w, cos, sin, qkg, bd, n_lat):
    B = stream.shape[0]
    T = stream.shape[1] + (0 if ctx is None else ctx.shape[1])
    tm = TOKEN_TILE
    x_specs, x_args = _stream_specs(stream, ctx, tm, lambda g: g[1], lambda g: g[0])
    return pl.pallas_call(
        functools.partial(_in_body, n_lat=n_lat, tm=tm, n_x=len(x_args)),
        grid=(T // tm, B),
        in_specs=x_specs + [
                  pl.BlockSpec((1, 16, D_MODEL), lambda j, b: (b, 0, 0)),
                  pl.BlockSpec((1, D_MODEL), lambda j, b: (0, 0)),
                  pl.BlockSpec((D_MODEL, N_PROJ), lambda j, b: (0, 0)),
                  pl.BlockSpec((tm, LANES), lambda j, b: (j, 0)),
                  pl.BlockSpec((tm, LANES), lambda j, b: (j, 0)),
                  pl.BlockSpec((8, LANES), lambda j, b: (0, 0)),
                  pl.BlockSpec((LANES, LANES), lambda j, b: (0, 0))],
        out_specs=pl.BlockSpec((1, tm, N_PROJ), lambda j, b: (b, j, 0)),
        out_shape=jax.ShapeDtypeStruct((B, T, N_PROJ), BF16),
        compiler_params=_cparams(2),
        name="in_proj",
    )(*x_args, modv, gain, w, cos, sin, qkg, bd)


ATT_BLOCK = 128
ATT_SPAN = ATT_BLOCK + 2 * WINDOW


ATT_HEAD_ORDER = (0, 4, 1, 5, 2, 6, 3, 7)


def _attend(qblk, k_parts, v_parts, biases, sinks):
    rows = ATT_GROUP * ATT_BLOCK
    rowi = lax.broadcasted_iota(jnp.int32, (rows, 1), 0)
    lane = lax.broadcasted_iota(jnp.int32, (1, LANES), 1)
    lower = lane < HEAD_DIM
    nt = (((1,), (1,)), ((), ()))
    heads = range(ATT_KV_HEADS)
    keep = [lower, jnp.logical_not(lower)]
    sink, scores = [], []
    for h in heads:
        qs = jnp.concatenate([jnp.where(keep[h], qblk[:, g * LANES:(g + 1) * LANES], jnp.zeros((), BF16))
                              for g in range(ATT_GROUP)], axis=0)
        col = jnp.full((rows, 1), sinks[ATT_GROUP * h + ATT_GROUP - 1], F32)
        for g in range(ATT_GROUP - 2, -1, -1):
            col = jnp.where(rowi < (g + 1) * ATT_BLOCK, sinks[ATT_GROUP * h + g], col)
        sink.append(col)
        pieces = []
        for k in k_parts:
            s = lax.dot_general(qs, k, nt, preferred_element_type=F32)
            pieces += [s[:, c * LANES:(c + 1) * LANES] for c in range(k.shape[0] // LANES)]
        scores.append([s if b is None else s + b for s, b in zip(pieces, biases)])
    top, probs = [], []
    for h in heads:
        m = scores[h][0]
        for s in scores[h][1:]:
            m = jnp.maximum(m, s)
        m = jnp.maximum(jnp.max(m, axis=-1, keepdims=True), sink[h])
        top.append(m)
        probs.append([jnp.exp((s - m).astype(BF16)) for s in scores[h]])
    normed = []
    for h in heads:
        acc = jnp.where(keep[h], 0.0, jnp.exp(sink[h] - top[h]))
        c0 = 0
        for v in v_parts[h]:
            n = v.shape[0] // LANES
            acc = acc + jnp.dot(jnp.concatenate(probs[h][c0:c0 + n], axis=1), v, preferred_element_type=F32)
            c0 += n
        normed.append(acc / pltpu.roll(acc, HEAD_DIM, 1))
    return jnp.concatenate([jnp.where(lower, normed[0][g * ATT_BLOCK:(g + 1) * ATT_BLOCK],
                                      normed[1][g * ATT_BLOCK:(g + 1) * ATT_BLOCK]) for g in range(ATT_GROUP)], axis=1)


def _att_body(sink_ref, q_ref, k_ref, v_ref, o_ref, v1_s, *, n_lat, n_ctx, with_ctx_out):
    sinks = [sink_ref[i] for i in range(ATT_HEADS)]
    lane = lax.broadcasted_iota(jnp.int32, (1, LANES), 1)
    vv = v_ref[0]
    v1_s[0] = jnp.where(lane < HEAD_DIM, vv, jnp.ones((), BF16))
    v1_s[1] = jnp.where(lane < HEAD_DIM, jnp.ones((), BF16), vv)
    k_ctx = k_ref[0, n_lat:n_lat + n_ctx, :]
    v_ctx = [v1_s[h, n_lat:n_lat + n_ctx, :] for h in range(ATT_KV_HEADS)]
    no_bias = [None] * (n_ctx // LANES)
    qi = lax.broadcasted_iota(jnp.int32, (ATT_GROUP * ATT_BLOCK, LANES), 0) % ATT_BLOCK
    ki = lax.broadcasted_iota(jnp.int32, (ATT_GROUP * ATT_BLOCK, LANES), 1)
    past_ok = jnp.where(ki >= qi, 0.0, NEG)
    ahead_ok = jnp.where(ki <= qi, 0.0, NEG)

    def block(q0, k0, n_keys, biases):
        k_parts = [k_ref[0, pl.ds(k0, n_keys), :], k_ctx]
        v_parts = [[v1_s[h, pl.ds(k0, n_keys), :], v_ctx[h]] for h in range(ATT_KV_HEADS)]
        out = _attend(q_ref[0, pl.ds(q0, ATT_BLOCK), :], k_parts, v_parts, biases + no_bias, sinks)
        o_ref[0, pl.ds(q0, ATT_BLOCK), :] = out.astype(BF16)

    def interior(i, carry):
        q0 = pl.multiple_of(i * ATT_BLOCK, ATT_BLOCK)
        block(q0, pl.multiple_of(q0 - WINDOW, ATT_BLOCK), ATT_SPAN, [past_ok, None, ahead_ok])
        return carry

    nq = n_lat // ATT_BLOCK
    block(0, 0, 2 * ATT_BLOCK, [None, ahead_ok])
    lax.fori_loop(1, nq - 1, interior, 0, unroll=7)
    block(n_lat - ATT_BLOCK, n_lat - 2 * ATT_BLOCK, 2 * ATT_BLOCK, [past_ok, None])
    if with_ctx_out:
        for c in range(n_ctx // ATT_BLOCK):
            r0 = n_lat + c * ATT_BLOCK
            out = _attend(q_ref[0, r0:r0 + ATT_BLOCK, :], [k_ctx], [[v_ctx[h]] for h in range(ATT_KV_HEADS)],
                          no_bias, sinks)
            o_ref[0, r0:r0 + ATT_BLOCK, :] = out.astype(BF16)
    else:
        o_ref[0, n_lat:, :] = jnp.zeros((n_ctx, ATT_WIDTH), BF16)


def _attention(p, sinks, n_lat, with_ctx_out):
    B, T, _ = p.shape
    return pl.pallas_call(
        functools.partial(_att_body, n_lat=n_lat, n_ctx=T - n_lat, with_ctx_out=with_ctx_out),
        grid=(B,),
        in_specs=[pl.BlockSpec(memory_space=pltpu.SMEM),
                  pl.BlockSpec((1, T, ATT_WIDTH), lambda b: (b, 0, COL_AQ // ATT_WIDTH)),
                  pl.BlockSpec((1, T, KV_WIDTH), lambda b: (b, 0, COL_AK // KV_WIDTH)),
                  pl.BlockSpec((1, T, KV_WIDTH), lambda b: (b, 0, COL_AV // KV_WIDTH))],
        out_specs=pl.BlockSpec((1, T, ATT_WIDTH), lambda b: (b, 0, 0)),
        out_shape=jax.ShapeDtypeStruct((B, T, ATT_WIDTH), BF16),
        scratch_shapes=[pltpu.VMEM((ATT_KV_HEADS, T, KV_WIDTH), BF16)],
        compiler_params=_cparams(1),
        name="attention",
    )(sinks, p, p, p)


GLA_BLOCK_A = 256
GLA_BLOCK_B = 128
CONV_PAD = 8
EPI_ROWS = 256


def _log_sigmoid(z):
    return jnp.minimum(z, 0.0) - jnp.log1p(jnp.exp(-jnp.abs(z)))


def _gla_body(gq_ref, gk_ref, gv_ref, gr_ref, cb_ref, cc_ref, ch_ref, gt_ref, wg_ref, gbias_ref, ng_ref,
              cw_ref, bd_ref, cum_ref, o_ref, qe_s, ke_s, dec_s, upd_s, prev_s, st_s, o_s, u_s, *, n_lat, n_ctx):
    T = n_lat + n_ctx
    C = GLA_CHUNK
    nt = (((1,), (1,)), ((), ()))
    tn = (((0,), (0,)), ((), ()))

    sr = lax.broadcasted_iota(jnp.int32, (GLA_WIDTH, 2 * GLA_QK_WIDTH), 0) // GLA_DV
    sl = (lax.broadcasted_iota(jnp.int32, (GLA_WIDTH, 2 * GLA_QK_WIDTH), 1) % GLA_QK_WIDTH) // GLA_DK
    state_mask2 = sr == sl

    def factors(i, carry):
        r0 = pl.multiple_of(i * GLA_BLOCK_A, GLA_BLOCK_A)
        rows = pl.ds(r0, GLA_BLOCK_A)
        gt = gt_ref[0, rows, :]
        q = gq_ref[0, rows, :].astype(F32) * (GLA_DK ** -0.5)
        k = gk_ref[0, rows, :].astype(F32)
        v = gv_ref[0, rows, :]
        z2 = jnp.dot(gt, wg_ref[...], preferred_element_type=F32) + gbias_ref[0:1, :]
        g2 = _log_sigmoid(z2) * (1.0 / GLA_GATE_NORM)
        cum = []
        for d in range(2):
            g = g2[:, d * GLA_QK_WIDTH:(d + 1) * GLA_QK_WIDTH]
            g_hi = g.astype(BF16)
            g_lo = (g - g_hi.astype(F32)).astype(BF16)
            cum.append(jnp.dot(cum_ref[d], jnp.concatenate([g_hi, g_lo], axis=1), preferred_element_type=F32))
        kl = []
        for d in range(2):
            b = cum[d][:GLA_BLOCK_A, :GLA_QK_WIDTH] + cum[d][:GLA_BLOCK_A, GLA_QK_WIDTH:]
            tot = cum[d][GLA_BLOCK_A:, :GLA_QK_WIDTH] + cum[d][GLA_BLOCK_A:, GLA_QK_WIDTH:]
            dec = jnp.exp(tot)
            ke = k * jnp.exp(-b)
            qe_s[d, rows, :] = (q * jnp.exp(b)).astype(BF16)
            ke_s[d, rows, :] = ke.astype(BF16)
            dec_s[d, rows, :] = dec
            kl.append((ke * dec).astype(BF16))
        kl2 = jnp.concatenate(kl, axis=1)
        for cc in range(GLA_BLOCK_A // C):
            upd = lax.dot_general(v[cc * C:(cc + 1) * C], kl2[cc * C:(cc + 1) * C], tn, preferred_element_type=F32)
            upd_s[i * (GLA_BLOCK_A // C) + cc] = jnp.where(state_mask2, upd, 0.0)
        return carry

    lax.fori_loop(0, T // GLA_BLOCK_A, factors, 0, unroll=3)

    nc_lat = n_lat // C
    nc_ctx = n_ctx // C
    st_s[...] = jnp.zeros_like(st_s)

    def scan(i, carry):
        in_ctx = i < nc_ctx
        cf = jnp.where(in_ctx, nc_lat + i, i - nc_ctx)
        cb = jnp.where(in_ctx, nc_lat + nc_ctx - 1 - i, nc_lat - 1 - (i - nc_ctx))
        for d, cid in enumerate((cf, cb)):
            st = st_s[d]
            prev_s[cid, :, d * GLA_QK_WIDTH:(d + 1) * GLA_QK_WIDTH] = st.astype(BF16)
            st_s[d] = (st * dec_s[d, pl.ds(pl.multiple_of(cid * C, C), 1), :]
                       + upd_s[cid, :, d * GLA_QK_WIDTH:(d + 1) * GLA_QK_WIDTH])
        return carry

    lax.fori_loop(0, nc_lat + nc_ctx, scan, 0)

    RB = GLA_BLOCK_B
    k_rows = lax.broadcasted_iota(jnp.int32, (GLA_HEADS * RB, GLA_QK_WIDTH), 0) // RB
    k_lanes = lax.broadcasted_iota(jnp.int32, (GLA_HEADS * RB, GLA_QK_WIDTH), 1) // GLA_DK
    key_heads = k_rows == k_lanes
    v_rows = lax.broadcasted_iota(jnp.int32, (GLA_HEADS * RB, GLA_WIDTH), 0) // RB
    v_lanes = lax.broadcasted_iota(jnp.int32, (GLA_HEADS * RB, GLA_WIDTH), 1) // GLA_DV
    value_heads = v_rows == v_lanes
    qr = lax.broadcasted_iota(jnp.int32, (RB, GLA_HEADS * RB), 0)
    kc = lax.broadcasted_iota(jnp.int32, (RB, GLA_HEADS * RB), 1) % RB
    same_chunk = (qr // C) == (kc // C)
    forward = kc <= qr
    zero = jnp.zeros((), BF16)

    def outputs(i, carry):
        r0 = pl.multiple_of(i * RB, RB)
        rows = pl.ds(r0, RB)
        v = gv_ref[0, rows, :]
        v4 = jnp.where(value_heads, jnp.concatenate([v] * GLA_HEADS, axis=0), zero)
        qe = [qe_s[d, rows, :] for d in range(2)]
        att = []
        for d in range(2):
            ke4 = jnp.where(key_heads, jnp.concatenate([ke_s[d, rows, :]] * GLA_HEADS, axis=0), zero)
            att.append(lax.dot_general(qe[d], ke4, nt, preferred_element_type=F32))
        both = jnp.where(same_chunk, jnp.where(forward, att[0], att[1]), 0.0).astype(BF16)
        qe2 = jnp.concatenate(qe, axis=1)
        inter = [lax.dot_general(qe2[cc * C:(cc + 1) * C], prev_s[i * (RB // C) + cc], nt,
                                 preferred_element_type=F32) for cc in range(RB // C)]
        o_s[rows, :] = jnp.dot(both, v4, preferred_element_type=F32) + jnp.concatenate(inter, axis=0)
        return carry

    lax.fori_loop(0, T // RB, outputs, 0, unroll=2)

    u_s[0:CONV_PAD, :] = jnp.zeros((CONV_PAD, CONV_WIDTH), F32)
    u_s[CONV_PAD + T:, :] = jnp.zeros((CONV_PAD, CONV_WIDTH), F32)
    u_s[CONV_PAD:CONV_PAD + T, :] = cc_ref[0].astype(F32) * ch_ref[0].astype(F32)
    w0 = cw_ref[0:1, :]
    w1 = cw_ref[1:2, :]
    w2 = cw_ref[2:3, :]
    for e in range(T // EPI_ROWS):
        r0 = e * EPI_ROWS
        o = o_s[r0:r0 + EPI_ROWS, :]
        ss = jnp.dot((o * o).astype(BF16), bd_ref[...], preferred_element_type=F32) * (1.0 / GLA_DV)
        on = o * lax.rsqrt(ss + EPS) * ng_ref[...]
        r = gr_ref[0, r0:r0 + EPI_ROWS, :].astype(F32)
        o_ref[0, r0:r0 + EPI_ROWS, 0:GLA_WIDTH] = (on * _silu(r)).astype(BF16)
        t = r0 + lax.broadcasted_iota(jnp.int32, (EPI_ROWS, 1), 0)
        up = u_s[CONV_PAD + r0 - 1:CONV_PAD + r0 - 1 + EPI_ROWS, :]
        mid = u_s[CONV_PAD + r0:CONV_PAD + r0 + EPI_ROWS, :]
        dn = u_s[CONV_PAD + r0 + 1:CONV_PAD + r0 + 1 + EPI_ROWS, :]
        up = jnp.where(t == n_lat, 0.0, up)
        dn = jnp.where(t == n_lat - 1, 0.0, dn)
        conv = w0 * up + w1 * mid + w2 * dn
        o_ref[0, r0:r0 + EPI_ROWS, GLA_WIDTH:] = (cb_ref[0, r0:r0 + EPI_ROWS, :].astype(F32) * conv).astype(BF16)


def _gla_cum_matrices():
    i = np.arange(GLA_BLOCK_A)
    same = (i[:, None] // GLA_CHUNK) == (i[None, :] // GLA_CHUNK)
    fwd = same & (i[None, :] <= i[:, None])
    bwd = same & (i[None, :] >= i[:, None])
    mats = np.stack([np.concatenate([fwd, same], axis=0), np.concatenate([bwd, same], axis=0)])
    return jnp.asarray(mats.astype(np.float32), dtype=BF16)


def _gla_conv(p, wg, gbias, ng, cw, bd, cum, n_lat):
    B, T, _ = p.shape
    nc = T // GLA_CHUNK

    def col(width, start):
        return pl.BlockSpec((1, T, width), lambda b: (b, 0, start // width))

    def const(shape):
        return pl.BlockSpec(shape, lambda b: (0,) * len(shape))

    return pl.pallas_call(
        functools.partial(_gla_body, n_lat=n_lat, n_ctx=T - n_lat),
        grid=(B,),
        in_specs=[col(GLA_QK_WIDTH, COL_GQ), col(GLA_QK_WIDTH, COL_GK), col(GLA_WIDTH, COL_GV), col(GLA_WIDTH, COL_GR),
                  col(CONV_WIDTH, COL_CB), col(CONV_WIDTH, COL_CC), col(CONV_WIDTH, COL_CH), col(LANES, COL_GT),
                  const((LANES, 2 * GLA_QK_WIDTH)), const((8, 2 * GLA_QK_WIDTH)),
                  const((1, GLA_WIDTH)), const((8, CONV_WIDTH)), const((GLA_WIDTH, GLA_WIDTH)),
                  const((2, 2 * GLA_BLOCK_A, GLA_BLOCK_A))],
        out_specs=pl.BlockSpec((1, T, GLA_WIDTH + CONV_WIDTH), lambda b: (b, 0, 0)),
        out_shape=jax.ShapeDtypeStruct((B, T, GLA_WIDTH + CONV_WIDTH), BF16),
        scratch_shapes=[pltpu.VMEM((2, T, GLA_QK_WIDTH), BF16), pltpu.VMEM((2, T, GLA_QK_WIDTH), BF16),
                        pltpu.VMEM((2, T, GLA_QK_WIDTH), F32),
                        pltpu.VMEM((nc, GLA_WIDTH, 2 * GLA_QK_WIDTH), F32),
                        pltpu.VMEM((nc, GLA_WIDTH, 2 * GLA_QK_WIDTH), BF16),
                        pltpu.VMEM((2, GLA_WIDTH, GLA_QK_WIDTH), F32),
                        pltpu.VMEM((T, GLA_WIDTH), F32),
                        pltpu.VMEM((T + 2 * CONV_PAD, CONV_WIDTH), F32)],
        compiler_params=_cparams(1),
        name="gla_conv",
    )(p, p, p, p, p, p, p, p, wg, gbias, ng, cw, bd, cum)


def _route(logits_t):
    mx = jnp.max(logits_t, axis=0, keepdims=True)
    ex = jnp.exp(logits_t - mx)
    probs = ex / jnp.sum(ex, axis=0, keepdims=True)
    P = [probs[e:e + 1] for e in range(N_EXPERTS)]
    scores = []
    for g in range(N_GROUPS):
        a, b, c, d = P[4 * g:4 * g + 4]
        scores.append(jnp.maximum(jnp.maximum(jnp.maximum(a + b, a + c), jnp.maximum(a + d, b + c)),
                                  jnp.maximum(b + d, c + d)))
    best = jnp.maximum(jnp.maximum(scores[0], scores[1]), jnp.maximum(scores[2], scores[3]))
    taken = jnp.zeros_like(best, dtype=jnp.bool_)
    sel = []
    for g in range(N_GROUPS):
        s = (scores[g] == best) & jnp.logical_not(taken)
        sel.append(s)
        taken = taken | s
    gsel = jnp.where(sel[1], 1.0, 0.0) + jnp.where(sel[2], 2.0, 0.0) + jnp.where(sel[3], 3.0, 0.0)
    ig = [jnp.where(sel[0], P[j], jnp.where(sel[1], P[4 + j], jnp.where(sel[2], P[8 + j], P[12 + j])))
          for j in range(EXPERTS_PER_GROUP)]

    def first_max(vals):
        v = jnp.maximum(jnp.maximum(vals[0], vals[1]), jnp.maximum(vals[2], vals[3]))
        tk = jnp.zeros_like(v, dtype=jnp.bool_)
        hot = []
        for x in vals:
            s = (x == v) & jnp.logical_not(tk)
            hot.append(s)
            tk = tk | s
        idx = jnp.where(hot[1], 1.0, 0.0) + jnp.where(hot[2], 2.0, 0.0) + jnp.where(hot[3], 3.0, 0.0)
        return v, hot, idx

    _, hot1, i1 = first_max(ig)
    _, _, i2 = first_max([jnp.where(hot1[j], -1.0, ig[j]) for j in range(EXPERTS_PER_GROUP)])
    lo = jnp.minimum(i1, i2)
    hi = jnp.maximum(i1, i2)
    pair = jnp.where(lo == 0.0, hi - 1.0, jnp.where(lo == 1.0, hi + 1.0, 5.0))
    return gsel * N_PAIRS + pair


def _class_rank(cls, tri_ref, cnt_s):
    n = cls.shape[1]
    cid = lax.broadcasted_iota(jnp.int32, (CLS_ROWS, n), 0).astype(F32)
    onehot = jnp.where(cls == cid, 1.0, 0.0)
    segs = [onehot[:, k * LANES:(k + 1) * LANES] for k in range(n // LANES)]
    before = jnp.dot(jnp.concatenate(segs, axis=0).astype(BF16), tri_ref[...], preferred_element_type=F32)
    base = cnt_s[...]
    ranks = []
    for k, seg in enumerate(segs):
        ranks.append(jnp.sum(seg * (before[k * CLS_ROWS:(k + 1) * CLS_ROWS] + base), axis=0, keepdims=True))
        base = base + jnp.sum(seg, axis=1, keepdims=True)
    cnt_s[...] = base
    return jnp.concatenate(ranks, axis=1)


def _out_body(*refs, n_lat, tm, n_x):
    x_refs = refs[:n_x]
    ya_ref, yg_ref, mod_ref, wo_ref, g_ref, wr_ref, br_ref, tri_ref, xo_ref, rt_ref, cnt_ref, cnt_s, logit_s = refs[n_x:]
    j = pl.program_id(1)
    is_last_tile = j == pl.num_programs(1) - 1

    @pl.when((pl.program_id(0) == 0) & (j == 0))
    def _():
        cnt_s[...] = jnp.zeros_like(cnt_s)

    mod = mod_ref[0]
    for k, r0 in enumerate(range(0, tm, SUB_ROWS)):
        rows = slice(r0, r0 + SUB_ROWS)
        row = j * tm + r0 + lax.broadcasted_iota(jnp.int32, (SUB_ROWS, 1), 0)
        is_ctx = row >= n_lat
        y = (jnp.dot(ya_ref[0, rows, :], wo_ref[0:ATT_WIDTH, :], preferred_element_type=F32)
             + jnp.dot(yg_ref[0, rows, :], wo_ref[ATT_WIDTH:, :], preferred_element_type=F32))
        xn = _stream_piece(x_refs, k, tm // SUB_ROWS, is_last_tile) + _row_mod(mod, is_ctx, 2) * y
        xo_ref[0, rows, :] = xn
        h2 = _norm_modulate(xn, g_ref[...], mod, is_ctx, 3, 4).astype(BF16)
        logit_s[rows, :] = jnp.dot(h2, wr_ref[...], preferred_element_type=F32) + br_ref[0:1, :]
    cls = _route(logit_s[...].T[0:N_EXPERTS, :])
    rank = _class_rank(cls, tri_ref, cnt_s)
    rt_ref[0] = jnp.concatenate([cls, rank, jnp.zeros((6, tm), F32)], axis=0).astype(jnp.int32)
    cnt_ref[...] = jnp.broadcast_to(cnt_s[...], (CLS_ROWS, LANES)).astype(jnp.int32)


def _out_proj(ya, yg, stream, ctx, modv, wo, gain, wr, br, tri, n_lat, rows, tm):
    B = stream.shape[0]
    nj = rows // tm
    x_specs, x_args = _stream_specs(stream, ctx, tm, lambda g: g[0], lambda g: g[1])
    return pl.pallas_call(
        functools.partial(_out_body, n_lat=n_lat, tm=tm, n_x=len(x_args)),
        grid=(B, nj),
        in_specs=x_specs + [
                  pl.BlockSpec((1, tm, ATT_WIDTH), lambda b, j: (b, j, 0)),
                  pl.BlockSpec((1, tm, GLA_WIDTH + CONV_WIDTH), lambda b, j: (b, j, 0)),
                  pl.BlockSpec((1, 16, D_MODEL), lambda b, j: (b, 0, 0)),
                  pl.BlockSpec((D_MODEL, D_MODEL), lambda b, j: (0, 0)),
                  pl.BlockSpec((1, D_MODEL), lambda b, j: (0, 0)),
                  pl.BlockSpec((D_MODEL, LANES), lambda b, j: (0, 0)),
                  pl.BlockSpec((8, LANES), lambda b, j: (0, 0)),
                  pl.BlockSpec((LANES, LANES), lambda b, j: (0, 0))],
        out_specs=[pl.BlockSpec((1, tm, D_MODEL), lambda b, j: (b, j, 0)),
                   pl.BlockSpec((1, 8, tm), lambda b, j: (b * nj + j, 0, 0)),
                   pl.BlockSpec((CLS_ROWS, LANES), lambda b, j: (0, 0))],
        out_shape=[jax.ShapeDtypeStruct((B, rows, D_MODEL), F32),
                   jax.ShapeDtypeStruct((B * nj, 8, tm), jnp.int32),
                   jax.ShapeDtypeStruct((CLS_ROWS, LANES), jnp.int32)],
        scratch_shapes=[pltpu.VMEM((CLS_ROWS, 1), F32), pltpu.VMEM((tm, LANES), F32)],
        compiler_params=_cparams(2),
        name="out_proj_router",
    )(*x_args, ya, yg, modv, wo, gain, wr, br, tri)


ROW_UNROLL = 8
IDX_STRIDE = 1024


def _idx_slot(idx, slot, tm):
    return idx.at[pl.ds(pl.multiple_of(slot * IDX_STRIDE, IDX_STRIDE), tm)]


def _issue_rows(tm, idx, slot, make_copy):
    base = slot * IDX_STRIDE

    def trip(i, c):
        for u in range(ROW_UNROLL):
            r = i * ROW_UNROLL + u
            make_copy(r, idx[base + r]).start(priority=u % 2)
        return c

    lax.fori_loop(0, tm // ROW_UNROLL, trip, 0)


PAD_SLOTS = 32
PAD_BITS = 9


def _zero_pad_rows(pad_ref, zeros_ref, hs_ref, sem):
    def pieces(c, fn):
        start = pad_ref[c]
        n = pad_ref[PAD_SLOTS + c]
        for bit in range(PAD_BITS):
            size = 1 << bit
            below = n & (size - 1)

            @pl.when((n & size) != 0)
            def _():
                fn(pltpu.make_async_copy(zeros_ref.at[pl.ds(0, size)], hs_ref.at[pl.ds(start + below, size)], sem))

        def block(i, carry):
            fn(pltpu.make_async_copy(zeros_ref, hs_ref.at[pl.ds(start + n + i * MOE_TILE, MOE_TILE)], sem))
            return carry

        lax.fori_loop(0, pad_ref[2 * PAD_SLOTS + c], block, 0)

    def start_all(c, carry):
        pieces(c, lambda cp: cp.start())
        return carry

    def wait_all(c, carry):
        pieces(c, lambda cp: cp.wait())
        return carry

    lax.fori_loop(0, N_CLASSES + 1, start_all, 0)
    lax.fori_loop(0, N_CLASSES + 1, wait_all, 0)


def _disp_body(pad_ref, x_ref, mod_ref, g_ref, dest_ref, hs_ref, buf, idx, sem_i, sem_d, sem_z, *, n_lat, tm, nj,
               n_steps):
    j = pl.program_id(1)
    s = pl.program_id(0) * nj + j
    last = n_steps - 1
    slot = s % 2

    def idx_fetch(step, sl):
        return pltpu.make_async_copy(dest_ref.at[step], _idx_slot(idx, sl, tm), sem_i.at[sl])

    def drain(sl):
        pltpu.make_async_copy(buf.at[sl], hs_ref.at[pl.ds(0, tm)], sem_d.at[sl]).wait()

    @pl.when(s == 0)
    def _():
        idx_fetch(s, slot).start()
        buf[1] = jnp.zeros((tm, 8, LANES), F32)
        _zero_pad_rows(pad_ref, buf.at[1, pl.ds(0, MOE_TILE)], hs_ref, sem_z)

    row = j * tm + lax.broadcasted_iota(jnp.int32, (tm, 1), 0)
    is_ctx = row >= n_lat
    h2 = _norm_modulate(x_ref[0], g_ref[...], mod_ref[0], is_ctx, 3, 4)
    buf[slot] = h2.reshape(tm, 8, LANES)
    idx_fetch(s, slot).wait()
    _issue_rows(tm, idx, slot, lambda r, d: pltpu.make_async_copy(buf.at[slot, r], hs_ref.at[d], sem_d.at[slot]))

    @pl.when(s > 0)
    def _():
        drain(1 - slot)

    @pl.when(s < last)
    def _():
        idx_fetch(s + 1, 1 - slot).start()

    @pl.when(s == last)
    def _():
        drain(slot)


def _dispatch(pad, xx, modv, gain, dest, n_rows, n_lat, tm):
    B, rows, _ = xx.shape
    nj = rows // tm
    assert tm >= MOE_TILE
    return pl.pallas_call(
        functools.partial(_disp_body, n_lat=n_lat, tm=tm, nj=nj, n_steps=B * nj),
        grid=(B, nj),
        in_specs=[pl.BlockSpec(memory_space=pltpu.SMEM),
                  pl.BlockSpec((1, tm, D_MODEL), lambda b, j: (b, j, 0)),
                  pl.BlockSpec((1, 16, D_MODEL), lambda b, j: (b, 0, 0)),
                  pl.BlockSpec((1, D_MODEL), lambda b, j: (0, 0)),
                  pl.BlockSpec(memory_space=pl.ANY)],
        out_specs=pl.BlockSpec(memory_space=pl.ANY),
        scratch_shapes=[pltpu.VMEM((2, tm, 8, LANES), F32), pltpu.SMEM((2 * IDX_STRIDE,), jnp.int32),
                        pltpu.SemaphoreType.DMA((2,)), pltpu.SemaphoreType.DMA((2,)), pltpu.SemaphoreType.DMA],
        out_shape=jax.ShapeDtypeStruct((n_rows, 8, LANES), F32),
        compiler_params=_cparams(2),
        name="dispatch",
    )(pad, xx, modv, gain, dest)


def _moe_body(tile_ref, e_lo_ref, e_hi_ref, valid_ref, hs_ref, wg1, wu1, wd1, wg2, wu2, wd2, wr_ref, br_ref, ys_ref):
    g = pl.program_id(0)
    tm = hs_ref.shape[0]

    @pl.when(valid_ref[g] == 1)
    def _():
        x = hs_ref[...].reshape(tm, D_MODEL)
        e_lo = e_lo_ref[g]
        e_hi = e_hi_ref[g]
        dw = wr_ref[pl.ds(e_lo, 1), :] - wr_ref[pl.ds(e_hi, 1), :]
        d = jnp.sum(x * dw, axis=-1, keepdims=True) + (br_ref[e_lo] - br_ref[e_hi])
        w_lo = jax.nn.sigmoid(d)
        w_hi = jax.nn.sigmoid(-d)
        h = x.astype(BF16)

        def act(wg, wu, w):
            a = _silu(jnp.dot(h, wg[0], preferred_element_type=F32)) * jnp.dot(h, wu[0], preferred_element_type=F32)
            return (a * w).astype(BF16)

        y = (jnp.dot(act(wg1, wu1, w_lo), wd1[0], preferred_element_type=F32)
             + jnp.dot(act(wg2, wu2, w_hi), wd2[0], preferred_element_type=F32))
        ys_ref[...] = y.reshape(tm, 8, LANES)

    @pl.when(valid_ref[g] == 0)
    def _():
        ys_ref[...] = jnp.zeros_like(ys_ref)


def _moe(hs, tile, e_lo, e_hi, valid, wg, wu, wd, wr_t, br):
    n_tiles = tile.shape[0]
    tm = MOE_TILE

    def w_in(sel):
        return pl.BlockSpec((1, D_MODEL, D_EXPERT), lambda g, t, lo, hi, v: ((lo, hi)[sel][g], 0, 0))

    def w_out(sel):
        return pl.BlockSpec((1, D_EXPERT, D_MODEL), lambda g, t, lo, hi, v: ((lo, hi)[sel][g], 0, 0))

    return pl.pallas_call(
        _moe_body,
        grid_spec=pltpu.PrefetchScalarGridSpec(
            num_scalar_prefetch=4,
            grid=(n_tiles,),
            in_specs=[pl.BlockSpec((tm, 8, LANES), lambda g, t, lo, hi, v: (t[g], 0, 0)),
                      w_in(0), w_in(0), w_out(0), w_in(1), w_in(1), w_out(1),
                      pl.BlockSpec((N_EXPERTS, D_MODEL), lambda g, t, lo, hi, v: (0, 0)),
                      pl.BlockSpec(memory_space=pltpu.SMEM)],
            out_specs=pl.BlockSpec((tm, 8, LANES), lambda g, t, lo, hi, v: (g, 0, 0))),
        out_shape=jax.ShapeDtypeStruct((n_tiles * tm, 8, LANES), F32),
        compiler_params=_cparams(1),
        name="moe_pairs",
    )(tile, e_lo, e_hi, valid, hs, wg, wu, wd, wg, wu, wd, wr_t, br)


def _fin_body(x_ref, mod_ref, dest_ref, ys_ref, xo_ref, buf, idx, sem_i, sem_d, *, n_lat, tm, nj, n_steps):
    j = pl.program_id(1)
    s = pl.program_id(0) * nj + j
    last = n_steps - 1
    slot = s % 2

    def idx_fetch(step, sl):
        return pltpu.make_async_copy(dest_ref.at[step], _idx_slot(idx, sl, tm), sem_i.at[sl])

    def gather(sl):
        _issue_rows(tm, idx, sl, lambda r, d: pltpu.make_async_copy(ys_ref.at[d], buf.at[sl, r], sem_d.at[sl]))

    @pl.when(s == 0)
    def _():
        idx_fetch(0, 0).start()
        idx_fetch(0, 0).wait()
        gather(0)
        if last > 0:
            idx_fetch(1, 1).start()

    @pl.when(s < last)
    def _():
        idx_fetch(s + 1, 1 - slot).wait()
        gather(1 - slot)

    @pl.when(s + 2 <= last)
    def _():
        idx_fetch(s + 2, slot).start()

    pltpu.make_async_copy(ys_ref.at[pl.ds(0, tm)], buf.at[slot], sem_d.at[slot]).wait()
    row = j * tm + lax.broadcasted_iota(jnp.int32, (tm, 1), 0)
    is_ctx = row >= n_lat
    xo_ref[0] = x_ref[0] + _row_mod(mod_ref[0], is_ctx, 5) * buf[slot].reshape(tm, D_MODEL)


def _combine(xx, modv, dest, ys, n_lat, tm):
    B, rows, _ = xx.shape
    nj = rows // tm
    return pl.pallas_call(
        functools.partial(_fin_body, n_lat=n_lat, tm=tm, nj=nj, n_steps=B * nj),
        grid=(B, nj),
        in_specs=[pl.BlockSpec((1, tm, D_MODEL), lambda b, j: (b, j, 0)),
                  pl.BlockSpec((1, 16, D_MODEL), lambda b, j: (b, 0, 0)),
                  pl.BlockSpec(memory_space=pl.ANY),
                  pl.BlockSpec(memory_space=pl.ANY)],
        out_specs=pl.BlockSpec((1, tm, D_MODEL), lambda b, j: (b, j, 0)),
        scratch_shapes=[pltpu.VMEM((2, tm, 8, LANES), F32), pltpu.SMEM((2 * IDX_STRIDE,), jnp.int32),
                        pltpu.SemaphoreType.DMA((2,)), pltpu.SemaphoreType.DMA((2,))],
        out_shape=jax.ShapeDtypeStruct((B, rows, D_MODEL), F32),
        compiler_params=_cparams(2),
        name="combine",
    )(xx, modv, dest, ys)


def _rope_tables(n_lat, n_ctx):
    rows = n_lat // GRID_W
    row, col = jnp.meshgrid(jnp.arange(rows), jnp.arange(GRID_W), indexing="ij")
    n_freq = HEAD_DIM // 4
    inv_freq = ROPE_BASE ** (-jnp.arange(n_freq, dtype=F32) / n_freq)
    ang = jnp.concatenate([row.reshape(-1, 1).astype(F32) * inv_freq, col.reshape(-1, 1).astype(F32) * inv_freq],
                          axis=-1)
    cos = jnp.tile(jnp.cos(ang), (1, LANES // (HEAD_DIM // 2)))
    sin = jnp.tile(jnp.sin(ang), (1, LANES // (HEAD_DIM // 2)))
    sign = jnp.where((jnp.arange(LANES) % HEAD_DIM) < HEAD_DIM // 2, -1.0, 1.0).astype(F32)
    cos = jnp.concatenate([cos, jnp.ones((n_ctx, LANES), F32)], axis=0)
    sin = jnp.concatenate([sin * sign, jnp.zeros((n_ctx, LANES), F32)], axis=0)
    return cos, sin


def _block_diag_ones(n, blk):
    i = np.arange(n) // blk
    return jnp.asarray((i[:, None] == i[None, :]).astype(np.float32), dtype=BF16)


def _routing_tables(counts, n_tiles):
    tiles_c = (counts + MOE_TILE - 1) // MOE_TILE
    tile_end = jnp.cumsum(tiles_c)
    tile_start = tile_end - tiles_c
    off = tile_start * MOE_TILE
    total = tile_end[-1]
    g = jnp.arange(n_tiles, dtype=jnp.int32)
    valid = (g < total).astype(jnp.int32)
    g_eff = jnp.minimum(g, total - 1)
    c_of = jnp.sum((g_eff[:, None] >= tile_end[None, :]).astype(jnp.int32), axis=1)
    group = c_of // N_PAIRS
    pair = c_of % N_PAIRS
    lo = jnp.asarray(PAIR_LO, jnp.int32)
    hi = jnp.asarray(PAIR_HI, jnp.int32)
    e_lo = group * EXPERTS_PER_GROUP + jnp.sum((pair[:, None] == jnp.arange(N_PAIRS)[None, :]) * lo[None, :], axis=1)
    e_hi = group * EXPERTS_PER_GROUP + jnp.sum((pair[:, None] == jnp.arange(N_PAIRS)[None, :]) * hi[None, :], axis=1)
    fill = jnp.zeros((PAD_SLOTS - N_CLASSES - 1,), jnp.int32)
    pad = jnp.concatenate([off + counts, (total * MOE_TILE)[None], fill,
                           tiles_c * MOE_TILE - counts, jnp.zeros((1,), jnp.int32), fill,
                           jnp.zeros((N_CLASSES,), jnp.int32), (n_tiles - total)[None], fill])
    return off.astype(jnp.int32), pad.astype(jnp.int32), g_eff, e_lo.astype(jnp.int32), e_hi.astype(jnp.int32), valid


def kernel(x, c, ctx, c_ctx, w_ada, b_ada, norm_mix_g, norm_ffn_g, w_in, q_norm_g, k_norm_g, attn_sink, gla_gate_w,
           gla_gate_b, gla_norm_g, conv_w, w_out, w_router, b_router, w_gate_e, w_up_e, w_down_e):
    B, S, D = x.shape
    L = ctx.shape[1]
    T = S + L
    assert D == D_MODEL and T % TOKEN_TILE == 0 and S % LAT_TILE == 0 and S % GRID_W == 0
    assert S % ATT_BLOCK == 0 and L % ATT_BLOCK == 0 and S >= ATT_SPAN and T % EPI_ROWS == 0

    cond_rows = -(-(B + 1) // 8) * 8
    cond = jnp.zeros((cond_rows, D), F32).at[:B].set(c).at[B].set(c_ctx)
    mod_all = _modulation(cond, w_ada, b_ada)

    cos, sin = _rope_tables(S, L)
    bd_head = _block_diag_ones(LANES, HEAD_DIM)
    bd_gla = _block_diag_ones(GLA_WIDTH, GLA_DV)
    cum = _gla_cum_matrices()
    tri = jnp.asarray(np.triu(np.ones((LANES, LANES), np.float32), 1), dtype=BF16)
    stream, stream_ctx = x, ctx

    for l in range(DEPTH):
        last = l == DEPTH - 1
        m_lat = mod_all[l, :B].reshape(B, 6, D)
        m_ctx = jnp.broadcast_to(mod_all[l, B].reshape(1, 6, D), (B, 6, D))
        modv = jnp.concatenate([m_lat, m_ctx, jnp.zeros((B, 4, D), F32)], axis=1)

        wl = w_in[l]
        order = jnp.asarray(ATT_HEAD_ORDER)
        wq = wl[:, :ATT_WIDTH].reshape(D, ATT_HEADS, HEAD_DIM)[:, order, :].reshape(D, ATT_WIDTH)
        w_perm = jnp.concatenate([wq, wl[:, ATT_WIDTH:1536], wl[:, 1568:], wl[:, 1536:1568],
                                  jnp.zeros((D, N_PROJ - wl.shape[1]), F32)], axis=1).astype(BF16)
        qg = jnp.tile(q_norm_g[l], LANES // HEAD_DIM) * (HEAD_DIM ** -0.5)
        kg = jnp.tile(k_norm_g[l], LANES // HEAD_DIM)
        qkg = jnp.stack([qg] * (ATT_WIDTH // LANES) + [kg] + [jnp.zeros_like(kg)] * 3)
        p = _in_proj(stream, stream_ctx, modv, norm_mix_g[l].reshape(1, D), w_perm, cos, sin, qkg, bd_head, S)

        y_att = _attention(p, attn_sink[l], S, not last)

        pad_rows = jnp.zeros((LANES - 2 * GLA_GATE_RANK, GLA_QK_WIDTH), F32)
        zero_rank = jnp.zeros((GLA_GATE_RANK, GLA_QK_WIDTH), F32)
        wgf = jnp.concatenate([gla_gate_w[l, 0], zero_rank, pad_rows], axis=0)
        wgb = jnp.concatenate([zero_rank, gla_gate_w[l, 1], pad_rows], axis=0)
        wg = jnp.concatenate([wgf, wgb], axis=1).astype(BF16)
        gbias = jnp.concatenate([gla_gate_b[l].reshape(1, 2 * GLA_QK_WIDTH),
                                 jnp.zeros((7, 2 * GLA_QK_WIDTH), F32)], axis=0)
        ng = jnp.tile(gla_norm_g[l], GLA_HEADS).reshape(1, GLA_WIDTH)
        cw = jnp.concatenate([conv_w[l], jnp.zeros((5, CONV_WIDTH), F32)], axis=0)
        y_gc = _gla_conv(p, wg, gbias, ng, cw, bd_gla, cum, S)

        rows, tm = (S, LAT_TILE) if last else (T, TOKEN_TILE)
        wr = jnp.concatenate([w_router, jnp.zeros((D, LANES - N_EXPERTS), F32)], axis=1).astype(BF16)
        br = jnp.zeros((8, LANES), F32).at[0, :N_EXPERTS].set(b_router)
        ffn_g = norm_ffn_g[l].reshape(1, D)
        wo_att = w_out[l, :ATT_WIDTH].reshape(ATT_HEADS, HEAD_DIM, D)[order].reshape(ATT_WIDTH, D)
        wo = jnp.concatenate([wo_att, w_out[l, ATT_WIDTH:]], axis=0).astype(BF16)
        xx_mid, route, counts = _out_proj(y_att, y_gc, stream, stream_ctx, modv, wo, ffn_g, wr, br, tri, S, rows, tm)

        n_tiles = -(-(B * rows) // MOE_TILE) + N_CLASSES
        off, pad, tile, e_lo, e_hi, valid = _routing_tables(counts[:N_CLASSES, 0], n_tiles)

        dest = route[:, 1, :]
        for cls_id in range(N_CLASSES):
            dest = dest + jnp.where(route[:, 0, :] == cls_id, off[cls_id], 0)
        hs = _dispatch(pad, xx_mid, modv, ffn_g, dest, n_tiles * MOE_TILE, S, tm)
        ys = _moe(hs, tile, e_lo, e_hi, valid, w_gate_e[l].astype(BF16), w_up_e[l].astype(BF16),
                  w_down_e[l].astype(BF16), w_router.T, b_router)
        stream, stream_ctx = _combine(xx_mid, modv, dest, ys, S, tm), None
    return stream
```

```python
import functools

import numpy as np
import jax
import jax.numpy as jnp
from jax import lax
from jax.experimental import pallas as pl
from jax.experimental.pallas import tpu as pltpu

D_MODEL = 1024
DEPTH = 2
GRID_W = 64
EPS = 1e-6
HEAD_DIM = 64
ATT_HEADS = 8
ATT_KV_HEADS = 2
ATT_GROUP = ATT_HEADS // ATT_KV_HEADS
ATT_WIDTH = ATT_HEADS * HEAD_DIM
WINDOW = 128
ROPE_BASE = 10000.0
GLA_HEADS = 4
GLA_DV = 64
GLA_DK = 32
GLA_WIDTH = GLA_HEADS * GLA_DV
GLA_GATE_RANK = 16
GLA_GATE_NORM = 16.0
GLA_CHUNK = 64
CONV_WIDTH = 256
N_EXPERTS = 16
N_GROUPS = 4
EXPERTS_PER_GROUP = 4
D_EXPERT = D_MODEL // 2

LANES = 128
KV_WIDTH = ATT_KV_HEADS * HEAD_DIM
GLA_QK_WIDTH = GLA_HEADS * GLA_DK
COL_AQ, COL_AK, COL_AV = 0, 512, 640
COL_GQ, COL_GK, COL_GV, COL_GR = 768, 896, 1024, 1280
COL_CB, COL_CC, COL_CH, COL_GT = 1536, 1792, 2048, 2304
N_PROJ = 2432
QK_COLS = COL_AV
N_PAIRS = 6
N_CLASSES = N_GROUPS * N_PAIRS
PAIR_LO = (0, 0, 0, 1, 1, 2)
PAIR_HI = (1, 2, 3, 2, 3, 3)
CLS_ROWS = 32
NEG = -1e30

TOKEN_TILE = 768
LAT_TILE = 1024
MOE_TILE = 512
SUB_ROWS = 256
VMEM_LIMIT = 56 * 1024 * 1024

F32 = jnp.float32
BF16 = jnp.bfloat16


def _cparams(n_axes):
    return pltpu.CompilerParams(dimension_semantics=("arbitrary",) * n_axes, vmem_limit_bytes=VMEM_LIMIT)


def _silu(x):
    return x * jax.nn.sigmoid(x)


def _mod_body(c_ref, w_ref, b_ref, o_ref):
    c = c_ref[...]
    a = _silu(c).astype(BF16)
    o_ref[0] = jnp.dot(a, w_ref[0].astype(BF16), preferred_element_type=F32) + b_ref[0]


def _modulation(cond, w_ada, b_ada):
    rows = cond.shape[0]
    nblk = w_ada.shape[2] // D_MODEL
    return pl.pallas_call(
        _mod_body,
        grid=(DEPTH, nblk),
        in_specs=[pl.BlockSpec((rows, D_MODEL), lambda l, n: (0, 0)),
                  pl.BlockSpec((1, D_MODEL, D_MODEL), lambda l, n: (l, 0, n)),
                  pl.BlockSpec((1, 1, D_MODEL), lambda l, n: (l, 0, n))],
        out_specs=pl.BlockSpec((1, rows, D_MODEL), lambda l, n: (l, 0, n)),
        out_shape=jax.ShapeDtypeStruct((DEPTH, rows, w_ada.shape[2]), F32),
        compiler_params=_cparams(2),
        name="modulation",
    )(cond, w_ada, b_ada.reshape(DEPTH, 1, -1))


def _row_mod(mod, is_ctx, i):
    return jnp.where(is_ctx, mod[6 + i:7 + i], mod[i:i + 1])


def _norm_modulate(x, gain, mod, is_ctx, i_shift, i_scale):
    ms = jnp.mean(x * x, axis=-1, keepdims=True)
    xn = x * lax.rsqrt(ms + EPS) * gain
    return xn * (1.0 + _row_mod(mod, is_ctx, i_scale)) + _row_mod(mod, is_ctx, i_shift)


def _stream_specs(stream, ctx, tm, b_of, j_of):
    n_sub = tm // SUB_ROWS
    last_piece = stream.shape[1] // SUB_ROWS - 1
    specs = [pl.BlockSpec((1, SUB_ROWS, D_MODEL),
                          lambda *g, k=k: (b_of(g), jnp.minimum(j_of(g) * n_sub + k, last_piece), 0))
             for k in range(n_sub)]
    args = [stream] * n_sub
    if ctx is not None:
        assert ctx.shape[1] == SUB_ROWS and (stream.shape[1] + SUB_ROWS) % tm == 0
        specs.append(pl.BlockSpec((1, SUB_ROWS, D_MODEL), lambda *g: (b_of(g), 0, 0)))
        args.append(ctx)
    return specs, args


def _stream_piece(x_refs, k, n_sub, is_last_tile):
    x = x_refs[k][0]
    if len(x_refs) > n_sub and k == n_sub - 1:
        x = jnp.where(is_last_tile, x_refs[n_sub][0], x)
    return x


def _in_tile(piece, j, mod_ref, g_ref, w_ref, cos_ref, sin_ref, qkg_ref, bd_ref, o_ref, n_lat, tm):
    lane = lax.broadcasted_iota(jnp.int32, (1, LANES), 1)
    first_half = (lane % HEAD_DIM) < (HEAD_DIM // 2)
    for k, r0 in enumerate(range(0, tm, SUB_ROWS)):
        rows = slice(r0, r0 + SUB_ROWS)
        row = j * tm + r0 + lax.broadcasted_iota(jnp.int32, (SUB_ROWS, 1), 0)
        is_ctx = row >= n_lat
        x = piece(k, is_ctx)
        h = _norm_modulate(x, g_ref[...], mod_ref[0], is_ctx, 0, 1).astype(BF16)
        qk = jnp.dot(h, w_ref[:, :QK_COLS], preferred_element_type=F32)
        cos = cos_ref[rows, :]
        sin = sin_ref[rows, :]
        for c in range(QK_COLS // LANES):
            xc = qk[:, c * LANES:(c + 1) * LANES]
            ss = jnp.dot((xc * xc).astype(BF16), bd_ref[...], preferred_element_type=F32) * (1.0 / HEAD_DIM)
            xc = xc * lax.rsqrt(ss + EPS) * qkg_ref[c:c + 1, :]
            rot = jnp.where(first_half, pltpu.roll(xc, LANES - HEAD_DIM // 2, 1), pltpu.roll(xc, HEAD_DIM // 2, 1))
            o_ref[0, rows, c * LANES:(c + 1) * LANES] = (xc * cos + rot * sin).astype(BF16)
        o_ref[0, rows, QK_COLS:] = jnp.dot(h, w_ref[:, QK_COLS:], preferred_element_type=F32).astype(BF16)


def _in_body(*refs, n_lat, tm, n_x):
    x_refs = refs[:n_x]
    j = pl.program_id(0)
    is_last_tile = j == pl.num_programs(0) - 1
    _in_tile(lambda k, is_ctx: _stream_piece(x_refs, k, tm // SUB_ROWS, is_last_tile), j, *refs[n_x:], n_lat, tm)


def _in_proj(stream, ctx, modv, gain, w, cos, sin, qkg, bd, n_lat):
    B = stream.shape[0]
    T = stream.shape[1] + (0 if ctx is None else ctx.shape[1])
    tm = TOKEN_TILE
    x_specs, x_args = _stream_specs(stream, ctx, tm, lambda g: g[1], lambda g: g[0])
    return pl.pallas_call(
        functools.partial(_in_body, n_lat=n_lat, tm=tm, n_x=len(x_args)),
        grid=(T // tm, B),
        in_specs=x_specs + [
                  pl.BlockSpec((1, 16, D_MODEL), lambda j, b: (b, 0, 0)),
                  pl.BlockSpec((1, D_MODEL), lambda j, b: (0, 0)),
                  pl.BlockSpec((D_MODEL, N_PROJ), lambda j, b: (0, 0)),
                  pl.BlockSpec((tm, LANES), lambda j, b: (j, 0)),
                  pl.BlockSpec((tm, LANES), lambda j, b: (j, 0)),
                  pl.BlockSpec((8, LANES), lambda j, b: (0, 0)),
                  pl.BlockSpec((LANES, LANES), lambda j, b: (0, 0))],
        out_specs=pl.BlockSpec((1, tm, N_PROJ), lambda j, b: (b, j, 0)),
        out_shape=jax.ShapeDtypeStruct((B, T, N_PROJ), BF16),
        compiler_params=_cparams(2),
        name="in_proj",
    )(*x_args, modv, gain, w, cos, sin, qkg, bd)


ATT_BLOCK = 128
ATT_SPAN = ATT_BLOCK + 2 * WINDOW


ATT_HEAD_ORDER = (0, 4, 1, 5, 2, 6, 3, 7)


def _attend(qblk, k_parts, v_parts, biases, sinks):
    rows = ATT_GROUP * ATT_BLOCK
    rowi = lax.broadcasted_iota(jnp.int32, (rows, 1), 0)
    lane = lax.broadcasted_iota(jnp.int32, (1, LANES), 1)
    lower = lane < HEAD_DIM
    nt = (((1,), (1,)), ((), ()))
    heads = range(ATT_KV_HEADS)
    keep = [lower, jnp.logical_not(lower)]
    sink, scores = [], []
    for h in heads:
        qs = jnp.concatenate([jnp.where(keep[h], qblk[:, g * LANES:(g + 1) * LANES], jnp.zeros((), BF16))
                              for g in range(ATT_GROUP)], axis=0)
        col = jnp.full((rows, 1), sinks[ATT_GROUP * h + ATT_GROUP - 1], F32)
        for g in range(ATT_GROUP - 2, -1, -1):
            col = jnp.where(rowi < (g + 1) * ATT_BLOCK, sinks[ATT_GROUP * h + g], col)
        sink.append(col)
        pieces = []
        for k in k_parts:
            s = lax.dot_general(qs, k, nt, preferred_element_type=F32)
            pieces += [s[:, c * LANES:(c + 1) * LANES] for c in range(k.shape[0] // LANES)]
        scores.append([s if b is None else s + b for s, b in zip(pieces, biases)])
    top, probs = [], []
    for h in heads:
        m = scores[h][0]
        for s in scores[h][1:]:
            m = jnp.maximum(m, s)
        m = jnp.maximum(jnp.max(m, axis=-1, keepdims=True), sink[h])
        top.append(m)
        probs.append([jnp.exp((s - m).astype(BF16)) for s in scores[h]])
    normed = []
    for h in heads:
        acc = jnp.where(keep[h], 0.0, jnp.exp(sink[h] - top[h]))
        c0 = 0
        for v in v_parts[h]:
            n = v.shape[0] // LANES
            acc = acc + jnp.dot(jnp.concatenate(probs[h][c0:c0 + n], axis=1), v, preferred_element_type=F32)
            c0 += n
        normed.append(acc / pltpu.roll(acc, HEAD_DIM, 1))
    return jnp.concatenate([jnp.where(lower, normed[0][g * ATT_BLOCK:(g + 1) * ATT_BLOCK],
                                      normed[1][g * ATT_BLOCK:(g + 1) * ATT_BLOCK]) for g in range(ATT_GROUP)], axis=1)


def _att_body(sink_ref, q_ref, k_ref, v_ref, o_ref, v1_s, *, n_lat, n_ctx, with_ctx_out):
    sinks = [sink_ref[i] for i in range(ATT_HEADS)]
    lane = lax.broadcasted_iota(jnp.int32, (1, LANES), 1)
    vv = v_ref[0]
    v1_s[0] = jnp.where(lane < HEAD_DIM, vv, jnp.ones((), BF16))
    v1_s[1] = jnp.where(lane < HEAD_DIM, jnp.ones((), BF16), vv)
    k_ctx = k_ref[0, n_lat:n_lat + n_ctx, :]
    v_ctx = [v1_s[h, n_lat:n_lat + n_ctx, :] for h in range(ATT_KV_HEADS)]
    no_bias = [None] * (n_ctx // LANES)
    qi = lax.broadcasted_iota(jnp.int32, (ATT_GROUP * ATT_BLOCK, LANES), 0) % ATT_BLOCK
    ki = lax.broadcasted_iota(jnp.int32, (ATT_GROUP * ATT_BLOCK, LANES), 1)
    past_ok = jnp.where(ki >= qi, 0.0, NEG)
    ahead_ok = jnp.where(ki <= qi, 0.0, NEG)

    def block(q0, k0, n_keys, biases):
        k_parts = [k_ref[0, pl.ds(k0, n_keys), :], k_ctx]
        v_parts = [[v1_s[h, pl.ds(k0, n_keys), :], v_ctx[h]] for h in range(ATT_KV_HEADS)]
        out = _attend(q_ref[0, pl.ds(q0, ATT_BLOCK), :], k_parts, v_parts, biases + no_bias, sinks)
        o_ref[0, pl.ds(q0, ATT_BLOCK), :] = out.astype(BF16)

    def interior(i, carry):
        q0 = pl.multiple_of(i * ATT_BLOCK, ATT_BLOCK)
        block(q0, pl.multiple_of(q0 - WINDOW, ATT_BLOCK), ATT_SPAN, [past_ok, None, ahead_ok])
        return carry

    nq = n_lat // ATT_BLOCK
    block(0, 0, 2 * ATT_BLOCK, [None, ahead_ok])
    lax.fori_loop(1, nq - 1, interior, 0, unroll=7)
    block(n_lat - ATT_BLOCK, n_lat - 2 * ATT_BLOCK, 2 * ATT_BLOCK, [past_ok, None])
    if with_ctx_out:
        for c in range(n_ctx // ATT_BLOCK):
            r0 = n_lat + c * ATT_BLOCK
            out = _attend(q_ref[0, r0:r0 + ATT_BLOCK, :], [k_ctx], [[v_ctx[h]] for h in range(ATT_KV_HEADS)],
                          no_bias, sinks)
            o_ref[0, r0:r0 + ATT_BLOCK, :] = out.astype(BF16)
    else:
        o_ref[0, n_lat:, :] = jnp.zeros((n_ctx, ATT_WIDTH), BF16)


def _attention(p, sinks, n_lat, with_ctx_out):
    B, T, _ = p.shape
    return pl.pallas_call(
        functools.partial(_att_body, n_lat=n_lat, n_ctx=T - n_lat, with_ctx_out=with_ctx_out),
        grid=(B,),
        in_specs=[pl.BlockSpec(memory_space=pltpu.SMEM),
                  pl.BlockSpec((1, T, ATT_WIDTH), lambda b: (b, 0, COL_AQ // ATT_WIDTH)),
                  pl.BlockSpec((1, T, KV_WIDTH), lambda b: (b, 0, COL_AK // KV_WIDTH)),
                  pl.BlockSpec((1, T, KV_WIDTH), lambda b: (b, 0, COL_AV // KV_WIDTH))],
        out_specs=pl.BlockSpec((1, T, ATT_WIDTH), lambda b: (b, 0, 0)),
        out_shape=jax.ShapeDtypeStruct((B, T, ATT_WIDTH), BF16),
        scratch_shapes=[pltpu.VMEM((ATT_KV_HEADS, T, KV_WIDTH), BF16)],
        compiler_params=_cparams(1),
        name="attention",
    )(sinks, p, p, p)


GLA_BLOCK_A = 256
GLA_BLOCK_B = 128
CONV_PAD = 8
EPI_ROWS = 256


def _log_sigmoid(z):
    return jnp.minimum(z, 0.0) - jnp.log1p(jnp.exp(-jnp.abs(z)))


def _gla_body(gq_ref, gk_ref, gv_ref, gr_ref, cb_ref, cc_ref, ch_ref, gt_ref, wg_ref, gbias_ref, ng_ref,
              cw_ref, bd_ref, cum_ref, o_ref, qe_s, ke_s, dec_s, upd_s, prev_s, st_s, o_s, u_s, *, n_lat, n_ctx):
    T = n_lat + n_ctx
    C = GLA_CHUNK
    nt = (((1,), (1,)), ((), ()))
    tn = (((0,), (0,)), ((), ()))

    sr = lax.broadcasted_iota(jnp.int32, (GLA_WIDTH, 2 * GLA_QK_WIDTH), 0) // GLA_DV
    sl = (lax.broadcasted_iota(jnp.int32, (GLA_WIDTH, 2 * GLA_QK_WIDTH), 1) % GLA_QK_WIDTH) // GLA_DK
    state_mask2 = sr == sl

    def factors(i, carry):
        r0 = pl.multiple_of(i * GLA_BLOCK_A, GLA_BLOCK_A)
        rows = pl.ds(r0, GLA_BLOCK_A)
        gt = gt_ref[0, rows, :]
        q = gq_ref[0, rows, :].astype(F32) * (GLA_DK ** -0.5)
        k = gk_ref[0, rows, :].astype(F32)
        v = gv_ref[0, rows, :]
        z2 = jnp.dot(gt, wg_ref[...], preferred_element_type=F32) + gbias_ref[0:1, :]
        g2 = _log_sigmoid(z2) * (1.0 / GLA_GATE_NORM)
        cum = []
        for d in range(2):
            g = g2[:, d * GLA_QK_WIDTH:(d + 1) * GLA_QK_WIDTH]
            g_hi = g.astype(BF16)
            g_lo = (g - g_hi.astype(F32)).astype(BF16)
            cum.append(jnp.dot(cum_ref[d], jnp.concatenate([g_hi, g_lo], axis=1), preferred_element_type=F32))
        kl = []
        for d in range(2):
            b = cum[d][:GLA_BLOCK_A, :GLA_QK_WIDTH] + cum[d][:GLA_BLOCK_A, GLA_QK_WIDTH:]
            tot = cum[d][GLA_BLOCK_A:, :GLA_QK_WIDTH] + cum[d][GLA_BLOCK_A:, GLA_QK_WIDTH:]
            dec = jnp.exp(tot)
            ke = k * jnp.exp(-b)
            qe_s[d, rows, :] = (q * jnp.exp(b)).astype(BF16)
            ke_s[d, rows, :] = ke.astype(BF16)
            dec_s[d, rows, :] = dec
            kl.append((ke * dec).astype(BF16))
        kl2 = jnp.concatenate(kl, axis=1)
        for cc in range(GLA_BLOCK_A // C):
            upd = lax.dot_general(v[cc * C:(cc + 1) * C], kl2[cc * C:(cc + 1) * C], tn, preferred_element_type=F32)
            upd_s[i * (GLA_BLOCK_A // C) + cc] = jnp.where(state_mask2, upd, 0.0)
        return carry

    lax.fori_loop(0, T // GLA_BLOCK_A, factors, 0, unroll=True)

    nc_lat = n_lat // C
    nc_ctx = n_ctx // C
    st_s[...] = jnp.zeros_like(st_s)

    def scan(i, carry):
        in_ctx = i < nc_ctx
        cf = jnp.where(in_ctx, nc_lat + i, i - nc_ctx)
        cb = jnp.where(in_ctx, nc_lat + nc_ctx - 1 - i, nc_lat - 1 - (i - nc_ctx))
        for d, cid in enumerate((cf, cb)):
            st = st_s[d]
            prev_s[cid, :, d * GLA_QK_WIDTH:(d + 1) * GLA_QK_WIDTH] = st.astype(BF16)
            st_s[d] = (st * dec_s[d, pl.ds(pl.multiple_of(cid * C, C), 1), :]
                       + upd_s[cid, :, d * GLA_QK_WIDTH:(d + 1) * GLA_QK_WIDTH])
        return carry

    lax.fori_loop(0, nc_lat + nc_ctx, scan, 0)

    RB = GLA_BLOCK_B
    k_rows = lax.broadcasted_iota(jnp.int32, (GLA_HEADS * RB, GLA_QK_WIDTH), 0) // RB
    k_lanes = lax.broadcasted_iota(jnp.int32, (GLA_HEADS * RB, GLA_QK_WIDTH), 1) // GLA_DK
    key_heads = k_rows == k_lanes
    v_rows = lax.broadcasted_iota(jnp.int32, (GLA_HEADS * RB, GLA_WIDTH), 0) // RB
    v_lanes = lax.broadcasted_iota(jnp.int32, (GLA_HEADS * RB, GLA_WIDTH), 1) // GLA_DV
    value_heads = v_rows == v_lanes
    qr = lax.broadcasted_iota(jnp.int32, (RB, GLA_HEADS * RB), 0)
    kc = lax.broadcasted_iota(jnp.int32, (RB, GLA_HEADS * RB), 1) % RB
    same_chunk = (qr // C) == (kc // C)
    forward = kc <= qr
    zero = jnp.zeros((), BF16)

    def outputs(i, carry):
        r0 = pl.multiple_of(i * RB, RB)
        rows = pl.ds(r0, RB)
        v = gv_ref[0, rows, :]
        v4 = jnp.where(value_heads, jnp.concatenate([v] * GLA_HEADS, axis=0), zero)
        qe = [qe_s[d, rows, :] for d in range(2)]
        att = []
        for d in range(2):
            ke4 = jnp.where(key_heads, jnp.concatenate([ke_s[d, rows, :]] * GLA_HEADS, axis=0), zero)
            att.append(lax.dot_general(qe[d], ke4, nt, preferred_element_type=F32))
        both = jnp.where(same_chunk, jnp.where(forward, att[0], att[1]), 0.0).astype(BF16)
        qe2 = jnp.concatenate(qe, axis=1)
        inter = [lax.dot_general(qe2[cc * C:(cc + 1) * C], prev_s[i * (RB // C) + cc], nt,
                                 preferred_element_type=F32) for cc in range(RB // C)]
        o_s[rows, :] = jnp.dot(both, v4, preferred_element_type=F32) + jnp.concatenate(inter, axis=0)
        return carry

    lax.fori_loop(0, T // RB, outputs, 0, unroll=6)

    u_s[0:CONV_PAD, :] = jnp.zeros((CONV_PAD, CONV_WIDTH), F32)
    u_s[CONV_PAD + T:, :] = jnp.zeros((CONV_PAD, CONV_WIDTH), F32)
    u_s[CONV_PAD:CONV_PAD + T, :] = cc_ref[0].astype(F32) * ch_ref[0].astype(F32)
    w0 = cw_ref[0:1, :]
    w1 = cw_ref[1:2, :]
    w2 = cw_ref[2:3, :]
    for e in range(T // EPI_ROWS):
        r0 = e * EPI_ROWS
        o = o_s[r0:r0 + EPI_ROWS, :]
        ss = jnp.dot((o * o).astype(BF16), bd_ref[...], preferred_element_type=F32) * (1.0 / GLA_DV)
        on = o * lax.rsqrt(ss + EPS) * ng_ref[...]
        r = gr_ref[0, r0:r0 + EPI_ROWS, :].astype(F32)
        o_ref[0, r0:r0 + EPI_ROWS, 0:GLA_WIDTH] = (on * _silu(r)).astype(BF16)
        t = r0 + lax.broadcasted_iota(jnp.int32, (EPI_ROWS, 1), 0)
        up = u_s[CONV_PAD + r0 - 1:CONV_PAD + r0 - 1 + EPI_ROWS, :]
        mid = u_s[CONV_PAD + r0:CONV_PAD + r0 + EPI_ROWS, :]
        dn = u_s[CONV_PAD + r0 + 1:CONV_PAD + r0 + 1 + EPI_ROWS, :]
        up = jnp.where(t == n_lat, 0.0, up)
        dn = jnp.where(t == n_lat - 1, 0.0, dn)
        conv = w0 * up + w1 * mid + w2 * dn
        o_ref[0, r0:r0 + EPI_ROWS, GLA_WIDTH:] = (cb_ref[0, r0:r0 + EPI_ROWS, :].astype(F32) * conv).astype(BF16)


def _gla_cum_matrices():
    i = np.arange(GLA_BLOCK_A)
    same = (i[:, None] // GLA_CHUNK) == (i[None, :] // GLA_CHUNK)
    fwd = same & (i[None, :] <= i[:, None])
    bwd = same & (i[None, :] >= i[:, None])
    mats = np.stack([np.concatenate([fwd, same], axis=0), np.concatenate([bwd, same], axis=0)])
    return jnp.asarray(mats.astype(np.float32), dtype=BF16)


def _gla_conv(p, wg, gbias, ng, cw, bd, cum, n_lat):
    B, T, _ = p.shape
    nc = T // GLA_CHUNK

    def col(width, start):
        return pl.BlockSpec((1, T, width), lambda b: (b, 0, start // width))

    def const(shape):
        return pl.BlockSpec(shape, lambda b: (0,) * len(shape))

    return pl.pallas_call(
        functools.partial(_gla_body, n_lat=n_lat, n_ctx=T - n_lat),
        grid=(B,),
        in_specs=[col(GLA_QK_WIDTH, COL_GQ), col(GLA_QK_WIDTH, COL_GK), col(GLA_WIDTH, COL_GV), col(GLA_WIDTH, COL_GR),
                  col(CONV_WIDTH, COL_CB), col(CONV_WIDTH, COL_CC), col(CONV_WIDTH, COL_CH), col(LANES, COL_GT),
                  const((LANES, 2 * GLA_QK_WIDTH)), const((8, 2 * GLA_QK_WIDTH)),
                  const((1, GLA_WIDTH)), const((8, CONV_WIDTH)), const((GLA_WIDTH, GLA_WIDTH)),
                  const((2, 2 * GLA_BLOCK_A, GLA_BLOCK_A))],
        out_specs=pl.BlockSpec((1, T, GLA_WIDTH + CONV_WIDTH), lambda b: (b, 0, 0)),
        out_shape=jax.ShapeDtypeStruct((B, T, GLA_WIDTH + CONV_WIDTH), BF16),
        scratch_shapes=[pltpu.VMEM((2, T, GLA_QK_WIDTH), BF16), pltpu.VMEM((2, T, GLA_QK_WIDTH), BF16),
                        pltpu.VMEM((2, T, GLA_QK_WIDTH), F32),
                        pltpu.VMEM((nc, GLA_WIDTH, 2 * GLA_QK_WIDTH), F32),
                        pltpu.VMEM((nc, GLA_WIDTH, 2 * GLA_QK_WIDTH), BF16),
                        pltpu.VMEM((2, GLA_WIDTH, GLA_QK_WIDTH), F32),
                        pltpu.VMEM((T, GLA_WIDTH), F32),
                        pltpu.VMEM((T + 2 * CONV_PAD, CONV_WIDTH), F32)],
        compiler_params=_cparams(1),
        name="gla_conv",
    )(p, p, p, p, p, p, p, p, wg, gbias, ng, cw, bd, cum)


def _route(logits_t):
    mx = jnp.max(logits_t, axis=0, keepdims=True)
    ex = jnp.exp(logits_t - mx)
    probs = ex / jnp.sum(ex, axis=0, keepdims=True)
    P = [probs[e:e + 1] for e in range(N_EXPERTS)]
    scores = []
    for g in range(N_GROUPS):
        a, b, c, d = P[4 * g:4 * g + 4]
        scores.append(jnp.maximum(jnp.maximum(jnp.maximum(a + b, a + c), jnp.maximum(a + d, b + c)),
                                  jnp.maximum(b + d, c + d)))
    best = jnp.maximum(jnp.maximum(scores[0], scores[1]), jnp.maximum(scores[2], scores[3]))
    taken = jnp.zeros_like(best, dtype=jnp.bool_)
    sel = []
    for g in range(N_GROUPS):
        s = (scores[g] == best) & jnp.logical_not(taken)
        sel.append(s)
        taken = taken | s
    gsel = jnp.where(sel[1], 1.0, 0.0) + jnp.where(sel[2], 2.0, 0.0) + jnp.where(sel[3], 3.0, 0.0)
    ig = [jnp.where(sel[0], P[j], jnp.where(sel[1], P[4 + j], jnp.where(sel[2], P[8 + j], P[12 + j])))
          for j in range(EXPERTS_PER_GROUP)]

    def first_max(vals):
        v = jnp.maximum(jnp.maximum(vals[0], vals[1]), jnp.maximum(vals[2], vals[3]))
        tk = jnp.zeros_like(v, dtype=jnp.bool_)
        hot = []
        for x in vals:
            s = (x == v) & jnp.logical_not(tk)
            hot.append(s)
            tk = tk | s
        idx = jnp.where(hot[1], 1.0, 0.0) + jnp.where(hot[2], 2.0, 0.0) + jnp.where(hot[3], 3.0, 0.0)
        return v, hot, idx

    _, hot1, i1 = first_max(ig)
    _, _, i2 = first_max([jnp.where(hot1[j], -1.0, ig[j]) for j in range(EXPERTS_PER_GROUP)])
    lo = jnp.minimum(i1, i2)
    hi = jnp.maximum(i1, i2)
    pair = jnp.where(lo == 0.0, hi - 1.0, jnp.where(lo == 1.0, hi + 1.0, 5.0))
    return gsel * N_PAIRS + pair


def _class_rank(cls, tri_ref, cnt_s):
    n = cls.shape[1]
    cid = lax.broadcasted_iota(jnp.int32, (CLS_ROWS, n), 0).astype(F32)
    onehot = jnp.where(cls == cid, 1.0, 0.0)
    segs = [onehot[:, k * LANES:(k + 1) * LANES] for k in range(n // LANES)]
    before = jnp.dot(jnp.concatenate(segs, axis=0).astype(BF16), tri_ref[...], preferred_element_type=F32)
    base = cnt_s[...]
    ranks = []
    for k, seg in enumerate(segs):
        ranks.append(jnp.sum(seg * (before[k * CLS_ROWS:(k + 1) * CLS_ROWS] + base), axis=0, keepdims=True))
        base = base + jnp.sum(seg, axis=1, keepdims=True)
    cnt_s[...] = base
    return jnp.concatenate(ranks, axis=1)


def _out_body(*refs, n_lat, tm, n_x):
    x_refs = refs[:n_x]
    ya_ref, yg_ref, mod_ref, wo_ref, g_ref, wr_ref, br_ref, tri_ref, xo_ref, rt_ref, cnt_ref, cnt_s, logit_s = refs[n_x:]
    j = pl.program_id(1)
    is_last_tile = j == pl.num_programs(1) - 1

    @pl.when((pl.program_id(0) == 0) & (j == 0))
    def _():
        cnt_s[...] = jnp.zeros_like(cnt_s)

    mod = mod_ref[0]
    subs = list(enumerate(range(0, tm, SUB_ROWS)))
    ys = [jnp.dot(ya_ref[0, r0:r0 + SUB_ROWS, :], wo_ref[0:ATT_WIDTH, :], preferred_element_type=F32)
          + jnp.dot(yg_ref[0, r0:r0 + SUB_ROWS, :], wo_ref[ATT_WIDTH:, :], preferred_element_type=F32) for _, r0 in subs]
    h2s = []
    for k, r0 in subs:
        row = j * tm + r0 + lax.broadcasted_iota(jnp.int32, (SUB_ROWS, 1), 0)
        is_ctx = row >= n_lat
        xn = _stream_piece(x_refs, k, tm // SUB_ROWS, is_last_tile) + _row_mod(mod, is_ctx, 2) * ys[k]
        xo_ref[0, r0:r0 + SUB_ROWS, :] = xn
        h2s.append(_norm_modulate(xn, g_ref[...], mod, is_ctx, 3, 4).astype(BF16))
    for k, r0 in subs:
        logit_s[r0:r0 + SUB_ROWS, :] = jnp.dot(h2s[k], wr_ref[...], preferred_element_type=F32) + br_ref[0:1, :]
    cls = _route(logit_s[...].T[0:N_EXPERTS, :])
    rank = _class_rank(cls, tri_ref, cnt_s)
    rt_ref[0] = jnp.concatenate([cls, rank, jnp.zeros((6, tm), F32)], axis=0).astype(jnp.int32)
    cnt_ref[...] = jnp.broadcast_to(cnt_s[...], (CLS_ROWS, LANES)).astype(jnp.int32)


def _out_proj(ya, yg, stream, ctx, modv, wo, gain, wr, br, tri, n_lat, rows, tm):
    B = stream.shape[0]
    nj = rows // tm
    x_specs, x_args = _stream_specs(stream, ctx, tm, lambda g: g[0], lambda g: g[1])
    return pl.pallas_call(
        functools.partial(_out_body, n_lat=n_lat, tm=tm, n_x=len(x_args)),
        grid=(B, nj),
        in_specs=x_specs + [
                  pl.BlockSpec((1, tm, ATT_WIDTH), lambda b, j: (b, j, 0)),
                  pl.BlockSpec((1, tm, GLA_WIDTH + CONV_WIDTH), lambda b, j: (b, j, 0)),
                  pl.BlockSpec((1, 16, D_MODEL), lambda b, j: (b, 0, 0)),
                  pl.BlockSpec((D_MODEL, D_MODEL), lambda b, j: (0, 0)),
                  pl.BlockSpec((1, D_MODEL), lambda b, j: (0, 0)),
                  pl.BlockSpec((D_MODEL, LANES), lambda b, j: (0, 0)),
                  pl.BlockSpec((8, LANES), lambda b, j: (0, 0)),
                  pl.BlockSpec((LANES, LANES), lambda b, j: (0, 0))],
        out_specs=[pl.BlockSpec((1, tm, D_MODEL), lambda b, j: (b, j, 0)),
                   pl.BlockSpec((1, 8, tm), lambda b, j: (b * nj + j, 0, 0)),
                   pl.BlockSpec((CLS_ROWS, LANES), lambda b, j: (0, 0))],
        out_shape=[jax.ShapeDtypeStruct((B, rows, D_MODEL), F32),
                   jax.ShapeDtypeStruct((B * nj, 8, tm), jnp.int32),
                   jax.ShapeDtypeStruct((CLS_ROWS, LANES), jnp.int32)],
        scratch_shapes=[pltpu.VMEM((CLS_ROWS, 1), F32), pltpu.VMEM((tm, LANES), F32)],
        compiler_params=_cparams(2),
        name="out_proj_router",
    )(*x_args, ya, yg, modv, wo, gain, wr, br, tri)


ROW_UNROLL = 8
IDX_STRIDE = 1024


def _idx_slot(idx, slot, tm):
    return idx.at[pl.ds(pl.multiple_of(slot * IDX_STRIDE, IDX_STRIDE), tm)]


def _issue_rows(tm, idx, slot, make_copy):
    base = slot * IDX_STRIDE

    def trip(i, c):
        for u in range(ROW_UNROLL):
            r = i * ROW_UNROLL + u
            make_copy(r, idx[base + r]).start(priority=u % 2)
        return c

    lax.fori_loop(0, tm // ROW_UNROLL, trip, 0)


PAD_SLOTS = 32
PAD_BITS = 9


def _zero_pad_rows(pad_ref, zeros_ref, hs_ref, sem):
    def pieces(c, fn):
        start = pad_ref[c]
        n = pad_ref[PAD_SLOTS + c]
        for bit in range(PAD_BITS):
            size = 1 << bit
            below = n & (size - 1)

            @pl.when((n & size) != 0)
            def _():
                fn(pltpu.make_async_copy(zeros_ref.at[pl.ds(0, size)], hs_ref.at[pl.ds(start + below, size)], sem))

        def block(i, carry):
            fn(pltpu.make_async_copy(zeros_ref, hs_ref.at[pl.ds(start + n + i * MOE_TILE, MOE_TILE)], sem))
            return carry

        lax.fori_loop(0, pad_ref[2 * PAD_SLOTS + c], block, 0)

    def start_all(c, carry):
        pieces(c, lambda cp: cp.start())
        return carry

    def wait_all(c, carry):
        pieces(c, lambda cp: cp.wait())
        return carry

    lax.fori_loop(0, N_CLASSES + 1, start_all, 0)
    lax.fori_loop(0, N_CLASSES + 1, wait_all, 0)


def _disp_body(pad_ref, x_ref, mod_ref, g_ref, dest_ref, hs_ref, buf, idx, sem_i, sem_d, sem_z, *, n_lat, tm, nj,
               n_steps):
    j = pl.program_id(1)
    s = pl.program_id(0) * nj + j
    last = n_steps - 1
    slot = s % 2

    def idx_fetch(step, sl):
        return pltpu.make_async_copy(dest_ref.at[step], _idx_slot(idx, sl, tm), sem_i.at[sl])

    def drain(sl):
        pltpu.make_async_copy(buf.at[sl], hs_ref.at[pl.ds(0, tm)], sem_d.at[sl]).wait()

    @pl.when(s == 0)
    def _():
        idx_fetch(s, slot).start()
        buf[1] = jnp.zeros((tm, 8, LANES), F32)
        _zero_pad_rows(pad_ref, buf.at[1, pl.ds(0, MOE_TILE)], hs_ref, sem_z)

    row = j * tm + lax.broadcasted_iota(jnp.int32, (tm, 1), 0)
    is_ctx = row >= n_lat
    h2 = _norm_modulate(x_ref[0], g_ref[...], mod_ref[0], is_ctx, 3, 4)
    buf[slot] = h2.reshape(tm, 8, LANES)
    idx_fetch(s, slot).wait()
    _issue_rows(tm, idx, slot, lambda r, d: pltpu.make_async_copy(buf.at[slot, r], hs_ref.at[d], sem_d.at[slot]))

    @pl.when(s > 0)
    def _():
        drain(1 - slot)

    @pl.when(s < last)
    def _():
        idx_fetch(s + 1, 1 - slot).start()

    @pl.when(s == last)
    def _():
        drain(slot)


def _dispatch(pad, xx, modv, gain, dest, n_rows, n_lat, tm):
    B, rows, _ = xx.shape
    nj = rows // tm
    assert tm >= MOE_TILE
    return pl.pallas_call(
        functools.partial(_disp_body, n_lat=n_lat, tm=tm, nj=nj, n_steps=B * nj),
        grid=(B, nj),
        in_specs=[pl.BlockSpec(memory_space=pltpu.SMEM),
                  pl.BlockSpec((1, tm, D_MODEL), lambda b, j: (b, j, 0)),
                  pl.BlockSpec((1, 16, D_MODEL), lambda b, j: (b, 0, 0)),
                  pl.BlockSpec((1, D_MODEL), lambda b, j: (0, 0)),
                  pl.BlockSpec(memory_space=pl.ANY)],
        out_specs=pl.BlockSpec(memory_space=pl.ANY),
        scratch_shapes=[pltpu.VMEM((2, tm, 8, LANES), F32), pltpu.SMEM((2 * IDX_STRIDE,), jnp.int32),
                        pltpu.SemaphoreType.DMA((2,)), pltpu.SemaphoreType.DMA((2,)), pltpu.SemaphoreType.DMA],
        out_shape=jax.ShapeDtypeStruct((n_rows, 8, LANES), F32),
        compiler_params=_cparams(2),
        name="dispatch",
    )(pad, xx, modv, gain, dest)


def _moe_body(tile_ref, e_lo_ref, e_hi_ref, valid_ref, hs_ref, wg1, wu1, wd1, wg2, wu2, wd2, wr_ref, br_ref, ys_ref):
    g = pl.program_id(0)
    tm = hs_ref.shape[0]

    @pl.when(valid_ref[g] == 1)
    def _():
        x = hs_ref[...].reshape(tm, D_MODEL)
        e_lo = e_lo_ref[g]
        e_hi = e_hi_ref[g]
        dw = wr_ref[pl.ds(e_lo, 1), :] - wr_ref[pl.ds(e_hi, 1), :]
        d = jnp.sum(x * dw, axis=-1, keepdims=True) + (br_ref[e_lo] - br_ref[e_hi])
        w_lo = jax.nn.sigmoid(d)
        w_hi = jax.nn.sigmoid(-d)
        h = x.astype(BF16)

        def act(wg, wu, w):
            a = _silu(jnp.dot(h, wg[0], preferred_element_type=F32)) * jnp.dot(h, wu[0], preferred_element_type=F32)
            return (a * w).astype(BF16)

        y = (jnp.dot(act(wg1, wu1, w_lo), wd1[0], preferred_element_type=F32)
             + jnp.dot(act(wg2, wu2, w_hi), wd2[0], preferred_element_type=F32))
        ys_ref[...] = y.reshape(tm, 8, LANES)

    @pl.when(valid_ref[g] == 0)
    def _():
        ys_ref[...] = jnp.zeros_like(ys_ref)


def _moe(hs, tile, e_lo, e_hi, valid, wg, wu, wd, wr_t, br):
    n_tiles = tile.shape[0]
    tm = MOE_TILE

    def w_in(sel):
        return pl.BlockSpec((1, D_MODEL, D_EXPERT), lambda g, t, lo, hi, v: ((lo, hi)[sel][g], 0, 0))

    def w_out(sel):
        return pl.BlockSpec((1, D_EXPERT, D_MODEL), lambda g, t, lo, hi, v: ((lo, hi)[sel][g], 0, 0))

    return pl.pallas_call(
        _moe_body,
        grid_spec=pltpu.PrefetchScalarGridSpec(
            num_scalar_prefetch=4,
            grid=(n_tiles,),
            in_specs=[pl.BlockSpec((tm, 8, LANES), lambda g, t, lo, hi, v: (t[g], 0, 0)),
                      w_in(0), w_in(0), w_out(0), w_in(1), w_in(1), w_out(1),
                      pl.BlockSpec((N_EXPERTS, D_MODEL), lambda g, t, lo, hi, v: (0, 0)),
                      pl.BlockSpec(memory_space=pltpu.SMEM)],
            out_specs=pl.BlockSpec((tm, 8, LANES), lambda g, t, lo, hi, v: (g, 0, 0))),
        out_shape=jax.ShapeDtypeStruct((n_tiles * tm, 8, LANES), F32),
        compiler_params=_cparams(1),
        name="moe_pairs",
    )(tile, e_lo, e_hi, valid, hs, wg, wu, wd, wg, wu, wd, wr_t, br)


def _gathered_slot(dest_ref, ys_ref, buf, idx, sem_i, sem_d, s, n_steps, tm):
    last = n_steps - 1
    slot = s % 2

    def idx_fetch(step, sl):
        return pltpu.make_async_copy(dest_ref.at[step], _idx_slot(idx, sl, tm), sem_i.at[sl])

    def gather(sl):
        _issue_rows(tm, idx, sl, lambda r, d: pltpu.make_async_copy(ys_ref.at[d], buf.at[sl, r], sem_d.at[sl]))

    @pl.when(s == 0)
    def _():
        idx_fetch(0, 0).start()
        idx_fetch(0, 0).wait()
        gather(0)
        if last > 0:
            idx_fetch(1, 1).start()

    @pl.when(s < last)
    def _():
        idx_fetch(s + 1, 1 - slot).wait()
        gather(1 - slot)

    @pl.when(s + 2 <= last)
    def _():
        idx_fetch(s + 2, slot).start()

    pltpu.make_async_copy(ys_ref.at[pl.ds(0, tm)], buf.at[slot], sem_d.at[slot]).wait()
    return slot


def _fin_body(x_ref, mod_ref, dest_ref, ys_ref, xo_ref, buf, idx, sem_i, sem_d, *, n_lat, tm, nj, n_steps):
    j = pl.program_id(1)
    slot = _gathered_slot(dest_ref, ys_ref, buf, idx, sem_i, sem_d, pl.program_id(0) * nj + j, n_steps, tm)
    row = j * tm + lax.broadcasted_iota(jnp.int32, (tm, 1), 0)
    is_ctx = row >= n_lat
    xo_ref[0] = x_ref[0] + _row_mod(mod_ref[0], is_ctx, 5) * buf[slot].reshape(tm, D_MODEL)


def _fin_in_body(x_ref, modp_ref, dest_ref, ys_ref, modn_ref, g_ref, w_ref, cos_ref, sin_ref, qkg_ref, bd_ref,
                 xo_ref, p_ref, buf, idx, sem_i, sem_d, *, n_lat, tm, nj, n_steps):
    j = pl.program_id(1)
    slot = _gathered_slot(dest_ref, ys_ref, buf, idx, sem_i, sem_d, pl.program_id(0) * nj + j, n_steps, tm)

    def piece(k, is_ctx):
        rows = slice(k * SUB_ROWS, (k + 1) * SUB_ROWS)
        y = buf[slot, rows].reshape(SUB_ROWS, D_MODEL)
        xn = x_ref[0, rows, :] + _row_mod(modp_ref[0], is_ctx, 5) * y
        xo_ref[0, rows, :] = xn
        return xn

    _in_tile(piece, j, modn_ref, g_ref, w_ref, cos_ref, sin_ref, qkg_ref, bd_ref, p_ref, n_lat, tm)


def _combine_in_proj(xx, modv_prev, dest, ys, modv_next, gain, w, cos, sin, qkg, bd, n_lat):
    B, T, _ = xx.shape
    tm = TOKEN_TILE
    nj = T // tm

    def const(shape):
        return pl.BlockSpec(shape, lambda b, j: (0,) * len(shape))

    return pl.pallas_call(
        functools.partial(_fin_in_body, n_lat=n_lat, tm=tm, nj=nj, n_steps=B * nj),
        grid=(B, nj),
        in_specs=[pl.BlockSpec((1, tm, D_MODEL), lambda b, j: (b, j, 0)),
                  pl.BlockSpec((1, 16, D_MODEL), lambda b, j: (b, 0, 0)),
                  pl.BlockSpec(memory_space=pl.ANY),
                  pl.BlockSpec(memory_space=pl.ANY),
                  pl.BlockSpec((1, 16, D_MODEL), lambda b, j: (b, 0, 0)),
                  const((1, D_MODEL)), const((D_MODEL, N_PROJ)),
                  pl.BlockSpec((tm, LANES), lambda b, j: (j, 0)),
                  pl.BlockSpec((tm, LANES), lambda b, j: (j, 0)),
                  const((8, LANES)), const((LANES, LANES))],
        out_specs=[pl.BlockSpec((1, tm, D_MODEL), lambda b, j: (b, j, 0)),
                   pl.BlockSpec((1, tm, N_PROJ), lambda b, j: (b, j, 0))],
        out_shape=[jax.ShapeDtypeStruct((B, T, D_MODEL), F32), jax.ShapeDtypeStruct((B, T, N_PROJ), BF16)],
        scratch_shapes=[pltpu.VMEM((2, tm, 8, LANES), F32), pltpu.SMEM((2 * IDX_STRIDE,), jnp.int32),
                        pltpu.SemaphoreType.DMA((2,)), pltpu.SemaphoreType.DMA((2,))],
        compiler_params=_cparams(2),
        name="combine_in_proj",
    )(xx, modv_prev, dest, ys, modv_next, gain, w, cos, sin, qkg, bd)


def _combine(xx, modv, dest, ys, n_lat, tm):
    B, rows, _ = xx.shape
    nj = rows // tm
    return pl.pallas_call(
        functools.partial(_fin_body, n_lat=n_lat, tm=tm, nj=nj, n_steps=B * nj),
        grid=(B, nj),
        in_specs=[pl.BlockSpec((1, tm, D_MODEL), lambda b, j: (b, j, 0)),
                  pl.BlockSpec((1, 16, D_MODEL), lambda b, j: (b, 0, 0)),
                  pl.BlockSpec(memory_space=pl.ANY),
                  pl.BlockSpec(memory_space=pl.ANY)],
        out_specs=pl.BlockSpec((1, tm, D_MODEL), lambda b, j: (b, j, 0)),
        scratch_shapes=[pltpu.VMEM((2, tm, 8, LANES), F32), pltpu.SMEM((2 * IDX_STRIDE,), jnp.int32),
                        pltpu.SemaphoreType.DMA((2,)), pltpu.SemaphoreType.DMA((2,))],
        out_shape=jax.ShapeDtypeStruct((B, rows, D_MODEL), F32),
        compiler_params=_cparams(2),
        name="combine",
    )(xx, modv, dest, ys)


def _rope_tables(n_lat, n_ctx):
    rows = n_lat // GRID_W
    row, col = jnp.meshgrid(jnp.arange(rows), jnp.arange(GRID_W), indexing="ij")
    n_freq = HEAD_DIM // 4
    inv_freq = ROPE_BASE ** (-jnp.arange(n_freq, dtype=F32) / n_freq)
    ang = jnp.concatenate([row.reshape(-1, 1).astype(F32) * inv_freq, col.reshape(-1, 1).astype(F32) * inv_freq],
                          axis=-1)
    cos = jnp.tile(jnp.cos(ang), (1, LANES // (HEAD_DIM // 2)))
    sin = jnp.tile(jnp.sin(ang), (1, LANES // (HEAD_DIM // 2)))
    sign = jnp.where((jnp.arange(LANES) % HEAD_DIM) < HEAD_DIM // 2, -1.0, 1.0).astype(F32)
    cos = jnp.concatenate([cos, jnp.ones((n_ctx, LANES), F32)], axis=0)
    sin = jnp.concatenate([sin * sign, jnp.zeros((n_ctx, LANES), F32)], axis=0)
    return cos, sin


def _block_diag_ones(n, blk):
    i = np.arange(n) // blk
    return jnp.asarray((i[:, None] == i[None, :]).astype(np.float32), dtype=BF16)


def _routing_tables(counts, n_tiles):
    tiles_c = (counts + MOE_TILE - 1) // MOE_TILE
    tile_end = jnp.cumsum(tiles_c)
    tile_start = tile_end - tiles_c
    off = tile_start * MOE_TILE
    total = tile_end[-1]
    g = jnp.arange(n_tiles, dtype=jnp.int32)
    valid = (g < total).astype(jnp.int32)
    g_eff = jnp.minimum(g, total - 1)
    c_of = jnp.sum((g_eff[:, None] >= tile_end[None, :]).astype(jnp.int32), axis=1)
    group = c_of // N_PAIRS
    pair = c_of % N_PAIRS
    lo = jnp.asarray(PAIR_LO, jnp.int32)
    hi = jnp.asarray(PAIR_HI, jnp.int32)
    e_lo = group * EXPERTS_PER_GROUP + jnp.sum((pair[:, None] == jnp.arange(N_PAIRS)[None, :]) * lo[None, :], axis=1)
    e_hi = group * EXPERTS_PER_GROUP + jnp.sum((pair[:, None] == jnp.arange(N_PAIRS)[None, :]) * hi[None, :], axis=1)
    fill = jnp.zeros((PAD_SLOTS - N_CLASSES - 1,), jnp.int32)
    pad = jnp.concatenate([off + counts, (total * MOE_TILE)[None], fill,
                           tiles_c * MOE_TILE - counts, jnp.zeros((1,), jnp.int32), fill,
                           jnp.zeros((N_CLASSES,), jnp.int32), (n_tiles - total)[None], fill])
    return off.astype(jnp.int32), pad.astype(jnp.int32), g_eff, e_lo.astype(jnp.int32), e_hi.astype(jnp.int32), valid


def kernel(x, c, ctx, c_ctx, w_ada, b_ada, norm_mix_g, norm_ffn_g, w_in, q_norm_g, k_norm_g, attn_sink, gla_gate_w,
           gla_gate_b, gla_norm_g, conv_w, w_out, w_router, b_router, w_gate_e, w_up_e, w_down_e):
    B, S, D = x.shape
    L = ctx.shape[1]
    T = S + L
    assert D == D_MODEL and T % TOKEN_TILE == 0 and S % LAT_TILE == 0 and S % GRID_W == 0
    assert S % ATT_BLOCK == 0 and L % ATT_BLOCK == 0 and S >= ATT_SPAN and T % EPI_ROWS == 0

    cond_rows = -(-(B + 1) // 8) * 8
    cond = jnp.zeros((cond_rows, D), F32).at[:B].set(c).at[B].set(c_ctx)
    mod_all = _modulation(cond, w_ada, b_ada)

    cos, sin = _rope_tables(S, L)
    bd_head = _block_diag_ones(LANES, HEAD_DIM)
    bd_gla = _block_diag_ones(GLA_WIDTH, GLA_DV)
    cum = _gla_cum_matrices()
    tri = jnp.asarray(np.triu(np.ones((LANES, LANES), np.float32), 1), dtype=BF16)
    stream, stream_ctx = x, ctx

    order = jnp.asarray(ATT_HEAD_ORDER)

    def mod_table(l):
        m_lat = mod_all[l, :B].reshape(B, 6, D)
        m_ctx = jnp.broadcast_to(mod_all[l, B].reshape(1, 6, D), (B, 6, D))
        return jnp.concatenate([m_lat, m_ctx, jnp.zeros((B, 4, D), F32)], axis=1)

    def in_proj_args(l):
        wl = w_in[l]
        wq = wl[:, :ATT_WIDTH].reshape(D, ATT_HEADS, HEAD_DIM)[:, order, :].reshape(D, ATT_WIDTH)
        w_perm = jnp.concatenate([wq, wl[:, ATT_WIDTH:1536], wl[:, 1568:], wl[:, 1536:1568],
                                  jnp.zeros((D, N_PROJ - wl.shape[1]), F32)], axis=1).astype(BF16)
        qg = jnp.tile(q_norm_g[l], LANES // HEAD_DIM) * (HEAD_DIM ** -0.5)
        kg = jnp.tile(k_norm_g[l], LANES // HEAD_DIM)
        qkg = jnp.stack([qg] * (ATT_WIDTH // LANES) + [kg] + [jnp.zeros_like(kg)] * 3)
        return mod_table(l), norm_mix_g[l].reshape(1, D), w_perm, cos, sin, qkg, bd_head

    p = _in_proj(stream, stream_ctx, *in_proj_args(0), S)
    for l in range(DEPTH):
        last = l == DEPTH - 1
        modv = mod_table(l)

        y_att = _attention(p, attn_sink[l], S, not last)

        pad_rows = jnp.zeros((LANES - 2 * GLA_GATE_RANK, GLA_QK_WIDTH), F32)
        zero_rank = jnp.zeros((GLA_GATE_RANK, GLA_QK_WIDTH), F32)
        wgf = jnp.concatenate([gla_gate_w[l, 0], zero_rank, pad_rows], axis=0)
        wgb = jnp.concatenate([zero_rank, gla_gate_w[l, 1], pad_rows], axis=0)
        wg = jnp.concatenate([wgf, wgb], axis=1).astype(BF16)
        gbias = jnp.concatenate([gla_gate_b[l].reshape(1, 2 * GLA_QK_WIDTH),
                                 jnp.zeros((7, 2 * GLA_QK_WIDTH), F32)], axis=0)
        ng = jnp.tile(gla_norm_g[l], GLA_HEADS).reshape(1, GLA_WIDTH)
        cw = jnp.concatenate([conv_w[l], jnp.zeros((5, CONV_WIDTH), F32)], axis=0)
        y_gc = _gla_conv(p, wg, gbias, ng, cw, bd_gla, cum, S)

        rows, tm = (S, LAT_TILE) if last else (T, TOKEN_TILE)
        wr = jnp.concatenate([w_router, jnp.zeros((D, LANES - N_EXPERTS), F32)], axis=1).astype(BF16)
        br = jnp.zeros((8, LANES), F32).at[0, :N_EXPERTS].set(b_router)
        ffn_g = norm_ffn_g[l].reshape(1, D)
        wo_att = w_out[l, :ATT_WIDTH].reshape(ATT_HEADS, HEAD_DIM, D)[order].reshape(ATT_WIDTH, D)
        wo = jnp.concatenate([wo_att, w_out[l, ATT_WIDTH:]], axis=0).astype(BF16)
        xx_mid, route, counts = _out_proj(y_att, y_gc, stream, stream_ctx, modv, wo, ffn_g, wr, br, tri, S, rows, tm)

        n_tiles = -(-(B * rows) // MOE_TILE) + N_CLASSES
        off, pad, tile, e_lo, e_hi, valid = _routing_tables(counts[:N_CLASSES, 0], n_tiles)

        dest = route[:, 1, :]
        for cls_id in range(N_CLASSES):
            dest = dest + jnp.where(route[:, 0, :] == cls_id, off[cls_id], 0)
        hs = _dispatch(pad, xx_mid, modv, ffn_g, dest, n_tiles * MOE_TILE, S, tm)
        ys = _moe(hs, tile, e_lo, e_hi, valid, w_gate_e[l].astype(BF16), w_up_e[l].astype(BF16),
                  w_down_e[l].astype(BF16), w_router.T, b_router)
        if last:
            return _combine(xx_mid, modv, dest, ys, S, tm)
        (stream, p), stream_ctx = _combine_in_proj(xx_mid, modv, dest, ys, *in_proj_args(l + 1), S), None
```

```python
import functools

import numpy as np
import jax
import jax.numpy as jnp
from jax import lax
from jax.experimental import pallas as pl
from jax.experimental.pallas import tpu as pltpu

D_MODEL = 1024
DEPTH = 2
GRID_W = 64
EPS = 1e-6
HEAD_DIM = 64
ATT_HEADS = 8
ATT_KV_HEADS = 2
ATT_GROUP = ATT_HEADS // ATT_KV_HEADS
ATT_WIDTH = ATT_HEADS * HEAD_DIM
WINDOW = 128
ROPE_BASE = 10000.0
GLA_HEADS = 4
GLA_DV = 64
GLA_DK = 32
GLA_WIDTH = GLA_HEADS * GLA_DV
GLA_GATE_RANK = 16
GLA_GATE_NORM = 16.0
GLA_CHUNK = 64
CONV_WIDTH = 256
N_EXPERTS = 16
N_GROUPS = 4
EXPERTS_PER_GROUP = 4
D_EXPERT = D_MODEL // 2

LANES = 128
KV_WIDTH = ATT_KV_HEADS * HEAD_DIM
GLA_QK_WIDTH = GLA_HEADS * GLA_DK
COL_AQ, COL_AK, COL_AV = 0, 512, 640
COL_GQ, COL_GK, COL_GV, COL_GR = 768, 896, 1024, 1280
COL_CB, COL_CC, COL_CH, COL_GT = 1536, 1792, 2048, 2304
N_PROJ = 2432
QK_COLS = COL_AV
N_PAIRS = 6
N_CLASSES = N_GROUPS * N_PAIRS
PAIR_LO = (0, 0, 0, 1, 1, 2)
PAIR_HI = (1, 2, 3, 2, 3, 3)
CLS_ROWS = 32
NEG = -1e30

TOKEN_TILE = 768
LAT_TILE = 1024
MOE_TILE = 512
SUB_ROWS = 256
VMEM_LIMIT = 56 * 1024 * 1024

F32 = jnp.float32
BF16 = jnp.bfloat16


def _cparams(n_axes):
    return pltpu.CompilerParams(dimension_semantics=("arbitrary",) * n_axes, vmem_limit_bytes=VMEM_LIMIT)


def _silu(x):
    return x * jax.nn.sigmoid(x)


def _mod_body(c_ref, w_ref, b_ref, o_ref):
    c = c_ref[...]
    a = _silu(c).astype(BF16)
    o_ref[0] = jnp.dot(a, w_ref[0].astype(BF16), preferred_element_type=F32) + b_ref[0]


def _modulation(cond, w_ada, b_ada):
    rows = cond.shape[0]
    nblk = w_ada.shape[2] // D_MODEL
    return pl.pallas_call(
        _mod_body,
        grid=(DEPTH, nblk),
        in_specs=[pl.BlockSpec((rows, D_MODEL), lambda l, n: (0, 0)),
                  pl.BlockSpec((1, D_MODEL, D_MODEL), lambda l, n: (l, 0, n)),
                  pl.BlockSpec((1, 1, D_MODEL), lambda l, n: (l, 0, n))],
        out_specs=pl.BlockSpec((1, rows, D_MODEL), lambda l, n: (l, 0, n)),
        out_shape=jax.ShapeDtypeStruct((DEPTH, rows, w_ada.shape[2]), F32),
        compiler_params=_cparams(2),
        name="modulation",
    )(cond, w_ada, b_ada.reshape(DEPTH, 1, -1))


def _row_mod(mod, is_ctx, i):
    return jnp.where(is_ctx, mod[6 + i:7 + i], mod[i:i + 1])


def _norm_modulate(x, gain, mod, is_ctx, i_shift, i_scale):
    ms = jnp.mean(x * x, axis=-1, keepdims=True)
    xn = x * lax.rsqrt(ms + EPS) * gain
    return xn * (1.0 + _row_mod(mod, is_ctx, i_scale)) + _row_mod(mod, is_ctx, i_shift)


def _stream_specs(stream, ctx, tm, b_of, j_of):
    n_sub = tm // SUB_ROWS
    last_piece = stream.shape[1] // SUB_ROWS - 1
    specs = [pl.BlockSpec((1, SUB_ROWS, D_MODEL),
                          lambda *g, k=k: (b_of(g), jnp.minimum(j_of(g) * n_sub + k, last_piece), 0))
             for k in range(n_sub)]
    args = [stream] * n_sub
    if ctx is not None:
        assert ctx.shape[1] == SUB_ROWS and (stream.shape[1] + SUB_ROWS) % tm == 0
        specs.append(pl.BlockSpec((1, SUB_ROWS, D_MODEL), lambda *g: (b_of(g), 0, 0)))
        args.append(ctx)
    return specs, args


def _stream_piece(x_refs, k, n_sub, is_last_tile):
    x = x_refs[k][0]
    if len(x_refs) > n_sub and k == n_sub - 1:
        x = jnp.where(is_last_tile, x_refs[n_sub][0], x)
    return x


def _in_tile(piece, j, mod_ref, g_ref, w_ref, cos_ref, sin_ref, qkg_ref, bd_ref, o_ref, n_lat, tm):
    lane = lax.broadcasted_iota(jnp.int32, (1, LANES), 1)
    first_half = (lane % HEAD_DIM) < (HEAD_DIM // 2)
    for k, r0 in enumerate(range(0, tm, SUB_ROWS)):
        rows = slice(r0, r0 + SUB_ROWS)
        row = j * tm + r0 + lax.broadcasted_iota(jnp.int32, (SUB_ROWS, 1), 0)
        is_ctx = row >= n_lat
        x = piece(k, is_ctx)
        h = _norm_modulate(x, g_ref[...], mod_ref[0], is_ctx, 0, 1).astype(BF16)
        qk = jnp.dot(h, w_ref[:, :QK_COLS], preferred_element_type=F32)
        cos = cos_ref[rows, :]
        sin = sin_ref[rows, :]
        for c in range(QK_COLS // LANES):
            xc = qk[:, c * LANES:(c + 1) * LANES]
            ss = jnp.dot((xc * xc).astype(BF16), bd_ref[...], preferred_element_type=F32) * (1.0 / HEAD_DIM)
            xc = xc * lax.rsqrt(ss + EPS) * qkg_ref[c:c + 1, :]
            rot = jnp.where(first_half, pltpu.roll(xc, LANES - HEAD_DIM // 2, 1), pltpu.roll(xc, HEAD_DIM // 2, 1))
            o_ref[0, rows, c * LANES:(c + 1) * LANES] = (xc * cos + rot * sin).astype(BF16)
        o_ref[0, rows, QK_COLS:] = jnp.dot(h, w_ref[:, QK_COLS:], preferred_element_type=F32).astype(BF16)


def _in_body(*refs, n_lat, tm, n_x):
    x_refs = refs[:n_x]
    j = pl.program_id(0)
    is_last_tile = j == pl.num_programs(0) - 1
    _in_tile(lambda k, is_ctx: _stream_piece(x_refs, k, tm // SUB_ROWS, is_last_tile), j, *refs[n_x:], n_lat, tm)


def _in_proj(stream, ctx, modv, gain, w, cos, sin, qkg, bd, n_lat):
    B = stream.shape[0]
    T = stream.shape[1] + (0 if ctx is None else ctx.shape[1])
    tm = TOKEN_TILE
    x_specs, x_args = _stream_specs(stream, ctx, tm, lambda g: g[1], lambda g: g[0])
    return pl.pallas_call(
        functools.partial(_in_body, n_lat=n_lat, tm=tm, n_x=len(x_args)),
        grid=(T // tm, B),
        in_specs=x_specs + [
                  pl.BlockSpec((1, 16, D_MODEL), lambda j, b: (b, 0, 0)),
                  pl.BlockSpec((1, D_MODEL), lambda j, b: (0, 0)),
                  pl.BlockSpec((D_MODEL, N_PROJ), lambda j, b: (0, 0)),
                  pl.BlockSpec((tm, LANES), lambda j, b: (j, 0)),
                  pl.BlockSpec((tm, LANES), lambda j, b: (j, 0)),
                  pl.BlockSpec((8, LANES), lambda j, b: (0, 0)),
                  pl.BlockSpec((LANES, LANES), lambda j, b: (0, 0))],
        out_specs=pl.BlockSpec((1, tm, N_PROJ), lambda j, b: (b, j, 0)),
        out_shape=jax.ShapeDtypeStruct((B, T, N_PROJ), BF16),
        compiler_params=_cparams(2),
        name="in_proj",
    )(*x_args, modv, gain, w, cos, sin, qkg, bd)


ATT_BLOCK = 128
ATT_SPAN = ATT_BLOCK + 2 * WINDOW


ATT_HEAD_ORDER = (0, 4, 1, 5, 2, 6, 3, 7)


def _attend(qblk, k_parts, v_parts, biases, sinks):
    rows = ATT_GROUP * ATT_BLOCK
    rowi = lax.broadcasted_iota(jnp.int32, (rows, 1), 0)
    lane = lax.broadcasted_iota(jnp.int32, (1, LANES), 1)
    lower = lane < HEAD_DIM
    nt = (((1,), (1,)), ((), ()))
    heads = range(ATT_KV_HEADS)
    keep = [lower, jnp.logical_not(lower)]
    sink, scores = [], []
    for h in heads:
        qs = jnp.concatenate([jnp.where(keep[h], qblk[:, g * LANES:(g + 1) * LANES], jnp.zeros((), BF16))
                              for g in range(ATT_GROUP)], axis=0)
        col = jnp.full((rows, 1), sinks[ATT_GROUP * h + ATT_GROUP - 1], F32)
        for g in range(ATT_GROUP - 2, -1, -1):
            col = jnp.where(rowi < (g + 1) * ATT_BLOCK, sinks[ATT_GROUP * h + g], col)
        sink.append(col)
        pieces = []
        for k in k_parts:
            s = lax.dot_general(qs, k, nt, preferred_element_type=F32)
            pieces += [s[:, c * LANES:(c + 1) * LANES] for c in range(k.shape[0] // LANES)]
        scores.append([s if b is None else s + b for s, b in zip(pieces, biases)])
    top, probs = [], []
    for h in heads:
        m = scores[h][0]
        for s in scores[h][1:]:
            m = jnp.maximum(m, s)
        m = jnp.maximum(jnp.max(m, axis=-1, keepdims=True), sink[h])
        top.append(m)
        probs.append([jnp.exp((s - m).astype(BF16)) for s in scores[h]])
    normed = []
    for h in heads:
        acc = jnp.where(keep[h], 0.0, jnp.exp(sink[h] - top[h]))
        c0 = 0
        for v in v_parts[h]:
            n = v.shape[0] // LANES
            acc = acc + jnp.dot(jnp.concatenate(probs[h][c0:c0 + n], axis=1), v, preferred_element_type=F32)
            c0 += n
        normed.append(acc / pltpu.roll(acc, HEAD_DIM, 1))
    return jnp.concatenate([jnp.where(lower, normed[0][g * ATT_BLOCK:(g + 1) * ATT_BLOCK],
                                      normed[1][g * ATT_BLOCK:(g + 1) * ATT_BLOCK]) for g in range(ATT_GROUP)], axis=1)


def _att_body(sink_ref, q_ref, k_ref, v_ref, o_ref, v1_s, *, n_lat, n_ctx, with_ctx_out):
    sinks = [sink_ref[i] for i in range(ATT_HEADS)]
    lane = lax.broadcasted_iota(jnp.int32, (1, LANES), 1)
    vv = v_ref[0]
    v1_s[0] = jnp.where(lane < HEAD_DIM, vv, jnp.ones((), BF16))
    v1_s[1] = jnp.where(lane < HEAD_DIM, jnp.ones((), BF16), vv)
    k_ctx = k_ref[0, n_lat:n_lat + n_ctx, :]
    v_ctx = [v1_s[h, n_lat:n_lat + n_ctx, :] for h in range(ATT_KV_HEADS)]
    no_bias = [None] * (n_ctx // LANES)
    qi = lax.broadcasted_iota(jnp.int32, (ATT_GROUP * ATT_BLOCK, LANES), 0) % ATT_BLOCK
    ki = lax.broadcasted_iota(jnp.int32, (ATT_GROUP * ATT_BLOCK, LANES), 1)
    past_ok = jnp.where(ki >= qi, 0.0, NEG)
    ahead_ok = jnp.where(ki <= qi, 0.0, NEG)

    def block(q0, k0, n_keys, biases):
        k_parts = [k_ref[0, pl.ds(k0, n_keys), :], k_ctx]
        v_parts = [[v1_s[h, pl.ds(k0, n_keys), :], v_ctx[h]] for h in range(ATT_KV_HEADS)]
        out = _attend(q_ref[0, pl.ds(q0, ATT_BLOCK), :], k_parts, v_parts, biases + no_bias, sinks)
        o_ref[0, pl.ds(q0, ATT_BLOCK), :] = out.astype(BF16)

    def interior(i, carry):
        q0 = pl.multiple_of(i * ATT_BLOCK, ATT_BLOCK)
        block(q0, pl.multiple_of(q0 - WINDOW, ATT_BLOCK), ATT_SPAN, [past_ok, None, ahead_ok])
        return carry

    nq = n_lat // ATT_BLOCK
    block(0, 0, 2 * ATT_BLOCK, [None, ahead_ok])
    lax.fori_loop(1, nq - 1, interior, 0, unroll=7)
    block(n_lat - ATT_BLOCK, n_lat - 2 * ATT_BLOCK, 2 * ATT_BLOCK, [past_ok, None])
    if with_ctx_out:
        for c in range(n_ctx // ATT_BLOCK):
            r0 = n_lat + c * ATT_BLOCK
            out = _attend(q_ref[0, r0:r0 + ATT_BLOCK, :], [k_ctx], [[v_ctx[h]] for h in range(ATT_KV_HEADS)],
                          no_bias, sinks)
            o_ref[0, r0:r0 + ATT_BLOCK, :] = out.astype(BF16)
    else:
        o_ref[0, n_lat:, :] = jnp.zeros((n_ctx, ATT_WIDTH), BF16)


def _attention(p, sinks, n_lat, with_ctx_out):
    B, T, _ = p.shape
    return pl.pallas_call(
        functools.partial(_att_body, n_lat=n_lat, n_ctx=T - n_lat, with_ctx_out=with_ctx_out),
        grid=(B,),
        in_specs=[pl.BlockSpec(memory_space=pltpu.SMEM),
                  pl.BlockSpec((1, T, ATT_WIDTH), lambda b: (b, 0, COL_AQ // ATT_WIDTH)),
                  pl.BlockSpec((1, T, KV_WIDTH), lambda b: (b, 0, COL_AK // KV_WIDTH)),
                  pl.BlockSpec((1, T, KV_WIDTH), lambda b: (b, 0, COL_AV // KV_WIDTH))],
        out_specs=pl.BlockSpec((1, T, ATT_WIDTH), lambda b: (b, 0, 0)),
        out_shape=jax.ShapeDtypeStruct((B, T, ATT_WIDTH), BF16),
        scratch_shapes=[pltpu.VMEM((ATT_KV_HEADS, T, KV_WIDTH), BF16)],
        compiler_params=_cparams(1),
        name="attention",
    )(sinks, p, p, p)


GLA_BLOCK_A = 256
GLA_BLOCK_B = 128
CONV_PAD = 8
EPI_ROWS = 256


def _log_sigmoid(z):
    return jnp.minimum(z, 0.0) - jnp.log1p(jnp.exp(-jnp.abs(z)))


def _gla_body(gq_ref, gk_ref, gv_ref, gr_ref, cb_ref, cc_ref, ch_ref, gt_ref, wg_ref, gbias_ref, ng_ref,
              cw_ref, bd_ref, cum_ref, o_ref, qe_s, ke_s, dec_s, upd_s, prev_s, st_s, o_s, u_s, *, n_lat, n_ctx):
    T = n_lat + n_ctx
    C = GLA_CHUNK
    nt = (((1,), (1,)), ((), ()))
    tn = (((0,), (0,)), ((), ()))

    sr = lax.broadcasted_iota(jnp.int32, (GLA_WIDTH, 2 * GLA_QK_WIDTH), 0) // GLA_DV
    sl = (lax.broadcasted_iota(jnp.int32, (GLA_WIDTH, 2 * GLA_QK_WIDTH), 1) % GLA_QK_WIDTH) // GLA_DK
    state_mask2 = sr == sl

    def factors(i, carry):
        r0 = pl.multiple_of(i * GLA_BLOCK_A, GLA_BLOCK_A)
        rows = pl.ds(r0, GLA_BLOCK_A)
        gt = gt_ref[0, rows, :]
        q = gq_ref[0, rows, :].astype(F32) * (GLA_DK ** -0.5)
        k = gk_ref[0, rows, :].astype(F32)
        v = gv_ref[0, rows, :]
        z2 = jnp.dot(gt, wg_ref[...], preferred_element_type=F32) + gbias_ref[0:1, :]
        g2 = _log_sigmoid(z2) * (1.0 / GLA_GATE_NORM)
        cum = []
        for d in range(2):
            g = g2[:, d * GLA_QK_WIDTH:(d + 1) * GLA_QK_WIDTH]
            g_hi = g.astype(BF16)
            g_lo = (g - g_hi.astype(F32)).astype(BF16)
            cum.append(jnp.dot(cum_ref[d], jnp.concatenate([g_hi, g_lo], axis=1), preferred_element_type=F32))
        kl = []
        for d in range(2):
            b = cum[d][:GLA_BLOCK_A, :GLA_QK_WIDTH] + cum[d][:GLA_BLOCK_A, GLA_QK_WIDTH:]
            tot = cum[d][GLA_BLOCK_A:, :GLA_QK_WIDTH] + cum[d][GLA_BLOCK_A:, GLA_QK_WIDTH:]
            dec = jnp.exp(tot)
            ke = k * jnp.exp(-b)
            qe_s[d, rows, :] = (q * jnp.exp(b)).astype(BF16)
            ke_s[d, rows, :] = ke.astype(BF16)
            dec_s[d, rows, :] = dec
            kl.append((ke * dec).astype(BF16))
        kl2 = jnp.concatenate(kl, axis=1)
        for cc in range(GLA_BLOCK_A // C):
            upd = lax.dot_general(v[cc * C:(cc + 1) * C], kl2[cc * C:(cc + 1) * C], tn, preferred_element_type=F32)
            upd_s[i * (GLA_BLOCK_A // C) + cc] = jnp.where(state_mask2, upd, 0.0)
        return carry

    lax.fori_loop(0, T // GLA_BLOCK_A, factors, 0, unroll=True)

    nc_lat = n_lat // C
    nc_ctx = n_ctx // C
    st_s[...] = jnp.zeros_like(st_s)

    def scan(i, carry):
        in_ctx = i < nc_ctx
        cf = jnp.where(in_ctx, nc_lat + i, i - nc_ctx)
        cb = jnp.where(in_ctx, nc_lat + nc_ctx - 1 - i, nc_lat - 1 - (i - nc_ctx))
        for d, cid in enumerate((cf, cb)):
            st = st_s[d]
            prev_s[cid, :, d * GLA_QK_WIDTH:(d + 1) * GLA_QK_WIDTH] = st.astype(BF16)
            st_s[d] = (st * dec_s[d, pl.ds(pl.multiple_of(cid * C, C), 1), :]
                       + upd_s[cid, :, d * GLA_QK_WIDTH:(d + 1) * GLA_QK_WIDTH])
        return carry

    lax.fori_loop(0, nc_lat + nc_ctx, scan, 0)

    RB = GLA_BLOCK_B
    k_rows = lax.broadcasted_iota(jnp.int32, (GLA_HEADS * RB, GLA_QK_WIDTH), 0) // RB
    k_lanes = lax.broadcasted_iota(jnp.int32, (GLA_HEADS * RB, GLA_QK_WIDTH), 1) // GLA_DK
    key_heads = k_rows == k_lanes
    v_rows = lax.broadcasted_iota(jnp.int32, (GLA_HEADS * RB, GLA_WIDTH), 0) // RB
    v_lanes = lax.broadcasted_iota(jnp.int32, (GLA_HEADS * RB, GLA_WIDTH), 1) // GLA_DV
    value_heads = v_rows == v_lanes
    qr = lax.broadcasted_iota(jnp.int32, (RB, GLA_HEADS * RB), 0)
    kc = lax.broadcasted_iota(jnp.int32, (RB, GLA_HEADS * RB), 1) % RB
    same_chunk = (qr // C) == (kc // C)
    forward = kc <= qr
    zero = jnp.zeros((), BF16)

    def outputs(i, carry):
        r0 = pl.multiple_of(i * RB, RB)
        rows = pl.ds(r0, RB)
        v = gv_ref[0, rows, :]
        v4 = jnp.where(value_heads, jnp.concatenate([v] * GLA_HEADS, axis=0), zero)
        qe = [qe_s[d, rows, :] for d in range(2)]
        att = []
        for d in range(2):
            ke4 = jnp.where(key_heads, jnp.concatenate([ke_s[d, rows, :]] * GLA_HEADS, axis=0), zero)
            att.append(lax.dot_general(qe[d], ke4, nt, preferred_element_type=F32))
        both = jnp.where(same_chunk, jnp.where(forward, att[0], att[1]), 0.0).astype(BF16)
        qe2 = jnp.concatenate(qe, axis=1)
        inter = [lax.dot_general(qe2[cc * C:(cc + 1) * C], prev_s[i * (RB // C) + cc], nt,
                                 preferred_element_type=F32) for cc in range(RB // C)]
        o_s[rows, :] = jnp.dot(both, v4, preferred_element_type=F32) + jnp.concatenate(inter, axis=0)
        return carry

    lax.fori_loop(0, T // RB, outputs, 0, unroll=6)

    u_s[0:CONV_PAD, :] = jnp.zeros((CONV_PAD, CONV_WIDTH), F32)
    u_s[CONV_PAD + T:, :] = jnp.zeros((CONV_PAD, CONV_WIDTH), F32)
    u_s[CONV_PAD:CONV_PAD + T, :] = cc_ref[0].astype(F32) * ch_ref[0].astype(F32)
    w0 = cw_ref[0:1, :]
    w1 = cw_ref[1:2, :]
    w2 = cw_ref[2:3, :]
    for e in range(T // EPI_ROWS):
        r0 = e * EPI_ROWS
        o = o_s[r0:r0 + EPI_ROWS, :]
        ss = jnp.dot((o * o).astype(BF16), bd_ref[...], preferred_element_type=F32) * (1.0 / GLA_DV)
        on = o * lax.rsqrt(ss + EPS) * ng_ref[...]
        r = gr_ref[0, r0:r0 + EPI_ROWS, :].astype(F32)
        o_ref[0, r0:r0 + EPI_ROWS, 0:GLA_WIDTH] = (on * _silu(r)).astype(BF16)
        t = r0 + lax.broadcasted_iota(jnp.int32, (EPI_ROWS, 1), 0)
        up = u_s[CONV_PAD + r0 - 1:CONV_PAD + r0 - 1 + EPI_ROWS, :]
        mid = u_s[CONV_PAD + r0:CONV_PAD + r0 + EPI_ROWS, :]
        dn = u_s[CONV_PAD + r0 + 1:CONV_PAD + r0 + 1 + EPI_ROWS, :]
        up = jnp.where(t == n_lat, 0.0, up)
        dn = jnp.where(t == n_lat - 1, 0.0, dn)
        conv = w0 * up + w1 * mid + w2 * dn
        o_ref[0, r0:r0 + EPI_ROWS, GLA_WIDTH:] = (cb_ref[0, r0:r0 + EPI_ROWS, :].astype(F32) * conv).astype(BF16)


def _gla_cum_matrices():
    i = np.arange(GLA_BLOCK_A)
    same = (i[:, None] // GLA_CHUNK) == (i[None, :] // GLA_CHUNK)
    fwd = same & (i[None, :] <= i[:, None])
    bwd = same & (i[None, :] >= i[:, None])
    mats = np.stack([np.concatenate([fwd, same], axis=0), np.concatenate([bwd, same], axis=0)])
    return jnp.asarray(mats.astype(np.float32), dtype=BF16)


def _gla_conv(p, wg, gbias, ng, cw, bd, cum, n_lat):
    B, T, _ = p.shape
    nc = T // GLA_CHUNK

    def col(width, start):
        return pl.BlockSpec((1, T, width), lambda b: (b, 0, start // width))

    def const(shape):
        return pl.BlockSpec(shape, lambda b: (0,) * len(shape))

    return pl.pallas_call(
        functools.partial(_gla_body, n_lat=n_lat, n_ctx=T - n_lat),
        grid=(B,),
        in_specs=[col(GLA_QK_WIDTH, COL_GQ), col(GLA_QK_WIDTH, COL_GK), col(GLA_WIDTH, COL_GV), col(GLA_WIDTH, COL_GR),
                  col(CONV_WIDTH, COL_CB), col(CONV_WIDTH, COL_CC), col(CONV_WIDTH, COL_CH), col(LANES, COL_GT),
                  const((LANES, 2 * GLA_QK_WIDTH)), const((8, 2 * GLA_QK_WIDTH)),
                  const((1, GLA_WIDTH)), const((8, CONV_WIDTH)), const((GLA_WIDTH, GLA_WIDTH)),
                  const((2, 2 * GLA_BLOCK_A, GLA_BLOCK_A))],
        out_specs=pl.BlockSpec((1, T, GLA_WIDTH + CONV_WIDTH), lambda b: (b, 0, 0)),
        out_shape=jax.ShapeDtypeStruct((B, T, GLA_WIDTH + CONV_WIDTH), BF16),
        scratch_shapes=[pltpu.VMEM((2, T, GLA_QK_WIDTH), BF16), pltpu.VMEM((2, T, GLA_QK_WIDTH), BF16),
                        pltpu.VMEM((2, T, GLA_QK_WIDTH), F32),
                        pltpu.VMEM((nc, GLA_WIDTH, 2 * GLA_QK_WIDTH), F32),
                        pltpu.VMEM((nc, GLA_WIDTH, 2 * GLA_QK_WIDTH), BF16),
                        pltpu.VMEM((2, GLA_WIDTH, GLA_QK_WIDTH), F32),
                        pltpu.VMEM((T, GLA_WIDTH), F32),
                        pltpu.VMEM((T + 2 * CONV_PAD, CONV_WIDTH), F32)],
        compiler_params=_cparams(1),
        name="gla_conv",
    )(p, p, p, p, p, p, p, p, wg, gbias, ng, cw, bd, cum)


def _route(logits_t):
    mx = jnp.max(logits_t, axis=0, keepdims=True)
    ex = jnp.exp(logits_t - mx)
    probs = ex / jnp.sum(ex, axis=0, keepdims=True)
    P = [probs[e:e + 1] for e in range(N_EXPERTS)]
    scores = []
    for g in range(N_GROUPS):
        a, b, c, d = P[4 * g:4 * g + 4]
        scores.append(jnp.maximum(jnp.maximum(jnp.maximum(a + b, a + c), jnp.maximum(a + d, b + c)),
                                  jnp.maximum(b + d, c + d)))
    best = jnp.maximum(jnp.maximum(scores[0], scores[1]), jnp.maximum(scores[2], scores[3]))
    taken = jnp.zeros_like(best, dtype=jnp.bool_)
    sel = []
    for g in range(N_GROUPS):
        s = (scores[g] == best) & jnp.logical_not(taken)
        sel.append(s)
        taken = taken | s
    gsel = jnp.where(sel[1], 1.0, 0.0) + jnp.where(sel[2], 2.0, 0.0) + jnp.where(sel[3], 3.0, 0.0)
    ig = [jnp.where(sel[0], P[j], jnp.where(sel[1], P[4 + j], jnp.where(sel[2], P[8 + j], P[12 + j])))
          for j in range(EXPERTS_PER_GROUP)]

    def first_max(vals):
        v = jnp.maximum(jnp.maximum(vals[0], vals[1]), jnp.maximum(vals[2], vals[3]))
        tk = jnp.zeros_like(v, dtype=jnp.bool_)
        hot = []
        for x in vals:
            s = (x == v) & jnp.logical_not(tk)
            hot.append(s)
            tk = tk | s
        idx = jnp.where(hot[1], 1.0, 0.0) + jnp.where(hot[2], 2.0, 0.0) + jnp.where(hot[3], 3.0, 0.0)
        return v, hot, idx

    _, hot1, i1 = first_max(ig)
    _, _, i2 = first_max([jnp.where(hot1[j], -1.0, ig[j]) for j in range(EXPERTS_PER_GROUP)])
    lo = jnp.minimum(i1, i2)
    hi = jnp.maximum(i1, i2)
    pair = jnp.where(lo == 0.0, hi - 1.0, jnp.where(lo == 1.0, hi + 1.0, 5.0))
    return gsel * N_PAIRS + pair


def _class_rank(cls, tri_ref, cnt_s):
    n = cls.shape[1]
    cid = lax.broadcasted_iota(jnp.int32, (CLS_ROWS, n), 0).astype(F32)
    onehot = jnp.where(cls == cid, 1.0, 0.0)
    segs = [onehot[:, k * LANES:(k + 1) * LANES] for k in range(n // LANES)]
    before = jnp.dot(jnp.concatenate(segs, axis=0).astype(BF16), tri_ref[...], preferred_element_type=F32)
    base = cnt_s[...]
    ranks = []
    for k, seg in enumerate(segs):
        ranks.append(jnp.sum(seg * (before[k * CLS_ROWS:(k + 1) * CLS_ROWS] + base), axis=0, keepdims=True))
        base = base + jnp.sum(seg, axis=1, keepdims=True)
    cnt_s[...] = base
    return jnp.concatenate(ranks, axis=1)


def _out_body(*refs, n_lat, tm, n_x):
    x_refs = refs[:n_x]
    (ya_ref, yg_ref, mod_ref, wo_ref, g_ref, wr_ref, br_ref, tri_ref, xo_ref, h2t_ref, rt_ref, cnt_ref, cnt_s,
     logit_s) = refs[n_x:]
    j = pl.program_id(1)
    is_last_tile = j == pl.num_programs(1) - 1

    @pl.when((pl.program_id(0) == 0) & (j == 0))
    def _():
        cnt_s[...] = jnp.zeros_like(cnt_s)

    mod = mod_ref[0]
    subs = list(enumerate(range(0, tm, SUB_ROWS)))
    ys = [jnp.dot(ya_ref[0, r0:r0 + SUB_ROWS, :], wo_ref[0:ATT_WIDTH, :], preferred_element_type=F32)
          + jnp.dot(yg_ref[0, r0:r0 + SUB_ROWS, :], wo_ref[ATT_WIDTH:, :], preferred_element_type=F32) for _, r0 in subs]
    h2s = []
    for k, r0 in subs:
        row = j * tm + r0 + lax.broadcasted_iota(jnp.int32, (SUB_ROWS, 1), 0)
        is_ctx = row >= n_lat
        xn = _stream_piece(x_refs, k, tm // SUB_ROWS, is_last_tile) + _row_mod(mod, is_ctx, 2) * ys[k]
        xo_ref[0, r0:r0 + SUB_ROWS, :] = xn
        h2 = _norm_modulate(xn, g_ref[...], mod, is_ctx, 3, 4)
        h2t_ref[r0:r0 + SUB_ROWS] = h2.reshape(SUB_ROWS, 8, LANES)
        h2s.append(h2.astype(BF16))
    for k, r0 in subs:
        logit_s[r0:r0 + SUB_ROWS, :] = jnp.dot(h2s[k], wr_ref[...], preferred_element_type=F32) + br_ref[0:1, :]
    cls = _route(logit_s[...].T[0:N_EXPERTS, :])
    rank = _class_rank(cls, tri_ref, cnt_s)
    rt_ref[0] = jnp.concatenate([cls, rank, jnp.zeros((6, tm), F32)], axis=0).astype(jnp.int32)
    cnt_ref[...] = jnp.broadcast_to(cnt_s[...], (CLS_ROWS, LANES)).astype(jnp.int32)


def _out_proj(ya, yg, stream, ctx, modv, wo, gain, wr, br, tri, n_lat, rows, tm):
    B = stream.shape[0]
    nj = rows // tm
    x_specs, x_args = _stream_specs(stream, ctx, tm, lambda g: g[0], lambda g: g[1])
    return pl.pallas_call(
        functools.partial(_out_body, n_lat=n_lat, tm=tm, n_x=len(x_args)),
        grid=(B, nj),
        in_specs=x_specs + [
                  pl.BlockSpec((1, tm, ATT_WIDTH), lambda b, j: (b, j, 0)),
                  pl.BlockSpec((1, tm, GLA_WIDTH + CONV_WIDTH), lambda b, j: (b, j, 0)),
                  pl.BlockSpec((1, 16, D_MODEL), lambda b, j: (b, 0, 0)),
                  pl.BlockSpec((D_MODEL, D_MODEL), lambda b, j: (0, 0)),
                  pl.BlockSpec((1, D_MODEL), lambda b, j: (0, 0)),
                  pl.BlockSpec((D_MODEL, LANES), lambda b, j: (0, 0)),
                  pl.BlockSpec((8, LANES), lambda b, j: (0, 0)),
                  pl.BlockSpec((LANES, LANES), lambda b, j: (0, 0))],
        out_specs=[pl.BlockSpec((1, tm, D_MODEL), lambda b, j: (b, j, 0)),
                   pl.BlockSpec((tm, 8, LANES), lambda b, j: (b * nj + j, 0, 0)),
                   pl.BlockSpec((1, 8, tm), lambda b, j: (b * nj + j, 0, 0)),
                   pl.BlockSpec((CLS_ROWS, LANES), lambda b, j: (0, 0))],
        out_shape=[jax.ShapeDtypeStruct((B, rows, D_MODEL), F32),
                   jax.ShapeDtypeStruct((B * rows, 8, LANES), F32),
                   jax.ShapeDtypeStruct((B * nj, 8, tm), jnp.int32),
                   jax.ShapeDtypeStruct((CLS_ROWS, LANES), jnp.int32)],
        scratch_shapes=[pltpu.VMEM((CLS_ROWS, 1), F32), pltpu.VMEM((tm, LANES), F32)],
        compiler_params=_cparams(2),
        name="out_proj_router",
    )(*x_args, ya, yg, modv, wo, gain, wr, br, tri)


ROW_UNROLL = 8
IDX_STRIDE = 1024


def _idx_slot(idx, slot, tm):
    return idx.at[pl.ds(pl.multiple_of(slot * IDX_STRIDE, IDX_STRIDE), tm)]


def _issue_rows(tm, idx, slot, make_copy):
    base = slot * IDX_STRIDE

    def trip(i, c):
        for u in range(ROW_UNROLL):
            r = i * ROW_UNROLL + u
            make_copy(r, idx[base + r]).start(priority=u % 2)
        return c

    lax.fori_loop(0, tm // ROW_UNROLL, trip, 0)


PAD_SLOTS = 32
PAD_BITS = 9


def _zero_pad_rows(pad_ref, zeros_ref, hs_ref, sem):
    def pieces(c, fn):
        start = pad_ref[c]
        n = pad_ref[PAD_SLOTS + c]
        for bit in range(PAD_BITS):
            size = 1 << bit
            below = n & (size - 1)

            @pl.when((n & size) != 0)
            def _():
                fn(pltpu.make_async_copy(zeros_ref.at[pl.ds(0, size)], hs_ref.at[pl.ds(start + below, size)], sem))

        def block(i, carry):
            fn(pltpu.make_async_copy(zeros_ref, hs_ref.at[pl.ds(start + n + i * MOE_TILE, MOE_TILE)], sem))
            return carry

        lax.fori_loop(0, pad_ref[2 * PAD_SLOTS + c], block, 0)

    def start_all(c, carry):
        pieces(c, lambda cp: cp.start())
        return carry

    def wait_all(c, carry):
        pieces(c, lambda cp: cp.wait())
        return carry

    lax.fori_loop(0, N_CLASSES + 1, start_all, 0)
    lax.fori_loop(0, N_CLASSES + 1, wait_all, 0)


def _disp_body(pad_ref, dest_ref, h2t_ref, hs_ref, zeros_s, idx, sem_i, sem_d, sem_z, *, tm, n_steps):
    s = pl.program_id(0)
    last = n_steps - 1
    slot = s % 2

    def idx_fetch(step, sl):
        return pltpu.make_async_copy(dest_ref.at[step], _idx_slot(idx, sl, tm), sem_i.at[sl])

    def drain(sl):
        pltpu.make_async_copy(h2t_ref.at[pl.ds(0, tm)], hs_ref.at[pl.ds(0, tm)], sem_d.at[sl]).wait()

    @pl.when(s == 0)
    def _():
        idx_fetch(s, slot).start()
        zeros_s[...] = jnp.zeros_like(zeros_s)
        _zero_pad_rows(pad_ref, zeros_s, hs_ref, sem_z)

    idx_fetch(s, slot).wait()
    first = s * tm
    _issue_rows(tm, idx, slot, lambda r, d: pltpu.make_async_copy(h2t_ref.at[first + r], hs_ref.at[d],
                                                                  sem_d.at[slot]))

    @pl.when(s > 0)
    def _():
        drain(1 - slot)

    @pl.when(s < last)
    def _():
        idx_fetch(s + 1, 1 - slot).start()

    @pl.when(s == last)
    def _():
        drain(slot)


def _dispatch(pad, dest, h2t, n_rows):
    n_steps, tm = dest.shape
    return pl.pallas_call(
        functools.partial(_disp_body, tm=tm, n_steps=n_steps),
        grid=(n_steps,),
        in_specs=[pl.BlockSpec(memory_space=pltpu.SMEM),
                  pl.BlockSpec(memory_space=pl.ANY),
                  pl.BlockSpec(memory_space=pl.ANY)],
        out_specs=pl.BlockSpec(memory_space=pl.ANY),
        scratch_shapes=[pltpu.VMEM((MOE_TILE, 8, LANES), F32), pltpu.SMEM((2 * IDX_STRIDE,), jnp.int32),
                        pltpu.SemaphoreType.DMA((2,)), pltpu.SemaphoreType.DMA((2,)), pltpu.SemaphoreType.DMA],
        out_shape=jax.ShapeDtypeStruct((n_rows, 8, LANES), F32),
        compiler_params=_cparams(1),
        name="dispatch",
    )(pad, dest, h2t)


def _moe_body(tile_ref, e_lo_ref, e_hi_ref, valid_ref, hs_ref, wg1, wu1, wd1, wg2, wu2, wd2, wr_ref, br_ref, ys_ref):
    g = pl.program_id(0)
    tm = hs_ref.shape[0]

    @pl.when(valid_ref[g] == 1)
    def _():
        x = hs_ref[...].reshape(tm, D_MODEL)
        e_lo = e_lo_ref[g]
        e_hi = e_hi_ref[g]
        dw = wr_ref[pl.ds(e_lo, 1), :] - wr_ref[pl.ds(e_hi, 1), :]
        d = jnp.sum(x * dw, axis=-1, keepdims=True) + (br_ref[e_lo] - br_ref[e_hi])
        w_lo = jax.nn.sigmoid(d)
        w_hi = jax.nn.sigmoid(-d)
        h = x.astype(BF16)

        def act(wg, wu, w):
            a = _silu(jnp.dot(h, wg[0], preferred_element_type=F32)) * jnp.dot(h, wu[0], preferred_element_type=F32)
            return (a * w).astype(BF16)

        y = (jnp.dot(act(wg1, wu1, w_lo), wd1[0], preferred_element_type=F32)
             + jnp.dot(act(wg2, wu2, w_hi), wd2[0], preferred_element_type=F32))
        ys_ref[...] = y.reshape(tm, 8, LANES)

    @pl.when(valid_ref[g] == 0)
    def _():
        ys_ref[...] = jnp.zeros_like(ys_ref)


def _moe(hs, tile, e_lo, e_hi, valid, wg, wu, wd, wr_t, br, first_expert):
    n_tiles = tile.shape[0]
    tm = MOE_TILE

    def w_in(sel):
        return pl.BlockSpec((1, D_MODEL, D_EXPERT), lambda g, t, lo, hi, v: (first_expert + (lo, hi)[sel][g], 0, 0))

    def w_out(sel):
        return pl.BlockSpec((1, D_EXPERT, D_MODEL), lambda g, t, lo, hi, v: (first_expert + (lo, hi)[sel][g], 0, 0))

    return pl.pallas_call(
        _moe_body,
        grid_spec=pltpu.PrefetchScalarGridSpec(
            num_scalar_prefetch=4,
            grid=(n_tiles,),
            in_specs=[pl.BlockSpec((tm, 8, LANES), lambda g, t, lo, hi, v: (t[g], 0, 0)),
                      w_in(0), w_in(0), w_out(0), w_in(1), w_in(1), w_out(1),
                      pl.BlockSpec((N_EXPERTS, D_MODEL), lambda g, t, lo, hi, v: (0, 0)),
                      pl.BlockSpec(memory_space=pltpu.SMEM)],
            out_specs=pl.BlockSpec((tm, 8, LANES), lambda g, t, lo, hi, v: (g, 0, 0))),
        out_shape=jax.ShapeDtypeStruct((n_tiles * tm, 8, LANES), F32),
        compiler_params=_cparams(1),
        name="moe_pairs",
    )(tile, e_lo, e_hi, valid, hs, wg, wu, wd, wg, wu, wd, wr_t, br)


def _gathered_slot(dest_ref, ys_ref, buf, idx, sem_i, sem_d, s, n_steps, tm):
    last = n_steps - 1
    slot = s % 2

    def idx_fetch(step, sl):
        return pltpu.make_async_copy(dest_ref.at[step], _idx_slot(idx, sl, tm), sem_i.at[sl])

    def gather(sl):
        _issue_rows(tm, idx, sl, lambda r, d: pltpu.make_async_copy(ys_ref.at[d], buf.at[sl, r], sem_d.at[sl]))

    @pl.when(s == 0)
    def _():
        idx_fetch(0, 0).start()
        idx_fetch(0, 0).wait()
        gather(0)
        if last > 0:
            idx_fetch(1, 1).start()

    @pl.when(s < last)
    def _():
        idx_fetch(s + 1, 1 - slot).wait()
        gather(1 - slot)

    @pl.when(s + 2 <= last)
    def _():
        idx_fetch(s + 2, slot).start()

    pltpu.make_async_copy(ys_ref.at[pl.ds(0, tm)], buf.at[slot], sem_d.at[slot]).wait()
    return slot


def _fin_body(x_ref, mod_ref, dest_ref, ys_ref, xo_ref, buf, idx, sem_i, sem_d, *, n_lat, tm, nj, n_steps):
    j = pl.program_id(1)
    slot = _gathered_slot(dest_ref, ys_ref, buf, idx, sem_i, sem_d, pl.program_id(0) * nj + j, n_steps, tm)
    row = j * tm + lax.broadcasted_iota(jnp.int32, (tm, 1), 0)
    is_ctx = row >= n_lat
    xo_ref[0] = x_ref[0] + _row_mod(mod_ref[0], is_ctx, 5) * buf[slot].reshape(tm, D_MODEL)


def _combine(xx, modv, dest, ys, n_lat, tm):
    B, rows, _ = xx.shape
    nj = rows // tm
    return pl.pallas_call(
        functools.partial(_fin_body, n_lat=n_lat, tm=tm, nj=nj, n_steps=B * nj),
        grid=(B, nj),
        in_specs=[pl.BlockSpec((1, tm, D_MODEL), lambda b, j: (b, j, 0)),
                  pl.BlockSpec((1, 16, D_MODEL), lambda b, j: (b, 0, 0)),
                  pl.BlockSpec(memory_space=pl.ANY),
                  pl.BlockSpec(memory_space=pl.ANY)],
        out_specs=pl.BlockSpec((1, tm, D_MODEL), lambda b, j: (b, j, 0)),
        scratch_shapes=[pltpu.VMEM((2, tm, 8, LANES), F32), pltpu.SMEM((2 * IDX_STRIDE,), jnp.int32),
                        pltpu.SemaphoreType.DMA((2,)), pltpu.SemaphoreType.DMA((2,))],
        out_shape=jax.ShapeDtypeStruct((B, rows, D_MODEL), F32),
        compiler_params=_cparams(2),
        name="combine",
    )(xx, modv, dest, ys)


def _rope_tables(n_lat, n_ctx):
    rows = n_lat // GRID_W
    row, col = jnp.meshgrid(jnp.arange(rows), jnp.arange(GRID_W), indexing="ij")
    n_freq = HEAD_DIM // 4
    inv_freq = ROPE_BASE ** (-jnp.arange(n_freq, dtype=F32) / n_freq)
    ang = jnp.concatenate([row.reshape(-1, 1).astype(F32) * inv_freq, col.reshape(-1, 1).astype(F32) * inv_freq],
                          axis=-1)
    cos = jnp.tile(jnp.cos(ang), (1, LANES // (HEAD_DIM // 2)))
    sin = jnp.tile(jnp.sin(ang), (1, LANES // (HEAD_DIM // 2)))
    sign = jnp.where((jnp.arange(LANES) % HEAD_DIM) < HEAD_DIM // 2, -1.0, 1.0).astype(F32)
    cos = jnp.concatenate([cos, jnp.ones((n_ctx, LANES), F32)], axis=0)
    sin = jnp.concatenate([sin * sign, jnp.zeros((n_ctx, LANES), F32)], axis=0)
    return cos, sin


def _block_diag_ones(n, blk):
    i = np.arange(n) // blk
    return jnp.asarray((i[:, None] == i[None, :]).astype(np.float32), dtype=BF16)


def _routing_tables(counts, n_tiles):
    tiles_c = (counts + MOE_TILE - 1) // MOE_TILE
    tile_end = jnp.cumsum(tiles_c)
    tile_start = tile_end - tiles_c
    off = tile_start * MOE_TILE
    total = tile_end[-1]
    g = jnp.arange(n_tiles, dtype=jnp.int32)
    valid = (g < total).astype(jnp.int32)
    g_eff = jnp.minimum(g, total - 1)
    c_of = jnp.sum((g_eff[:, None] >= tile_end[None, :]).astype(jnp.int32), axis=1)
    group = c_of // N_PAIRS
    pair = c_of % N_PAIRS
    lo = jnp.asarray(PAIR_LO, jnp.int32)
    hi = jnp.asarray(PAIR_HI, jnp.int32)
    e_lo = group * EXPERTS_PER_GROUP + jnp.sum((pair[:, None] == jnp.arange(N_PAIRS)[None, :]) * lo[None, :], axis=1)
    e_hi = group * EXPERTS_PER_GROUP + jnp.sum((pair[:, None] == jnp.arange(N_PAIRS)[None, :]) * hi[None, :], axis=1)
    fill = jnp.zeros((PAD_SLOTS - N_CLASSES - 1,), jnp.int32)
    pad = jnp.concatenate([off + counts, (total * MOE_TILE)[None], fill,
                           tiles_c * MOE_TILE - counts, jnp.zeros((1,), jnp.int32), fill,
                           jnp.zeros((N_CLASSES,), jnp.int32), (n_tiles - total)[None], fill])
    return off.astype(jnp.int32), pad.astype(jnp.int32), g_eff, e_lo.astype(jnp.int32), e_hi.astype(jnp.int32), valid


def kernel(x, c, ctx, c_ctx, w_ada, b_ada, norm_mix_g, norm_ffn_g, w_in, q_norm_g, k_norm_g, attn_sink, gla_gate_w,
           gla_gate_b, gla_norm_g, conv_w, w_out, w_router, b_router, w_gate_e, w_up_e, w_down_e):
    B, S, D = x.shape
    L = ctx.shape[1]
    T = S + L
    assert D == D_MODEL and T % TOKEN_TILE == 0 and S % LAT_TILE == 0 and S % GRID_W == 0
    assert S % ATT_BLOCK == 0 and L % ATT_BLOCK == 0 and S >= ATT_SPAN and T % EPI_ROWS == 0

    cond_rows = -(-(B + 1) // 8) * 8
    cond = jnp.zeros((cond_rows, D), F32).at[:B].set(c).at[B].set(c_ctx)
    mod_all = _modulation(cond, w_ada, b_ada)

    cos, sin = _rope_tables(S, L)
    bd_head = _block_diag_ones(LANES, HEAD_DIM)
    bd_gla = _block_diag_ones(GLA_WIDTH, GLA_DV)
    cum = _gla_cum_matrices()
    tri = jnp.asarray(np.triu(np.ones((LANES, LANES), np.float32), 1), dtype=BF16)
    stream, stream_ctx = x, ctx

    order = jnp.asarray(ATT_HEAD_ORDER)
    wg_all = w_gate_e.reshape(DEPTH * N_EXPERTS, D, D_EXPERT).astype(BF16)
    wu_all = w_up_e.reshape(DEPTH * N_EXPERTS, D, D_EXPERT).astype(BF16)
    wd_all = w_down_e.reshape(DEPTH * N_EXPERTS, D_EXPERT, D).astype(BF16)

    def mod_table(l):
        m_lat = mod_all[l, :B].reshape(B, 6, D)
        m_ctx = jnp.broadcast_to(mod_all[l, B].reshape(1, 6, D), (B, 6, D))
        return jnp.concatenate([m_lat, m_ctx, jnp.zeros((B, 4, D), F32)], axis=1)

    def in_proj_args(l):
        wl = w_in[l]
        wq = wl[:, :ATT_WIDTH].reshape(D, ATT_HEADS, HEAD_DIM)[:, order, :].reshape(D, ATT_WIDTH)
        w_perm = jnp.concatenate([wq, wl[:, ATT_WIDTH:1536], wl[:, 1568:], wl[:, 1536:1568],
                                  jnp.zeros((D, N_PROJ - wl.shape[1]), F32)], axis=1).astype(BF16)
        qg = jnp.tile(q_norm_g[l], LANES // HEAD_DIM) * (HEAD_DIM ** -0.5)
        kg = jnp.tile(k_norm_g[l], LANES // HEAD_DIM)
        qkg = jnp.stack([qg] * (ATT_WIDTH // LANES) + [kg] + [jnp.zeros_like(kg)] * 3)
        return mod_table(l), norm_mix_g[l].reshape(1, D), w_perm, cos, sin, qkg, bd_head

    p = _in_proj(stream, stream_ctx, *in_proj_args(0), S)
    for l in range(DEPTH):
        last = l == DEPTH - 1
        modv = mod_table(l)

        y_att = _attention(p, attn_sink[l], S, not last)

        pad_rows = jnp.zeros((LANES - 2 * GLA_GATE_RANK, GLA_QK_WIDTH), F32)
        zero_rank = jnp.zeros((GLA_GATE_RANK, GLA_QK_WIDTH), F32)
        wgf = jnp.concatenate([gla_gate_w[l, 0], zero_rank, pad_rows], axis=0)
        wgb = jnp.concatenate([zero_rank, gla_gate_w[l, 1], pad_rows], axis=0)
        wg = jnp.concatenate([wgf, wgb], axis=1).astype(BF16)
        gbias = jnp.concatenate([gla_gate_b[l].reshape(1, 2 * GLA_QK_WIDTH),
                                 jnp.zeros((7, 2 * GLA_QK_WIDTH), F32)], axis=0)
        ng = jnp.tile(gla_norm_g[l], GLA_HEADS).reshape(1, GLA_WIDTH)
        cw = jnp.concatenate([conv_w[l], jnp.zeros((5, CONV_WIDTH), F32)], axis=0)
        y_gc = _gla_conv(p, wg, gbias, ng, cw, bd_gla, cum, S)

        rows, tm = (S, LAT_TILE) if last else (T, TOKEN_TILE)
        wr = jnp.concatenate([w_router, jnp.zeros((D, LANES - N_EXPERTS), F32)], axis=1).astype(BF16)
        br = jnp.zeros((8, LANES), F32).at[0, :N_EXPERTS].set(b_router)
        ffn_g = norm_ffn_g[l].reshape(1, D)
        wo_att = w_out[l, :ATT_WIDTH].reshape(ATT_HEADS, HEAD_DIM, D)[order].reshape(ATT_WIDTH, D)
        wo = jnp.concatenate([wo_att, w_out[l, ATT_WIDTH:]], axis=0).astype(BF16)
        xx_mid, h2t, route, counts = _out_proj(y_att, y_gc, stream, stream_ctx, modv, wo, ffn_g, wr, br, tri, S, rows,
                                               tm)

        n_tiles = -(-(B * rows) // MOE_TILE) + N_CLASSES
        off, pad, tile, e_lo, e_hi, valid = _routing_tables(counts[:N_CLASSES, 0], n_tiles)

        dest = route[:, 1, :]
        for cls_id in range(N_CLASSES):
            dest = dest + jnp.where(route[:, 0, :] == cls_id, off[cls_id], 0)
        hs = _dispatch(pad, dest, h2t, n_tiles * MOE_TILE)
        ys = _moe(hs, tile, e_lo, e_hi, valid, wg_all, wu_all, wd_all, w_router.T, b_router, l * N_EXPERTS)
        stream, stream_ctx = _combine(xx_mid, modv, dest, ys, S, tm), None
        if not last:
            p = _in_proj(stream, None, *in_proj_args(l + 1), S)
    return stream
```

```python
import functools

import numpy as np
import jax
import jax.numpy as jnp
from jax import lax
from jax.experimental import pallas as pl
from jax.experimental.pallas import tpu as pltpu

D_MODEL = 1024
DEPTH = 2
GRID_W = 64
EPS = 1e-6
HEAD_DIM = 64
ATT_HEADS = 8
ATT_KV_HEADS = 2
ATT_GROUP = ATT_HEADS // ATT_KV_HEADS
ATT_WIDTH = ATT_HEADS * HEAD_DIM
WINDOW = 128
ROPE_BASE = 10000.0
GLA_HEADS = 4
GLA_DV = 64
GLA_DK = 32
GLA_WIDTH = GLA_HEADS * GLA_DV
GLA_GATE_RANK = 16
GLA_GATE_NORM = 16.0
GLA_CHUNK = 64
CONV_WIDTH = 256
N_EXPERTS = 16
N_GROUPS = 4
EXPERTS_PER_GROUP = 4
D_EXPERT = D_MODEL // 2

LANES = 128
KV_WIDTH = ATT_KV_HEADS * HEAD_DIM
GLA_QK_WIDTH = GLA_HEADS * GLA_DK
COL_AQ, COL_AK, COL_AV = 0, 512, 640
COL_GQ, COL_GK, COL_GV, COL_GR = 768, 896, 1024, 1280
COL_CB, COL_CC, COL_CH, COL_GT = 1536, 1792, 2048, 2304
N_PROJ = 2432
QK_COLS = COL_AV
N_PAIRS = 6
N_CLASSES = N_GROUPS * N_PAIRS
PAIR_LO = (0, 0, 0, 1, 1, 2)
PAIR_HI = (1, 2, 3, 2, 3, 3)
CLS_ROWS = 32
NEG = -1e30

TOKEN_TILE = 768
LAT_TILE = 1024
MOE_TILE = 512
SUB_ROWS = 256
VMEM_LIMIT = 56 * 1024 * 1024

F32 = jnp.float32
BF16 = jnp.bfloat16


def _cparams(n_axes):
    return pltpu.CompilerParams(dimension_semantics=("arbitrary",) * n_axes, vmem_limit_bytes=VMEM_LIMIT)


def _silu(x):
    return x * jax.nn.sigmoid(x)


def _mod_body(c_ref, w_ref, b_ref, o_ref):
    c = c_ref[...]
    a = _silu(c).astype(BF16)
    o_ref[0] = jnp.dot(a, w_ref[0].astype(BF16), preferred_element_type=F32) + b_ref[0]


def _modulation(cond, w_ada, b_ada):
    rows = cond.shape[0]
    nblk = w_ada.shape[2] // D_MODEL
    return pl.pallas_call(
        _mod_body,
        grid=(DEPTH, nblk),
        in_specs=[pl.BlockSpec((rows, D_MODEL), lambda l, n: (0, 0)),
                  pl.BlockSpec((1, D_MODEL, D_MODEL), lambda l, n: (l, 0, n)),
                  pl.BlockSpec((1, 1, D_MODEL), lambda l, n: (l, 0, n))],
        out_specs=pl.BlockSpec((1, rows, D_MODEL), lambda l, n: (l, 0, n)),
        out_shape=jax.ShapeDtypeStruct((DEPTH, rows, w_ada.shape[2]), F32),
        compiler_params=_cparams(2),
        name="modulation",
    )(cond, w_ada, b_ada.reshape(DEPTH, 1, -1))


def _row_mod(mod, is_ctx, i):
    return jnp.where(is_ctx, mod[6 + i:7 + i], mod[i:i + 1])


def _norm_modulate(x, gain, mod, is_ctx, i_shift, i_scale):
    ms = jnp.mean(x * x, axis=-1, keepdims=True)
    xn = x * lax.rsqrt(ms + EPS) * gain
    return xn * (1.0 + _row_mod(mod, is_ctx, i_scale)) + _row_mod(mod, is_ctx, i_shift)


def _stream_specs(stream, ctx, tm, b_of, j_of):
    n_sub = tm // SUB_ROWS
    last_piece = stream.shape[1] // SUB_ROWS - 1
    specs = [pl.BlockSpec((1, SUB_ROWS, D_MODEL),
                          lambda *g, k=k: (b_of(g), jnp.minimum(j_of(g) * n_sub + k, last_piece), 0))
             for k in range(n_sub)]
    args = [stream] * n_sub
    if ctx is not None:
        assert ctx.shape[1] == SUB_ROWS and (stream.shape[1] + SUB_ROWS) % tm == 0
        specs.append(pl.BlockSpec((1, SUB_ROWS, D_MODEL), lambda *g: (b_of(g), 0, 0)))
        args.append(ctx)
    return specs, args


def _stream_piece(x_refs, k, n_sub, is_last_tile):
    x = x_refs[k][0]
    if len(x_refs) > n_sub and k == n_sub - 1:
        x = jnp.where(is_last_tile, x_refs[n_sub][0], x)
    return x


def _in_tile(piece, j, mod_ref, g_ref, w_ref, cos_ref, sin_ref, qkg_ref, bd_ref, o_ref, n_lat, tm):
    lane = lax.broadcasted_iota(jnp.int32, (1, LANES), 1)
    first_half = (lane % HEAD_DIM) < (HEAD_DIM // 2)
    for k, r0 in enumerate(range(0, tm, SUB_ROWS)):
        rows = slice(r0, r0 + SUB_ROWS)
        row = j * tm + r0 + lax.broadcasted_iota(jnp.int32, (SUB_ROWS, 1), 0)
        is_ctx = row >= n_lat
        x = piece(k, is_ctx)
        h = _norm_modulate(x, g_ref[...], mod_ref[0], is_ctx, 0, 1).astype(BF16)
        qk = jnp.dot(h, w_ref[:, :QK_COLS], preferred_element_type=F32)
        cos = cos_ref[rows, :]
        sin = sin_ref[rows, :]
        for c in range(QK_COLS // LANES):
            xc = qk[:, c * LANES:(c + 1) * LANES]
            ss = jnp.dot((xc * xc).astype(BF16), bd_ref[...], preferred_element_type=F32) * (1.0 / HEAD_DIM)
            xc = xc * lax.rsqrt(ss + EPS) * qkg_ref[c:c + 1, :]
            rot = jnp.where(first_half, pltpu.roll(xc, LANES - HEAD_DIM // 2, 1), pltpu.roll(xc, HEAD_DIM // 2, 1))
            o_ref[0, rows, c * LANES:(c + 1) * LANES] = (xc * cos + rot * sin).astype(BF16)
        o_ref[0, rows, QK_COLS:] = jnp.dot(h, w_ref[:, QK_COLS:], preferred_element_type=F32).astype(BF16)


def _in_body(*refs, n_lat, tm, n_x):
    x_refs = refs[:n_x]
    j = pl.program_id(0)
    is_last_tile = j == pl.num_programs(0) - 1
    _in_tile(lambda k, is_ctx: _stream_piece(x_refs, k, tm // SUB_ROWS, is_last_tile), j, *refs[n_x:], n_lat, tm)


def _in_proj(stream, ctx, modv, gain, w, cos, sin, qkg, bd, n_lat):
    B = stream.shape[0]
    T = stream.shape[1] + (0 if ctx is None else ctx.shape[1])
    tm = TOKEN_TILE
    x_specs, x_args = _stream_specs(stream, ctx, tm, lambda g: g[1], lambda g: g[0])
    return pl.pallas_call(
        functools.partial(_in_body, n_lat=n_lat, tm=tm, n_x=len(x_args)),
        grid=(T // tm, B),
        in_specs=x_specs + [
                  pl.BlockSpec((1, 16, D_MODEL), lambda j, b: (b, 0, 0)),
                  pl.BlockSpec((1, D_MODEL), lambda j, b: (0, 0)),
                  pl.BlockSpec((D_MODEL, N_PROJ), lambda j, b: (0, 0)),
                  pl.BlockSpec((tm, LANES), lambda j, b: (j, 0)),
                  pl.BlockSpec((tm, LANES), lambda j, b: (j, 0)),
                  pl.BlockSpec((8, LANES), lambda j, b: (0, 0)),
                  pl.BlockSpec((LANES, LANES), lambda j, b: (0, 0))],
        out_specs=pl.BlockSpec((1, tm, N_PROJ), lambda j, b: (b, j, 0)),
        out_shape=jax.ShapeDtypeStruct((B, T, N_PROJ), BF16),
        compiler_params=_cparams(2),
        name="in_proj",
    )(*x_args, modv, gain, w, cos, sin, qkg, bd)


ATT_BLOCK = 128
ATT_SPAN = ATT_BLOCK + 2 * WINDOW


ATT_HEAD_ORDER = (0, 4, 1, 5, 2, 6, 3, 7)


def _attend(qblk, k_parts, v_parts, biases, sinks):
    rows = ATT_GROUP * ATT_BLOCK
    rowi = lax.broadcasted_iota(jnp.int32, (rows, 1), 0)
    lane = lax.broadcasted_iota(jnp.int32, (1, LANES), 1)
    lower = lane < HEAD_DIM
    nt = (((1,), (1,)), ((), ()))
    heads = range(ATT_KV_HEADS)
    keep = [lower, jnp.logical_not(lower)]
    sink, scores = [], []
    for h in heads:
        qs = jnp.concatenate([jnp.where(keep[h], qblk[:, g * LANES:(g + 1) * LANES], jnp.zeros((), BF16))
                              for g in range(ATT_GROUP)], axis=0)
        col = jnp.full((rows, 1), sinks[ATT_GROUP * h + ATT_GROUP - 1], F32)
        for g in range(ATT_GROUP - 2, -1, -1):
            col = jnp.where(rowi < (g + 1) * ATT_BLOCK, sinks[ATT_GROUP * h + g], col)
        sink.append(col)
        pieces = []
        for k in k_parts:
            s = lax.dot_general(qs, k, nt, preferred_element_type=F32)
            pieces += [s[:, c * LANES:(c + 1) * LANES] for c in range(k.shape[0] // LANES)]
        scores.append([s if b is None else s + b for s, b in zip(pieces, biases)])
    top, probs = [], []
    for h in heads:
        m = scores[h][0]
        for s in scores[h][1:]:
            m = jnp.maximum(m, s)
        m = jnp.maximum(jnp.max(m, axis=-1, keepdims=True), sink[h])
        top.append(m)
        probs.append([jnp.exp((s - m).astype(BF16)) for s in scores[h]])
    normed = []
    for h in heads:
        acc = jnp.where(keep[h], 0.0, jnp.exp(sink[h] - top[h]))
        c0 = 0
        for v in v_parts[h]:
            n = v.shape[0] // LANES
            acc = acc + jnp.dot(jnp.concatenate(probs[h][c0:c0 + n], axis=1), v, preferred_element_type=F32)
            c0 += n
        normed.append(acc / pltpu.roll(acc, HEAD_DIM, 1))
    return jnp.concatenate([jnp.where(lower, normed[0][g * ATT_BLOCK:(g + 1) * ATT_BLOCK],
                                      normed[1][g * ATT_BLOCK:(g + 1) * ATT_BLOCK]) for g in range(ATT_GROUP)], axis=1)


def _att_body(sink_ref, q_ref, k_ref, v_ref, o_ref, v1_s, *, n_lat, n_ctx, with_ctx_out):
    sinks = [sink_ref[i] for i in range(ATT_HEADS)]
    lane = lax.broadcasted_iota(jnp.int32, (1, LANES), 1)
    vv = v_ref[0]
    v1_s[0] = jnp.where(lane < HEAD_DIM, vv, jnp.ones((), BF16))
    v1_s[1] = jnp.where(lane < HEAD_DIM, jnp.ones((), BF16), vv)
    k_ctx = k_ref[0, n_lat:n_lat + n_ctx, :]
    v_ctx = [v1_s[h, n_lat:n_lat + n_ctx, :] for h in range(ATT_KV_HEADS)]
    no_bias = [None] * (n_ctx // LANES)
    qi = lax.broadcasted_iota(jnp.int32, (ATT_GROUP * ATT_BLOCK, LANES), 0) % ATT_BLOCK
    ki = lax.broadcasted_iota(jnp.int32, (ATT_GROUP * ATT_BLOCK, LANES), 1)
    past_ok = jnp.where(ki >= qi, 0.0, NEG)
    ahead_ok = jnp.where(ki <= qi, 0.0, NEG)

    def block(q0, k0, n_keys, biases):
        k_parts = [k_ref[0, pl.ds(k0, n_keys), :], k_ctx]
        v_parts = [[v1_s[h, pl.ds(k0, n_keys), :], v_ctx[h]] for h in range(ATT_KV_HEADS)]
        out = _attend(q_ref[0, pl.ds(q0, ATT_BLOCK), :], k_parts, v_parts, biases + no_bias, sinks)
        o_ref[0, pl.ds(q0, ATT_BLOCK), :] = out.astype(BF16)

    def interior(i, carry):
        q0 = pl.multiple_of(i * ATT_BLOCK, ATT_BLOCK)
        block(q0, pl.multiple_of(q0 - WINDOW, ATT_BLOCK), ATT_SPAN, [past_ok, None, ahead_ok])
        return carry

    nq = n_lat // ATT_BLOCK
    block(0, 0, 2 * ATT_BLOCK, [None, ahead_ok])
    lax.fori_loop(1, nq - 1, interior, 0, unroll=7)
    block(n_lat - ATT_BLOCK, n_lat - 2 * ATT_BLOCK, 2 * ATT_BLOCK, [past_ok, None])
    if with_ctx_out:
        for c in range(n_ctx // ATT_BLOCK):
            r0 = n_lat + c * ATT_BLOCK
            out = _attend(q_ref[0, r0:r0 + ATT_BLOCK, :], [k_ctx], [[v_ctx[h]] for h in range(ATT_KV_HEADS)],
                          no_bias, sinks)
            o_ref[0, r0:r0 + ATT_BLOCK, :] = out.astype(BF16)
    else:
        o_ref[0, n_lat:, :] = jnp.zeros((n_ctx, ATT_WIDTH), BF16)


def _attention(p, sinks, n_lat, with_ctx_out):
    B, T, _ = p.shape
    return pl.pallas_call(
        functools.partial(_att_body, n_lat=n_lat, n_ctx=T - n_lat, with_ctx_out=with_ctx_out),
        grid=(B,),
        in_specs=[pl.BlockSpec(memory_space=pltpu.SMEM),
                  pl.BlockSpec((1, T, ATT_WIDTH), lambda b: (b, 0, COL_AQ // ATT_WIDTH)),
                  pl.BlockSpec((1, T, KV_WIDTH), lambda b: (b, 0, COL_AK // KV_WIDTH)),
                  pl.BlockSpec((1, T, KV_WIDTH), lambda b: (b, 0, COL_AV // KV_WIDTH))],
        out_specs=pl.BlockSpec((1, T, ATT_WIDTH), lambda b: (b, 0, 0)),
        out_shape=jax.ShapeDtypeStruct((B, T, ATT_WIDTH), BF16),
        scratch_shapes=[pltpu.VMEM((ATT_KV_HEADS, T, KV_WIDTH), BF16)],
        compiler_params=_cparams(1),
        name="attention",
    )(sinks, p, p, p)


GLA_BLOCK_A = 256
GLA_BLOCK_B = 128
CONV_PAD = 8
EPI_ROWS = 256


def _log_sigmoid(z):
    return jnp.minimum(z, 0.0) - jnp.log1p(jnp.exp(-jnp.abs(z)))


def _gla_body(gq_ref, gk_ref, gv_ref, gr_ref, cb_ref, cc_ref, ch_ref, gt_ref, wg_ref, gbias_ref, ng_ref,
              cw_ref, bd_ref, cum_ref, o_ref, qe_s, ke_s, dec_s, upd_s, prev_s, st_s, o_s, u_s, *, n_lat, n_ctx):
    T = n_lat + n_ctx
    C = GLA_CHUNK
    nt = (((1,), (1,)), ((), ()))
    tn = (((0,), (0,)), ((), ()))

    sr = lax.broadcasted_iota(jnp.int32, (GLA_WIDTH, 2 * GLA_QK_WIDTH), 0) // GLA_DV
    sl = (lax.broadcasted_iota(jnp.int32, (GLA_WIDTH, 2 * GLA_QK_WIDTH), 1) % GLA_QK_WIDTH) // GLA_DK
    state_mask2 = sr == sl

    def factors(i, carry):
        r0 = pl.multiple_of(i * GLA_BLOCK_A, GLA_BLOCK_A)
        rows = pl.ds(r0, GLA_BLOCK_A)
        gt = gt_ref[0, rows, :]
        q = gq_ref[0, rows, :].astype(F32) * (GLA_DK ** -0.5)
        k = gk_ref[0, rows, :].astype(F32)
        v = gv_ref[0, rows, :]
        z2 = jnp.dot(gt, wg_ref[...], preferred_element_type=F32) + gbias_ref[0:1, :]
        g2 = _log_sigmoid(z2) * (1.0 / GLA_GATE_NORM)
        cum = []
        for d in range(2):
            g = g2[:, d * GLA_QK_WIDTH:(d + 1) * GLA_QK_WIDTH]
            g_hi = g.astype(BF16)
            g_lo = (g - g_hi.astype(F32)).astype(BF16)
            cum.append(jnp.dot(cum_ref[d], jnp.concatenate([g_hi, g_lo], axis=1), preferred_element_type=F32))
        kl = []
        for d in range(2):
            b = cum[d][:GLA_BLOCK_A, :GLA_QK_WIDTH] + cum[d][:GLA_BLOCK_A, GLA_QK_WIDTH:]
            tot = cum[d][GLA_BLOCK_A:, :GLA_QK_WIDTH] + cum[d][GLA_BLOCK_A:, GLA_QK_WIDTH:]
            dec = jnp.exp(tot)
            ke = k * jnp.exp(-b)
            qe_s[d, rows, :] = (q * jnp.exp(b)).astype(BF16)
            ke_s[d, rows, :] = ke.astype(BF16)
            dec_s[d, rows, :] = dec
            kl.append((ke * dec).astype(BF16))
        kl2 = jnp.concatenate(kl, axis=1)
        for cc in range(GLA_BLOCK_A // C):
            upd = lax.dot_general(v[cc * C:(cc + 1) * C], kl2[cc * C:(cc + 1) * C], tn, preferred_element_type=F32)
            upd_s[i * (GLA_BLOCK_A // C) + cc] = jnp.where(state_mask2, upd, 0.0)
        return carry

    lax.fori_loop(0, T // GLA_BLOCK_A, factors, 0, unroll=True)

    nc_lat = n_lat // C
    nc_ctx = n_ctx // C
    st_s[...] = jnp.zeros_like(st_s)

    def scan(i, carry):
        in_ctx = i < nc_ctx
        cf = jnp.where(in_ctx, nc_lat + i, i - nc_ctx)
        cb = jnp.where(in_ctx, nc_lat + nc_ctx - 1 - i, nc_lat - 1 - (i - nc_ctx))
        for d, cid in enumerate((cf, cb)):
            st = st_s[d]
            prev_s[cid, :, d * GLA_QK_WIDTH:(d + 1) * GLA_QK_WIDTH] = st.astype(BF16)
            st_s[d] = (st * dec_s[d, pl.ds(pl.multiple_of(cid * C, C), 1), :]
                       + upd_s[cid, :, d * GLA_QK_WIDTH:(d + 1) * GLA_QK_WIDTH])
        return carry

    lax.fori_loop(0, nc_lat + nc_ctx, scan, 0)

    RB = GLA_BLOCK_B
    k_rows = lax.broadcasted_iota(jnp.int32, (GLA_HEADS * RB, GLA_QK_WIDTH), 0) // RB
    k_lanes = lax.broadcasted_iota(jnp.int32, (GLA_HEADS * RB, GLA_QK_WIDTH), 1) // GLA_DK
    key_heads = k_rows == k_lanes
    v_rows = lax.broadcasted_iota(jnp.int32, (GLA_HEADS * RB, GLA_WIDTH), 0) // RB
    v_lanes = lax.broadcasted_iota(jnp.int32, (GLA_HEADS * RB, GLA_WIDTH), 1) // GLA_DV
    value_heads = v_rows == v_lanes
    qr = lax.broadcasted_iota(jnp.int32, (RB, GLA_HEADS * RB), 0)
    kc = lax.broadcasted_iota(jnp.int32, (RB, GLA_HEADS * RB), 1) % RB
    same_chunk = (qr // C) == (kc // C)
    forward = kc <= qr
    zero = jnp.zeros((), BF16)

    def outputs(i, carry):
        r0 = pl.multiple_of(i * RB, RB)
        rows = pl.ds(r0, RB)
        v = gv_ref[0, rows, :]
        v4 = jnp.where(value_heads, jnp.concatenate([v] * GLA_HEADS, axis=0), zero)
        qe = [qe_s[d, rows, :] for d in range(2)]
        att = []
        for d in range(2):
            ke4 = jnp.where(key_heads, jnp.concatenate([ke_s[d, rows, :]] * GLA_HEADS, axis=0), zero)
            att.append(lax.dot_general(qe[d], ke4, nt, preferred_element_type=F32))
        both = jnp.where(same_chunk, jnp.where(forward, att[0], att[1]), 0.0).astype(BF16)
        qe2 = jnp.concatenate(qe, axis=1)
        inter = [lax.dot_general(qe2[cc * C:(cc + 1) * C], prev_s[i * (RB // C) + cc], nt,
                                 preferred_element_type=F32) for cc in range(RB // C)]
        o_s[rows, :] = jnp.dot(both, v4, preferred_element_type=F32) + jnp.concatenate(inter, axis=0)
        return carry

    lax.fori_loop(0, T // RB, outputs, 0, unroll=6)

    u_s[0:CONV_PAD, :] = jnp.zeros((CONV_PAD, CONV_WIDTH), F32)
    u_s[CONV_PAD + T:, :] = jnp.zeros((CONV_PAD, CONV_WIDTH), F32)
    u_s[CONV_PAD:CONV_PAD + T, :] = cc_ref[0].astype(F32) * ch_ref[0].astype(F32)
    w0 = cw_ref[0:1, :]
    w1 = cw_ref[1:2, :]
    w2 = cw_ref[2:3, :]
    for e in range(T // EPI_ROWS):
        r0 = e * EPI_ROWS
        o = o_s[r0:r0 + EPI_ROWS, :]
        ss = jnp.dot((o * o).astype(BF16), bd_ref[...], preferred_element_type=F32) * (1.0 / GLA_DV)
        on = o * lax.rsqrt(ss + EPS) * ng_ref[...]
        r = gr_ref[0, r0:r0 + EPI_ROWS, :].astype(F32)
        o_ref[0, r0:r0 + EPI_ROWS, 0:GLA_WIDTH] = (on * _silu(r)).astype(BF16)
        t = r0 + lax.broadcasted_iota(jnp.int32, (EPI_ROWS, 1), 0)
        up = u_s[CONV_PAD + r0 - 1:CONV_PAD + r0 - 1 + EPI_ROWS, :]
        mid = u_s[CONV_PAD + r0:CONV_PAD + r0 + EPI_ROWS, :]
        dn = u_s[CONV_PAD + r0 + 1:CONV_PAD + r0 + 1 + EPI_ROWS, :]
        up = jnp.where(t == n_lat, 0.0, up)
        dn = jnp.where(t == n_lat - 1, 0.0, dn)
        conv = w0 * up + w1 * mid + w2 * dn
        o_ref[0, r0:r0 + EPI_ROWS, GLA_WIDTH:] = (cb_ref[0, r0:r0 + EPI_ROWS, :].astype(F32) * conv).astype(BF16)


def _gla_cum_matrices():
    i = np.arange(GLA_BLOCK_A)
    same = (i[:, None] // GLA_CHUNK) == (i[None, :] // GLA_CHUNK)
    fwd = same & (i[None, :] <= i[:, None])
    bwd = same & (i[None, :] >= i[:, None])
    mats = np.stack([np.concatenate([fwd, same], axis=0), np.concatenate([bwd, same], axis=0)])
    return jnp.asarray(mats.astype(np.float32), dtype=BF16)


def _gla_conv(p, wg, gbias, ng, cw, bd, cum, n_lat):
    B, T, _ = p.shape
    nc = T // GLA_CHUNK

    def col(width, start):
        return pl.BlockSpec((1, T, width), lambda b: (b, 0, start // width))

    def const(shape):
        return pl.BlockSpec(shape, lambda b: (0,) * len(shape))

    return pl.pallas_call(
        functools.partial(_gla_body, n_lat=n_lat, n_ctx=T - n_lat),
        grid=(B,),
        in_specs=[col(GLA_QK_WIDTH, COL_GQ), col(GLA_QK_WIDTH, COL_GK), col(GLA_WIDTH, COL_GV), col(GLA_WIDTH, COL_GR),
                  col(CONV_WIDTH, COL_CB), col(CONV_WIDTH, COL_CC), col(CONV_WIDTH, COL_CH), col(LANES, COL_GT),
                  const((LANES, 2 * GLA_QK_WIDTH)), const((8, 2 * GLA_QK_WIDTH)),
                  const((1, GLA_WIDTH)), const((8, CONV_WIDTH)), const((GLA_WIDTH, GLA_WIDTH)),
                  const((2, 2 * GLA_BLOCK_A, GLA_BLOCK_A))],
        out_specs=pl.BlockSpec((1, T, GLA_WIDTH + CONV_WIDTH), lambda b: (b, 0, 0)),
        out_shape=jax.ShapeDtypeStruct((B, T, GLA_WIDTH + CONV_WIDTH), BF16),
        scratch_shapes=[pltpu.VMEM((2, T, GLA_QK_WIDTH), BF16), pltpu.VMEM((2, T, GLA_QK_WIDTH), BF16),
                        pltpu.VMEM((2, T, GLA_QK_WIDTH), F32),
                        pltpu.VMEM((nc, GLA_WIDTH, 2 * GLA_QK_WIDTH), F32),
                        pltpu.VMEM((nc, GLA_WIDTH, 2 * GLA_QK_WIDTH), BF16),
                        pltpu.VMEM((2, GLA_WIDTH, GLA_QK_WIDTH), F32),
                        pltpu.VMEM((T, GLA_WIDTH), F32),
                        pltpu.VMEM((T + 2 * CONV_PAD, CONV_WIDTH), F32)],
        compiler_params=_cparams(1),
        name="gla_conv",
    )(p, p, p, p, p, p, p, p, wg, gbias, ng, cw, bd, cum)


def _route(logits_t):
    mx = jnp.max(logits_t, axis=0, keepdims=True)
    ex = jnp.exp(logits_t - mx)
    probs = ex / jnp.sum(ex, axis=0, keepdims=True)
    P = [probs[e:e + 1] for e in range(N_EXPERTS)]
    scores = []
    for g in range(N_GROUPS):
        a, b, c, d = P[4 * g:4 * g + 4]
        scores.append(jnp.maximum(jnp.maximum(jnp.maximum(a + b, a + c), jnp.maximum(a + d, b + c)),
                                  jnp.maximum(b + d, c + d)))
    best = jnp.maximum(jnp.maximum(scores[0], scores[1]), jnp.maximum(scores[2], scores[3]))
    taken = jnp.zeros_like(best, dtype=jnp.bool_)
    sel = []
    for g in range(N_GROUPS):
        s = (scores[g] == best) & jnp.logical_not(taken)
        sel.append(s)
        taken = taken | s
    gsel = jnp.where(sel[1], 1.0, 0.0) + jnp.where(sel[2], 2.0, 0.0) + jnp.where(sel[3], 3.0, 0.0)
    ig = [jnp.where(sel[0], P[j], jnp.where(sel[1], P[4 + j], jnp.where(sel[2], P[8 + j], P[12 + j])))
          for j in range(EXPERTS_PER_GROUP)]

    def first_max(vals):
        v = jnp.maximum(jnp.maximum(vals[0], vals[1]), jnp.maximum(vals[2], vals[3]))
        tk = jnp.zeros_like(v, dtype=jnp.bool_)
        hot = []
        for x in vals:
            s = (x == v) & jnp.logical_not(tk)
            hot.append(s)
            tk = tk | s
        idx = jnp.where(hot[1], 1.0, 0.0) + jnp.where(hot[2], 2.0, 0.0) + jnp.where(hot[3], 3.0, 0.0)
        return v, hot, idx

    _, hot1, i1 = first_max(ig)
    _, _, i2 = first_max([jnp.where(hot1[j], -1.0, ig[j]) for j in range(EXPERTS_PER_GROUP)])
    lo = jnp.minimum(i1, i2)
    hi = jnp.maximum(i1, i2)
    pair = jnp.where(lo == 0.0, hi - 1.0, jnp.where(lo == 1.0, hi + 1.0, 5.0))
    return gsel * N_PAIRS + pair


def _class_rank(cls, tri_ref, cnt_s):
    n = cls.shape[1]
    cid = lax.broadcasted_iota(jnp.int32, (CLS_ROWS, n), 0).astype(F32)
    onehot = jnp.where(cls == cid, 1.0, 0.0)
    segs = [onehot[:, k * LANES:(k + 1) * LANES] for k in range(n // LANES)]
    before = jnp.dot(jnp.concatenate(segs, axis=0).astype(BF16), tri_ref[...], preferred_element_type=F32)
    base = cnt_s[...]
    ranks = []
    for k, seg in enumerate(segs):
        ranks.append(jnp.sum(seg * (before[k * CLS_ROWS:(k + 1) * CLS_ROWS] + base), axis=0, keepdims=True))
        base = base + jnp.sum(seg, axis=1, keepdims=True)
    cnt_s[...] = base
    return jnp.concatenate(ranks, axis=1)


def _out_body(*refs, n_lat, tm, n_x):
    x_refs = refs[:n_x]
    ya_ref, yg_ref, mod_ref, wo_ref, g_ref, wr_ref, br_ref, tri_ref, xo_ref, rt_ref, cnt_ref, cnt_s, logit_s = refs[n_x:]
    j = pl.program_id(1)
    is_last_tile = j == pl.num_programs(1) - 1

    @pl.when((pl.program_id(0) == 0) & (j == 0))
    def _():
        cnt_s[...] = jnp.zeros_like(cnt_s)

    mod = mod_ref[0]
    subs = list(enumerate(range(0, tm, SUB_ROWS)))
    ys = [jnp.dot(ya_ref[0, r0:r0 + SUB_ROWS, :], wo_ref[0:ATT_WIDTH, :], preferred_element_type=F32)
          + jnp.dot(yg_ref[0, r0:r0 + SUB_ROWS, :], wo_ref[ATT_WIDTH:, :], preferred_element_type=F32) for _, r0 in subs]
    h2s = []
    for k, r0 in subs:
        row = j * tm + r0 + lax.broadcasted_iota(jnp.int32, (SUB_ROWS, 1), 0)
        is_ctx = row >= n_lat
        xn = _stream_piece(x_refs, k, tm // SUB_ROWS, is_last_tile) + _row_mod(mod, is_ctx, 2) * ys[k]
        xo_ref[0, r0:r0 + SUB_ROWS, :] = xn
        h2s.append(_norm_modulate(xn, g_ref[...], mod, is_ctx, 3, 4).astype(BF16))
    for k, r0 in subs:
        logit_s[r0:r0 + SUB_ROWS, :] = jnp.dot(h2s[k], wr_ref[...], preferred_element_type=F32) + br_ref[0:1, :]
    cls = _route(logit_s[...].T[0:N_EXPERTS, :])
    rank = _class_rank(cls, tri_ref, cnt_s)
    rt_ref[0] = jnp.concatenate([cls, rank, jnp.zeros((6, tm), F32)], axis=0).astype(jnp.int32)
    cnt_ref[...] = jnp.broadcast_to(cnt_s[...], (CLS_ROWS, LANES)).astype(jnp.int32)


def _out_proj(ya, yg, stream, ctx, modv, wo, gain, wr, br, tri, n_lat, rows, tm):
    B = stream.shape[0]
    nj = rows // tm
    x_specs, x_args = _stream_specs(stream, ctx, tm, lambda g: g[0], lambda g: g[1])
    return pl.pallas_call(
        functools.partial(_out_body, n_lat=n_lat, tm=tm, n_x=len(x_args)),
        grid=(B, nj),
        in_specs=x_specs + [
                  pl.BlockSpec((1, tm, ATT_WIDTH), lambda b, j: (b, j, 0)),
                  pl.BlockSpec((1, tm, GLA_WIDTH + CONV_WIDTH), lambda b, j: (b, j, 0)),
                  pl.BlockSpec((1, 16, D_MODEL), lambda b, j: (b, 0, 0)),
                  pl.BlockSpec((D_MODEL, D_MODEL), lambda b, j: (0, 0)),
                  pl.BlockSpec((1, D_MODEL), lambda b, j: (0, 0)),
                  pl.BlockSpec((D_MODEL, LANES), lambda b, j: (0, 0)),
                  pl.BlockSpec((8, LANES), lambda b, j: (0, 0)),
                  pl.BlockSpec((LANES, LANES), lambda b, j: (0, 0))],
        out_specs=[pl.BlockSpec((1, tm, D_MODEL), lambda b, j: (b, j, 0)),
                   pl.BlockSpec((1, 8, tm), lambda b, j: (b * nj + j, 0, 0)),
                   pl.BlockSpec((CLS_ROWS, LANES), lambda b, j: (0, 0))],
        out_shape=[jax.ShapeDtypeStruct((B, rows, D_MODEL), F32),
                   jax.ShapeDtypeStruct((B * nj, 8, tm), jnp.int32),
                   jax.ShapeDtypeStruct((CLS_ROWS, LANES), jnp.int32)],
        scratch_shapes=[pltpu.VMEM((CLS_ROWS, 1), F32), pltpu.VMEM((tm, LANES), F32)],
        compiler_params=_cparams(2),
        name="out_proj_router",
    )(*x_args, ya, yg, modv, wo, gain, wr, br, tri)


ROW_UNROLL = 8
IDX_STRIDE = 1024


def _idx_slot(idx, slot, tm):
    return idx.at[pl.ds(pl.multiple_of(slot * IDX_STRIDE, IDX_STRIDE), tm)]


def _issue_rows(tm, idx, slot, make_copy):
    base = slot * IDX_STRIDE

    def trip(i, c):
        for u in range(ROW_UNROLL):
            r = i * ROW_UNROLL + u
            make_copy(r, idx[base + r]).start(priority=u % 2)
        return c

    lax.fori_loop(0, tm // ROW_UNROLL, trip, 0)


PAD_SLOTS = 32
PAD_BITS = 9


def _zero_pad_rows(pad_ref, zeros_ref, hs_ref, sem):
    def pieces(c, fn):
        start = pad_ref[c]
        n = pad_ref[PAD_SLOTS + c]
        for bit in range(PAD_BITS):
            size = 1 << bit
            below = n & (size - 1)

            @pl.when((n & size) != 0)
            def _():
                fn(pltpu.make_async_copy(zeros_ref.at[pl.ds(0, size)], hs_ref.at[pl.ds(start + below, size)], sem))

        def block(i, carry):
            fn(pltpu.make_async_copy(zeros_ref, hs_ref.at[pl.ds(start + n + i * MOE_TILE, MOE_TILE)], sem))
            return carry

        lax.fori_loop(0, pad_ref[2 * PAD_SLOTS + c], block, 0)

    def start_all(c, carry):
        pieces(c, lambda cp: cp.start())
        return carry

    def wait_all(c, carry):
        pieces(c, lambda cp: cp.wait())
        return carry

    lax.fori_loop(0, N_CLASSES + 1, start_all, 0)
    lax.fori_loop(0, N_CLASSES + 1, wait_all, 0)


def _disp_body(pad_ref, x_ref, mod_ref, g_ref, dest_ref, hs_ref, buf, idx, sem_i, sem_d, sem_z, *, n_lat, tm, nj,
               n_steps):
    j = pl.program_id(1)
    s = pl.program_id(0) * nj + j
    last = n_steps - 1
    slot = s % 2

    def idx_fetch(step, sl):
        return pltpu.make_async_copy(dest_ref.at[step], _idx_slot(idx, sl, tm), sem_i.at[sl])

    def drain(sl):
        pltpu.make_async_copy(buf.at[sl], hs_ref.at[pl.ds(0, tm)], sem_d.at[sl]).wait()

    @pl.when(s == 0)
    def _():
        idx_fetch(s, slot).start()
        buf[1] = jnp.zeros((tm, 8, LANES), F32)
        _zero_pad_rows(pad_ref, buf.at[1, pl.ds(0, MOE_TILE)], hs_ref, sem_z)

    row = j * tm + lax.broadcasted_iota(jnp.int32, (tm, 1), 0)
    is_ctx = row >= n_lat
    h2 = _norm_modulate(x_ref[0], g_ref[...], mod_ref[0], is_ctx, 3, 4)
    buf[slot] = h2.reshape(tm, 8, LANES)
    idx_fetch(s, slot).wait()
    _issue_rows(tm, idx, slot, lambda r, d: pltpu.make_async_copy(buf.at[slot, r], hs_ref.at[d], sem_d.at[slot]))

    @pl.when(s > 0)
    def _():
        drain(1 - slot)

    @pl.when(s < last)
    def _():
        idx_fetch(s + 1, 1 - slot).start()

    @pl.when(s == last)
    def _():
        drain(slot)


def _dispatch(pad, xx, modv, gain, dest, n_rows, n_lat, tm):
    B, rows, _ = xx.shape
    nj = rows // tm
    assert tm >= MOE_TILE
    return pl.pallas_call(
        functools.partial(_disp_body, n_lat=n_lat, tm=tm, nj=nj, n_steps=B * nj),
        grid=(B, nj),
        in_specs=[pl.BlockSpec(memory_space=pltpu.SMEM),
                  pl.BlockSpec((1, tm, D_MODEL), lambda b, j: (b, j, 0)),
                  pl.BlockSpec((1, 16, D_MODEL), lambda b, j: (b, 0, 0)),
                  pl.BlockSpec((1, D_MODEL), lambda b, j: (0, 0)),
                  pl.BlockSpec(memory_space=pl.ANY)],
        out_specs=pl.BlockSpec(memory_space=pl.ANY),
        scratch_shapes=[pltpu.VMEM((2, tm, 8, LANES), F32), pltpu.SMEM((2 * IDX_STRIDE,), jnp.int32),
                        pltpu.SemaphoreType.DMA((2,)), pltpu.SemaphoreType.DMA((2,)), pltpu.SemaphoreType.DMA],
        out_shape=jax.ShapeDtypeStruct((n_rows, 8, LANES), F32),
        compiler_params=_cparams(2),
        name="dispatch",
    )(pad, xx, modv, gain, dest)


def _moe_body(tile_ref, e_lo_ref, e_hi_ref, valid_ref, hs_ref, wg1, wu1, wd1, wg2, wu2, wd2, wr_ref, br_ref, ys_ref):
    g = pl.program_id(0)
    tm = hs_ref.shape[0]

    @pl.when(valid_ref[g] == 1)
    def _():
        x = hs_ref[...].reshape(tm, D_MODEL)
        e_lo = e_lo_ref[g]
        e_hi = e_hi_ref[g]
        dw = wr_ref[pl.ds(e_lo, 1), :] - wr_ref[pl.ds(e_hi, 1), :]
        d = jnp.sum(x * dw, axis=-1, keepdims=True) + (br_ref[e_lo] - br_ref[e_hi])
        w_lo = jax.nn.sigmoid(d)
        w_hi = jax.nn.sigmoid(-d)
        h = x.astype(BF16)

        def act(wg, wu, w):
            a = _silu(jnp.dot(h, wg[0], preferred_element_type=F32)) * jnp.dot(h, wu[0], preferred_element_type=F32)
            return (a * w).astype(BF16)

        y = (jnp.dot(act(wg1, wu1, w_lo), wd1[0], preferred_element_type=F32)
             + jnp.dot(act(wg2, wu2, w_hi), wd2[0], preferred_element_type=F32))
        ys_ref[...] = y.reshape(tm, 8, LANES)

    @pl.when(valid_ref[g] == 0)
    def _():
        ys_ref[...] = jnp.zeros_like(ys_ref)


def _moe(hs, tile, e_lo, e_hi, valid, wg, wu, wd, wr_t, br, first_expert):
    n_tiles = tile.shape[0]
    tm = MOE_TILE

    def w_in(sel):
        return pl.BlockSpec((1, D_MODEL, D_EXPERT), lambda g, t, lo, hi, v: (first_expert + (lo, hi)[sel][g], 0, 0))

    def w_out(sel):
        return pl.BlockSpec((1, D_EXPERT, D_MODEL), lambda g, t, lo, hi, v: (first_expert + (lo, hi)[sel][g], 0, 0))

    return pl.pallas_call(
        _moe_body,
        grid_spec=pltpu.PrefetchScalarGridSpec(
            num_scalar_prefetch=4,
            grid=(n_tiles,),
            in_specs=[pl.BlockSpec((tm, 8, LANES), lambda g, t, lo, hi, v: (t[g], 0, 0)),
                      w_in(0), w_in(0), w_out(0), w_in(1), w_in(1), w_out(1),
                      pl.BlockSpec((N_EXPERTS, D_MODEL), lambda g, t, lo, hi, v: (0, 0)),
                      pl.BlockSpec(memory_space=pltpu.SMEM)],
            out_specs=pl.BlockSpec((tm, 8, LANES), lambda g, t, lo, hi, v: (g, 0, 0))),
        out_shape=jax.ShapeDtypeStruct((n_tiles * tm, 8, LANES), F32),
        compiler_params=_cparams(1),
        name="moe_pairs",
    )(tile, e_lo, e_hi, valid, hs, wg, wu, wd, wg, wu, wd, wr_t, br)


def _gathered_slot(dest_ref, ys_ref, buf, idx, sem_i, sem_d, s, n_steps, tm):
    last = n_steps - 1
    slot = s % 2

    def idx_fetch(step, sl):
        return pltpu.make_async_copy(dest_ref.at[step], _idx_slot(idx, sl, tm), sem_i.at[sl])

    def gather(sl):
        _issue_rows(tm, idx, sl, lambda r, d: pltpu.make_async_copy(ys_ref.at[d], buf.at[sl, r], sem_d.at[sl]))

    @pl.when(s == 0)
    def _():
        idx_fetch(0, 0).start()
        idx_fetch(0, 0).wait()
        gather(0)
        if last > 0:
            idx_fetch(1, 1).start()

    @pl.when(s < last)
    def _():
        idx_fetch(s + 1, 1 - slot).wait()
        gather(1 - slot)

    @pl.when(s + 2 <= last)
    def _():
        idx_fetch(s + 2, slot).start()

    pltpu.make_async_copy(ys_ref.at[pl.ds(0, tm)], buf.at[slot], sem_d.at[slot]).wait()
    return slot


def _fin_body(x_ref, mod_ref, dest_ref, ys_ref, xo_ref, buf, idx, sem_i, sem_d, *, n_lat, tm, nj, n_steps):
    j = pl.program_id(1)
    slot = _gathered_slot(dest_ref, ys_ref, buf, idx, sem_i, sem_d, pl.program_id(0) * nj + j, n_steps, tm)
    row = j * tm + lax.broadcasted_iota(jnp.int32, (tm, 1), 0)
    is_ctx = row >= n_lat
    xo_ref[0] = x_ref[0] + _row_mod(mod_ref[0], is_ctx, 5) * buf[slot].reshape(tm, D_MODEL)


def _combine(xx, modv, dest, ys, n_lat, tm):
    B, rows, _ = xx.shape
    nj = rows // tm
    return pl.pallas_call(
        functools.partial(_fin_body, n_lat=n_lat, tm=tm, nj=nj, n_steps=B * nj),
        grid=(B, nj),
        in_specs=[pl.BlockSpec((1, tm, D_MODEL), lambda b, j: (b, j, 0)),
                  pl.BlockSpec((1, 16, D_MODEL), lambda b, j: (b, 0, 0)),
                  pl.BlockSpec(memory_space=pl.ANY),
                  pl.BlockSpec(memory_space=pl.ANY)],
        out_specs=pl.BlockSpec((1, tm, D_MODEL), lambda b, j: (b, j, 0)),
        scratch_shapes=[pltpu.VMEM((2, tm, 8, LANES), F32), pltpu.SMEM((2 * IDX_STRIDE,), jnp.int32),
                        pltpu.SemaphoreType.DMA((2,)), pltpu.SemaphoreType.DMA((2,))],
        out_shape=jax.ShapeDtypeStruct((B, rows, D_MODEL), F32),
        compiler_params=_cparams(2),
        name="combine",
    )(xx, modv, dest, ys)


def _rope_tables(n_lat, n_ctx):
    rows = n_lat // GRID_W
    row, col = jnp.meshgrid(jnp.arange(rows), jnp.arange(GRID_W), indexing="ij")
    n_freq = HEAD_DIM // 4
    inv_freq = ROPE_BASE ** (-jnp.arange(n_freq, dtype=F32) / n_freq)
    ang = jnp.concatenate([row.reshape(-1, 1).astype(F32) * inv_freq, col.reshape(-1, 1).astype(F32) * inv_freq],
                          axis=-1)
    cos = jnp.tile(jnp.cos(ang), (1, LANES // (HEAD_DIM // 2)))
    sin = jnp.tile(jnp.sin(ang), (1, LANES // (HEAD_DIM // 2)))
    sign = jnp.where((jnp.arange(LANES) % HEAD_DIM) < HEAD_DIM // 2, -1.0, 1.0).astype(F32)
    cos = jnp.concatenate([cos, jnp.ones((n_ctx, LANES), F32)], axis=0)
    sin = jnp.concatenate([sin * sign, jnp.zeros((n_ctx, LANES), F32)], axis=0)
    return cos, sin


def _block_diag_ones(n, blk):
    i = np.arange(n) // blk
    return jnp.asarray((i[:, None] == i[None, :]).astype(np.float32), dtype=BF16)


def _routing_tables(counts, n_tiles):
    tiles_c = (counts + MOE_TILE - 1) // MOE_TILE
    tile_end = jnp.cumsum(tiles_c)
    tile_start = tile_end - tiles_c
    off = tile_start * MOE_TILE
    total = tile_end[-1]
    g = jnp.arange(n_tiles, dtype=jnp.int32)
    valid = (g < total).astype(jnp.int32)
    g_eff = jnp.minimum(g, total - 1)
    c_of = jnp.sum((g_eff[:, None] >= tile_end[None, :]).astype(jnp.int32), axis=1)
    group = c_of // N_PAIRS
    pair = c_of % N_PAIRS
    lo = jnp.asarray(PAIR_LO, jnp.int32)
    hi = jnp.asarray(PAIR_HI, jnp.int32)
    e_lo = group * EXPERTS_PER_GROUP + jnp.sum((pair[:, None] == jnp.arange(N_PAIRS)[None, :]) * lo[None, :], axis=1)
    e_hi = group * EXPERTS_PER_GROUP + jnp.sum((pair[:, None] == jnp.arange(N_PAIRS)[None, :]) * hi[None, :], axis=1)
    fill = jnp.zeros((PAD_SLOTS - N_CLASSES - 1,), jnp.int32)
    pad = jnp.concatenate([off + counts, (total * MOE_TILE)[None], fill,
                           tiles_c * MOE_TILE - counts, jnp.zeros((1,), jnp.int32), fill,
                           jnp.zeros((N_CLASSES,), jnp.int32), (n_tiles - total)[None], fill])
    return off.astype(jnp.int32), pad.astype(jnp.int32), g_eff, e_lo.astype(jnp.int32), e_hi.astype(jnp.int32), valid


def kernel(x, c, ctx, c_ctx, w_ada, b_ada, norm_mix_g, norm_ffn_g, w_in, q_norm_g, k_norm_g, attn_sink, gla_gate_w,
           gla_gate_b, gla_norm_g, conv_w, w_out, w_router, b_router, w_gate_e, w_up_e, w_down_e):
    B, S, D = x.shape
    L = ctx.shape[1]
    T = S + L
    assert D == D_MODEL and T % TOKEN_TILE == 0 and S % LAT_TILE == 0 and S % GRID_W == 0
    assert S % ATT_BLOCK == 0 and L % ATT_BLOCK == 0 and S >= ATT_SPAN and T % EPI_ROWS == 0

    cond_rows = -(-(B + 1) // 8) * 8
    cond = jnp.zeros((cond_rows, D), F32).at[:B].set(c).at[B].set(c_ctx)
    mod_all = _modulation(cond, w_ada, b_ada)

    cos, sin = _rope_tables(S, L)
    bd_head = _block_diag_ones(LANES, HEAD_DIM)
    bd_gla = _block_diag_ones(GLA_WIDTH, GLA_DV)
    cum = _gla_cum_matrices()
    tri = jnp.asarray(np.triu(np.ones((LANES, LANES), np.float32), 1), dtype=BF16)
    stream, stream_ctx = x, ctx

    order = jnp.asarray(ATT_HEAD_ORDER)
    wg_all = w_gate_e.reshape(DEPTH * N_EXPERTS, D, D_EXPERT).astype(BF16)
    wu_all = w_up_e.reshape(DEPTH * N_EXPERTS, D, D_EXPERT).astype(BF16)
    wd_all = w_down_e.reshape(DEPTH * N_EXPERTS, D_EXPERT, D).astype(BF16)

    def mod_table(l):
        m_lat = mod_all[l, :B].reshape(B, 6, D)
        m_ctx = jnp.broadcast_to(mod_all[l, B].reshape(1, 6, D), (B, 6, D))
        return jnp.concatenate([m_lat, m_ctx, jnp.zeros((B, 4, D), F32)], axis=1)

    def in_proj_args(l):
        wl = w_in[l]
        wq = wl[:, :ATT_WIDTH].reshape(D, ATT_HEADS, HEAD_DIM)[:, order, :].reshape(D, ATT_WIDTH)
        w_perm = jnp.concatenate([wq, wl[:, ATT_WIDTH:1536], wl[:, 1568:], wl[:, 1536:1568],
                                  jnp.zeros((D, N_PROJ - wl.shape[1]), F32)], axis=1).astype(BF16)
        qg = jnp.tile(q_norm_g[l], LANES // HEAD_DIM) * (HEAD_DIM ** -0.5)
        kg = jnp.tile(k_norm_g[l], LANES // HEAD_DIM)
        qkg = jnp.stack([qg] * (ATT_WIDTH // LANES) + [kg] + [jnp.zeros_like(kg)] * 3)
        return mod_table(l), norm_mix_g[l].reshape(1, D), w_perm, cos, sin, qkg, bd_head

    p = _in_proj(stream, stream_ctx, *in_proj_args(0), S)
    for l in range(DEPTH):
        last = l == DEPTH - 1
        modv = mod_table(l)

        y_att = _attention(p, attn_sink[l], S, not last)

        pad_rows = jnp.zeros((LANES - 2 * GLA_GATE_RANK, GLA_QK_WIDTH), F32)
        zero_rank = jnp.zeros((GLA_GATE_RANK, GLA_QK_WIDTH), F32)
        wgf = jnp.concatenate([gla_gate_w[l, 0], zero_rank, pad_rows], axis=0)
        wgb = jnp.concatenate([zero_rank, gla_gate_w[l, 1], pad_rows], axis=0)
        wg = jnp.concatenate([wgf, wgb], axis=1).astype(BF16)
        gbias = jnp.concatenate([gla_gate_b[l].reshape(1, 2 * GLA_QK_WIDTH),
                                 jnp.zeros((7, 2 * GLA_QK_WIDTH), F32)], axis=0)
        ng = jnp.tile(gla_norm_g[l], GLA_HEADS).reshape(1, GLA_WIDTH)
        cw = jnp.concatenate([conv_w[l], jnp.zeros((5, CONV_WIDTH), F32)], axis=0)
        y_gc = _gla_conv(p, wg, gbias, ng, cw, bd_gla, cum, S)

        rows, tm = (S, LAT_TILE) if last else (T, TOKEN_TILE)
        wr = jnp.concatenate([w_router, jnp.zeros((D, LANES - N_EXPERTS), F32)], axis=1).astype(BF16)
        br = jnp.zeros((8, LANES), F32).at[0, :N_EXPERTS].set(b_router)
        ffn_g = norm_ffn_g[l].reshape(1, D)
        wo_att = w_out[l, :ATT_WIDTH].reshape(ATT_HEADS, HEAD_DIM, D)[order].reshape(ATT_WIDTH, D)
        wo = jnp.concatenate([wo_att, w_out[l, ATT_WIDTH:]], axis=0).astype(BF16)
        xx_mid, route, counts = _out_proj(y_att, y_gc, stream, stream_ctx, modv, wo, ffn_g, wr, br, tri, S, rows, tm)

        n_tiles = -(-(B * rows) // MOE_TILE) + N_CLASSES
        off, pad, tile, e_lo, e_hi, valid = _routing_tables(counts[:N_CLASSES, 0], n_tiles)

        dest = route[:, 1, :]
        for cls_id in range(N_CLASSES):
            dest = dest + jnp.where(route[:, 0, :] == cls_id, off[cls_id], 0)
        hs = _dispatch(pad, xx_mid, modv, ffn_g, dest, n_tiles * MOE_TILE, S, tm)
        ys = _moe(hs, tile, e_lo, e_hi, valid, wg_all, wu_all, wd_all, w_router.T, b_router, l * N_EXPERTS)
        stream, stream_ctx = _combine(xx_mid, modv, dest, ys, S, tm), None
        if not last:
            p = _in_proj(stream, None, *in_proj_args(l + 1), S)
    return stream
```

```python
import functools

import numpy as np
import jax
import jax.numpy as jnp
from jax import lax
from jax.experimental import pallas as pl
from jax.experimental.pallas import tpu as pltpu

D_MODEL = 1024
DEPTH = 2
GRID_W = 64
EPS = 1e-6
HEAD_DIM = 64
ATT_HEADS = 8
ATT_KV_HEADS = 2
ATT_GROUP = ATT_HEADS // ATT_KV_HEADS
ATT_WIDTH = ATT_HEADS * HEAD_DIM
WINDOW = 128
ROPE_BASE = 10000.0
GLA_HEADS = 4
GLA_DV = 64
GLA_DK = 32
GLA_WIDTH = GLA_HEADS * GLA_DV
GLA_GATE_RANK = 16
GLA_GATE_NORM = 16.0
GLA_CHUNK = 64
CONV_WIDTH = 256
N_EXPERTS = 16
N_GROUPS = 4
EXPERTS_PER_GROUP = 4
D_EXPERT = D_MODEL // 2

LANES = 128
KV_WIDTH = ATT_KV_HEADS * HEAD_DIM
GLA_QK_WIDTH = GLA_HEADS * GLA_DK
COL_AQ, COL_AK, COL_AV = 0, 512, 640
COL_GQ, COL_GK, COL_GV, COL_GR = 768, 896, 1024, 1280
COL_CB, COL_CC, COL_CH, COL_GT = 1536, 1792, 2048, 2304
N_PROJ = 2432
QK_COLS = COL_AV
N_PAIRS = 6
N_CLASSES = N_GROUPS * N_PAIRS
PAIR_LO = (0, 0, 0, 1, 1, 2)
PAIR_HI = (1, 2, 3, 2, 3, 3)
CLS_ROWS = 32
NEG = -1e30

TOKEN_TILE = 768
LAT_TILE = 1024
MOE_TILE = 512
SUB_ROWS = 256
VMEM_LIMIT = 56 * 1024 * 1024

F32 = jnp.float32
BF16 = jnp.bfloat16


def _cparams(n_axes):
    return pltpu.CompilerParams(dimension_semantics=("arbitrary",) * n_axes, vmem_limit_bytes=VMEM_LIMIT)


def _silu(x):
    return x * jax.nn.sigmoid(x)


def _mod_body(c_ref, w_ref, b_ref, o_ref):
    c = c_ref[...]
    a = _silu(c).astype(BF16)
    o_ref[0] = jnp.dot(a, w_ref[0].astype(BF16), preferred_element_type=F32) + b_ref[0]


def _modulation(cond, w_ada, b_ada):
    rows = cond.shape[0]
    nblk = w_ada.shape[2] // D_MODEL
    return pl.pallas_call(
        _mod_body,
        grid=(DEPTH, nblk),
        in_specs=[pl.BlockSpec((rows, D_MODEL), lambda l, n: (0, 0)),
                  pl.BlockSpec((1, D_MODEL, D_MODEL), lambda l, n: (l, 0, n)),
                  pl.BlockSpec((1, 1, D_MODEL), lambda l, n: (l, 0, n))],
        out_specs=pl.BlockSpec((1, rows, D_MODEL), lambda l, n: (l, 0, n)),
        out_shape=jax.ShapeDtypeStruct((DEPTH, rows, w_ada.shape[2]), F32),
        compiler_params=_cparams(2),
        name="modulation",
    )(cond, w_ada, b_ada.reshape(DEPTH, 1, -1))


def _row_mod(mod, is_ctx, i):
    return jnp.where(is_ctx, mod[6 + i:7 + i], mod[i:i + 1])


def _norm_modulate(x, gain, mod, is_ctx, i_shift, i_scale):
    ms = jnp.mean(x * x, axis=-1, keepdims=True)
    xn = x * lax.rsqrt(ms + EPS) * gain
    return xn * (1.0 + _row_mod(mod, is_ctx, i_scale)) + _row_mod(mod, is_ctx, i_shift)


def _stream_specs(stream, ctx, tm, b_of, j_of):
    n_sub = tm // SUB_ROWS
    last_piece = stream.shape[1] // SUB_ROWS - 1
    specs = [pl.BlockSpec((1, SUB_ROWS, D_MODEL),
                          lambda *g, k=k: (b_of(g), jnp.minimum(j_of(g) * n_sub + k, last_piece), 0))
             for k in range(n_sub)]
    args = [stream] * n_sub
    if ctx is not None:
        assert ctx.shape[1] == SUB_ROWS and (stream.shape[1] + SUB_ROWS) % tm == 0
        specs.append(pl.BlockSpec((1, SUB_ROWS, D_MODEL), lambda *g: (b_of(g), 0, 0)))
        args.append(ctx)
    return specs, args


def _stream_piece(x_refs, k, n_sub, is_last_tile):
    x = x_refs[k][0]
    if len(x_refs) > n_sub and k == n_sub - 1:
        x = jnp.where(is_last_tile, x_refs[n_sub][0], x)
    return x


def _in_tile(piece, j, mod_ref, g_ref, w_ref, cos_ref, sin_ref, qkg_ref, bd_ref, o_ref, n_lat, tm):
    lane = lax.broadcasted_iota(jnp.int32, (1, LANES), 1)
    first_half = (lane % HEAD_DIM) < (HEAD_DIM // 2)
    for k, r0 in enumerate(range(0, tm, SUB_ROWS)):
        rows = slice(r0, r0 + SUB_ROWS)
        row = j * tm + r0 + lax.broadcasted_iota(jnp.int32, (SUB_ROWS, 1), 0)
        is_ctx = row >= n_lat
        x = piece(k, is_ctx)
        h = _norm_modulate(x, g_ref[...], mod_ref[0], is_ctx, 0, 1).astype(BF16)
        qk = jnp.dot(h, w_ref[:, :QK_COLS], preferred_element_type=F32)
        cos = cos_ref[rows, :]
        sin = sin_ref[rows, :]
        for c in range(QK_COLS // LANES):
            xc = qk[:, c * LANES:(c + 1) * LANES]
            ss = jnp.dot((xc * xc).astype(BF16), bd_ref[...], preferred_element_type=F32) * (1.0 / HEAD_DIM)
            xc = xc * lax.rsqrt(ss + EPS) * qkg_ref[c:c + 1, :]
            rot = jnp.where(first_half, pltpu.roll(xc, LANES - HEAD_DIM // 2, 1), pltpu.roll(xc, HEAD_DIM // 2, 1))
            o_ref[0, rows, c * LANES:(c + 1) * LANES] = (xc * cos + rot * sin).astype(BF16)
        o_ref[0, rows, QK_COLS:] = jnp.dot(h, w_ref[:, QK_COLS:], preferred_element_type=F32).astype(BF16)


def _in_body(*refs, n_lat, tm, n_x):
    x_refs = refs[:n_x]
    j = pl.program_id(0)
    is_last_tile = j == pl.num_programs(0) - 1
    _in_tile(lambda k, is_ctx: _stream_piece(x_refs, k, tm // SUB_ROWS, is_last_tile), j, *refs[n_x:], n_lat, tm)


def _in_proj(stream, ctx, modv, gain, w, cos, sin, qkg, bd, n_lat):
    B = stream.shape[0]
    T = stream.shape[1] + (0 if ctx is None else ctx.shape[1])
    tm = TOKEN_TILE
    x_specs, x_args = _stream_specs(stream, ctx, tm, lambda g: g[1], lambda g: g[0])
    return pl.pallas_call(
        functools.partial(_in_body, n_lat=n_lat, tm=tm, n_x=len(x_args)),
        grid=(T // tm, B),
        in_specs=x_specs + [
                  pl.BlockSpec((1, 16, D_MODEL), lambda j, b: (b, 0, 0)),
                  pl.BlockSpec((1, D_MODEL), lambda j, b: (0, 0)),
                  pl.BlockSpec((D_MODEL, N_PROJ), lambda j, b: (0, 0)),
                  pl.BlockSpec((tm, LANES), lambda j, b: (j, 0)),
                  pl.BlockSpec((tm, LANES), lambda j, b: (j, 0)),
                  pl.BlockSpec((8, LANES), lambda j, b: (0, 0)),
                  pl.BlockSpec((LANES, LANES), lambda j, b: (0, 0))],
        out_specs=pl.BlockSpec((1, tm, N_PROJ), lambda j, b: (b, j, 0)),
        out_shape=jax.ShapeDtypeStruct((B, T, N_PROJ), BF16),
        compiler_params=_cparams(2),
        name="in_proj",
    )(*x_args, modv, gain, w, cos, sin, qkg, bd)


ATT_BLOCK = 128
ATT_SPAN = ATT_BLOCK + 2 * WINDOW


ATT_HEAD_ORDER = (0, 4, 1, 5, 2, 6, 3, 7)


def _attend(qblk, k_parts, v_parts, biases, sinks):
    rows = ATT_GROUP * ATT_BLOCK
    rowi = lax.broadcasted_iota(jnp.int32, (rows, 1), 0)
    lane = lax.broadcasted_iota(jnp.int32, (1, LANES), 1)
    lower = lane < HEAD_DIM
    nt = (((1,), (1,)), ((), ()))
    heads = range(ATT_KV_HEADS)
    keep = [lower, jnp.logical_not(lower)]
    sink, scores = [], []
    for h in heads:
        qs = jnp.concatenate([jnp.where(keep[h], qblk[:, g * LANES:(g + 1) * LANES], jnp.zeros((), BF16))
                              for g in range(ATT_GROUP)], axis=0)
        col = jnp.full((rows, 1), sinks[ATT_GROUP * h + ATT_GROUP - 1], F32)
        for g in range(ATT_GROUP - 2, -1, -1):
            col = jnp.where(rowi < (g + 1) * ATT_BLOCK, sinks[ATT_GROUP * h + g], col)
        sink.append(col)
        pieces = []
        for k in k_parts:
            s = lax.dot_general(qs, k, nt, preferred_element_type=F32)
            pieces += [s[:, c * LANES:(c + 1) * LANES] for c in range(k.shape[0] // LANES)]
        scores.append([s if b is None else s + b for s, b in zip(pieces, biases)])
    top, probs = [], []
    for h in heads:
        m = scores[h][0]
        for s in scores[h][1:]:
            m = jnp.maximum(m, s)
        m = jnp.maximum(jnp.max(m, axis=-1, keepdims=True), sink[h])
        top.append(m)
        probs.append([jnp.exp((s - m).astype(BF16)) for s in scores[h]])
    normed = []
    for h in heads:
        acc = jnp.where(keep[h], 0.0, jnp.exp(sink[h] - top[h]))
        c0 = 0
        for v in v_parts[h]:
            n = v.shape[0] // LANES
            acc = acc + jnp.dot(jnp.concatenate(probs[h][c0:c0 + n], axis=1), v, preferred_element_type=F32)
            c0 += n
        normed.append(acc / pltpu.roll(acc, HEAD_DIM, 1))
    return jnp.concatenate([jnp.where(lower, normed[0][g * ATT_BLOCK:(g + 1) * ATT_BLOCK],
                                      normed[1][g * ATT_BLOCK:(g + 1) * ATT_BLOCK]) for g in range(ATT_GROUP)], axis=1)


def _att_body(sink_ref, q_ref, k_ref, v_ref, o_ref, v1_s, *, n_lat, n_ctx, with_ctx_out):
    sinks = [sink_ref[i] for i in range(ATT_HEADS)]
    lane = lax.broadcasted_iota(jnp.int32, (1, LANES), 1)
    vv = v_ref[0]
    v1_s[0] = jnp.where(lane < HEAD_DIM, vv, jnp.ones((), BF16))
    v1_s[1] = jnp.where(lane < HEAD_DIM, jnp.ones((), BF16), vv)
    k_ctx = k_ref[0, n_lat:n_lat + n_ctx, :]
    v_ctx = [v1_s[h, n_lat:n_lat + n_ctx, :] for h in range(ATT_KV_HEADS)]
    no_bias = [None] * (n_ctx // LANES)
    qi = lax.broadcasted_iota(jnp.int32, (ATT_GROUP * ATT_BLOCK, LANES), 0) % ATT_BLOCK
    ki = lax.broadcasted_iota(jnp.int32, (ATT_GROUP * ATT_BLOCK, LANES), 1)
    past_ok = jnp.where(ki >= qi, 0.0, NEG)
    ahead_ok = jnp.where(ki <= qi, 0.0, NEG)

    def block(q0, k0, n_keys, biases):
        k_parts = [k_ref[0, pl.ds(k0, n_keys), :], k_ctx]
        v_parts = [[v1_s[h, pl.ds(k0, n_keys), :], v_ctx[h]] for h in range(ATT_KV_HEADS)]
        out = _attend(q_ref[0, pl.ds(q0, ATT_BLOCK), :], k_parts, v_parts, biases + no_bias, sinks)
        o_ref[0, pl.ds(q0, ATT_BLOCK), :] = out.astype(BF16)

    def interior(i, carry):
        q0 = pl.multiple_of(i * ATT_BLOCK, ATT_BLOCK)
        block(q0, pl.multiple_of(q0 - WINDOW, ATT_BLOCK), ATT_SPAN, [past_ok, None, ahead_ok])
        return carry

    nq = n_lat // ATT_BLOCK
    block(0, 0, 2 * ATT_BLOCK, [None, ahead_ok])
    lax.fori_loop(1, nq - 1, interior, 0, unroll=7)
    block(n_lat - ATT_BLOCK, n_lat - 2 * ATT_BLOCK, 2 * ATT_BLOCK, [past_ok, None])
    if with_ctx_out:
        for c in range(n_ctx // ATT_BLOCK):
            r0 = n_lat + c * ATT_BLOCK
            out = _attend(q_ref[0, r0:r0 + ATT_BLOCK, :], [k_ctx], [[v_ctx[h]] for h in range(ATT_KV_HEADS)],
                          no_bias, sinks)
            o_ref[0, r0:r0 + ATT_BLOCK, :] = out.astype(BF16)
    else:
        o_ref[0, n_lat:, :] = jnp.zeros((n_ctx, ATT_WIDTH), BF16)


def _attention(p, sinks, n_lat, with_ctx_out):
    B, T, _ = p.shape
    return pl.pallas_call(
        functools.partial(_att_body, n_lat=n_lat, n_ctx=T - n_lat, with_ctx_out=with_ctx_out),
        grid=(B,),
        in_specs=[pl.BlockSpec(memory_space=pltpu.SMEM),
                  pl.BlockSpec((1, T, ATT_WIDTH), lambda b: (b, 0, COL_AQ // ATT_WIDTH)),
                  pl.BlockSpec((1, T, KV_WIDTH), lambda b: (b, 0, COL_AK // KV_WIDTH)),
                  pl.BlockSpec((1, T, KV_WIDTH), lambda b: (b, 0, COL_AV // KV_WIDTH))],
        out_specs=pl.BlockSpec((1, T, ATT_WIDTH), lambda b: (b, 0, 0)),
        out_shape=jax.ShapeDtypeStruct((B, T, ATT_WIDTH), BF16),
        scratch_shapes=[pltpu.VMEM((ATT_KV_HEADS, T, KV_WIDTH), BF16)],
        compiler_params=_cparams(1),
        name="attention",
    )(sinks, p, p, p)


GLA_BLOCK_A = 256
GLA_BLOCK_B = 128
CONV_PAD = 8
EPI_ROWS = 256


def _log_sigmoid(z):
    return jnp.minimum(z, 0.0) - jnp.log1p(jnp.exp(-jnp.abs(z)))


def _gla_body(gq_ref, gk_ref, gv_ref, gr_ref, cb_ref, cc_ref, ch_ref, gt_ref, wg_ref, gbias_ref, ng_ref,
              cw_ref, bd_ref, cum_ref, o_ref, qe_s, ke_s, dec_s, upd_s, prev_s, st_s, o_s, u_s, *, n_lat, n_ctx):
    T = n_lat + n_ctx
    C = GLA_CHUNK
    nt = (((1,), (1,)), ((), ()))
    tn = (((0,), (0,)), ((), ()))

    sr = lax.broadcasted_iota(jnp.int32, (GLA_WIDTH, 2 * GLA_QK_WIDTH), 0) // GLA_DV
    sl = (lax.broadcasted_iota(jnp.int32, (GLA_WIDTH, 2 * GLA_QK_WIDTH), 1) % GLA_QK_WIDTH) // GLA_DK
    state_mask2 = sr == sl

    def factors(i, carry):
        r0 = pl.multiple_of(i * GLA_BLOCK_A, GLA_BLOCK_A)
        rows = pl.ds(r0, GLA_BLOCK_A)
        gt = gt_ref[0, rows, :]
        q = gq_ref[0, rows, :].astype(F32) * (GLA_DK ** -0.5)
        k = gk_ref[0, rows, :].astype(F32)
        v = gv_ref[0, rows, :]
        z2 = jnp.dot(gt, wg_ref[...], preferred_element_type=F32) + gbias_ref[0:1, :]
        g2 = _log_sigmoid(z2) * (1.0 / GLA_GATE_NORM)
        cum = []
        for d in range(2):
            g = g2[:, d * GLA_QK_WIDTH:(d + 1) * GLA_QK_WIDTH]
            g_hi = g.astype(BF16)
            g_lo = (g - g_hi.astype(F32)).astype(BF16)
            cum.append(jnp.dot(cum_ref[d], jnp.concatenate([g_hi, g_lo], axis=1), preferred_element_type=F32))
        kl = []
        for d in range(2):
            b = cum[d][:GLA_BLOCK_A, :GLA_QK_WIDTH] + cum[d][:GLA_BLOCK_A, GLA_QK_WIDTH:]
            tot = cum[d][GLA_BLOCK_A:, :GLA_QK_WIDTH] + cum[d][GLA_BLOCK_A:, GLA_QK_WIDTH:]
            dec = jnp.exp(tot)
            ke = k * jnp.exp(-b)
            qe_s[d, rows, :] = (q * jnp.exp(b)).astype(BF16)
            ke_s[d, rows, :] = ke.astype(BF16)
            dec_s[d, rows, :] = dec
            kl.append((ke * dec).astype(BF16))
        kl2 = jnp.concatenate(kl, axis=1)
        for cc in range(GLA_BLOCK_A // C):
            upd = lax.dot_general(v[cc * C:(cc + 1) * C], kl2[cc * C:(cc + 1) * C], tn, preferred_element_type=F32)
            upd_s[i * (GLA_BLOCK_A // C) + cc] = jnp.where(state_mask2, upd, 0.0)
        return carry

    lax.fori_loop(0, T // GLA_BLOCK_A, factors, 0, unroll=True)

    nc_lat = n_lat // C
    nc_ctx = n_ctx // C
    st_s[...] = jnp.zeros_like(st_s)

    def scan(i, carry):
        in_ctx = i < nc_ctx
        cf = jnp.where(in_ctx, nc_lat + i, i - nc_ctx)
        cb = jnp.where(in_ctx, nc_lat + nc_ctx - 1 - i, nc_lat - 1 - (i - nc_ctx))
        for d, cid in enumerate((cf, cb)):
            st = st_s[d]
            prev_s[cid, :, d * GLA_QK_WIDTH:(d + 1) * GLA_QK_WIDTH] = st.astype(BF16)
            st_s[d] = (st * dec_s[d, pl.ds(pl.multiple_of(cid * C, C), 1), :]
                       + upd_s[cid, :, d * GLA_QK_WIDTH:(d + 1) * GLA_QK_WIDTH])
        return carry

    lax.fori_loop(0, nc_lat + nc_ctx, scan, 0)

    RB = GLA_BLOCK_B
    k_rows = lax.broadcasted_iota(jnp.int32, (GLA_HEADS * RB, GLA_QK_WIDTH), 0) // RB
    k_lanes = lax.broadcasted_iota(jnp.int32, (GLA_HEADS * RB, GLA_QK_WIDTH), 1) // GLA_DK
    key_heads = k_rows == k_lanes
    v_rows = lax.broadcasted_iota(jnp.int32, (GLA_HEADS * RB, GLA_WIDTH), 0) // RB
    v_lanes = lax.broadcasted_iota(jnp.int32, (GLA_HEADS * RB, GLA_WIDTH), 1) // GLA_DV
    value_heads = v_rows == v_lanes
    qr = lax.broadcasted_iota(jnp.int32, (RB, GLA_HEADS * RB), 0)
    kc = lax.broadcasted_iota(jnp.int32, (RB, GLA_HEADS * RB), 1) % RB
    same_chunk = (qr // C) == (kc // C)
    forward = kc <= qr
    zero = jnp.zeros((), BF16)

    def outputs(i, carry):
        r0 = pl.multiple_of(i * RB, RB)
        rows = pl.ds(r0, RB)
        v = gv_ref[0, rows, :]
        v4 = jnp.where(value_heads, jnp.concatenate([v] * GLA_HEADS, axis=0), zero)
        qe = [qe_s[d, rows, :] for d in range(2)]
        att = []
        for d in range(2):
            ke4 = jnp.where(key_heads, jnp.concatenate([ke_s[d, rows, :]] * GLA_HEADS, axis=0), zero)
            att.append(lax.dot_general(qe[d], ke4, nt, preferred_element_type=F32))
        both = jnp.where(same_chunk, jnp.where(forward, att[0], att[1]), 0.0).astype(BF16)
        qe2 = jnp.concatenate(qe, axis=1)
        inter = [lax.dot_general(qe2[cc * C:(cc + 1) * C], prev_s[i * (RB // C) + cc], nt,
                                 preferred_element_type=F32) for cc in range(RB // C)]
        o_s[rows, :] = jnp.dot(both, v4, preferred_element_type=F32) + jnp.concatenate(inter, axis=0)
        return carry

    lax.fori_loop(0, T // RB, outputs, 0, unroll=6)

    u_s[0:CONV_PAD, :] = jnp.zeros((CONV_PAD, CONV_WIDTH), F32)
    u_s[CONV_PAD + T:, :] = jnp.zeros((CONV_PAD, CONV_WIDTH), F32)
    u_s[CONV_PAD:CONV_PAD + T, :] = cc_ref[0].astype(F32) * ch_ref[0].astype(F32)
    w0 = cw_ref[0:1, :]
    w1 = cw_ref[1:2, :]
    w2 = cw_ref[2:3, :]
    for e in range(T // EPI_ROWS):
        r0 = e * EPI_ROWS
        o = o_s[r0:r0 + EPI_ROWS, :]
        ss = jnp.dot((o * o).astype(BF16), bd_ref[...], preferred_element_type=F32) * (1.0 / GLA_DV)
        on = o * lax.rsqrt(ss + EPS) * ng_ref[...]
        r = gr_ref[0, r0:r0 + EPI_ROWS, :].astype(F32)
        o_ref[0, r0:r0 + EPI_ROWS, 0:GLA_WIDTH] = (on * _silu(r)).astype(BF16)
        t = r0 + lax.broadcasted_iota(jnp.int32, (EPI_ROWS, 1), 0)
        up = u_s[CONV_PAD + r0 - 1:CONV_PAD + r0 - 1 + EPI_ROWS, :]
        mid = u_s[CONV_PAD + r0:CONV_PAD + r0 + EPI_ROWS, :]
        dn = u_s[CONV_PAD + r0 + 1:CONV_PAD + r0 + 1 + EPI_ROWS, :]
        up = jnp.where(t == n_lat, 0.0, up)
        dn = jnp.where(t == n_lat - 1, 0.0, dn)
        conv = w0 * up + w1 * mid + w2 * dn
        o_ref[0, r0:r0 + EPI_ROWS, GLA_WIDTH:] = (cb_ref[0, r0:r0 + EPI_ROWS, :].astype(F32) * conv).astype(BF16)


def _gla_cum_matrices():
    i = np.arange(GLA_BLOCK_A)
    same = (i[:, None] // GLA_CHUNK) == (i[None, :] // GLA_CHUNK)
    fwd = same & (i[None, :] <= i[:, None])
    bwd = same & (i[None, :] >= i[:, None])
    mats = np.stack([np.concatenate([fwd, same], axis=0), np.concatenate([bwd, same], axis=0)])
    return jnp.asarray(mats.astype(np.float32), dtype=BF16)


def _gla_conv(p, wg, gbias, ng, cw, bd, cum, n_lat):
    B, T, _ = p.shape
    nc = T // GLA_CHUNK

    def col(width, start):
        return pl.BlockSpec((1, T, width), lambda b: (b, 0, start // width))

    def const(shape):
        return pl.BlockSpec(shape, lambda b: (0,) * len(shape))

    return pl.pallas_call(
        functools.partial(_gla_body, n_lat=n_lat, n_ctx=T - n_lat),
        grid=(B,),
        in_specs=[col(GLA_QK_WIDTH, COL_GQ), col(GLA_QK_WIDTH, COL_GK), col(GLA_WIDTH, COL_GV), col(GLA_WIDTH, COL_GR),
                  col(CONV_WIDTH, COL_CB), col(CONV_WIDTH, COL_CC), col(CONV_WIDTH, COL_CH), col(LANES, COL_GT),
                  const((LANES, 2 * GLA_QK_WIDTH)), const((8, 2 * GLA_QK_WIDTH)),
                  const((1, GLA_WIDTH)), const((8, CONV_WIDTH)), const((GLA_WIDTH, GLA_WIDTH)),
                  const((2, 2 * GLA_BLOCK_A, GLA_BLOCK_A))],
        out_specs=pl.BlockSpec((1, T, GLA_WIDTH + CONV_WIDTH), lambda b: (b, 0, 0)),
        out_shape=jax.ShapeDtypeStruct((B, T, GLA_WIDTH + CONV_WIDTH), BF16),
        scratch_shapes=[pltpu.VMEM((2, T, GLA_QK_WIDTH), BF16), pltpu.VMEM((2, T, GLA_QK_WIDTH), BF16),
                        pltpu.VMEM((2, T, GLA_QK_WIDTH), F32),
                        pltpu.VMEM((nc, GLA_WIDTH, 2 * GLA_QK_WIDTH), F32),
                        pltpu.VMEM((nc, GLA_WIDTH, 2 * GLA_QK_WIDTH), BF16),
                        pltpu.VMEM((2, GLA_WIDTH, GLA_QK_WIDTH), F32),
                        pltpu.VMEM((T, GLA_WIDTH), F32),
                        pltpu.VMEM((T + 2 * CONV_PAD, CONV_WIDTH), F32)],
        compiler_params=_cparams(1),
        name="gla_conv",
    )(p, p, p, p, p, p, p, p, wg, gbias, ng, cw, bd, cum)


def _route(logits_t):
    mx = jnp.max(logits_t, axis=0, keepdims=True)
    ex = jnp.exp(logits_t - mx)
    probs = ex / jnp.sum(ex, axis=0, keepdims=True)
    P = [probs[e:e + 1] for e in range(N_EXPERTS)]
    scores = []
    for g in range(N_GROUPS):
        a, b, c, d = P[4 * g:4 * g + 4]
        scores.append(jnp.maximum(jnp.maximum(jnp.maximum(a + b, a + c), jnp.maximum(a + d, b + c)),
                                  jnp.maximum(b + d, c + d)))
    best = jnp.maximum(jnp.maximum(scores[0], scores[1]), jnp.maximum(scores[2], scores[3]))
    taken = jnp.zeros_like(best, dtype=jnp.bool_)
    sel = []
    for g in range(N_GROUPS):
        s = (scores[g] == best) & jnp.logical_not(taken)
        sel.append(s)
        taken = taken | s
    gsel = jnp.where(sel[1], 1.0, 0.0) + jnp.where(sel[2], 2.0, 0.0) + jnp.where(sel[3], 3.0, 0.0)
    ig = [jnp.where(sel[0], P[j], jnp.where(sel[1], P[4 + j], jnp.where(sel[2], P[8 + j], P[12 + j])))
          for j in range(EXPERTS_PER_GROUP)]

    def first_max(vals):
        v = jnp.maximum(jnp.maximum(vals[0], vals[1]), jnp.maximum(vals[2], vals[3]))
        tk = jnp.zeros_like(v, dtype=jnp.bool_)
        hot = []
        for x in vals:
            s = (x == v) & jnp.logical_not(tk)
            hot.append(s)
            tk = tk | s
        idx = jnp.where(hot[1], 1.0, 0.0) + jnp.where(hot[2], 2.0, 0.0) + jnp.where(hot[3], 3.0, 0.0)
        return v, hot, idx

    _, hot1, i1 = first_max(ig)
    _, _, i2 = first_max([jnp.where(hot1[j], -1.0, ig[j]) for j in range(EXPERTS_PER_GROUP)])
    lo = jnp.minimum(i1, i2)
    hi = jnp.maximum(i1, i2)
    pair = jnp.where(lo == 0.0, hi - 1.0, jnp.where(lo == 1.0, hi + 1.0, 5.0))
    return gsel * N_PAIRS + pair


def _class_rank(cls, tri_ref, cnt_s):
    n = cls.shape[1]
    cid = lax.broadcasted_iota(jnp.int32, (CLS_ROWS, n), 0).astype(F32)
    onehot = jnp.where(cls == cid, 1.0, 0.0)
    segs = [onehot[:, k * LANES:(k + 1) * LANES] for k in range(n // LANES)]
    before = jnp.dot(jnp.concatenate(segs, axis=0).astype(BF16), tri_ref[...], preferred_element_type=F32)
    base = cnt_s[...]
    ranks = []
    for k, seg in enumerate(segs):
        ranks.append(jnp.sum(seg * (before[k * CLS_ROWS:(k + 1) * CLS_ROWS] + base), axis=0, keepdims=True))
        base = base + jnp.sum(seg, axis=1, keepdims=True)
    cnt_s[...] = base
    return jnp.concatenate(ranks, axis=1)


def _out_body(*refs, n_lat, tm, n_x):
    x_refs = refs[:n_x]
    ya_ref, yg_ref, mod_ref, wo_ref, g_ref, wr_ref, br_ref, tri_ref, xo_ref, rt_ref, cnt_ref, cnt_s, logit_s = refs[n_x:]
    j = pl.program_id(1)
    is_last_tile = j == pl.num_programs(1) - 1

    @pl.when((pl.program_id(0) == 0) & (j == 0))
    def _():
        cnt_s[...] = jnp.zeros_like(cnt_s)

    mod = mod_ref[0]
    subs = list(enumerate(range(0, tm, SUB_ROWS)))
    ys = [jnp.dot(ya_ref[0, r0:r0 + SUB_ROWS, :], wo_ref[0:ATT_WIDTH, :], preferred_element_type=F32)
          + jnp.dot(yg_ref[0, r0:r0 + SUB_ROWS, :], wo_ref[ATT_WIDTH:, :], preferred_element_type=F32) for _, r0 in subs]
    h2s = []
    for k, r0 in subs:
        row = j * tm + r0 + lax.broadcasted_iota(jnp.int32, (SUB_ROWS, 1), 0)
        is_ctx = row >= n_lat
        xn = _stream_piece(x_refs, k, tm // SUB_ROWS, is_last_tile) + _row_mod(mod, is_ctx, 2) * ys[k]
        xo_ref[0, r0:r0 + SUB_ROWS, :] = xn
        h2s.append(_norm_modulate(xn, g_ref[...], mod, is_ctx, 3, 4).astype(BF16))
    for k, r0 in subs:
        logit_s[r0:r0 + SUB_ROWS, :] = jnp.dot(h2s[k], wr_ref[...], preferred_element_type=F32) + br_ref[0:1, :]
    cls = _route(logit_s[...].T[0:N_EXPERTS, :])
    rank = _class_rank(cls, tri_ref, cnt_s)
    rt_ref[0] = jnp.concatenate([cls, rank, jnp.zeros((6, tm), F32)], axis=0).astype(jnp.int32)
    cnt_ref[...] = jnp.broadcast_to(cnt_s[...], (CLS_ROWS, LANES)).astype(jnp.int32)


def _out_proj(ya, yg, stream, ctx, modv, wo, gain, wr, br, tri, n_lat, rows, tm):
    B = stream.shape[0]
    nj = rows // tm
    x_specs, x_args = _stream_specs(stream, ctx, tm, lambda g: g[0], lambda g: g[1])
    return pl.pallas_call(
        functools.partial(_out_body, n_lat=n_lat, tm=tm, n_x=len(x_args)),
        grid=(B, nj),
        in_specs=x_specs + [
                  pl.BlockSpec((1, tm, ATT_WIDTH), lambda b, j: (b, j, 0)),
                  pl.BlockSpec((1, tm, GLA_WIDTH + CONV_WIDTH), lambda b, j: (b, j, 0)),
                  pl.BlockSpec((1, 16, D_MODEL), lambda b, j: (b, 0, 0)),
                  pl.BlockSpec((D_MODEL, D_MODEL), lambda b, j: (0, 0)),
                  pl.BlockSpec((1, D_MODEL), lambda b, j: (0, 0)),
                  pl.BlockSpec((D_MODEL, LANES), lambda b, j: (0, 0)),
                  pl.BlockSpec((8, LANES), lambda b, j: (0, 0)),
                  pl.BlockSpec((LANES, LANES), lambda b, j: (0, 0))],
        out_specs=[pl.BlockSpec((1, tm, D_MODEL), lambda b, j: (b, j, 0)),
                   pl.BlockSpec((1, 8, tm), lambda b, j: (b * nj + j, 0, 0)),
                   pl.BlockSpec((CLS_ROWS, LANES), lambda b, j: (0, 0))],
        out_shape=[jax.ShapeDtypeStruct((B, rows, D_MODEL), F32),
                   jax.ShapeDtypeStruct((B * nj, 8, tm), jnp.int32),
                   jax.ShapeDtypeStruct((CLS_ROWS, LANES), jnp.int32)],
        scratch_shapes=[pltpu.VMEM((CLS_ROWS, 1), F32), pltpu.VMEM((tm, LANES), F32)],
        compiler_params=_cparams(2),
        name="out_proj_router",
    )(*x_args, ya, yg, modv, wo, gain, wr, br, tri)


ROW_UNROLL = 8
IDX_STRIDE = 1024


def _idx_slot(idx, slot, tm):
    return idx.at[pl.ds(pl.multiple_of(slot * IDX_STRIDE, IDX_STRIDE), tm)]


def _issue_rows(tm, idx, slot, make_copy):
    base = slot * IDX_STRIDE

    def trip(i, c):
        for u in range(ROW_UNROLL):
            r = i * ROW_UNROLL + u
            make_copy(r, idx[base + r]).start(priority=u % 2)
        return c

    lax.fori_loop(0, tm // ROW_UNROLL, trip, 0)


PAD_SLOTS = 32
PAD_BITS = 9


def _zero_pad_rows(pad_ref, zeros_ref, hs_ref, sem):
    def pieces(c, fn):
        start = pad_ref[c]
        n = pad_ref[PAD_SLOTS + c]
        for bit in range(PAD_BITS):
            size = 1 << bit
            below = n & (size - 1)

            @pl.when((n & size) != 0)
            def _():
                fn(pltpu.make_async_copy(zeros_ref.at[pl.ds(0, size)], hs_ref.at[pl.ds(start + below, size)], sem))

        def block(i, carry):
            fn(pltpu.make_async_copy(zeros_ref, hs_ref.at[pl.ds(start + n + i * MOE_TILE, MOE_TILE)], sem))
            return carry

        lax.fori_loop(0, pad_ref[2 * PAD_SLOTS + c], block, 0)

    def start_all(c, carry):
        pieces(c, lambda cp: cp.start())
        return carry

    def wait_all(c, carry):
        pieces(c, lambda cp: cp.wait())
        return carry

    lax.fori_loop(0, N_CLASSES + 1, start_all, 0)
    lax.fori_loop(0, N_CLASSES + 1, wait_all, 0)


def _issue_rows_inline(tm, idx, slot, make_copy):
    base = slot * IDX_STRIDE
    for r in range(tm):
        make_copy(r, idx[base + r]).start(priority=r % 2)


def _disp_body(pad_ref, x_ref, mod_ref, g_ref, dest_ref, hs_ref, buf0, buf1, buf2, idx, sem_i, sem_d, sem_z, *, n_lat,
               tm, nj, n_steps):
    bufs = (buf0, buf1, buf2)
    j = pl.program_id(1)
    s = pl.program_id(0) * nj + j
    last = n_steps - 1

    def idx_fetch(step):
        return pltpu.make_async_copy(dest_ref.at[step], _idx_slot(idx, step % 3, tm), sem_i.at[step % 3])

    def drain(k):
        pltpu.make_async_copy(bufs[k], hs_ref.at[pl.ds(0, tm)], sem_d.at[k]).wait()

    def row_copy(k):
        return lambda r, d: pltpu.make_async_copy(bufs[k].at[r], hs_ref.at[d], sem_d.at[k])

    def normalise(k):
        row = j * tm + lax.broadcasted_iota(jnp.int32, (tm, 1), 0)
        h2 = _norm_modulate(x_ref[0], g_ref[...], mod_ref[0], row >= n_lat, 3, 4)
        bufs[k][...] = h2.reshape(tm, 8, LANES)

    @pl.when(s == 0)
    def _():
        idx_fetch(0).start()
        buf1[...] = jnp.zeros((tm, 8, LANES), F32)
        _zero_pad_rows(pad_ref, buf1.at[pl.ds(0, MOE_TILE)], hs_ref, sem_z)
        normalise(0)

    @pl.when(s < last)
    def _():
        idx_fetch(s + 1).start()

    idx_fetch(s).wait()

    for k in range(3):
        prev = (k + 2) % 3

        @pl.when((s > 0) & (s % 3 == k))
        def _():
            _issue_rows_inline(tm, idx, prev, row_copy(prev))
            normalise(k)

        @pl.when((s > 1) & (s % 3 == k))
        def _():
            drain((k + 1) % 3)

        @pl.when((s == last) & (s % 3 == k))
        def _():
            _issue_rows(tm, idx, k, row_copy(k))
            drain(k)
            if last > 0:
                drain(prev)


def _dispatch(pad, xx, modv, gain, dest, n_rows, n_lat, tm):
    B, rows, _ = xx.shape
    nj = rows // tm
    assert tm >= MOE_TILE
    return pl.pallas_call(
        functools.partial(_disp_body, n_lat=n_lat, tm=tm, nj=nj, n_steps=B * nj),
        grid=(B, nj),
        in_specs=[pl.BlockSpec(memory_space=pltpu.SMEM),
                  pl.BlockSpec((1, tm, D_MODEL), lambda b, j: (b, j, 0)),
                  pl.BlockSpec((1, 16, D_MODEL), lambda b, j: (b, 0, 0)),
                  pl.BlockSpec((1, D_MODEL), lambda b, j: (0, 0)),
                  pl.BlockSpec(memory_space=pl.ANY)],
        out_specs=pl.BlockSpec(memory_space=pl.ANY),
        scratch_shapes=[pltpu.VMEM((tm, 8, LANES), F32), pltpu.VMEM((tm, 8, LANES), F32),
                        pltpu.VMEM((tm, 8, LANES), F32), pltpu.SMEM((3 * IDX_STRIDE,), jnp.int32),
                        pltpu.SemaphoreType.DMA((3,)), pltpu.SemaphoreType.DMA((3,)), pltpu.SemaphoreType.DMA],
        out_shape=jax.ShapeDtypeStruct((n_rows, 8, LANES), F32),
        compiler_params=_cparams(2),
        name="dispatch",
    )(pad, xx, modv, gain, dest)


def _moe_body(tile_ref, e_lo_ref, e_hi_ref, valid_ref, hs_ref, wg1, wu1, wd1, wg2, wu2, wd2, wr_ref, br_ref, ys_ref):
    g = pl.program_id(0)
    tm = hs_ref.shape[0]

    @pl.when(valid_ref[g] == 1)
    def _():
        x = hs_ref[...].reshape(tm, D_MODEL)
        e_lo = e_lo_ref[g]
        e_hi = e_hi_ref[g]
        dw = wr_ref[pl.ds(e_lo, 1), :] - wr_ref[pl.ds(e_hi, 1), :]
        d = jnp.sum(x * dw, axis=-1, keepdims=True) + (br_ref[e_lo] - br_ref[e_hi])
        w_lo = jax.nn.sigmoid(d)
        w_hi = jax.nn.sigmoid(-d)
        h = x.astype(BF16)

        def act(wg, wu, w):
            a = _silu(jnp.dot(h, wg[0], preferred_element_type=F32)) * jnp.dot(h, wu[0], preferred_element_type=F32)
            return (a * w).astype(BF16)

        y = (jnp.dot(act(wg1, wu1, w_lo), wd1[0], preferred_element_type=F32)
             + jnp.dot(act(wg2, wu2, w_hi), wd2[0], preferred_element_type=F32))
        ys_ref[...] = y.reshape(tm, 8, LANES)

    @pl.when(valid_ref[g] == 0)
    def _():
        ys_ref[...] = jnp.zeros_like(ys_ref)


def _moe(hs, tile, e_lo, e_hi, valid, wg, wu, wd, wr_t, br, first_expert):
    n_tiles = tile.shape[0]
    tm = MOE_TILE

    def w_in(sel):
        return pl.BlockSpec((1, D_MODEL, D_EXPERT), lambda g, t, lo, hi, v: (first_expert + (lo, hi)[sel][g], 0, 0))

    def w_out(sel):
        return pl.BlockSpec((1, D_EXPERT, D_MODEL), lambda g, t, lo, hi, v: (first_expert + (lo, hi)[sel][g], 0, 0))

    return pl.pallas_call(
        _moe_body,
        grid_spec=pltpu.PrefetchScalarGridSpec(
            num_scalar_prefetch=4,
            grid=(n_tiles,),
            in_specs=[pl.BlockSpec((tm, 8, LANES), lambda g, t, lo, hi, v: (t[g], 0, 0)),
                      w_in(0), w_in(0), w_out(0), w_in(1), w_in(1), w_out(1),
                      pl.BlockSpec((N_EXPERTS, D_MODEL), lambda g, t, lo, hi, v: (0, 0)),
                      pl.BlockSpec(memory_space=pltpu.SMEM)],
            out_specs=pl.BlockSpec((tm, 8, LANES), lambda g, t, lo, hi, v: (g, 0, 0))),
        out_shape=jax.ShapeDtypeStruct((n_tiles * tm, 8, LANES), F32),
        compiler_params=_cparams(1),
        name="moe_pairs",
    )(tile, e_lo, e_hi, valid, hs, wg, wu, wd, wg, wu, wd, wr_t, br)


def _fin_body(x_ref, mod_ref, dest_ref, ys_ref, xo_ref, buf0, buf1, buf2, idx, sem_i, sem_d, *, n_lat, tm, nj,
              n_steps):
    bufs = (buf0, buf1, buf2)
    j = pl.program_id(1)
    s = pl.program_id(0) * nj + j
    last = n_steps - 1

    def idx_fetch(step):
        return pltpu.make_async_copy(dest_ref.at[step], _idx_slot(idx, step % 3, tm), sem_i.at[step % 3])

    def row_copy(k):
        return lambda r, d: pltpu.make_async_copy(ys_ref.at[d], bufs[k].at[r], sem_d.at[k])

    def add_rows(k):
        row = j * tm + lax.broadcasted_iota(jnp.int32, (tm, 1), 0)
        xo_ref[0] = x_ref[0] + _row_mod(mod_ref[0], row >= n_lat, 5) * bufs[k][...].reshape(tm, D_MODEL)

    @pl.when(s == 0)
    def _():
        for t in range(min(2, n_steps)):
            idx_fetch(t).start()
            idx_fetch(t).wait()
            _issue_rows(tm, idx, t, row_copy(t))
        if last >= 2:
            idx_fetch(2).start()

    for k in range(3):
        ahead = (k + 2) % 3

        @pl.when((s % 3 == k) & (s + 2 <= last))
        def _():
            pltpu.make_async_copy(ys_ref.at[pl.ds(0, tm)], bufs[k], sem_d.at[k]).wait()
            idx_fetch(s + 2).wait()
            _issue_rows_inline(tm, idx, ahead, row_copy(ahead))
            add_rows(k)

        @pl.when((s % 3 == k) & (s + 2 > last))
        def _():
            pltpu.make_async_copy(ys_ref.at[pl.ds(0, tm)], bufs[k], sem_d.at[k]).wait()
            add_rows(k)

    @pl.when(s + 3 <= last)
    def _():
        idx_fetch(s + 3).start()


def _combine(xx, modv, dest, ys, n_lat, tm):
    B, rows, _ = xx.shape
    nj = rows // tm
    return pl.pallas_call(
        functools.partial(_fin_body, n_lat=n_lat, tm=tm, nj=nj, n_steps=B * nj),
        grid=(B, nj),
        in_specs=[pl.BlockSpec((1, tm, D_MODEL), lambda b, j: (b, j, 0)),
                  pl.BlockSpec((1, 16, D_MODEL), lambda b, j: (b, 0, 0)),
                  pl.BlockSpec(memory_space=pl.ANY),
                  pl.BlockSpec(memory_space=pl.ANY)],
        out_specs=pl.BlockSpec((1, tm, D_MODEL), lambda b, j: (b, j, 0)),
        scratch_shapes=[pltpu.VMEM((tm, 8, LANES), F32), pltpu.VMEM((tm, 8, LANES), F32),
                        pltpu.VMEM((tm, 8, LANES), F32), pltpu.SMEM((3 * IDX_STRIDE,), jnp.int32),
                        pltpu.SemaphoreType.DMA((3,)), pltpu.SemaphoreType.DMA((3,))],
        out_shape=jax.ShapeDtypeStruct((B, rows, D_MODEL), F32),
        compiler_params=_cparams(2),
        name="combine",
    )(xx, modv, dest, ys)


def _rope_tables(n_lat, n_ctx):
    rows = n_lat // GRID_W
    row, col = jnp.meshgrid(jnp.arange(rows), jnp.arange(GRID_W), indexing="ij")
    n_freq = HEAD_DIM // 4
    inv_freq = ROPE_BASE ** (-jnp.arange(n_freq, dtype=F32) / n_freq)
    ang = jnp.concatenate([row.reshape(-1, 1).astype(F32) * inv_freq, col.reshape(-1, 1).astype(F32) * inv_freq],
                          axis=-1)
    cos = jnp.tile(jnp.cos(ang), (1, LANES // (HEAD_DIM // 2)))
    sin = jnp.tile(jnp.sin(ang), (1, LANES // (HEAD_DIM // 2)))
    sign = jnp.where((jnp.arange(LANES) % HEAD_DIM) < HEAD_DIM // 2, -1.0, 1.0).astype(F32)
    cos = jnp.concatenate([cos, jnp.ones((n_ctx, LANES), F32)], axis=0)
    sin = jnp.concatenate([sin * sign, jnp.zeros((n_ctx, LANES), F32)], axis=0)
    return cos, sin


def _block_diag_ones(n, blk):
    i = np.arange(n) // blk
    return jnp.asarray((i[:, None] == i[None, :]).astype(np.float32), dtype=BF16)


def _routing_tables(counts, n_tiles):
    tiles_c = (counts + MOE_TILE - 1) // MOE_TILE
    tile_end = jnp.cumsum(tiles_c)
    tile_start = tile_end - tiles_c
    off = tile_start * MOE_TILE
    total = tile_end[-1]
    g = jnp.arange(n_tiles, dtype=jnp.int32)
    valid = (g < total).astype(jnp.int32)
    g_eff = jnp.minimum(g, total - 1)
    c_of = jnp.sum((g_eff[:, None] >= tile_end[None, :]).astype(jnp.int32), axis=1)
    group = c_of // N_PAIRS
    pair = c_of % N_PAIRS
    lo = jnp.asarray(PAIR_LO, jnp.int32)
    hi = jnp.asarray(PAIR_HI, jnp.int32)
    e_lo = group * EXPERTS_PER_GROUP + jnp.sum((pair[:, None] == jnp.arange(N_PAIRS)[None, :]) * lo[None, :], axis=1)
    e_hi = group * EXPERTS_PER_GROUP + jnp.sum((pair[:, None] == jnp.arange(N_PAIRS)[None, :]) * hi[None, :], axis=1)
    fill = jnp.zeros((PAD_SLOTS - N_CLASSES - 1,), jnp.int32)
    pad = jnp.concatenate([off + counts, (total * MOE_TILE)[None], fill,
                           tiles_c * MOE_TILE - counts, jnp.zeros((1,), jnp.int32), fill,
                           jnp.zeros((N_CLASSES,), jnp.int32), (n_tiles - total)[None], fill])
    return off.astype(jnp.int32), pad.astype(jnp.int32), g_eff, e_lo.astype(jnp.int32), e_hi.astype(jnp.int32), valid


def kernel(x, c, ctx, c_ctx, w_ada, b_ada, norm_mix_g, norm_ffn_g, w_in, q_norm_g, k_norm_g, attn_sink, gla_gate_w,
           gla_gate_b, gla_norm_g, conv_w, w_out, w_router, b_router, w_gate_e, w_up_e, w_down_e):
    B, S, D = x.shape
    L = ctx.shape[1]
    T = S + L
    assert D == D_MODEL and T % TOKEN_TILE == 0 and S % LAT_TILE == 0 and S % GRID_W == 0
    assert S % ATT_BLOCK == 0 and L % ATT_BLOCK == 0 and S >= ATT_SPAN and T % EPI_ROWS == 0

    cond_rows = -(-(B + 1) // 8) * 8
    cond = jnp.zeros((cond_rows, D), F32).at[:B].set(c).at[B].set(c_ctx)
    mod_all = _modulation(cond, w_ada, b_ada)

    cos, sin = _rope_tables(S, L)
    bd_head = _block_diag_ones(LANES, HEAD_DIM)
    bd_gla = _block_diag_ones(GLA_WIDTH, GLA_DV)
    cum = _gla_cum_matrices()
    tri = jnp.asarray(np.triu(np.ones((LANES, LANES), np.float32), 1), dtype=BF16)
    stream, stream_ctx = x, ctx

    order = jnp.asarray(ATT_HEAD_ORDER)
    wg_all = w_gate_e.reshape(DEPTH * N_EXPERTS, D, D_EXPERT).astype(BF16)
    wu_all = w_up_e.reshape(DEPTH * N_EXPERTS, D, D_EXPERT).astype(BF16)
    wd_all = w_down_e.reshape(DEPTH * N_EXPERTS, D_EXPERT, D).astype(BF16)

    def mod_table(l):
        m_lat = mod_all[l, :B].reshape(B, 6, D)
        m_ctx = jnp.broadcast_to(mod_all[l, B].reshape(1, 6, D), (B, 6, D))
        return jnp.concatenate([m_lat, m_ctx, jnp.zeros((B, 4, D), F32)], axis=1)

    def in_proj_args(l):
        wl = w_in[l]
        wq = wl[:, :ATT_WIDTH].reshape(D, ATT_HEADS, HEAD_DIM)[:, order, :].reshape(D, ATT_WIDTH)
        w_perm = jnp.concatenate([wq, wl[:, ATT_WIDTH:1536], wl[:, 1568:], wl[:, 1536:1568],
                                  jnp.zeros((D, N_PROJ - wl.shape[1]), F32)], axis=1).astype(BF16)
        qg = jnp.tile(q_norm_g[l], LANES // HEAD_DIM) * (HEAD_DIM ** -0.5)
        kg = jnp.tile(k_norm_g[l], LANES // HEAD_DIM)
        qkg = jnp.stack([qg] * (ATT_WIDTH // LANES) + [kg] + [jnp.zeros_like(kg)] * 3)
        return mod_table(l), norm_mix_g[l].reshape(1, D), w_perm, cos, sin, qkg, bd_head

    p = _in_proj(stream, stream_ctx, *in_proj_args(0), S)
    for l in range(DEPTH):
        last = l == DEPTH - 1
        modv = mod_table(l)

        y_att = _attention(p, attn_sink[l], S, not last)

        pad_rows = jnp.zeros((LANES - 2 * GLA_GATE_RANK, GLA_QK_WIDTH), F32)
        zero_rank = jnp.zeros((GLA_GATE_RANK, GLA_QK_WIDTH), F32)
        wgf = jnp.concatenate([gla_gate_w[l, 0], zero_rank, pad_rows], axis=0)
        wgb = jnp.concatenate([zero_rank, gla_gate_w[l, 1], pad_rows], axis=0)
        wg = jnp.concatenate([wgf, wgb], axis=1).astype(BF16)
        gbias = jnp.concatenate([gla_gate_b[l].reshape(1, 2 * GLA_QK_WIDTH),
                                 jnp.zeros((7, 2 * GLA_QK_WIDTH), F32)], axis=0)
        ng = jnp.tile(gla_norm_g[l], GLA_HEADS).reshape(1, GLA_WIDTH)
        cw = jnp.concatenate([conv_w[l], jnp.zeros((5, CONV_WIDTH), F32)], axis=0)
        y_gc = _gla_conv(p, wg, gbias, ng, cw, bd_gla, cum, S)

        rows, tm = (S, LAT_TILE) if last else (T, TOKEN_TILE)
        wr = jnp.concatenate([w_router, jnp.zeros((D, LANES - N_EXPERTS), F32)], axis=1).astype(BF16)
        br = jnp.zeros((8, LANES), F32).at[0, :N_EXPERTS].set(b_router)
        ffn_g = norm_ffn_g[l].reshape(1, D)
        wo_att = w_out[l, :ATT_WIDTH].reshape(ATT_HEADS, HEAD_DIM, D)[order].reshape(ATT_WIDTH, D)
        wo = jnp.concatenate([wo_att, w_out[l, ATT_WIDTH:]], axis=0).astype(BF16)
        xx_mid, route, counts = _out_proj(y_att, y_gc, stream, stream_ctx, modv, wo, ffn_g, wr, br, tri, S, rows, tm)

        n_tiles = -(-(B * rows) // MOE_TILE) + N_CLASSES
        off, pad, tile, e_lo, e_hi, valid = _routing_tables(counts[:N_CLASSES, 0], n_tiles)

        dest = route[:, 1, :]
        for cls_id in range(N_CLASSES):
            dest = dest + jnp.where(route[:, 0, :] == cls_id, off[cls_id], 0)
        hs = _dispatch(pad, xx_mid, modv, ffn_g, dest, n_tiles * MOE_TILE, S, tm)
        ys = _moe(hs, tile, e_lo, e_hi, valid, wg_all, wu_all, wd_all, w_router.T, b_router, l * N_EXPERTS)
        stream, stream_ctx = _combine(xx_mid, modv, dest, ys, S, tm), None
        if not last:
            p = _in_proj(stream, None, *in_proj_args(l + 1), S)
    return stream
```

```python
import functools

import numpy as np
import jax
import jax.numpy as jnp
from jax import lax
from jax.experimental import pallas as pl
from jax.experimental.pallas import tpu as pltpu

D_MODEL = 1024
DEPTH = 2
GRID_W = 64
EPS = 1e-6
HEAD_DIM = 64
ATT_HEADS = 8
ATT_KV_HEADS = 2
ATT_GROUP = ATT_HEADS // ATT_KV_HEADS
ATT_WIDTH = ATT_HEADS * HEAD_DIM
WINDOW = 128
ROPE_BASE = 10000.0
GLA_HEADS = 4
GLA_DV = 64
GLA_DK = 32
GLA_WIDTH = GLA_HEADS * GLA_DV
GLA_GATE_RANK = 16
GLA_GATE_NORM = 16.0
GLA_CHUNK = 64
CONV_WIDTH = 256
N_EXPERTS = 16
N_GROUPS = 4
EXPERTS_PER_GROUP = 4
D_EXPERT = D_MODEL // 2

LANES = 128
KV_WIDTH = ATT_KV_HEADS * HEAD_DIM
GLA_QK_WIDTH = GLA_HEADS * GLA_DK
COL_AQ, COL_AK, COL_AV = 0, 512, 640
COL_GQ, COL_GK, COL_GV, COL_GR = 768, 896, 1024, 1280
COL_CB, COL_CC, COL_CH, COL_GT = 1536, 1792, 2048, 2304
N_PROJ = 2432
QK_COLS = COL_AV
N_PAIRS = 6
N_CLASSES = N_GROUPS * N_PAIRS
PAIR_LO = (0, 0, 0, 1, 1, 2)
PAIR_HI = (1, 2, 3, 2, 3, 3)
CLS_ROWS = 32
NEG = -1e30

TOKEN_TILE = 768
LAT_TILE = 1024
MOE_TILE = 512
SUB_ROWS = 256
VMEM_LIMIT = 56 * 1024 * 1024

F32 = jnp.float32
BF16 = jnp.bfloat16


def _cparams(n_axes):
    return pltpu.CompilerParams(dimension_semantics=("arbitrary",) * n_axes, vmem_limit_bytes=VMEM_LIMIT)


def _silu(x):
    return x * jax.nn.sigmoid(x)


def _mod_body(c_ref, w_ref, b_ref, o_ref):
    c = c_ref[...]
    a = _silu(c).astype(BF16)
    o_ref[0] = jnp.dot(a, w_ref[0].astype(BF16), preferred_element_type=F32) + b_ref[0]


def _modulation(cond, w_ada, b_ada):
    rows = cond.shape[0]
    nblk = w_ada.shape[2] // D_MODEL
    return pl.pallas_call(
        _mod_body,
        grid=(DEPTH, nblk),
        in_specs=[pl.BlockSpec((rows, D_MODEL), lambda l, n: (0, 0)),
                  pl.BlockSpec((1, D_MODEL, D_MODEL), lambda l, n: (l, 0, n)),
                  pl.BlockSpec((1, 1, D_MODEL), lambda l, n: (l, 0, n))],
        out_specs=pl.BlockSpec((1, rows, D_MODEL), lambda l, n: (l, 0, n)),
        out_shape=jax.ShapeDtypeStruct((DEPTH, rows, w_ada.shape[2]), F32),
        compiler_params=_cparams(2),
        name="modulation",
    )(cond, w_ada, b_ada.reshape(DEPTH, 1, -1))


def _row_mod(mod, is_ctx, i):
    return jnp.where(is_ctx, mod[6 + i:7 + i], mod[i:i + 1])


def _norm_modulate(x, gain, mod, is_ctx, i_shift, i_scale):
    ms = jnp.mean(x * x, axis=-1, keepdims=True)
    xn = x * lax.rsqrt(ms + EPS) * gain
    return xn * (1.0 + _row_mod(mod, is_ctx, i_scale)) + _row_mod(mod, is_ctx, i_shift)


def _stream_specs(stream, ctx, tm, b_of, j_of):
    n_sub = tm // SUB_ROWS
    last_piece = stream.shape[1] // SUB_ROWS - 1
    specs = [pl.BlockSpec((1, SUB_ROWS, D_MODEL),
                          lambda *g, k=k: (b_of(g), jnp.minimum(j_of(g) * n_sub + k, last_piece), 0))
             for k in range(n_sub)]
    args = [stream] * n_sub
    if ctx is not None:
        assert ctx.shape[1] == SUB_ROWS and (stream.shape[1] + SUB_ROWS) % tm == 0
        specs.append(pl.BlockSpec((1, SUB_ROWS, D_MODEL), lambda *g: (b_of(g), 0, 0)))
        args.append(ctx)
    return specs, args


def _stream_piece(x_refs, k, n_sub, is_last_tile):
    x = x_refs[k][0]
    if len(x_refs) > n_sub and k == n_sub - 1:
        x = jnp.where(is_last_tile, x_refs[n_sub][0], x)
    return x


def _in_tile(piece, j, mod_ref, g_ref, w_ref, cos_ref, sin_ref, qkg_ref, bd_ref, o_ref, n_lat, tm):
    lane = lax.broadcasted_iota(jnp.int32, (1, LANES), 1)
    first_half = (lane % HEAD_DIM) < (HEAD_DIM // 2)
    for k, r0 in enumerate(range(0, tm, SUB_ROWS)):
        rows = slice(r0, r0 + SUB_ROWS)
        row = j * tm + r0 + lax.broadcasted_iota(jnp.int32, (SUB_ROWS, 1), 0)
        is_ctx = row >= n_lat
        x = piece(k, is_ctx)
        h = _norm_modulate(x, g_ref[...], mod_ref[0], is_ctx, 0, 1).astype(BF16)
        qk = jnp.dot(h, w_ref[:, :QK_COLS], preferred_element_type=F32)
        cos = cos_ref[rows, :]
        sin = sin_ref[rows, :]
        for c in range(QK_COLS // LANES):
            xc = qk[:, c * LANES:(c + 1) * LANES]
            ss = jnp.dot((xc * xc).astype(BF16), bd_ref[...], preferred_element_type=F32) * (1.0 / HEAD_DIM)
            xc = xc * lax.rsqrt(ss + EPS) * qkg_ref[c:c + 1, :]
            rot = jnp.where(first_half, pltpu.roll(xc, LANES - HEAD_DIM // 2, 1), pltpu.roll(xc, HEAD_DIM // 2, 1))
            o_ref[0, rows, c * LANES:(c + 1) * LANES] = (xc * cos + rot * sin).astype(BF16)
        o_ref[0, rows, QK_COLS:] = jnp.dot(h, w_ref[:, QK_COLS:], preferred_element_type=F32).astype(BF16)


def _in_body(*refs, n_lat, tm, n_x):
    x_refs = refs[:n_x]
    j = pl.program_id(0)
    is_last_tile = j == pl.num_programs(0) - 1
    _in_tile(lambda k, is_ctx: _stream_piece(x_refs, k, tm // SUB_ROWS, is_last_tile), j, *refs[n_x:], n_lat, tm)


def _in_proj(stream, ctx, modv, gain, w, cos, sin, qkg, bd, n_lat):
    B = stream.shape[0]
    T = stream.shape[1] + (0 if ctx is None else ctx.shape[1])
    tm = TOKEN_TILE
    x_specs, x_args = _stream_specs(stream, ctx, tm, lambda g: g[1], lambda g: g[0])
    return pl.pallas_call(
        functools.partial(_in_body, n_lat=n_lat, tm=tm, n_x=len(x_args)),
        grid=(T // tm, B),
        in_specs=x_specs + [
                  pl.BlockSpec((1, 16, D_MODEL), lambda j, b: (b, 0, 0)),
                  pl.BlockSpec((1, D_MODEL), lambda j, b: (0, 0)),
                  pl.BlockSpec((D_MODEL, N_PROJ), lambda j, b: (0, 0)),
                  pl.BlockSpec((tm, LANES), lambda j, b: (j, 0)),
                  pl.BlockSpec((tm, LANES), lambda j, b: (j, 0)),
                  pl.BlockSpec((8, LANES), lambda j, b: (0, 0)),
                  pl.BlockSpec((LANES, LANES), lambda j, b: (0, 0))],
        out_specs=pl.BlockSpec((1, tm, N_PROJ), lambda j, b: (b, j, 0)),
        out_shape=jax.ShapeDtypeStruct((B, T, N_PROJ), BF16),
        compiler_params=_cparams(2),
        name="in_proj",
    )(*x_args, modv, gain, w, cos, sin, qkg, bd)


ATT_BLOCK = 128
ATT_SPAN = ATT_BLOCK + 2 * WINDOW


ATT_HEAD_ORDER = (0, 4, 1, 5, 2, 6, 3, 7)


def _attend(qblk, k_parts, v_parts, biases, sinks):
    rows = ATT_GROUP * ATT_BLOCK
    rowi = lax.broadcasted_iota(jnp.int32, (rows, 1), 0)
    lane = lax.broadcasted_iota(jnp.int32, (1, LANES), 1)
    lower = lane < HEAD_DIM
    nt = (((1,), (1,)), ((), ()))
    heads = range(ATT_KV_HEADS)
    keep = [lower, jnp.logical_not(lower)]
    sink, scores = [], []
    for h in heads:
        qs = jnp.concatenate([jnp.where(keep[h], qblk[:, g * LANES:(g + 1) * LANES], jnp.zeros((), BF16))
                              for g in range(ATT_GROUP)], axis=0)
        col = jnp.full((rows, 1), sinks[ATT_GROUP * h + ATT_GROUP - 1], F32)
        for g in range(ATT_GROUP - 2, -1, -1):
            col = jnp.where(rowi < (g + 1) * ATT_BLOCK, sinks[ATT_GROUP * h + g], col)
        sink.append(col)
        pieces = []
        for k in k_parts:
            s = lax.dot_general(qs, k, nt, preferred_element_type=F32)
            pieces += [s[:, c * LANES:(c + 1) * LANES] for c in range(k.shape[0] // LANES)]
        scores.append([s if b is None else s + b for s, b in zip(pieces, biases)])
    top, probs = [], []
    for h in heads:
        m = scores[h][0]
        for s in scores[h][1:]:
            m = jnp.maximum(m, s)
        m = jnp.maximum(jnp.max(m, axis=-1, keepdims=True), sink[h])
        top.append(m)
        probs.append([jnp.exp((s - m).astype(BF16)) for s in scores[h]])
    normed = []
    for h in heads:
        acc = jnp.where(keep[h], 0.0, jnp.exp(sink[h] - top[h]))
        c0 = 0
        for v in v_parts[h]:
            n = v.shape[0] // LANES
            acc = acc + jnp.dot(jnp.concatenate(probs[h][c0:c0 + n], axis=1), v, preferred_element_type=F32)
            c0 += n
        normed.append(acc / pltpu.roll(acc, HEAD_DIM, 1))
    return jnp.concatenate([jnp.where(lower, normed[0][g * ATT_BLOCK:(g + 1) * ATT_BLOCK],
                                      normed[1][g * ATT_BLOCK:(g + 1) * ATT_BLOCK]) for g in range(ATT_GROUP)], axis=1)


def _att_body(sink_ref, q_ref, k_ref, v_ref, o_ref, v1_s, *, n_lat, n_ctx, with_ctx_out):
    sinks = [sink_ref[i] for i in range(ATT_HEADS)]
    lane = lax.broadcasted_iota(jnp.int32, (1, LANES), 1)
    vv = v_ref[0]
    v1_s[0] = jnp.where(lane < HEAD_DIM, vv, jnp.ones((), BF16))
    v1_s[1] = jnp.where(lane < HEAD_DIM, jnp.ones((), BF16), vv)
    k_ctx = k_ref[0, n_lat:n_lat + n_ctx, :]
    v_ctx = [v1_s[h, n_lat:n_lat + n_ctx, :] for h in range(ATT_KV_HEADS)]
    no_bias = [None] * (n_ctx // LANES)
    qi = lax.broadcasted_iota(jnp.int32, (ATT_GROUP * ATT_BLOCK, LANES), 0) % ATT_BLOCK
    ki = lax.broadcasted_iota(jnp.int32, (ATT_GROUP * ATT_BLOCK, LANES), 1)
    past_ok = jnp.where(ki >= qi, 0.0, NEG)
    ahead_ok = jnp.where(ki <= qi, 0.0, NEG)

    def block(q0, k0, n_keys, biases):
        k_parts = [k_ref[0, pl.ds(k0, n_keys), :], k_ctx]
        v_parts = [[v1_s[h, pl.ds(k0, n_keys), :], v_ctx[h]] for h in range(ATT_KV_HEADS)]
        out = _attend(q_ref[0, pl.ds(q0, ATT_BLOCK), :], k_parts, v_parts, biases + no_bias, sinks)
        o_ref[0, pl.ds(q0, ATT_BLOCK), :] = out.astype(BF16)

    def interior(i, carry):
        q0 = pl.multiple_of(i * ATT_BLOCK, ATT_BLOCK)
        block(q0, pl.multiple_of(q0 - WINDOW, ATT_BLOCK), ATT_SPAN, [past_ok, None, ahead_ok])
        return carry

    nq = n_lat // ATT_BLOCK
    block(0, 0, 2 * ATT_BLOCK, [None, ahead_ok])
    lax.fori_loop(1, nq - 1, interior, 0, unroll=7)
    block(n_lat - ATT_BLOCK, n_lat - 2 * ATT_BLOCK, 2 * ATT_BLOCK, [past_ok, None])
    if with_ctx_out:
        for c in range(n_ctx // ATT_BLOCK):
            r0 = n_lat + c * ATT_BLOCK
            out = _attend(q_ref[0, r0:r0 + ATT_BLOCK, :], [k_ctx], [[v_ctx[h]] for h in range(ATT_KV_HEADS)],
                          no_bias, sinks)
            o_ref[0, r0:r0 + ATT_BLOCK, :] = out.astype(BF16)
    else:
        o_ref[0, n_lat:, :] = jnp.zeros((n_ctx, ATT_WIDTH), BF16)


def _attention(p, sinks, n_lat, with_ctx_out):
    B, T, _ = p.shape
    return pl.pallas_call(
        functools.partial(_att_body, n_lat=n_lat, n_ctx=T - n_lat, with_ctx_out=with_ctx_out),
        grid=(B,),
        in_specs=[pl.BlockSpec(memory_space=pltpu.SMEM),
                  pl.BlockSpec((1, T, ATT_WIDTH), lambda b: (b, 0, COL_AQ // ATT_WIDTH)),
                  pl.BlockSpec((1, T, KV_WIDTH), lambda b: (b, 0, COL_AK // KV_WIDTH)),
                  pl.BlockSpec((1, T, KV_WIDTH), lambda b: (b, 0, COL_AV // KV_WIDTH))],
        out_specs=pl.BlockSpec((1, T, ATT_WIDTH), lambda b: (b, 0, 0)),
        out_shape=jax.ShapeDtypeStruct((B, T, ATT_WIDTH), BF16),
        scratch_shapes=[pltpu.VMEM((ATT_KV_HEADS, T, KV_WIDTH), BF16)],
        compiler_params=_cparams(1),
        name="attention",
    )(sinks, p, p, p)


GLA_BLOCK_A = 256
GLA_BLOCK_B = 128
CONV_PAD = 8
EPI_ROWS = 256


def _log_sigmoid(z):
    return jnp.minimum(z, 0.0) - jnp.log1p(jnp.exp(-jnp.abs(z)))


def _gla_body(gq_ref, gk_ref, gv_ref, gr_ref, cb_ref, cc_ref, ch_ref, gt_ref, wg_ref, gbias_ref, ng_ref,
              cw_ref, bd_ref, cum_ref, o_ref, qe_s, ke_s, dec_s, upd_s, prev_s, st_s, o_s, u_s, *, n_lat, n_ctx):
    T = n_lat + n_ctx
    C = GLA_CHUNK
    nt = (((1,), (1,)), ((), ()))
    tn = (((0,), (0,)), ((), ()))

    sr = lax.broadcasted_iota(jnp.int32, (GLA_WIDTH, 2 * GLA_QK_WIDTH), 0) // GLA_DV
    sl = (lax.broadcasted_iota(jnp.int32, (GLA_WIDTH, 2 * GLA_QK_WIDTH), 1) % GLA_QK_WIDTH) // GLA_DK
    state_mask2 = sr == sl

    def factors(i, carry):
        r0 = pl.multiple_of(i * GLA_BLOCK_A, GLA_BLOCK_A)
        rows = pl.ds(r0, GLA_BLOCK_A)
        gt = gt_ref[0, rows, :]
        q = gq_ref[0, rows, :].astype(F32) * (GLA_DK ** -0.5)
        k = gk_ref[0, rows, :].astype(F32)
        v = gv_ref[0, rows, :]
        z2 = jnp.dot(gt, wg_ref[...], preferred_element_type=F32) + gbias_ref[0:1, :]
        g2 = _log_sigmoid(z2) * (1.0 / GLA_GATE_NORM)
        cum = []
        for d in range(2):
            g = g2[:, d * GLA_QK_WIDTH:(d + 1) * GLA_QK_WIDTH]
            g_hi = g.astype(BF16)
            g_lo = (g - g_hi.astype(F32)).astype(BF16)
            cum.append(jnp.dot(cum_ref[d], jnp.concatenate([g_hi, g_lo], axis=1), preferred_element_type=F32))
        kl = []
        for d in range(2):
            b = cum[d][:GLA_BLOCK_A, :GLA_QK_WIDTH] + cum[d][:GLA_BLOCK_A, GLA_QK_WIDTH:]
            tot = cum[d][GLA_BLOCK_A:, :GLA_QK_WIDTH] + cum[d][GLA_BLOCK_A:, GLA_QK_WIDTH:]
            dec = jnp.exp(tot)
            ke = k * jnp.exp(-b)
            qe_s[d, rows, :] = (q * jnp.exp(b)).astype(BF16)
            ke_s[d, rows, :] = ke.astype(BF16)
            dec_s[d, rows, :] = dec
            kl.append((ke * dec).astype(BF16))
        kl2 = jnp.concatenate(kl, axis=1)
        for cc in range(GLA_BLOCK_A // C):
            upd = lax.dot_general(v[cc * C:(cc + 1) * C], kl2[cc * C:(cc + 1) * C], tn, preferred_element_type=F32)
            upd_s[i * (GLA_BLOCK_A // C) + cc] = jnp.where(state_mask2, upd, 0.0)
        return carry

    lax.fori_loop(0, T // GLA_BLOCK_A, factors, 0, unroll=True)

    nc_lat = n_lat // C
    nc_ctx = n_ctx // C
    st_s[...] = jnp.zeros_like(st_s)

    def scan(i, carry):
        in_ctx = i < nc_ctx
        cf = jnp.where(in_ctx, nc_lat + i, i - nc_ctx)
        cb = jnp.where(in_ctx, nc_lat + nc_ctx - 1 - i, nc_lat - 1 - (i - nc_ctx))
        for d, cid in enumerate((cf, cb)):
            st = st_s[d]
            prev_s[cid, :, d * GLA_QK_WIDTH:(d + 1) * GLA_QK_WIDTH] = st.astype(BF16)
            st_s[d] = (st * dec_s[d, pl.ds(pl.multiple_of(cid * C, C), 1), :]
                       + upd_s[cid, :, d * GLA_QK_WIDTH:(d + 1) * GLA_QK_WIDTH])
        return carry

    lax.fori_loop(0, nc_lat + nc_ctx, scan, 0)

    RB = GLA_BLOCK_B
    k_rows = lax.broadcasted_iota(jnp.int32, (GLA_HEADS * RB, GLA_QK_WIDTH), 0) // RB
    k_lanes = lax.broadcasted_iota(jnp.int32, (GLA_HEADS * RB, GLA_QK_WIDTH), 1) // GLA_DK
    key_heads = k_rows == k_lanes
    v_rows = lax.broadcasted_iota(jnp.int32, (GLA_HEADS * RB, GLA_WIDTH), 0) // RB
    v_lanes = lax.broadcasted_iota(jnp.int32, (GLA_HEADS * RB, GLA_WIDTH), 1) // GLA_DV
    value_heads = v_rows == v_lanes
    qr = lax.broadcasted_iota(jnp.int32, (RB, GLA_HEADS * RB), 0)
    kc = lax.broadcasted_iota(jnp.int32, (RB, GLA_HEADS * RB), 1) % RB
    same_chunk = (qr // C) == (kc // C)
    forward = kc <= qr
    zero = jnp.zeros((), BF16)

    def outputs(i, carry):
        r0 = pl.multiple_of(i * RB, RB)
        rows = pl.ds(r0, RB)
        v = gv_ref[0, rows, :]
        v4 = jnp.where(value_heads, jnp.concatenate([v] * GLA_HEADS, axis=0), zero)
        qe = [qe_s[d, rows, :] for d in range(2)]
        att = []
        for d in range(2):
            ke4 = jnp.where(key_heads, jnp.concatenate([ke_s[d, rows, :]] * GLA_HEADS, axis=0), zero)
            att.append(lax.dot_general(qe[d], ke4, nt, preferred_element_type=F32))
        both = jnp.where(same_chunk, jnp.where(forward, att[0], att[1]), 0.0).astype(BF16)
        qe2 = jnp.concatenate(qe, axis=1)
        inter = [lax.dot_general(qe2[cc * C:(cc + 1) * C], prev_s[i * (RB // C) + cc], nt,
                                 preferred_element_type=F32) for cc in range(RB // C)]
        o_s[rows, :] = jnp.dot(both, v4, preferred_element_type=F32) + jnp.concatenate(inter, axis=0)
        return carry

    lax.fori_loop(0, T // RB, outputs, 0, unroll=6)

    u_s[0:CONV_PAD, :] = jnp.zeros((CONV_PAD, CONV_WIDTH), F32)
    u_s[CONV_PAD + T:, :] = jnp.zeros((CONV_PAD, CONV_WIDTH), F32)
    u_s[CONV_PAD:CONV_PAD + T, :] = cc_ref[0].astype(F32) * ch_ref[0].astype(F32)
    w0 = cw_ref[0:1, :]
    w1 = cw_ref[1:2, :]
    w2 = cw_ref[2:3, :]
    for e in range(T // EPI_ROWS):
        r0 = e * EPI_ROWS
        o = o_s[r0:r0 + EPI_ROWS, :]
        ss = jnp.dot((o * o).astype(BF16), bd_ref[...], preferred_element_type=F32) * (1.0 / GLA_DV)
        on = o * lax.rsqrt(ss + EPS) * ng_ref[...]
        r = gr_ref[0, r0:r0 + EPI_ROWS, :].astype(F32)
        o_ref[0, r0:r0 + EPI_ROWS, 0:GLA_WIDTH] = (on * _silu(r)).astype(BF16)
        t = r0 + lax.broadcasted_iota(jnp.int32, (EPI_ROWS, 1), 0)
        up = u_s[CONV_PAD + r0 - 1:CONV_PAD + r0 - 1 + EPI_ROWS, :]
        mid = u_s[CONV_PAD + r0:CONV_PAD + r0 + EPI_ROWS, :]
        dn = u_s[CONV_PAD + r0 + 1:CONV_PAD + r0 + 1 + EPI_ROWS, :]
        up = jnp.where(t == n_lat, 0.0, up)
        dn = jnp.where(t == n_lat - 1, 0.0, dn)
        conv = w0 * up + w1 * mid + w2 * dn
        o_ref[0, r0:r0 + EPI_ROWS, GLA_WIDTH:] = (cb_ref[0, r0:r0 + EPI_ROWS, :].astype(F32) * conv).astype(BF16)


def _gla_cum_matrices():
    i = np.arange(GLA_BLOCK_A)
    same = (i[:, None] // GLA_CHUNK) == (i[None, :] // GLA_CHUNK)
    fwd = same & (i[None, :] <= i[:, None])
    bwd = same & (i[None, :] >= i[:, None])
    mats = np.stack([np.concatenate([fwd, same], axis=0), np.concatenate([bwd, same], axis=0)])
    return jnp.asarray(mats.astype(np.float32), dtype=BF16)


def _gla_conv(p, wg, gbias, ng, cw, bd, cum, n_lat):
    B, T, _ = p.shape
    nc = T // GLA_CHUNK

    def col(width, start):
        return pl.BlockSpec((1, T, width), lambda b: (b, 0, start // width))

    def const(shape):
        return pl.BlockSpec(shape, lambda b: (0,) * len(shape))

    return pl.pallas_call(
        functools.partial(_gla_body, n_lat=n_lat, n_ctx=T - n_lat),
        grid=(B,),
        in_specs=[col(GLA_QK_WIDTH, COL_GQ), col(GLA_QK_WIDTH, COL_GK), col(GLA_WIDTH, COL_GV), col(GLA_WIDTH, COL_GR),
                  col(CONV_WIDTH, COL_CB), col(CONV_WIDTH, COL_CC), col(CONV_WIDTH, COL_CH), col(LANES, COL_GT),
                  const((LANES, 2 * GLA_QK_WIDTH)), const((8, 2 * GLA_QK_WIDTH)),
                  const((1, GLA_WIDTH)), const((8, CONV_WIDTH)), const((GLA_WIDTH, GLA_WIDTH)),
                  const((2, 2 * GLA_BLOCK_A, GLA_BLOCK_A))],
        out_specs=pl.BlockSpec((1, T, GLA_WIDTH + CONV_WIDTH), lambda b: (b, 0, 0)),
        out_shape=jax.ShapeDtypeStruct((B, T, GLA_WIDTH + CONV_WIDTH), BF16),
        scratch_shapes=[pltpu.VMEM((2, T, GLA_QK_WIDTH), BF16), pltpu.VMEM((2, T, GLA_QK_WIDTH), BF16),
                        pltpu.VMEM((2, T, GLA_QK_WIDTH), F32),
                        pltpu.VMEM((nc, GLA_WIDTH, 2 * GLA_QK_WIDTH), F32),
                        pltpu.VMEM((nc, GLA_WIDTH, 2 * GLA_QK_WIDTH), BF16),
                        pltpu.VMEM((2, GLA_WIDTH, GLA_QK_WIDTH), F32),
                        pltpu.VMEM((T, GLA_WIDTH), F32),
                        pltpu.VMEM((T + 2 * CONV_PAD, CONV_WIDTH), F32)],
        compiler_params=_cparams(1),
        name="gla_conv",
    )(p, p, p, p, p, p, p, p, wg, gbias, ng, cw, bd, cum)


def _route(logits_t):
    mx = jnp.max(logits_t, axis=0, keepdims=True)
    ex = jnp.exp(logits_t - mx)
    probs = ex / jnp.sum(ex, axis=0, keepdims=True)
    P = [probs[e:e + 1] for e in range(N_EXPERTS)]
    scores = []
    for g in range(N_GROUPS):
        a, b, c, d = P[4 * g:4 * g + 4]
        scores.append(jnp.maximum(jnp.maximum(jnp.maximum(a + b, a + c), jnp.maximum(a + d, b + c)),
                                  jnp.maximum(b + d, c + d)))
    best = jnp.maximum(jnp.maximum(scores[0], scores[1]), jnp.maximum(scores[2], scores[3]))
    taken = jnp.zeros_like(best, dtype=jnp.bool_)
    sel = []
    for g in range(N_GROUPS):
        s = (scores[g] == best) & jnp.logical_not(taken)
        sel.append(s)
        taken = taken | s
    gsel = jnp.where(sel[1], 1.0, 0.0) + jnp.where(sel[2], 2.0, 0.0) + jnp.where(sel[3], 3.0, 0.0)
    ig = [jnp.where(sel[0], P[j], jnp.where(sel[1], P[4 + j], jnp.where(sel[2], P[8 + j], P[12 + j])))
          for j in range(EXPERTS_PER_GROUP)]

    def first_max(vals):
        v = jnp.maximum(jnp.maximum(vals[0], vals[1]), jnp.maximum(vals[2], vals[3]))
        tk = jnp.zeros_like(v, dtype=jnp.bool_)
        hot = []
        for x in vals:
            s = (x == v) & jnp.logical_not(tk)
            hot.append(s)
            tk = tk | s
        idx = jnp.where(hot[1], 1.0, 0.0) + jnp.where(hot[2], 2.0, 0.0) + jnp.where(hot[3], 3.0, 0.0)
        return v, hot, idx

    _, hot1, i1 = first_max(ig)
    _, _, i2 = first_max([jnp.where(hot1[j], -1.0, ig[j]) for j in range(EXPERTS_PER_GROUP)])
    lo = jnp.minimum(i1, i2)
    hi = jnp.maximum(i1, i2)
    pair = jnp.where(lo == 0.0, hi - 1.0, jnp.where(lo == 1.0, hi + 1.0, 5.0))
    return gsel * N_PAIRS + pair


def _class_rank(cls, tri_ref, cnt_s):
    n = cls.shape[1]
    cid = lax.broadcasted_iota(jnp.int32, (CLS_ROWS, n), 0).astype(F32)
    onehot = jnp.where(cls == cid, 1.0, 0.0)
    segs = [onehot[:, k * LANES:(k + 1) * LANES] for k in range(n // LANES)]
    before = jnp.dot(jnp.concatenate(segs, axis=0).astype(BF16), tri_ref[...], preferred_element_type=F32)
    base = cnt_s[...]
    ranks = []
    for k, seg in enumerate(segs):
        ranks.append(jnp.sum(seg * (before[k * CLS_ROWS:(k + 1) * CLS_ROWS] + base), axis=0, keepdims=True))
        base = base + jnp.sum(seg, axis=1, keepdims=True)
    cnt_s[...] = base
    return jnp.concatenate(ranks, axis=1)


def _out_body(*refs, n_lat, tm, n_x):
    x_refs = refs[:n_x]
    ya_ref, yg_ref, mod_ref, wo_ref, g_ref, wr_ref, br_ref, tri_ref, xo_ref, rt_ref, cnt_ref, cnt_s, logit_s = refs[n_x:]
    j = pl.program_id(1)
    is_last_tile = j == pl.num_programs(1) - 1

    @pl.when((pl.program_id(0) == 0) & (j == 0))
    def _():
        cnt_s[...] = jnp.zeros_like(cnt_s)

    mod = mod_ref[0]
    subs = list(enumerate(range(0, tm, SUB_ROWS)))
    ys = [jnp.dot(ya_ref[0, r0:r0 + SUB_ROWS, :], wo_ref[0:ATT_WIDTH, :], preferred_element_type=F32)
          + jnp.dot(yg_ref[0, r0:r0 + SUB_ROWS, :], wo_ref[ATT_WIDTH:, :], preferred_element_type=F32) for _, r0 in subs]
    h2s = []
    for k, r0 in subs:
        row = j * tm + r0 + lax.broadcasted_iota(jnp.int32, (SUB_ROWS, 1), 0)
        is_ctx = row >= n_lat
        xn = _stream_piece(x_refs, k, tm // SUB_ROWS, is_last_tile) + _row_mod(mod, is_ctx, 2) * ys[k]
        xo_ref[0, r0:r0 + SUB_ROWS, :] = xn
        h2s.append(_norm_modulate(xn, g_ref[...], mod, is_ctx, 3, 4).astype(BF16))
    for k, r0 in subs:
        logit_s[r0:r0 + SUB_ROWS, :] = jnp.dot(h2s[k], wr_ref[...], preferred_element_type=F32) + br_ref[0:1, :]
    cls = _route(logit_s[...].T[0:N_EXPERTS, :])
    rank = _class_rank(cls, tri_ref, cnt_s)
    rt_ref[0] = jnp.concatenate([cls, rank, jnp.zeros((6, tm), F32)], axis=0).astype(jnp.int32)
    cnt_ref[...] = jnp.broadcast_to(cnt_s[...], (CLS_ROWS, LANES)).astype(jnp.int32)


def _out_proj(ya, yg, stream, ctx, modv, wo, gain, wr, br, tri, n_lat, rows, tm):
    B = stream.shape[0]
    nj = rows // tm
    x_specs, x_args = _stream_specs(stream, ctx, tm, lambda g: g[0], lambda g: g[1])
    return pl.pallas_call(
        functools.partial(_out_body, n_lat=n_lat, tm=tm, n_x=len(x_args)),
        grid=(B, nj),
        in_specs=x_specs + [
                  pl.BlockSpec((1, tm, ATT_WIDTH), lambda b, j: (b, j, 0)),
                  pl.BlockSpec((1, tm, GLA_WIDTH + CONV_WIDTH), lambda b, j: (b, j, 0)),
                  pl.BlockSpec((1, 16, D_MODEL), lambda b, j: (b, 0, 0)),
                  pl.BlockSpec((D_MODEL, D_MODEL), lambda b, j: (0, 0)),
                  pl.BlockSpec((1, D_MODEL), lambda b, j: (0, 0)),
                  pl.BlockSpec((D_MODEL, LANES), lambda b, j: (0, 0)),
                  pl.BlockSpec((8, LANES), lambda b, j: (0, 0)),
                  pl.BlockSpec((LANES, LANES), lambda b, j: (0, 0))],
        out_specs=[pl.BlockSpec((1, tm, D_MODEL), lambda b, j: (b, j, 0)),
                   pl.BlockSpec((1, 8, tm), lambda b, j: (b * nj + j, 0, 0)),
                   pl.BlockSpec((CLS_ROWS, LANES), lambda b, j: (0, 0))],
        out_shape=[jax.ShapeDtypeStruct((B, rows, D_MODEL), F32),
                   jax.ShapeDtypeStruct((B * nj, 8, tm), jnp.int32),
                   jax.ShapeDtypeStruct((CLS_ROWS, LANES), jnp.int32)],
        scratch_shapes=[pltpu.VMEM((CLS_ROWS, 1), F32), pltpu.VMEM((tm, LANES), F32)],
        compiler_params=_cparams(2),
        name="out_proj_router",
    )(*x_args, ya, yg, modv, wo, gain, wr, br, tri)


ROW_UNROLL = 8
IDX_STRIDE = 1024


def _idx_slot(idx, slot, tm):
    return idx.at[pl.ds(pl.multiple_of(slot * IDX_STRIDE, IDX_STRIDE), tm)]


def _alternate(r):
    return r % 2


def _second_thread(r):
    return 1


def _issue_rows(tm, idx, slot, make_copy, priority=_alternate):
    base = slot * IDX_STRIDE

    def trip(i, c):
        for u in range(ROW_UNROLL):
            r = i * ROW_UNROLL + u
            make_copy(r, idx[base + r]).start(priority=priority(u))
        return c

    lax.fori_loop(0, tm // ROW_UNROLL, trip, 0)


PAD_SLOTS = 32
PAD_BITS = 9


def _zero_pad_rows(pad_ref, zeros_ref, hs_ref, sem):
    def pieces(c, fn):
        start = pad_ref[c]
        n = pad_ref[PAD_SLOTS + c]
        for bit in range(PAD_BITS):
            size = 1 << bit
            below = n & (size - 1)

            @pl.when((n & size) != 0)
            def _():
                fn(pltpu.make_async_copy(zeros_ref.at[pl.ds(0, size)], hs_ref.at[pl.ds(start + below, size)], sem))

        def block(i, carry):
            fn(pltpu.make_async_copy(zeros_ref, hs_ref.at[pl.ds(start + n + i * MOE_TILE, MOE_TILE)], sem))
            return carry

        lax.fori_loop(0, pad_ref[2 * PAD_SLOTS + c], block, 0)

    def start_all(c, carry):
        pieces(c, lambda cp: cp.start())
        return carry

    def wait_all(c, carry):
        pieces(c, lambda cp: cp.wait())
        return carry

    lax.fori_loop(0, N_CLASSES + 1, start_all, 0)
    lax.fori_loop(0, N_CLASSES + 1, wait_all, 0)


def _issue_rows_inline(tm, idx, slot, make_copy, priority=_alternate):
    base = slot * IDX_STRIDE
    for r in range(tm):
        make_copy(r, idx[base + r]).start(priority=priority(r))


def _disp_body(pad_ref, x_ref, mod_ref, g_ref, dest_ref, hs_ref, buf0, buf1, buf2, idx, sem_i, sem_d, sem_z, *, n_lat,
               tm, nj, n_steps):
    bufs = (buf0, buf1, buf2)
    j = pl.program_id(1)
    s = pl.program_id(0) * nj + j
    last = n_steps - 1

    def idx_fetch(step):
        return pltpu.make_async_copy(dest_ref.at[step], _idx_slot(idx, step % 3, tm), sem_i.at[step % 3])

    def drain(k):
        pltpu.make_async_copy(bufs[k], hs_ref.at[pl.ds(0, tm)], sem_d.at[k]).wait()

    def row_copy(k):
        return lambda r, d: pltpu.make_async_copy(bufs[k].at[r], hs_ref.at[d], sem_d.at[k])

    def normalise(k):
        row = j * tm + lax.broadcasted_iota(jnp.int32, (tm, 1), 0)
        h2 = _norm_modulate(x_ref[0], g_ref[...], mod_ref[0], row >= n_lat, 3, 4)
        bufs[k][...] = h2.reshape(tm, 8, LANES)

    @pl.when(s == 0)
    def _():
        idx_fetch(0).start()
        buf1[...] = jnp.zeros((tm, 8, LANES), F32)
        _zero_pad_rows(pad_ref, buf1.at[pl.ds(0, MOE_TILE)], hs_ref, sem_z)
        normalise(0)

    @pl.when(s < last)
    def _():
        idx_fetch(s + 1).start()

    idx_fetch(s).wait()

    for k in range(3):
        prev = (k + 2) % 3

        @pl.when((s > 0) & (s % 3 == k))
        def _():
            _issue_rows_inline(tm, idx, prev, row_copy(prev))
            normalise(k)

        @pl.when((s > 1) & (s % 3 == k))
        def _():
            drain((k + 1) % 3)

        @pl.when((s == last) & (s % 3 == k))
        def _():
            _issue_rows(tm, idx, k, row_copy(k))
            drain(k)
            if last > 0:
                drain(prev)


def _dispatch(pad, xx, modv, gain, dest, n_rows, n_lat, tm):
    B, rows, _ = xx.shape
    nj = rows // tm
    assert tm >= MOE_TILE
    return pl.pallas_call(
        functools.partial(_disp_body, n_lat=n_lat, tm=tm, nj=nj, n_steps=B * nj),
        grid=(B, nj),
        in_specs=[pl.BlockSpec(memory_space=pltpu.SMEM),
                  pl.BlockSpec((1, tm, D_MODEL), lambda b, j: (b, j, 0)),
                  pl.BlockSpec((1, 16, D_MODEL), lambda b, j: (b, 0, 0)),
                  pl.BlockSpec((1, D_MODEL), lambda b, j: (0, 0)),
                  pl.BlockSpec(memory_space=pl.ANY)],
        out_specs=pl.BlockSpec(memory_space=pl.ANY),
        scratch_shapes=[pltpu.VMEM((tm, 8, LANES), F32), pltpu.VMEM((tm, 8, LANES), F32),
                        pltpu.VMEM((tm, 8, LANES), F32), pltpu.SMEM((3 * IDX_STRIDE,), jnp.int32),
                        pltpu.SemaphoreType.DMA((3,)), pltpu.SemaphoreType.DMA((3,)), pltpu.SemaphoreType.DMA],
        out_shape=jax.ShapeDtypeStruct((n_rows, 8, LANES), F32),
        compiler_params=_cparams(2),
        name="dispatch",
    )(pad, xx, modv, gain, dest)


def _moe_body(tile_ref, e_lo_ref, e_hi_ref, valid_ref, hs_ref, wg1, wu1, wd1, wg2, wu2, wd2, wr_ref, br_ref, ys_ref):
    g = pl.program_id(0)
    tm = hs_ref.shape[0]

    @pl.when(valid_ref[g] == 1)
    def _():
        x = hs_ref[...].reshape(tm, D_MODEL)
        e_lo = e_lo_ref[g]
        e_hi = e_hi_ref[g]
        dw = wr_ref[pl.ds(e_lo, 1), :] - wr_ref[pl.ds(e_hi, 1), :]
        d = jnp.sum(x * dw, axis=-1, keepdims=True) + (br_ref[e_lo] - br_ref[e_hi])
        w_lo = jax.nn.sigmoid(d)
        w_hi = jax.nn.sigmoid(-d)
        h = x.astype(BF16)

        def act(wg, wu, w):
            a = _silu(jnp.dot(h, wg[0], preferred_element_type=F32)) * jnp.dot(h, wu[0], preferred_element_type=F32)
            return (a * w).astype(BF16)

        y = (jnp.dot(act(wg1, wu1, w_lo), wd1[0], preferred_element_type=F32)
             + jnp.dot(act(wg2, wu2, w_hi), wd2[0], preferred_element_type=F32))
        ys_ref[...] = y.reshape(tm, 8, LANES)

    @pl.when(valid_ref[g] == 0)
    def _():
        ys_ref[...] = jnp.zeros_like(ys_ref)


def _moe(hs, tile, e_lo, e_hi, valid, wg, wu, wd, wr_t, br, first_expert):
    n_tiles = tile.shape[0]
    tm = MOE_TILE

    def w_in(sel):
        return pl.BlockSpec((1, D_MODEL, D_EXPERT), lambda g, t, lo, hi, v: (first_expert + (lo, hi)[sel][g], 0, 0))

    def w_out(sel):
        return pl.BlockSpec((1, D_EXPERT, D_MODEL), lambda g, t, lo, hi, v: (first_expert + (lo, hi)[sel][g], 0, 0))

    return pl.pallas_call(
        _moe_body,
        grid_spec=pltpu.PrefetchScalarGridSpec(
            num_scalar_prefetch=4,
            grid=(n_tiles,),
            in_specs=[pl.BlockSpec((tm, 8, LANES), lambda g, t, lo, hi, v: (t[g], 0, 0)),
                      w_in(0), w_in(0), w_out(0), w_in(1), w_in(1), w_out(1),
                      pl.BlockSpec((N_EXPERTS, D_MODEL), lambda g, t, lo, hi, v: (0, 0)),
                      pl.BlockSpec(memory_space=pltpu.SMEM)],
            out_specs=pl.BlockSpec((tm, 8, LANES), lambda g, t, lo, hi, v: (g, 0, 0))),
        out_shape=jax.ShapeDtypeStruct((n_tiles * tm, 8, LANES), F32),
        compiler_params=_cparams(1),
        name="moe_pairs",
    )(tile, e_lo, e_hi, valid, hs, wg, wu, wd, wg, wu, wd, wr_t, br)


def _fin_body(x_ref, mod_ref, dest_ref, ys_ref, xo_ref, buf0, buf1, buf2, idx, sem_i, sem_d, *, n_lat, tm, nj,
              n_steps):
    bufs = (buf0, buf1, buf2)
    j = pl.program_id(1)
    s = pl.program_id(0) * nj + j
    last = n_steps - 1

    def idx_fetch(step):
        return pltpu.make_async_copy(dest_ref.at[step], _idx_slot(idx, step % 3, tm), sem_i.at[step % 3])

    def row_copy(k):
        return lambda r, d: pltpu.make_async_copy(ys_ref.at[d], bufs[k].at[r], sem_d.at[k])

    def add_rows(k):
        row = j * tm + lax.broadcasted_iota(jnp.int32, (tm, 1), 0)
        xo_ref[0] = x_ref[0] + _row_mod(mod_ref[0], row >= n_lat, 5) * bufs[k][...].reshape(tm, D_MODEL)

    @pl.when(s == 0)
    def _():
        for t in range(min(2, n_steps)):
            idx_fetch(t).start()
            idx_fetch(t).wait()
            _issue_rows(tm, idx, t, row_copy(t), _second_thread)
        if last >= 2:
            idx_fetch(2).start()

    for k in range(3):
        ahead = (k + 2) % 3

        @pl.when((s % 3 == k) & (s + 2 <= last))
        def _():
            pltpu.make_async_copy(ys_ref.at[pl.ds(0, tm)], bufs[k], sem_d.at[k]).wait()
            idx_fetch(s + 2).wait()
            _issue_rows_inline(tm, idx, ahead, row_copy(ahead), _second_thread)
            add_rows(k)

        @pl.when((s % 3 == k) & (s + 2 > last))
        def _():
            pltpu.make_async_copy(ys_ref.at[pl.ds(0, tm)], bufs[k], sem_d.at[k]).wait()
            add_rows(k)

    @pl.when(s + 3 <= last)
    def _():
        idx_fetch(s + 3).start()


def _combine(xx, modv, dest, ys, n_lat, tm):
    B, rows, _ = xx.shape
    nj = rows // tm
    return pl.pallas_call(
        functools.partial(_fin_body, n_lat=n_lat, tm=tm, nj=nj, n_steps=B * nj),
        grid=(B, nj),
        in_specs=[pl.BlockSpec((1, tm, D_MODEL), lambda b, j: (b, j, 0)),
                  pl.BlockSpec((1, 16, D_MODEL), lambda b, j: (b, 0, 0)),
                  pl.BlockSpec(memory_space=pl.ANY),
                  pl.BlockSpec(memory_space=pl.ANY)],
        out_specs=pl.BlockSpec((1, tm, D_MODEL), lambda b, j: (b, j, 0)),
        scratch_shapes=[pltpu.VMEM((tm, 8, LANES), F32), pltpu.VMEM((tm, 8, LANES), F32),
                        pltpu.VMEM((tm, 8, LANES), F32), pltpu.SMEM((3 * IDX_STRIDE,), jnp.int32),
                        pltpu.SemaphoreType.DMA((3,)), pltpu.SemaphoreType.DMA((3,))],
        out_shape=jax.ShapeDtypeStruct((B, rows, D_MODEL), F32),
        compiler_params=_cparams(2),
        name="combine",
    )(xx, modv, dest, ys)


def _rope_tables(n_lat, n_ctx):
    rows = n_lat // GRID_W
    row, col = jnp.meshgrid(jnp.arange(rows), jnp.arange(GRID_W), indexing="ij")
    n_freq = HEAD_DIM // 4
    inv_freq = ROPE_BASE ** (-jnp.arange(n_freq, dtype=F32) / n_freq)
    ang = jnp.concatenate([row.reshape(-1, 1).astype(F32) * inv_freq, col.reshape(-1, 1).astype(F32) * inv_freq],
                          axis=-1)
    cos = jnp.tile(jnp.cos(ang), (1, LANES // (HEAD_DIM // 2)))
    sin = jnp.tile(jnp.sin(ang), (1, LANES // (HEAD_DIM // 2)))
    sign = jnp.where((jnp.arange(LANES) % HEAD_DIM) < HEAD_DIM // 2, -1.0, 1.0).astype(F32)
    cos = jnp.concatenate([cos, jnp.ones((n_ctx, LANES), F32)], axis=0)
    sin = jnp.concatenate([sin * sign, jnp.zeros((n_ctx, LANES), F32)], axis=0)
    return cos, sin


def _block_diag_ones(n, blk):
    i = np.arange(n) // blk
    return jnp.asarray((i[:, None] == i[None, :]).astype(np.float32), dtype=BF16)


def _routing_tables(counts, n_tiles):
    tiles_c = (counts + MOE_TILE - 1) // MOE_TILE
    tile_end = jnp.cumsum(tiles_c)
    tile_start = tile_end - tiles_c
    off = tile_start * MOE_TILE
    total = tile_end[-1]
    g = jnp.arange(n_tiles, dtype=jnp.int32)
    valid = (g < total).astype(jnp.int32)
    g_eff = jnp.minimum(g, total - 1)
    c_of = jnp.sum((g_eff[:, None] >= tile_end[None, :]).astype(jnp.int32), axis=1)
    group = c_of // N_PAIRS
    pair = c_of % N_PAIRS
    lo = jnp.asarray(PAIR_LO, jnp.int32)
    hi = jnp.asarray(PAIR_HI, jnp.int32)
    e_lo = group * EXPERTS_PER_GROUP + jnp.sum((pair[:, None] == jnp.arange(N_PAIRS)[None, :]) * lo[None, :], axis=1)
    e_hi = group * EXPERTS_PER_GROUP + jnp.sum((pair[:, None] == jnp.arange(N_PAIRS)[None, :]) * hi[None, :], axis=1)
    fill = jnp.zeros((PAD_SLOTS - N_CLASSES - 1,), jnp.int32)
    pad = jnp.concatenate([off + counts, (total * MOE_TILE)[None], fill,
                           tiles_c * MOE_TILE - counts, jnp.zeros((1,), jnp.int32), fill,
                           jnp.zeros((N_CLASSES,), jnp.int32), (n_tiles - total)[None], fill])
    return off.astype(jnp.int32), pad.astype(jnp.int32), g_eff, e_lo.astype(jnp.int32), e_hi.astype(jnp.int32), valid


def kernel(x, c, ctx, c_ctx, w_ada, b_ada, norm_mix_g, norm_ffn_g, w_in, q_norm_g, k_norm_g, attn_sink, gla_gate_w,
           gla_gate_b, gla_norm_g, conv_w, w_out, w_router, b_router, w_gate_e, w_up_e, w_down_e):
    B, S, D = x.shape
    L = ctx.shape[1]
    T = S + L
    assert D == D_MODEL and T % TOKEN_TILE == 0 and S % LAT_TILE == 0 and S % GRID_W == 0
    assert S % ATT_BLOCK == 0 and L % ATT_BLOCK == 0 and S >= ATT_SPAN and T % EPI_ROWS == 0

    cond_rows = -(-(B + 1) // 8) * 8
    cond = jnp.zeros((cond_rows, D), F32).at[:B].set(c).at[B].set(c_ctx)
    mod_all = _modulation(cond, w_ada, b_ada)

    cos, sin = _rope_tables(S, L)
    bd_head = _block_diag_ones(LANES, HEAD_DIM)
    bd_gla = _block_diag_ones(GLA_WIDTH, GLA_DV)
    cum = _gla_cum_matrices()
    tri = jnp.asarray(np.triu(np.ones((LANES, LANES), np.float32), 1), dtype=BF16)
    stream, stream_ctx = x, ctx

    order = jnp.asarray(ATT_HEAD_ORDER)
    wg_all = w_gate_e.reshape(DEPTH * N_EXPERTS, D, D_EXPERT).astype(BF16)
    wu_all = w_up_e.reshape(DEPTH * N_EXPERTS, D, D_EXPERT).astype(BF16)
    wd_all = w_down_e.reshape(DEPTH * N_EXPERTS, D_EXPERT, D).astype(BF16)

    def mod_table(l):
        m_lat = mod_all[l, :B].reshape(B, 6, D)
        m_ctx = jnp.broadcast_to(mod_all[l, B].reshape(1, 6, D), (B, 6, D))
        return jnp.concatenate([m_lat, m_ctx, jnp.zeros((B, 4, D), F32)], axis=1)

    def in_proj_args(l):
        wl = w_in[l]
        wq = wl[:, :ATT_WIDTH].reshape(D, ATT_HEADS, HEAD_DIM)[:, order, :].reshape(D, ATT_WIDTH)
        w_perm = jnp.concatenate([wq, wl[:, ATT_WIDTH:1536], wl[:, 1568:], wl[:, 1536:1568],
                                  jnp.zeros((D, N_PROJ - wl.shape[1]), F32)], axis=1).astype(BF16)
        qg = jnp.tile(q_norm_g[l], LANES // HEAD_DIM) * (HEAD_DIM ** -0.5)
        kg = jnp.tile(k_norm_g[l], LANES // HEAD_DIM)
        qkg = jnp.stack([qg] * (ATT_WIDTH // LANES) + [kg] + [jnp.zeros_like(kg)] * 3)
        return mod_table(l), norm_mix_g[l].reshape(1, D), w_perm, cos, sin, qkg, bd_head

    p = _in_proj(stream, stream_ctx, *in_proj_args(0), S)
    for l in range(DEPTH):
        last = l == DEPTH - 1
        modv = mod_table(l)

        y_att = _attention(p, attn_sink[l], S, not last)

        pad_rows = jnp.zeros((LANES - 2 * GLA_GATE_RANK, GLA_QK_WIDTH), F32)
        zero_rank = jnp.zeros((GLA_GATE_RANK, GLA_QK_WIDTH), F32)
        wgf = jnp.concatenate([gla_gate_w[l, 0], zero_rank, pad_rows], axis=0)
        wgb = jnp.concatenate([zero_rank, gla_gate_w[l, 1], pad_rows], axis=0)
        wg = jnp.concatenate([wgf, wgb], axis=1).astype(BF16)
        gbias = jnp.concatenate([gla_gate_b[l].reshape(1, 2 * GLA_QK_WIDTH),
                                 jnp.zeros((7, 2 * GLA_QK_WIDTH), F32)], axis=0)
        ng = jnp.tile(gla_norm_g[l], GLA_HEADS).reshape(1, GLA_WIDTH)
        cw = jnp.concatenate([conv_w[l], jnp.zeros((5, CONV_WIDTH), F32)], axis=0)
        y_gc = _gla_conv(p, wg, gbias, ng, cw, bd_gla, cum, S)

        rows, tm = (S, LAT_TILE) if last else (T, TOKEN_TILE)
        wr = jnp.concatenate([w_router, jnp.zeros((D, LANES - N_EXPERTS), F32)], axis=1).astype(BF16)
        br = jnp.zeros((8, LANES), F32).at[0, :N_EXPERTS].set(b_router)
        ffn_g = norm_ffn_g[l].reshape(1, D)
        wo_att = w_out[l, :ATT_WIDTH].reshape(ATT_HEADS, HEAD_DIM, D)[order].reshape(ATT_WIDTH, D)
        wo = jnp.concatenate([wo_att, w_out[l, ATT_WIDTH:]], axis=0).astype(BF16)
        xx_mid, route, counts = _out_proj(y_att, y_gc, stream, stream_ctx, modv, wo, ffn_g, wr, br, tri, S, rows, tm)

        n_tiles = -(-(B * rows) // MOE_TILE) + N_CLASSES
        off, pad, tile, e_lo, e_hi, valid = _routing_tables(counts[:N_CLASSES, 0], n_tiles)

        dest = route[:, 1, :]
        for cls_id in range(N_CLASSES):
            dest = dest + jnp.where(route[:, 0, :] == cls_id, off[cls_id], 0)
        hs = _dispatch(pad, xx_mid, modv, ffn_g, dest, n_tiles * MOE_TILE, S, tm)
        ys = _moe(hs, tile, e_lo, e_hi, valid, wg_all, wu_all, wd_all, w_router.T, b_router, l * N_EXPERTS)
        stream, stream_ctx = _combine(xx_mid, modv, dest, ys, S, tm), None
        if not last:
            p = _in_proj(stream, None, *in_proj_args(l + 1), S)
    return stream
```

```python
import functools

import numpy as np
import jax
import jax.numpy as jnp
from jax import lax
from jax.experimental import pallas as pl
from jax.experimental.pallas import tpu as pltpu

D_MODEL = 1024
DEPTH = 2
GRID_W = 64
EPS = 1e-6
HEAD_DIM = 64
ATT_HEADS = 8
ATT_KV_HEADS = 2
ATT_GROUP = ATT_HEADS // ATT_KV_HEADS
ATT_WIDTH = ATT_HEADS * HEAD_DIM
WINDOW = 128
ROPE_BASE = 10000.0
GLA_HEADS = 4
GLA_DV = 64
GLA_DK = 32
GLA_WIDTH = GLA_HEADS * GLA_DV
GLA_GATE_RANK = 16
GLA_GATE_NORM = 16.0
GLA_CHUNK = 64
CONV_WIDTH = 256
N_EXPERTS = 16
N_GROUPS = 4
EXPERTS_PER_GROUP = 4
D_EXPERT = D_MODEL // 2

LANES = 128
KV_WIDTH = ATT_KV_HEADS * HEAD_DIM
GLA_QK_WIDTH = GLA_HEADS * GLA_DK
COL_AQ, COL_AK, COL_AV = 0, 512, 640
COL_GQ, COL_GK, COL_GV, COL_GR = 768, 896, 1024, 1280
COL_CB, COL_CC, COL_CH, COL_GT = 1536, 1792, 2048, 2304
N_PROJ = 2432
QK_COLS = COL_AV
N_PAIRS = 6
N_CLASSES = N_GROUPS * N_PAIRS
PAIR_LO = (0, 0, 0, 1, 1, 2)
PAIR_HI = (1, 2, 3, 2, 3, 3)
CLS_ROWS = 32
NEG = -1e30

TOKEN_TILE = 768
LAT_TILE = 1024
MOE_TILE = 512
SUB_ROWS = 256
VMEM_LIMIT = 56 * 1024 * 1024

F32 = jnp.float32
BF16 = jnp.bfloat16


def _cparams(n_axes):
    return pltpu.CompilerParams(dimension_semantics=("arbitrary",) * n_axes, vmem_limit_bytes=VMEM_LIMIT)


def _silu(x):
    return x * jax.nn.sigmoid(x)


def _mod_body(c_ref, w_ref, b_ref, o_ref):
    c = c_ref[...]
    a = _silu(c).astype(BF16)
    o_ref[0] = jnp.dot(a, w_ref[0].astype(BF16), preferred_element_type=F32) + b_ref[0]


def _modulation(cond, w_ada, b_ada):
    rows = cond.shape[0]
    nblk = w_ada.shape[2] // D_MODEL
    return pl.pallas_call(
        _mod_body,
        grid=(DEPTH, nblk),
        in_specs=[pl.BlockSpec((rows, D_MODEL), lambda l, n: (0, 0)),
                  pl.BlockSpec((1, D_MODEL, D_MODEL), lambda l, n: (l, 0, n)),
                  pl.BlockSpec((1, 1, D_MODEL), lambda l, n: (l, 0, n))],
        out_specs=pl.BlockSpec((1, rows, D_MODEL), lambda l, n: (l, 0, n)),
        out_shape=jax.ShapeDtypeStruct((DEPTH, rows, w_ada.shape[2]), F32),
        compiler_params=_cparams(2),
        name="modulation",
    )(cond, w_ada, b_ada.reshape(DEPTH, 1, -1))


def _row_mod(mod, is_ctx, i):
    return jnp.where(is_ctx, mod[6 + i:7 + i], mod[i:i + 1])


def _norm_modulate(x, gain, mod, is_ctx, i_shift, i_scale):
    ms = jnp.mean(x * x, axis=-1, keepdims=True)
    xn = x * lax.rsqrt(ms + EPS) * gain
    return xn * (1.0 + _row_mod(mod, is_ctx, i_scale)) + _row_mod(mod, is_ctx, i_shift)


def _stream_specs(stream, ctx, tm, b_of, j_of):
    n_sub = tm // SUB_ROWS
    last_piece = stream.shape[1] // SUB_ROWS - 1
    specs = [pl.BlockSpec((1, SUB_ROWS, D_MODEL),
                          lambda *g, k=k: (b_of(g), jnp.minimum(j_of(g) * n_sub + k, last_piece), 0))
             for k in range(n_sub)]
    args = [stream] * n_sub
    if ctx is not None:
        assert ctx.shape[1] == SUB_ROWS and (stream.shape[1] + SUB_ROWS) % tm == 0
        specs.append(pl.BlockSpec((1, SUB_ROWS, D_MODEL), lambda *g: (b_of(g), 0, 0)))
        args.append(ctx)
    return specs, args


def _stream_piece(x_refs, k, n_sub, is_last_tile):
    x = x_refs[k][0]
    if len(x_refs) > n_sub and k == n_sub - 1:
        x = jnp.where(is_last_tile, x_refs[n_sub][0], x)
    return x


def _in_tile(piece, j, mod_ref, g_ref, w_ref, cos_ref, sin_ref, qkg_ref, bd_ref, o_ref, n_lat, tm):
    lane = lax.broadcasted_iota(jnp.int32, (1, LANES), 1)
    first_half = (lane % HEAD_DIM) < (HEAD_DIM // 2)
    for k, r0 in enumerate(range(0, tm, SUB_ROWS)):
        rows = slice(r0, r0 + SUB_ROWS)
        row = j * tm + r0 + lax.broadcasted_iota(jnp.int32, (SUB_ROWS, 1), 0)
        is_ctx = row >= n_lat
        x = piece(k, is_ctx)
        h = _norm_modulate(x, g_ref[...], mod_ref[0], is_ctx, 0, 1).astype(BF16)
        qk = jnp.dot(h, w_ref[:, :QK_COLS], preferred_element_type=F32)
        cos = cos_ref[rows, :]
        sin = sin_ref[rows, :]
        for c in range(QK_COLS // LANES):
            xc = qk[:, c * LANES:(c + 1) * LANES]
            ss = jnp.dot((xc * xc).astype(BF16), bd_ref[...], preferred_element_type=F32) * (1.0 / HEAD_DIM)
            xc = xc * lax.rsqrt(ss + EPS) * qkg_ref[c:c + 1, :]
            rot = jnp.where(first_half, pltpu.roll(xc, LANES - HEAD_DIM // 2, 1), pltpu.roll(xc, HEAD_DIM // 2, 1))
            o_ref[0, rows, c * LANES:(c + 1) * LANES] = (xc * cos + rot * sin).astype(BF16)
        o_ref[0, rows, QK_COLS:] = jnp.dot(h, w_ref[:, QK_COLS:], preferred_element_type=F32).astype(BF16)


def _in_body(*refs, n_lat, tm, n_x):
    x_refs = refs[:n_x]
    j = pl.program_id(0)
    is_last_tile = j == pl.num_programs(0) - 1
    _in_tile(lambda k, is_ctx: _stream_piece(x_refs, k, tm // SUB_ROWS, is_last_tile), j, *refs[n_x:], n_lat, tm)


def _in_proj(stream, ctx, modv, gain, w, cos, sin, qkg, bd, n_lat):
    B = stream.shape[0]
    T = stream.shape[1] + (0 if ctx is None else ctx.shape[1])
    tm = TOKEN_TILE
    x_specs, x_args = _stream_specs(stream, ctx, tm, lambda g: g[1], lambda g: g[0])
    return pl.pallas_call(
        functools.partial(_in_body, n_lat=n_lat, tm=tm, n_x=len(x_args)),
        grid=(T // tm, B),
        in_specs=x_specs + [
                  pl.BlockSpec((1, 16, D_MODEL), lambda j, b: (b, 0, 0)),
                  pl.BlockSpec((1, D_MODEL), lambda j, b: (0, 0)),
                  pl.BlockSpec((D_MODEL, N_PROJ), lambda j, b: (0, 0)),
                  pl.BlockSpec((tm, LANES), lambda j, b: (j, 0)),
                  pl.BlockSpec((tm, LANES), lambda j, b: (j, 0)),
                  pl.BlockSpec((8, LANES), lambda j, b: (0, 0)),
                  pl.BlockSpec((LANES, LANES), lambda j, b: (0, 0))],
        out_specs=pl.BlockSpec((1, tm, N_PROJ), lambda j, b: (b, j, 0)),
        out_shape=jax.ShapeDtypeStruct((B, T, N_PROJ), BF16),
        compiler_params=_cparams(2),
        name="in_proj",
    )(*x_args, modv, gain, w, cos, sin, qkg, bd)


ATT_BLOCK = 128
ATT_SPAN = ATT_BLOCK + 2 * WINDOW


ATT_HEAD_ORDER = (0, 4, 1, 5, 2, 6, 3, 7)


def _attend(qblk, k_parts, v_parts, biases, sinks):
    rows = ATT_GROUP * ATT_BLOCK
    rowi = lax.broadcasted_iota(jnp.int32, (rows, 1), 0)
    lane = lax.broadcasted_iota(jnp.int32, (1, LANES), 1)
    lower = lane < HEAD_DIM
    nt = (((1,), (1,)), ((), ()))
    heads = range(ATT_KV_HEADS)
    keep = [lower, jnp.logical_not(lower)]
    sink, scores = [], []
    for h in heads:
        qs = jnp.concatenate([jnp.where(keep[h], qblk[:, g * LANES:(g + 1) * LANES], jnp.zeros((), BF16))
                              for g in range(ATT_GROUP)], axis=0)
        col = jnp.full((rows, 1), sinks[ATT_GROUP * h + ATT_GROUP - 1], F32)
        for g in range(ATT_GROUP - 2, -1, -1):
            col = jnp.where(rowi < (g + 1) * ATT_BLOCK, sinks[ATT_GROUP * h + g], col)
        sink.append(col)
        pieces = []
        for k in k_parts:
            s = lax.dot_general(qs, k, nt, preferred_element_type=F32)
            pieces += [s[:, c * LANES:(c + 1) * LANES] for c in range(k.shape[0] // LANES)]
        scores.append([s if b is None else s + b for s, b in zip(pieces, biases)])
    top, probs = [], []
    for h in heads:
        m = scores[h][0]
        for s in scores[h][1:]:
            m = jnp.maximum(m, s)
        m = jnp.maximum(jnp.max(m, axis=-1, keepdims=True), sink[h])
        top.append(m)
        probs.append([jnp.exp((s - m).astype(BF16)) for s in scores[h]])
    normed = []
    for h in heads:
        acc = jnp.where(keep[h], 0.0, jnp.exp(sink[h] - top[h]))
        c0 = 0
        for v in v_parts[h]:
            n = v.shape[0] // LANES
            acc = acc + jnp.dot(jnp.concatenate(probs[h][c0:c0 + n], axis=1), v, preferred_element_type=F32)
            c0 += n
        normed.append(acc / pltpu.roll(acc, HEAD_DIM, 1))
    return jnp.concatenate([jnp.where(lower, normed[0][g * ATT_BLOCK:(g + 1) * ATT_BLOCK],
                                      normed[1][g * ATT_BLOCK:(g + 1) * ATT_BLOCK]) for g in range(ATT_GROUP)], axis=1)


def _att_body(sink_ref, q_ref, k_ref, v_ref, o_ref, v1_s, *, n_lat, n_ctx, with_ctx_out):
    sinks = [sink_ref[i] for i in range(ATT_HEADS)]
    lane = lax.broadcasted_iota(jnp.int32, (1, LANES), 1)
    vv = v_ref[0]
    v1_s[0] = jnp.where(lane < HEAD_DIM, vv, jnp.ones((), BF16))
    v1_s[1] = jnp.where(lane < HEAD_DIM, jnp.ones((), BF16), vv)
    k_ctx = k_ref[0, n_lat:n_lat + n_ctx, :]
    v_ctx = [v1_s[h, n_lat:n_lat + n_ctx, :] for h in range(ATT_KV_HEADS)]
    no_bias = [None] * (n_ctx // LANES)
    qi = lax.broadcasted_iota(jnp.int32, (ATT_GROUP * ATT_BLOCK, LANES), 0) % ATT_BLOCK
    ki = lax.broadcasted_iota(jnp.int32, (ATT_GROUP * ATT_BLOCK, LANES), 1)
    past_ok = jnp.where(ki >= qi, 0.0, NEG)
    ahead_ok = jnp.where(ki <= qi, 0.0, NEG)

    def block(q0, k0, n_keys, biases):
        k_parts = [k_ref[0, pl.ds(k0, n_keys), :], k_ctx]
        v_parts = [[v1_s[h, pl.ds(k0, n_keys), :], v_ctx[h]] for h in range(ATT_KV_HEADS)]
        out = _attend(q_ref[0, pl.ds(q0, ATT_BLOCK), :], k_parts, v_parts, biases + no_bias, sinks)
        o_ref[0, pl.ds(q0, ATT_BLOCK), :] = out.astype(BF16)

    def interior(i, carry):
        q0 = pl.multiple_of(i * ATT_BLOCK, ATT_BLOCK)
        block(q0, pl.multiple_of(q0 - WINDOW, ATT_BLOCK), ATT_SPAN, [past_ok, None, ahead_ok])
        return carry

    nq = n_lat // ATT_BLOCK
    block(0, 0, 2 * ATT_BLOCK, [None, ahead_ok])
    lax.fori_loop(1, nq - 1, interior, 0, unroll=7)
    block(n_lat - ATT_BLOCK, n_lat - 2 * ATT_BLOCK, 2 * ATT_BLOCK, [past_ok, None])
    if with_ctx_out:
        for c in range(n_ctx // ATT_BLOCK):
            r0 = n_lat + c * ATT_BLOCK
            out = _attend(q_ref[0, r0:r0 + ATT_BLOCK, :], [k_ctx], [[v_ctx[h]] for h in range(ATT_KV_HEADS)],
                          no_bias, sinks)
            o_ref[0, r0:r0 + ATT_BLOCK, :] = out.astype(BF16)
    else:
        o_ref[0, n_lat:, :] = jnp.zeros((n_ctx, ATT_WIDTH), BF16)


def _attention(p, sinks, n_lat, with_ctx_out):
    B, T, _ = p.shape
    return pl.pallas_call(
        functools.partial(_att_body, n_lat=n_lat, n_ctx=T - n_lat, with_ctx_out=with_ctx_out),
        grid=(B,),
        in_specs=[pl.BlockSpec(memory_space=pltpu.SMEM),
                  pl.BlockSpec((1, T, ATT_WIDTH), lambda b: (b, 0, COL_AQ // ATT_WIDTH)),
                  pl.BlockSpec((1, T, KV_WIDTH), lambda b: (b, 0, COL_AK // KV_WIDTH)),
                  pl.BlockSpec((1, T, KV_WIDTH), lambda b: (b, 0, COL_AV // KV_WIDTH))],
        out_specs=pl.BlockSpec((1, T, ATT_WIDTH), lambda b: (b, 0, 0)),
        out_shape=jax.ShapeDtypeStruct((B, T, ATT_WIDTH), BF16),
        scratch_shapes=[pltpu.VMEM((ATT_KV_HEADS, T, KV_WIDTH), BF16)],
        compiler_params=_cparams(1),
        name="attention",
    )(sinks, p, p, p)


GLA_BLOCK_A = 256
GLA_BLOCK_B = 128
CONV_PAD = 8
EPI_ROWS = 256


def _log_sigmoid(z):
    return jnp.minimum(z, 0.0) - jnp.log1p(jnp.exp(-jnp.abs(z)))


def _gla_body(gq_ref, gk_ref, gv_ref, gr_ref, cb_ref, cc_ref, ch_ref, gt_ref, wg_ref, gbias_ref, ng_ref,
              cw_ref, bd_ref, cum_ref, o_ref, qe_s, ke_s, dec_s, upd_s, prev_s, st_s, o_s, u_s, *, n_lat, n_ctx,
              with_ctx_out):
    T = n_lat + n_ctx
    C = GLA_CHUNK
    nt = (((1,), (1,)), ((), ()))
    tn = (((0,), (0,)), ((), ()))

    sr = lax.broadcasted_iota(jnp.int32, (GLA_WIDTH, 2 * GLA_QK_WIDTH), 0) // GLA_DV
    sl = (lax.broadcasted_iota(jnp.int32, (GLA_WIDTH, 2 * GLA_QK_WIDTH), 1) % GLA_QK_WIDTH) // GLA_DK
    state_mask2 = sr == sl

    def factors(i, carry):
        r0 = pl.multiple_of(i * GLA_BLOCK_A, GLA_BLOCK_A)
        rows = pl.ds(r0, GLA_BLOCK_A)
        gt = gt_ref[0, rows, :]
        q = gq_ref[0, rows, :].astype(F32) * (GLA_DK ** -0.5)
        k = gk_ref[0, rows, :].astype(F32)
        v = gv_ref[0, rows, :]
        z2 = jnp.dot(gt, wg_ref[...], preferred_element_type=F32) + gbias_ref[0:1, :]
        g2 = _log_sigmoid(z2) * (1.0 / GLA_GATE_NORM)
        cum = []
        for d in range(2):
            g = g2[:, d * GLA_QK_WIDTH:(d + 1) * GLA_QK_WIDTH]
            g_hi = g.astype(BF16)
            g_lo = (g - g_hi.astype(F32)).astype(BF16)
            cum.append(jnp.dot(cum_ref[d], jnp.concatenate([g_hi, g_lo], axis=1), preferred_element_type=F32))
        kl = []
        for d in range(2):
            b = cum[d][:GLA_BLOCK_A, :GLA_QK_WIDTH] + cum[d][:GLA_BLOCK_A, GLA_QK_WIDTH:]
            tot = cum[d][GLA_BLOCK_A:, :GLA_QK_WIDTH] + cum[d][GLA_BLOCK_A:, GLA_QK_WIDTH:]
            dec = jnp.exp(tot)
            ke = k * jnp.exp(-b)
            qe_s[d, rows, :] = (q * jnp.exp(b)).astype(BF16)
            ke_s[d, rows, :] = ke.astype(BF16)
            dec_s[d, rows, :] = dec
            kl.append((ke * dec).astype(BF16))
        kl2 = jnp.concatenate(kl, axis=1)
        for cc in range(GLA_BLOCK_A // C):
            upd = lax.dot_general(v[cc * C:(cc + 1) * C], kl2[cc * C:(cc + 1) * C], tn, preferred_element_type=F32)
            upd_s[i * (GLA_BLOCK_A // C) + cc] = jnp.where(state_mask2, upd, 0.0)
        return carry

    lax.fori_loop(0, T // GLA_BLOCK_A, factors, 0, unroll=True)

    nc_lat = n_lat // C
    nc_ctx = n_ctx // C
    st_s[...] = jnp.zeros_like(st_s)

    def scan(i, carry):
        in_ctx = i < nc_ctx
        cf = jnp.where(in_ctx, nc_lat + i, i - nc_ctx)
        cb = jnp.where(in_ctx, nc_lat + nc_ctx - 1 - i, nc_lat - 1 - (i - nc_ctx))
        for d, cid in enumerate((cf, cb)):
            st = st_s[d]
            prev_s[cid, :, d * GLA_QK_WIDTH:(d + 1) * GLA_QK_WIDTH] = st.astype(BF16)
            st_s[d] = (st * dec_s[d, pl.ds(pl.multiple_of(cid * C, C), 1), :]
                       + upd_s[cid, :, d * GLA_QK_WIDTH:(d + 1) * GLA_QK_WIDTH])
        return carry

    lax.fori_loop(0, nc_lat + nc_ctx, scan, 0)

    RB = GLA_BLOCK_B
    k_rows = lax.broadcasted_iota(jnp.int32, (GLA_HEADS * RB, GLA_QK_WIDTH), 0) // RB
    k_lanes = lax.broadcasted_iota(jnp.int32, (GLA_HEADS * RB, GLA_QK_WIDTH), 1) // GLA_DK
    key_heads = k_rows == k_lanes
    v_rows = lax.broadcasted_iota(jnp.int32, (GLA_HEADS * RB, GLA_WIDTH), 0) // RB
    v_lanes = lax.broadcasted_iota(jnp.int32, (GLA_HEADS * RB, GLA_WIDTH), 1) // GLA_DV
    value_heads = v_rows == v_lanes
    qr = lax.broadcasted_iota(jnp.int32, (RB, GLA_HEADS * RB), 0)
    kc = lax.broadcasted_iota(jnp.int32, (RB, GLA_HEADS * RB), 1) % RB
    same_chunk = (qr // C) == (kc // C)
    forward = kc <= qr
    zero = jnp.zeros((), BF16)

    def outputs(i, carry):
        r0 = pl.multiple_of(i * RB, RB)
        rows = pl.ds(r0, RB)
        v = gv_ref[0, rows, :]
        v4 = jnp.where(value_heads, jnp.concatenate([v] * GLA_HEADS, axis=0), zero)
        qe = [qe_s[d, rows, :] for d in range(2)]
        att = []
        for d in range(2):
            ke4 = jnp.where(key_heads, jnp.concatenate([ke_s[d, rows, :]] * GLA_HEADS, axis=0), zero)
            att.append(lax.dot_general(qe[d], ke4, nt, preferred_element_type=F32))
        both = jnp.where(same_chunk, jnp.where(forward, att[0], att[1]), 0.0).astype(BF16)
        qe2 = jnp.concatenate(qe, axis=1)
        inter = [lax.dot_general(qe2[cc * C:(cc + 1) * C], prev_s[i * (RB // C) + cc], nt,
                                 preferred_element_type=F32) for cc in range(RB // C)]
        o_s[rows, :] = jnp.dot(both, v4, preferred_element_type=F32) + jnp.concatenate(inter, axis=0)
        return carry

    out_rows = T if with_ctx_out else n_lat
    n_out = out_rows // RB
    lax.fori_loop(0, n_out, outputs, 0, unroll=6 if n_out % 6 == 0 else 4)
    if not with_ctx_out:
        o_ref[0, n_lat:, :] = jnp.zeros((n_ctx, GLA_WIDTH + CONV_WIDTH), BF16)

    u_s[0:CONV_PAD, :] = jnp.zeros((CONV_PAD, CONV_WIDTH), F32)
    u_s[CONV_PAD + T:, :] = jnp.zeros((CONV_PAD, CONV_WIDTH), F32)
    u_s[CONV_PAD:CONV_PAD + T, :] = cc_ref[0].astype(F32) * ch_ref[0].astype(F32)
    w0 = cw_ref[0:1, :]
    w1 = cw_ref[1:2, :]
    w2 = cw_ref[2:3, :]
    for e in range(out_rows // EPI_ROWS):
        r0 = e * EPI_ROWS
        o = o_s[r0:r0 + EPI_ROWS, :]
        ss = jnp.dot((o * o).astype(BF16), bd_ref[...], preferred_element_type=F32) * (1.0 / GLA_DV)
        on = o * lax.rsqrt(ss + EPS) * ng_ref[...]
        r = gr_ref[0, r0:r0 + EPI_ROWS, :].astype(F32)
        o_ref[0, r0:r0 + EPI_ROWS, 0:GLA_WIDTH] = (on * _silu(r)).astype(BF16)
        t = r0 + lax.broadcasted_iota(jnp.int32, (EPI_ROWS, 1), 0)
        up = u_s[CONV_PAD + r0 - 1:CONV_PAD + r0 - 1 + EPI_ROWS, :]
        mid = u_s[CONV_PAD + r0:CONV_PAD + r0 + EPI_ROWS, :]
        dn = u_s[CONV_PAD + r0 + 1:CONV_PAD + r0 + 1 + EPI_ROWS, :]
        up = jnp.where(t == n_lat, 0.0, up)
        dn = jnp.where(t == n_lat - 1, 0.0, dn)
        conv = w0 * up + w1 * mid + w2 * dn
        o_ref[0, r0:r0 + EPI_ROWS, GLA_WIDTH:] = (cb_ref[0, r0:r0 + EPI_ROWS, :].astype(F32) * conv).astype(BF16)


def _gla_cum_matrices():
    i = np.arange(GLA_BLOCK_A)
    same = (i[:, None] // GLA_CHUNK) == (i[None, :] // GLA_CHUNK)
    fwd = same & (i[None, :] <= i[:, None])
    bwd = same & (i[None, :] >= i[:, None])
    mats = np.stack([np.concatenate([fwd, same], axis=0), np.concatenate([bwd, same], axis=0)])
    return jnp.asarray(mats.astype(np.float32), dtype=BF16)


def _gla_conv(p, wg, gbias, ng, cw, bd, cum, n_lat, with_ctx_out):
    B, T, _ = p.shape
    nc = T // GLA_CHUNK

    def col(width, start):
        return pl.BlockSpec((1, T, width), lambda b: (b, 0, start // width))

    def const(shape):
        return pl.BlockSpec(shape, lambda b: (0,) * len(shape))

    return pl.pallas_call(
        functools.partial(_gla_body, n_lat=n_lat, n_ctx=T - n_lat, with_ctx_out=with_ctx_out),
        grid=(B,),
        in_specs=[col(GLA_QK_WIDTH, COL_GQ), col(GLA_QK_WIDTH, COL_GK), col(GLA_WIDTH, COL_GV), col(GLA_WIDTH, COL_GR),
                  col(CONV_WIDTH, COL_CB), col(CONV_WIDTH, COL_CC), col(CONV_WIDTH, COL_CH), col(LANES, COL_GT),
                  const((LANES, 2 * GLA_QK_WIDTH)), const((8, 2 * GLA_QK_WIDTH)),
                  const((1, GLA_WIDTH)), const((8, CONV_WIDTH)), const((GLA_WIDTH, GLA_WIDTH)),
                  const((2, 2 * GLA_BLOCK_A, GLA_BLOCK_A))],
        out_specs=pl.BlockSpec((1, T, GLA_WIDTH + CONV_WIDTH), lambda b: (b, 0, 0)),
        out_shape=jax.ShapeDtypeStruct((B, T, GLA_WIDTH + CONV_WIDTH), BF16),
        scratch_shapes=[pltpu.VMEM((2, T, GLA_QK_WIDTH), BF16), pltpu.VMEM((2, T, GLA_QK_WIDTH), BF16),
                        pltpu.VMEM((2, T, GLA_QK_WIDTH), F32),
                        pltpu.VMEM((nc, GLA_WIDTH, 2 * GLA_QK_WIDTH), F32),
                        pltpu.VMEM((nc, GLA_WIDTH, 2 * GLA_QK_WIDTH), BF16),
                        pltpu.VMEM((2, GLA_WIDTH, GLA_QK_WIDTH), F32),
                        pltpu.VMEM((T, GLA_WIDTH), F32),
                        pltpu.VMEM((T + 2 * CONV_PAD, CONV_WIDTH), F32)],
        compiler_params=_cparams(1),
        name="gla_conv",
    )(p, p, p, p, p, p, p, p, wg, gbias, ng, cw, bd, cum)


def _route(logits_t):
    mx = jnp.max(logits_t, axis=0, keepdims=True)
    ex = jnp.exp(logits_t - mx)
    probs = ex / jnp.sum(ex, axis=0, keepdims=True)
    P = [probs[e:e + 1] for e in range(N_EXPERTS)]
    scores = []
    for g in range(N_GROUPS):
        a, b, c, d = P[4 * g:4 * g + 4]
        scores.append(jnp.maximum(jnp.maximum(jnp.maximum(a + b, a + c), jnp.maximum(a + d, b + c)),
                                  jnp.maximum(b + d, c + d)))
    best = jnp.maximum(jnp.maximum(scores[0], scores[1]), jnp.maximum(scores[2], scores[3]))
    taken = jnp.zeros_like(best, dtype=jnp.bool_)
    sel = []
    for g in range(N_GROUPS):
        s = (scores[g] == best) & jnp.logical_not(taken)
        sel.append(s)
        taken = taken | s
    gsel = jnp.where(sel[1], 1.0, 0.0) + jnp.where(sel[2], 2.0, 0.0) + jnp.where(sel[3], 3.0, 0.0)
    ig = [jnp.where(sel[0], P[j], jnp.where(sel[1], P[4 + j], jnp.where(sel[2], P[8 + j], P[12 + j])))
          for j in range(EXPERTS_PER_GROUP)]

    def first_max(vals):
        v = jnp.maximum(jnp.maximum(vals[0], vals[1]), jnp.maximum(vals[2], vals[3]))
        tk = jnp.zeros_like(v, dtype=jnp.bool_)
        hot = []
        for x in vals:
            s = (x == v) & jnp.logical_not(tk)
            hot.append(s)
            tk = tk | s
        idx = jnp.where(hot[1], 1.0, 0.0) + jnp.where(hot[2], 2.0, 0.0) + jnp.where(hot[3], 3.0, 0.0)
        return v, hot, idx

    _, hot1, i1 = first_max(ig)
    _, _, i2 = first_max([jnp.where(hot1[j], -1.0, ig[j]) for j in range(EXPERTS_PER_GROUP)])
    lo = jnp.minimum(i1, i2)
    hi = jnp.maximum(i1, i2)
    pair = jnp.where(lo == 0.0, hi - 1.0, jnp.where(lo == 1.0, hi + 1.0, 5.0))
    return gsel * N_PAIRS + pair


def _class_rank(cls, tri_ref, cnt_s):
    n = cls.shape[1]
    cid = lax.broadcasted_iota(jnp.int32, (CLS_ROWS, n), 0).astype(F32)
    onehot = jnp.where(cls == cid, 1.0, 0.0)
    segs = [onehot[:, k * LANES:(k + 1) * LANES] for k in range(n // LANES)]
    before = jnp.dot(jnp.concatenate(segs, axis=0).astype(BF16), tri_ref[...], preferred_element_type=F32)
    base = cnt_s[...]
    ranks = []
    for k, seg in enumerate(segs):
        ranks.append(jnp.sum(seg * (before[k * CLS_ROWS:(k + 1) * CLS_ROWS] + base), axis=0, keepdims=True))
        base = base + jnp.sum(seg, axis=1, keepdims=True)
    cnt_s[...] = base
    return jnp.concatenate(ranks, axis=1)


def _out_body(*refs, n_lat, tm, n_x):
    x_refs = refs[:n_x]
    ya_ref, yg_ref, mod_ref, wo_ref, g_ref, wr_ref, br_ref, tri_ref, xo_ref, rt_ref, cnt_ref, cnt_s, logit_s = refs[n_x:]
    j = pl.program_id(1)
    is_last_tile = j == pl.num_programs(1) - 1

    @pl.when((pl.program_id(0) == 0) & (j == 0))
    def _():
        cnt_s[...] = jnp.zeros_like(cnt_s)

    mod = mod_ref[0]
    subs = list(enumerate(range(0, tm, SUB_ROWS)))
    ys = [jnp.dot(ya_ref[0, r0:r0 + SUB_ROWS, :], wo_ref[0:ATT_WIDTH, :], preferred_element_type=F32)
          + jnp.dot(yg_ref[0, r0:r0 + SUB_ROWS, :], wo_ref[ATT_WIDTH:, :], preferred_element_type=F32) for _, r0 in subs]
    h2s = []
    for k, r0 in subs:
        row = j * tm + r0 + lax.broadcasted_iota(jnp.int32, (SUB_ROWS, 1), 0)
        is_ctx = row >= n_lat
        xn = _stream_piece(x_refs, k, tm // SUB_ROWS, is_last_tile) + _row_mod(mod, is_ctx, 2) * ys[k]
        xo_ref[0, r0:r0 + SUB_ROWS, :] = xn
        h2s.append(_norm_modulate(xn, g_ref[...], mod, is_ctx, 3, 4).astype(BF16))
    for k, r0 in subs:
        logit_s[r0:r0 + SUB_ROWS, :] = jnp.dot(h2s[k], wr_ref[...], preferred_element_type=F32) + br_ref[0:1, :]
    cls = _route(logit_s[...].T[0:N_EXPERTS, :])
    rank = _class_rank(cls, tri_ref, cnt_s)
    rt_ref[0] = jnp.concatenate([cls, rank, jnp.zeros((6, tm), F32)], axis=0).astype(jnp.int32)
    cnt_ref[...] = jnp.broadcast_to(cnt_s[...], (CLS_ROWS, LANES)).astype(jnp.int32)


def _out_proj(ya, yg, stream, ctx, modv, wo, gain, wr, br, tri, n_lat, rows, tm):
    B = stream.shape[0]
    nj = rows // tm
    x_specs, x_args = _stream_specs(stream, ctx, tm, lambda g: g[0], lambda g: g[1])
    return pl.pallas_call(
        functools.partial(_out_body, n_lat=n_lat, tm=tm, n_x=len(x_args)),
        grid=(B, nj),
        in_specs=x_specs + [
                  pl.BlockSpec((1, tm, ATT_WIDTH), lambda b, j: (b, j, 0)),
                  pl.BlockSpec((1, tm, GLA_WIDTH + CONV_WIDTH), lambda b, j: (b, j, 0)),
                  pl.BlockSpec((1, 16, D_MODEL), lambda b, j: (b, 0, 0)),
                  pl.BlockSpec((D_MODEL, D_MODEL), lambda b, j: (0, 0)),
                  pl.BlockSpec((1, D_MODEL), lambda b, j: (0, 0)),
                  pl.BlockSpec((D_MODEL, LANES), lambda b, j: (0, 0)),
                  pl.BlockSpec((8, LANES), lambda b, j: (0, 0)),
                  pl.BlockSpec((LANES, LANES), lambda b, j: (0, 0))],
        out_specs=[pl.BlockSpec((1, tm, D_MODEL), lambda b, j: (b, j, 0)),
                   pl.BlockSpec((1, 8, tm), lambda b, j: (b * nj + j, 0, 0)),
                   pl.BlockSpec((CLS_ROWS, LANES), lambda b, j: (0, 0))],
        out_shape=[jax.ShapeDtypeStruct((B, rows, D_MODEL), F32),
                   jax.ShapeDtypeStruct((B * nj, 8, tm), jnp.int32),
                   jax.ShapeDtypeStruct((CLS_ROWS, LANES), jnp.int32)],
        scratch_shapes=[pltpu.VMEM((CLS_ROWS, 1), F32), pltpu.VMEM((tm, LANES), F32)],
        compiler_params=_cparams(2),
        name="out_proj_router",
    )(*x_args, ya, yg, modv, wo, gain, wr, br, tri)


ROW_UNROLL = 8
IDX_STRIDE = 1024


def _idx_slot(idx, slot, tm):
    return idx.at[pl.ds(pl.multiple_of(slot * IDX_STRIDE, IDX_STRIDE), tm)]


def _issue_rows(tm, idx, slot, make_copy):
    base = slot * IDX_STRIDE

    def trip(i, c):
        for u in range(ROW_UNROLL):
            r = i * ROW_UNROLL + u
            make_copy(r, idx[base + r]).start(priority=u % 2)
        return c

    lax.fori_loop(0, tm // ROW_UNROLL, trip, 0)


PAD_SLOTS = 32
PAD_BITS = 9


def _zero_pad_rows(pad_ref, zeros_ref, hs_ref, sem):
    def pieces(c, fn):
        start = pad_ref[c]
        n = pad_ref[PAD_SLOTS + c]
        for bit in range(PAD_BITS):
            size = 1 << bit
            below = n & (size - 1)

            @pl.when((n & size) != 0)
            def _():
                fn(pltpu.make_async_copy(zeros_ref.at[pl.ds(0, size)], hs_ref.at[pl.ds(start + below, size)], sem))

        def block(i, carry):
            fn(pltpu.make_async_copy(zeros_ref, hs_ref.at[pl.ds(start + n + i * MOE_TILE, MOE_TILE)], sem))
            return carry

        lax.fori_loop(0, pad_ref[2 * PAD_SLOTS + c], block, 0)

    def start_all(c, carry):
        pieces(c, lambda cp: cp.start())
        return carry

    def wait_all(c, carry):
        pieces(c, lambda cp: cp.wait())
        return carry

    lax.fori_loop(0, N_CLASSES + 1, start_all, 0)
    lax.fori_loop(0, N_CLASSES + 1, wait_all, 0)


def _issue_rows_inline(tm, idx, slot, make_copy):
    base = slot * IDX_STRIDE
    for r in range(tm):
        make_copy(r, idx[base + r]).start(priority=r % 2)


def _disp_body(pad_ref, x_ref, mod_ref, g_ref, dest_ref, hs_ref, buf0, buf1, buf2, idx, sem_i, sem_d, sem_z, *, n_lat,
               tm, nj, n_steps):
    bufs = (buf0, buf1, buf2)
    j = pl.program_id(1)
    s = pl.program_id(0) * nj + j
    last = n_steps - 1

    def idx_fetch(step):
        return pltpu.make_async_copy(dest_ref.at[step], _idx_slot(idx, step % 3, tm), sem_i.at[step % 3])

    def drain(k):
        pltpu.make_async_copy(bufs[k], hs_ref.at[pl.ds(0, tm)], sem_d.at[k]).wait()

    def row_copy(k):
        return lambda r, d: pltpu.make_async_copy(bufs[k].at[r], hs_ref.at[d], sem_d.at[k])

    def normalise(k):
        row = j * tm + lax.broadcasted_iota(jnp.int32, (tm, 1), 0)
        h2 = _norm_modulate(x_ref[0], g_ref[...], mod_ref[0], row >= n_lat, 3, 4)
        bufs[k][...] = h2.reshape(tm, 8, LANES)

    @pl.when(s == 0)
    def _():
        idx_fetch(0).start()
        buf1[...] = jnp.zeros((tm, 8, LANES), F32)
        _zero_pad_rows(pad_ref, buf1.at[pl.ds(0, MOE_TILE)], hs_ref, sem_z)
        normalise(0)

    @pl.when(s < last)
    def _():
        idx_fetch(s + 1).start()

    idx_fetch(s).wait()

    for k in range(3):
        prev = (k + 2) % 3

        @pl.when((s > 0) & (s % 3 == k))
        def _():
            _issue_rows_inline(tm, idx, prev, row_copy(prev))
            normalise(k)

        @pl.when((s > 1) & (s % 3 == k))
        def _():
            drain((k + 1) % 3)

        @pl.when((s == last) & (s % 3 == k))
        def _():
            _issue_rows(tm, idx, k, row_copy(k))
            drain(k)
            if last > 0:
                drain(prev)


def _dispatch(pad, xx, modv, gain, dest, n_rows, n_lat, tm):
    B, rows, _ = xx.shape
    nj = rows // tm
    assert tm >= MOE_TILE
    return pl.pallas_call(
        functools.partial(_disp_body, n_lat=n_lat, tm=tm, nj=nj, n_steps=B * nj),
        grid=(B, nj),
        in_specs=[pl.BlockSpec(memory_space=pltpu.SMEM),
                  pl.BlockSpec((1, tm, D_MODEL), lambda b, j: (b, j, 0)),
                  pl.BlockSpec((1, 16, D_MODEL), lambda b, j: (b, 0, 0)),
                  pl.BlockSpec((1, D_MODEL), lambda b, j: (0, 0)),
                  pl.BlockSpec(memory_space=pl.ANY)],
        out_specs=pl.BlockSpec(memory_space=pl.ANY),
        scratch_shapes=[pltpu.VMEM((tm, 8, LANES), F32), pltpu.VMEM((tm, 8, LANES), F32),
                        pltpu.VMEM((tm, 8, LANES), F32), pltpu.SMEM((3 * IDX_STRIDE,), jnp.int32),
                        pltpu.SemaphoreType.DMA((3,)), pltpu.SemaphoreType.DMA((3,)), pltpu.SemaphoreType.DMA],
        out_shape=jax.ShapeDtypeStruct((n_rows, 8, LANES), F32),
        compiler_params=_cparams(2),
        name="dispatch",
    )(pad, xx, modv, gain, dest)


def _moe_body(tile_ref, e_lo_ref, e_hi_ref, valid_ref, hs_ref, wg1, wu1, wd1, wg2, wu2, wd2, wr_ref, br_ref, ys_ref):
    g = pl.program_id(0)
    tm = hs_ref.shape[0]

    @pl.when(valid_ref[g] == 1)
    def _():
        x = hs_ref[...].reshape(tm, D_MODEL)
        e_lo = e_lo_ref[g]
        e_hi = e_hi_ref[g]
        dw = wr_ref[pl.ds(e_lo, 1), :] - wr_ref[pl.ds(e_hi, 1), :]
        d = jnp.sum(x * dw, axis=-1, keepdims=True) + (br_ref[e_lo] - br_ref[e_hi])
        w_lo = jax.nn.sigmoid(d)
        w_hi = jax.nn.sigmoid(-d)
        h = x.astype(BF16)

        def act(wg, wu, w):
            a = _silu(jnp.dot(h, wg[0], preferred_element_type=F32)) * jnp.dot(h, wu[0], preferred_element_type=F32)
            return (a * w).astype(BF16)

        y = (jnp.dot(act(wg1, wu1, w_lo), wd1[0], preferred_element_type=F32)
             + jnp.dot(act(wg2, wu2, w_hi), wd2[0], preferred_element_type=F32))
        ys_ref[...] = y.reshape(tm, 8, LANES)

    @pl.when(valid_ref[g] == 0)
    def _():
        ys_ref[...] = jnp.zeros_like(ys_ref)


def _moe(hs, tile, e_lo, e_hi, valid, wg, wu, wd, wr_t, br, first_expert):
    n_tiles = tile.shape[0]
    tm = MOE_TILE

    def w_in(sel):
        return pl.BlockSpec((1, D_MODEL, D_EXPERT), lambda g, t, lo, hi, v: (first_expert + (lo, hi)[sel][g], 0, 0))

    def w_out(sel):
        return pl.BlockSpec((1, D_EXPERT, D_MODEL), lambda g, t, lo, hi, v: (first_expert + (lo, hi)[sel][g], 0, 0))

    return pl.pallas_call(
        _moe_body,
        grid_spec=pltpu.PrefetchScalarGridSpec(
            num_scalar_prefetch=4,
            grid=(n_tiles,),
            in_specs=[pl.BlockSpec((tm, 8, LANES), lambda g, t, lo, hi, v: (t[g], 0, 0)),
                      w_in(0), w_in(0), w_out(0), w_in(1), w_in(1), w_out(1),
                      pl.BlockSpec((N_EXPERTS, D_MODEL), lambda g, t, lo, hi, v: (0, 0)),
                      pl.BlockSpec(memory_space=pltpu.SMEM)],
            out_specs=pl.BlockSpec((tm, 8, LANES), lambda g, t, lo, hi, v: (g, 0, 0))),
        out_shape=jax.ShapeDtypeStruct((n_tiles * tm, 8, LANES), F32),
        compiler_params=_cparams(1),
        name="moe_pairs",
    )(tile, e_lo, e_hi, valid, hs, wg, wu, wd, wg, wu, wd, wr_t, br)


def _fin_body(x_ref, mod_ref, dest_ref, ys_ref, xo_ref, buf0, buf1, buf2, idx, sem_i, sem_d, *, n_lat, tm, nj,
              n_steps):
    bufs = (buf0, buf1, buf2)
    j = pl.program_id(1)
    s = pl.program_id(0) * nj + j
    last = n_steps - 1

    def idx_fetch(step):
        return pltpu.make_async_copy(dest_ref.at[step], _idx_slot(idx, step % 3, tm), sem_i.at[step % 3])

    def row_copy(k):
        return lambda r, d: pltpu.make_async_copy(ys_ref.at[d], bufs[k].at[r], sem_d.at[k])

    def add_rows(k):
        row = j * tm + lax.broadcasted_iota(jnp.int32, (tm, 1), 0)
        xo_ref[0] = x_ref[0] + _row_mod(mod_ref[0], row >= n_lat, 5) * bufs[k][...].reshape(tm, D_MODEL)

    @pl.when(s == 0)
    def _():
        for t in range(min(2, n_steps)):
            idx_fetch(t).start()
            idx_fetch(t).wait()
            _issue_rows(tm, idx, t, row_copy(t))
        if last >= 2:
            idx_fetch(2).start()

    for k in range(3):
        ahead = (k + 2) % 3

        @pl.when((s % 3 == k) & (s + 2 <= last))
        def _():
            pltpu.make_async_copy(ys_ref.at[pl.ds(0, tm)], bufs[k], sem_d.at[k]).wait()
            idx_fetch(s + 2).wait()
            _issue_rows_inline(tm, idx, ahead, row_copy(ahead))
            add_rows(k)

        @pl.when((s % 3 == k) & (s + 2 > last))
        def _():
            pltpu.make_async_copy(ys_ref.at[pl.ds(0, tm)], bufs[k], sem_d.at[k]).wait()
            add_rows(k)

    @pl.when(s + 3 <= last)
    def _():
        idx_fetch(s + 3).start()


def _combine(xx, modv, dest, ys, n_lat, tm):
    B, rows, _ = xx.shape
    nj = rows // tm
    return pl.pallas_call(
        functools.partial(_fin_body, n_lat=n_lat, tm=tm, nj=nj, n_steps=B * nj),
        grid=(B, nj),
        in_specs=[pl.BlockSpec((1, tm, D_MODEL), lambda b, j: (b, j, 0)),
                  pl.BlockSpec((1, 16, D_MODEL), lambda b, j: (b, 0, 0)),
                  pl.BlockSpec(memory_space=pl.ANY),
                  pl.BlockSpec(memory_space=pl.ANY)],
        out_specs=pl.BlockSpec((1, tm, D_MODEL), lambda b, j: (b, j, 0)),
        scratch_shapes=[pltpu.VMEM((tm, 8, LANES), F32), pltpu.VMEM((tm, 8, LANES), F32),
                        pltpu.VMEM((tm, 8, LANES), F32), pltpu.SMEM((3 * IDX_STRIDE,), jnp.int32),
                        pltpu.SemaphoreType.DMA((3,)), pltpu.SemaphoreType.DMA((3,))],
        out_shape=jax.ShapeDtypeStruct((B, rows, D_MODEL), F32),
        compiler_params=_cparams(2),
        name="combine",
    )(xx, modv, dest, ys)


def _rope_tables(n_lat, n_ctx):
    rows = n_lat // GRID_W
    row, col = jnp.meshgrid(jnp.arange(rows), jnp.arange(GRID_W), indexing="ij")
    n_freq = HEAD_DIM // 4
    inv_freq = ROPE_BASE ** (-jnp.arange(n_freq, dtype=F32) / n_freq)
    ang = jnp.concatenate([row.reshape(-1, 1).astype(F32) * inv_freq, col.reshape(-1, 1).astype(F32) * inv_freq],
                          axis=-1)
    cos = jnp.tile(jnp.cos(ang), (1, LANES // (HEAD_DIM // 2)))
    sin = jnp.tile(jnp.sin(ang), (1, LANES // (HEAD_DIM // 2)))
    sign = jnp.where((jnp.arange(LANES) % HEAD_DIM) < HEAD_DIM // 2, -1.0, 1.0).astype(F32)
    cos = jnp.concatenate([cos, jnp.ones((n_ctx, LANES), F32)], axis=0)
    sin = jnp.concatenate([sin * sign, jnp.zeros((n_ctx, LANES), F32)], axis=0)
    return cos, sin


def _block_diag_ones(n, blk):
    i = np.arange(n) // blk
    return jnp.asarray((i[:, None] == i[None, :]).astype(np.float32), dtype=BF16)


def _routing_tables(counts, n_tiles):
    tiles_c = (counts + MOE_TILE - 1) // MOE_TILE
    tile_end = jnp.cumsum(tiles_c)
    tile_start = tile_end - tiles_c
    off = tile_start * MOE_TILE
    total = tile_end[-1]
    g = jnp.arange(n_tiles, dtype=jnp.int32)
    valid = (g < total).astype(jnp.int32)
    g_eff = jnp.minimum(g, total - 1)
    c_of = jnp.sum((g_eff[:, None] >= tile_end[None, :]).astype(jnp.int32), axis=1)
    group = c_of // N_PAIRS
    pair = c_of % N_PAIRS
    lo = jnp.asarray(PAIR_LO, jnp.int32)
    hi = jnp.asarray(PAIR_HI, jnp.int32)
    e_lo = group * EXPERTS_PER_GROUP + jnp.sum((pair[:, None] == jnp.arange(N_PAIRS)[None, :]) * lo[None, :], axis=1)
    e_hi = group * EXPERTS_PER_GROUP + jnp.sum((pair[:, None] == jnp.arange(N_PAIRS)[None, :]) * hi[None, :], axis=1)
    fill = jnp.zeros((PAD_SLOTS - N_CLASSES - 1,), jnp.int32)
    pad = jnp.concatenate([off + counts, (total * MOE_TILE)[None], fill,
                           tiles_c * MOE_TILE - counts, jnp.zeros((1,), jnp.int32), fill,
                           jnp.zeros((N_CLASSES,), jnp.int32), (n_tiles - total)[None], fill])
    return off.astype(jnp.int32), pad.astype(jnp.int32), g_eff, e_lo.astype(jnp.int32), e_hi.astype(jnp.int32), valid


def kernel(x, c, ctx, c_ctx, w_ada, b_ada, norm_mix_g, norm_ffn_g, w_in, q_norm_g, k_norm_g, attn_sink, gla_gate_w,
           gla_gate_b, gla_norm_g, conv_w, w_out, w_router, b_router, w_gate_e, w_up_e, w_down_e):
    B, S, D = x.shape
    L = ctx.shape[1]
    T = S + L
    assert D == D_MODEL and T % TOKEN_TILE == 0 and S % LAT_TILE == 0 and S % GRID_W == 0
    assert S % ATT_BLOCK == 0 and L % ATT_BLOCK == 0 and S >= ATT_SPAN and T % EPI_ROWS == 0

    cond_rows = -(-(B + 1) // 8) * 8
    cond = jnp.zeros((cond_rows, D), F32).at[:B].set(c).at[B].set(c_ctx)
    mod_all = _modulation(cond, w_ada, b_ada)

    cos, sin = _rope_tables(S, L)
    bd_head = _block_diag_ones(LANES, HEAD_DIM)
    bd_gla = _block_diag_ones(GLA_WIDTH, GLA_DV)
    cum = _gla_cum_matrices()
    tri = jnp.asarray(np.triu(np.ones((LANES, LANES), np.float32), 1), dtype=BF16)
    stream, stream_ctx = x, ctx

    order = jnp.asarray(ATT_HEAD_ORDER)
    wg_all = w_gate_e.reshape(DEPTH * N_EXPERTS, D, D_EXPERT).astype(BF16)
    wu_all = w_up_e.reshape(DEPTH * N_EXPERTS, D, D_EXPERT).astype(BF16)
    wd_all = w_down_e.reshape(DEPTH * N_EXPERTS, D_EXPERT, D).astype(BF16)

    def mod_table(l):
        m_lat = mod_all[l, :B].reshape(B, 6, D)
        m_ctx = jnp.broadcast_to(mod_all[l, B].reshape(1, 6, D), (B, 6, D))
        return jnp.concatenate([m_lat, m_ctx, jnp.zeros((B, 4, D), F32)], axis=1)

    def in_proj_args(l):
        wl = w_in[l]
        wq = wl[:, :ATT_WIDTH].reshape(D, ATT_HEADS, HEAD_DIM)[:, order, :].reshape(D, ATT_WIDTH)
        w_perm = jnp.concatenate([wq, wl[:, ATT_WIDTH:1536], wl[:, 1568:], wl[:, 1536:1568],
                                  jnp.zeros((D, N_PROJ - wl.shape[1]), F32)], axis=1).astype(BF16)
        qg = jnp.tile(q_norm_g[l], LANES // HEAD_DIM) * (HEAD_DIM ** -0.5)
        kg = jnp.tile(k_norm_g[l], LANES // HEAD_DIM)
        qkg = jnp.stack([qg] * (ATT_WIDTH // LANES) + [kg] + [jnp.zeros_like(kg)] * 3)
        return mod_table(l), norm_mix_g[l].reshape(1, D), w_perm, cos, sin, qkg, bd_head

    p = _in_proj(stream, stream_ctx, *in_proj_args(0), S)
    for l in range(DEPTH):
        last = l == DEPTH - 1
        modv = mod_table(l)

        y_att = _attention(p, attn_sink[l], S, not last)

        pad_rows = jnp.zeros((LANES - 2 * GLA_GATE_RANK, GLA_QK_WIDTH), F32)
        zero_rank = jnp.zeros((GLA_GATE_RANK, GLA_QK_WIDTH), F32)
        wgf = jnp.concatenate([gla_gate_w[l, 0], zero_rank, pad_rows], axis=0)
        wgb = jnp.concatenate([zero_rank, gla_gate_w[l, 1], pad_rows], axis=0)
        wg = jnp.concatenate([wgf, wgb], axis=1).astype(BF16)
        gbias = jnp.concatenate([gla_gate_b[l].reshape(1, 2 * GLA_QK_WIDTH),
                                 jnp.zeros((7, 2 * GLA_QK_WIDTH), F32)], axis=0)
        ng = jnp.tile(gla_norm_g[l], GLA_HEADS).reshape(1, GLA_WIDTH)
        cw = jnp.concatenate([conv_w[l], jnp.zeros((5, CONV_WIDTH), F32)], axis=0)
        y_gc = _gla_conv(p, wg, gbias, ng, cw, bd_gla, cum, S, not last)

        rows, tm = (S, LAT_TILE) if last else (T, TOKEN_TILE)
        wr = jnp.concatenate([w_router, jnp.zeros((D, LANES - N_EXPERTS), F32)], axis=1).astype(BF16)
        br = jnp.zeros((8, LANES), F32).at[0, :N_EXPERTS].set(b_router)
        ffn_g = norm_ffn_g[l].reshape(1, D)
        wo_att = w_out[l, :ATT_WIDTH].reshape(ATT_HEADS, HEAD_DIM, D)[order].reshape(ATT_WIDTH, D)
        wo = jnp.concatenate([wo_att, w_out[l, ATT_WIDTH:]], axis=0).astype(BF16)
        xx_mid, route, counts = _out_proj(y_att, y_gc, stream, stream_ctx, modv, wo, ffn_g, wr, br, tri, S, rows, tm)

        n_tiles = -(-(B * rows) // MOE_TILE) + N_CLASSES
        off, pad, tile, e_lo, e_hi, valid = _routing_tables(counts[:N_CLASSES, 0], n_tiles)

        dest = route[:, 1, :]
        for cls_id in range(N_CLASSES):
            dest = dest + jnp.where(route[:, 0, :] == cls_id, off[cls_id], 0)
        hs = _dispatch(pad, xx_mid, modv, ffn_g, dest, n_tiles * MOE_TILE, S, tm)
        ys = _moe(hs, tile, e_lo, e_hi, valid, wg_all, wu_all, wd_all, w_router.T, b_router, l * N_EXPERTS)
        stream, stream_ctx = _combine(xx_mid, modv, dest, ys, S, tm), None
        if not last:
            p = _in_proj(stream, None, *in_proj_args(l + 1), S)
    return stream
```

```python
import functools

import numpy as np
import jax
import jax.numpy as jnp
from jax import lax
from jax.experimental import pallas as pl
from jax.experimental.pallas import tpu as pltpu

D_MODEL = 1024
DEPTH = 2
GRID_W = 64
EPS = 1e-6
HEAD_DIM = 64
ATT_HEADS = 8
ATT_KV_HEADS = 2
ATT_GROUP = ATT_HEADS // ATT_KV_HEADS
ATT_WIDTH = ATT_HEADS * HEAD_DIM
WINDOW = 128
ROPE_BASE = 10000.0
GLA_HEADS = 4
GLA_DV = 64
GLA_DK = 32
GLA_WIDTH = GLA_HEADS * GLA_DV
GLA_GATE_RANK = 16
GLA_GATE_NORM = 16.0
GLA_CHUNK = 64
CONV_WIDTH = 256
N_EXPERTS = 16
N_GROUPS = 4
EXPERTS_PER_GROUP = 4
D_EXPERT = D_MODEL // 2

LANES = 128
KV_WIDTH = ATT_KV_HEADS * HEAD_DIM
GLA_QK_WIDTH = GLA_HEADS * GLA_DK
COL_AQ, COL_AK, COL_AV = 0, 512, 640
COL_GQ, COL_GK, COL_GV, COL_GR = 768, 896, 1024, 1280
COL_CB, COL_CC, COL_CH, COL_GT = 1536, 1792, 2048, 2304
N_PROJ = 2432
SRC_GATES = ATT_WIDTH + 2 * KV_WIDTH + 2 * GLA_QK_WIDTH + 2 * GLA_WIDTH
SRC_CONV = SRC_GATES + 2 * GLA_GATE_RANK
QK_COLS = COL_AV
N_PAIRS = 6
N_CLASSES = N_GROUPS * N_PAIRS
PAIR_LO = (0, 0, 0, 1, 1, 2)
PAIR_HI = (1, 2, 3, 2, 3, 3)
CLS_ROWS = 32
NEG = -1e30

TOKEN_TILE = 768
LAT_TILE = 1024
MOE_TILE = 512
SUB_ROWS = 256
VMEM_LIMIT = 56 * 1024 * 1024

F32 = jnp.float32
BF16 = jnp.bfloat16


def _cparams(n_axes):
    return pltpu.CompilerParams(dimension_semantics=("arbitrary",) * n_axes, vmem_limit_bytes=VMEM_LIMIT)


def _silu(x):
    return x * jax.nn.sigmoid(x)


def _mod_body(c_ref, w_ref, b_ref, o_ref):
    c = c_ref[...]
    a = _silu(c).astype(BF16)
    o_ref[0] = jnp.dot(a, w_ref[0].astype(BF16), preferred_element_type=F32) + b_ref[0]


def _modulation(cond, w_ada, b_ada):
    rows = cond.shape[0]
    nblk = w_ada.shape[2] // D_MODEL
    return pl.pallas_call(
        _mod_body,
        grid=(DEPTH, nblk),
        in_specs=[pl.BlockSpec((rows, D_MODEL), lambda l, n: (0, 0)),
                  pl.BlockSpec((1, D_MODEL, D_MODEL), lambda l, n: (l, 0, n)),
                  pl.BlockSpec((1, 1, D_MODEL), lambda l, n: (l, 0, n))],
        out_specs=pl.BlockSpec((1, rows, D_MODEL), lambda l, n: (l, 0, n)),
        out_shape=jax.ShapeDtypeStruct((DEPTH, rows, w_ada.shape[2]), F32),
        compiler_params=_cparams(2),
        name="modulation",
    )(cond, w_ada, b_ada.reshape(DEPTH, 1, -1))


def _row_mod(mod, is_ctx, i):
    return jnp.where(is_ctx, mod[6 + i:7 + i], mod[i:i + 1])


def _norm_modulate(x, gain, mod, is_ctx, i_shift, i_scale):
    ms = jnp.mean(x * x, axis=-1, keepdims=True)
    xn = x * lax.rsqrt(ms + EPS) * gain
    return xn * (1.0 + _row_mod(mod, is_ctx, i_scale)) + _row_mod(mod, is_ctx, i_shift)


def _stream_specs(stream, ctx, tm, b_of, j_of):
    n_sub = tm // SUB_ROWS
    last_piece = stream.shape[1] // SUB_ROWS - 1
    specs = [pl.BlockSpec((1, SUB_ROWS, D_MODEL),
                          lambda *g, k=k: (b_of(g), jnp.minimum(j_of(g) * n_sub + k, last_piece), 0))
             for k in range(n_sub)]
    args = [stream] * n_sub
    if ctx is not None:
        assert ctx.shape[1] == SUB_ROWS and (stream.shape[1] + SUB_ROWS) % tm == 0
        specs.append(pl.BlockSpec((1, SUB_ROWS, D_MODEL), lambda *g: (b_of(g), 0, 0)))
        args.append(ctx)
    return specs, args


def _stream_piece(x_refs, k, n_sub, is_last_tile):
    x = x_refs[k][0]
    if len(x_refs) > n_sub and k == n_sub - 1:
        x = jnp.where(is_last_tile, x_refs[n_sub][0], x)
    return x


def _in_tile(piece, j, mod_ref, g_ref, w_ref, cos_ref, sin_ref, qkg_ref, bd_ref, o_ref, n_lat, tm):
    lane = lax.broadcasted_iota(jnp.int32, (1, LANES), 1)
    first_half = (lane % HEAD_DIM) < (HEAD_DIM // 2)
    for k, r0 in enumerate(range(0, tm, SUB_ROWS)):
        rows = slice(r0, r0 + SUB_ROWS)
        row = j * tm + r0 + lax.broadcasted_iota(jnp.int32, (SUB_ROWS, 1), 0)
        is_ctx = row >= n_lat
        x = piece(k, is_ctx)
        h = _norm_modulate(x, g_ref[...], mod_ref[0], is_ctx, 0, 1).astype(BF16)
        qk = jnp.dot(h, w_ref[:, :QK_COLS], preferred_element_type=F32)
        cos = cos_ref[rows, :]
        sin = sin_ref[rows, :]
        for c in range(QK_COLS // LANES):
            xc = qk[:, c * LANES:(c + 1) * LANES]
            ss = jnp.dot((xc * xc).astype(BF16), bd_ref[...], preferred_element_type=F32) * (1.0 / HEAD_DIM)
            xc = xc * lax.rsqrt(ss + EPS) * qkg_ref[c:c + 1, :]
            rot = jnp.where(first_half, pltpu.roll(xc, LANES - HEAD_DIM // 2, 1), pltpu.roll(xc, HEAD_DIM // 2, 1))
            o_ref[0, rows, c * LANES:(c + 1) * LANES] = (xc * cos + rot * sin).astype(BF16)
        o_ref[0, rows, QK_COLS:] = jnp.dot(h, w_ref[:, QK_COLS:], preferred_element_type=F32).astype(BF16)


def _in_body(*refs, n_lat, tm, n_x):
    x_refs = refs[:n_x]
    j = pl.program_id(0)
    is_last_tile = j == pl.num_programs(0) - 1
    _in_tile(lambda k, is_ctx: _stream_piece(x_refs, k, tm // SUB_ROWS, is_last_tile), j, *refs[n_x:], n_lat, tm)


def _in_proj(stream, ctx, modv, gain, w, cos, sin, qkg, bd, n_lat):
    B = stream.shape[0]
    T = stream.shape[1] + (0 if ctx is None else ctx.shape[1])
    tm = TOKEN_TILE
    x_specs, x_args = _stream_specs(stream, ctx, tm, lambda g: g[1], lambda g: g[0])
    return pl.pallas_call(
        functools.partial(_in_body, n_lat=n_lat, tm=tm, n_x=len(x_args)),
        grid=(T // tm, B),
        in_specs=x_specs + [
                  pl.BlockSpec((1, 16, D_MODEL), lambda j, b: (b, 0, 0)),
                  pl.BlockSpec((1, D_MODEL), lambda j, b: (0, 0)),
                  pl.BlockSpec((D_MODEL, N_PROJ), lambda j, b: (0, 0)),
                  pl.BlockSpec((tm, LANES), lambda j, b: (j, 0)),
                  pl.BlockSpec((tm, LANES), lambda j, b: (j, 0)),
                  pl.BlockSpec((8, LANES), lambda j, b: (0, 0)),
                  pl.BlockSpec((LANES, LANES), lambda j, b: (0, 0))],
        out_specs=pl.BlockSpec((1, tm, N_PROJ), lambda j, b: (b, j, 0)),
        out_shape=jax.ShapeDtypeStruct((B, T, N_PROJ), BF16),
        compiler_params=_cparams(2),
        name="in_proj",
    )(*x_args, modv, gain, w, cos, sin, qkg, bd)


ATT_BLOCK = 128
ATT_SPAN = ATT_BLOCK + 2 * WINDOW


ATT_HEAD_ORDER = (0, 4, 1, 5, 2, 6, 3, 7)


def _attend(qblk, k_parts, v_parts, biases, sinks):
    rows = ATT_GROUP * ATT_BLOCK
    rowi = lax.broadcasted_iota(jnp.int32, (rows, 1), 0)
    lane = lax.broadcasted_iota(jnp.int32, (1, LANES), 1)
    lower = lane < HEAD_DIM
    nt = (((1,), (1,)), ((), ()))
    heads = range(ATT_KV_HEADS)
    keep = [lower, jnp.logical_not(lower)]
    sink, scores = [], []
    for h in heads:
        qs = jnp.concatenate([jnp.where(keep[h], qblk[:, g * LANES:(g + 1) * LANES], jnp.zeros((), BF16))
                              for g in range(ATT_GROUP)], axis=0)
        col = jnp.full((rows, 1), sinks[ATT_GROUP * h + ATT_GROUP - 1], F32)
        for g in range(ATT_GROUP - 2, -1, -1):
            col = jnp.where(rowi < (g + 1) * ATT_BLOCK, sinks[ATT_GROUP * h + g], col)
        sink.append(col)
        pieces = []
        for k in k_parts:
            s = lax.dot_general(qs, k, nt, preferred_element_type=F32)
            pieces += [s[:, c * LANES:(c + 1) * LANES] for c in range(k.shape[0] // LANES)]
        scores.append([s if b is None else s + b for s, b in zip(pieces, biases)])
    top, probs = [], []
    for h in heads:
        m = scores[h][0]
        for s in scores[h][1:]:
            m = jnp.maximum(m, s)
        m = jnp.maximum(jnp.max(m, axis=-1, keepdims=True), sink[h])
        top.append(m)
        probs.append([jnp.exp((s - m).astype(BF16)) for s in scores[h]])
    normed = []
    for h in heads:
        acc = jnp.where(keep[h], 0.0, jnp.exp(sink[h] - top[h]))
        c0 = 0
        for v in v_parts[h]:
            n = v.shape[0] // LANES
            acc = acc + jnp.dot(jnp.concatenate(probs[h][c0:c0 + n], axis=1), v, preferred_element_type=F32)
            c0 += n
        normed.append(acc / pltpu.roll(acc, HEAD_DIM, 1))
    return jnp.concatenate([jnp.where(lower, normed[0][g * ATT_BLOCK:(g + 1) * ATT_BLOCK],
                                      normed[1][g * ATT_BLOCK:(g + 1) * ATT_BLOCK]) for g in range(ATT_GROUP)], axis=1)


def _att_body(sink_ref, q_ref, k_ref, v_ref, o_ref, v1_s, *, n_lat, n_ctx, with_ctx_out):
    sinks = [sink_ref[i] for i in range(ATT_HEADS)]
    lane = lax.broadcasted_iota(jnp.int32, (1, LANES), 1)
    vv = v_ref[0]
    v1_s[0] = jnp.where(lane < HEAD_DIM, vv, jnp.ones((), BF16))
    v1_s[1] = jnp.where(lane < HEAD_DIM, jnp.ones((), BF16), vv)
    k_ctx = k_ref[0, n_lat:n_lat + n_ctx, :]
    v_ctx = [v1_s[h, n_lat:n_lat + n_ctx, :] for h in range(ATT_KV_HEADS)]
    no_bias = [None] * (n_ctx // LANES)
    qi = lax.broadcasted_iota(jnp.int32, (ATT_GROUP * ATT_BLOCK, LANES), 0) % ATT_BLOCK
    ki = lax.broadcasted_iota(jnp.int32, (ATT_GROUP * ATT_BLOCK, LANES), 1)
    past_ok = jnp.where(ki >= qi, 0.0, NEG)
    ahead_ok = jnp.where(ki <= qi, 0.0, NEG)

    def block(q0, k0, n_keys, biases):
        k_parts = [k_ref[0, pl.ds(k0, n_keys), :], k_ctx]
        v_parts = [[v1_s[h, pl.ds(k0, n_keys), :], v_ctx[h]] for h in range(ATT_KV_HEADS)]
        out = _attend(q_ref[0, pl.ds(q0, ATT_BLOCK), :], k_parts, v_parts, biases + no_bias, sinks)
        o_ref[0, pl.ds(q0, ATT_BLOCK), :] = out.astype(BF16)

    def interior(i, carry):
        q0 = pl.multiple_of(i * ATT_BLOCK, ATT_BLOCK)
        block(q0, pl.multiple_of(q0 - WINDOW, ATT_BLOCK), ATT_SPAN, [past_ok, None, ahead_ok])
        return carry

    nq = n_lat // ATT_BLOCK
    block(0, 0, 2 * ATT_BLOCK, [None, ahead_ok])
    lax.fori_loop(1, nq - 1, interior, 0, unroll=7)
    block(n_lat - ATT_BLOCK, n_lat - 2 * ATT_BLOCK, 2 * ATT_BLOCK, [past_ok, None])
    if with_ctx_out:
        for c in range(n_ctx // ATT_BLOCK):
            r0 = n_lat + c * ATT_BLOCK
            out = _attend(q_ref[0, r0:r0 + ATT_BLOCK, :], [k_ctx], [[v_ctx[h]] for h in range(ATT_KV_HEADS)],
                          no_bias, sinks)
            o_ref[0, r0:r0 + ATT_BLOCK, :] = out.astype(BF16)
    else:
        o_ref[0, n_lat:, :] = jnp.zeros((n_ctx, ATT_WIDTH), BF16)


def _attention(p, sinks, n_lat, with_ctx_out):
    B, T, _ = p.shape
    return pl.pallas_call(
        functools.partial(_att_body, n_lat=n_lat, n_ctx=T - n_lat, with_ctx_out=with_ctx_out),
        grid=(B,),
        in_specs=[pl.BlockSpec(memory_space=pltpu.SMEM),
                  pl.BlockSpec((1, T, ATT_WIDTH), lambda b: (b, 0, COL_AQ // ATT_WIDTH)),
                  pl.BlockSpec((1, T, KV_WIDTH), lambda b: (b, 0, COL_AK // KV_WIDTH)),
                  pl.BlockSpec((1, T, KV_WIDTH), lambda b: (b, 0, COL_AV // KV_WIDTH))],
        out_specs=pl.BlockSpec((1, T, ATT_WIDTH), lambda b: (b, 0, 0)),
        out_shape=jax.ShapeDtypeStruct((B, T, ATT_WIDTH), BF16),
        scratch_shapes=[pltpu.VMEM((ATT_KV_HEADS, T, KV_WIDTH), BF16)],
        compiler_params=_cparams(1),
        name="attention",
    )(sinks, p, p, p)


GLA_BLOCK_A = 256
GLA_BLOCK_B = 128
CONV_PAD = 8
EPI_ROWS = 256


def _log_sigmoid(z):
    return jnp.minimum(z, 0.0) - jnp.log1p(jnp.exp(-jnp.abs(z)))


def _gla_body(gq_ref, gk_ref, gv_ref, gr_ref, cb_ref, cc_ref, ch_ref, gt_ref, wg_ref, gbias_ref, ng_ref,
              cw_ref, bd_ref, cum_ref, o_ref, qe_s, ke_s, dec_s, upd_s, prev_s, st_s, o_s, u_s, *, n_lat, n_ctx,
              with_ctx_out):
    T = n_lat + n_ctx
    C = GLA_CHUNK
    nt = (((1,), (1,)), ((), ()))
    tn = (((0,), (0,)), ((), ()))

    sr = lax.broadcasted_iota(jnp.int32, (GLA_WIDTH, 2 * GLA_QK_WIDTH), 0) // GLA_DV
    sl = (lax.broadcasted_iota(jnp.int32, (GLA_WIDTH, 2 * GLA_QK_WIDTH), 1) % GLA_QK_WIDTH) // GLA_DK
    state_mask2 = sr == sl

    def factors(i, carry):
        r0 = pl.multiple_of(i * GLA_BLOCK_A, GLA_BLOCK_A)
        rows = pl.ds(r0, GLA_BLOCK_A)
        gt = gt_ref[0, rows, :]
        q = gq_ref[0, rows, :].astype(F32) * (GLA_DK ** -0.5)
        k = gk_ref[0, rows, :].astype(F32)
        v = gv_ref[0, rows, :]
        z2 = jnp.dot(gt, wg_ref[...], preferred_element_type=F32) + gbias_ref[0:1, :]
        g2 = _log_sigmoid(z2) * (1.0 / GLA_GATE_NORM)
        cum = []
        for d in range(2):
            g = g2[:, d * GLA_QK_WIDTH:(d + 1) * GLA_QK_WIDTH]
            g_hi = g.astype(BF16)
            g_lo = (g - g_hi.astype(F32)).astype(BF16)
            cum.append(jnp.dot(cum_ref[d], jnp.concatenate([g_hi, g_lo], axis=1), preferred_element_type=F32))
        kl = []
        for d in range(2):
            b = cum[d][:GLA_BLOCK_A, :GLA_QK_WIDTH] + cum[d][:GLA_BLOCK_A, GLA_QK_WIDTH:]
            tot = cum[d][GLA_BLOCK_A:, :GLA_QK_WIDTH] + cum[d][GLA_BLOCK_A:, GLA_QK_WIDTH:]
            dec = jnp.exp(tot)
            ke = k * jnp.exp(-b)
            qe_s[d, rows, :] = (q * jnp.exp(b)).astype(BF16)
            ke_s[d, rows, :] = ke.astype(BF16)
            dec_s[d, rows, :] = dec
            kl.append((ke * dec).astype(BF16))
        kl2 = jnp.concatenate(kl, axis=1)
        for cc in range(GLA_BLOCK_A // C):
            upd = lax.dot_general(v[cc * C:(cc + 1) * C], kl2[cc * C:(cc + 1) * C], tn, preferred_element_type=F32)
            upd_s[i * (GLA_BLOCK_A // C) + cc] = jnp.where(state_mask2, upd, 0.0)
        return carry

    lax.fori_loop(0, T // GLA_BLOCK_A, factors, 0, unroll=True)

    nc_lat = n_lat // C
    nc_ctx = n_ctx // C
    st_s[...] = jnp.zeros_like(st_s)

    def scan(i, carry):
        in_ctx = i < nc_ctx
        cf = jnp.where(in_ctx, nc_lat + i, i - nc_ctx)
        cb = jnp.where(in_ctx, nc_lat + nc_ctx - 1 - i, nc_lat - 1 - (i - nc_ctx))
        for d, cid in enumerate((cf, cb)):
            st = st_s[d]
            prev_s[cid, :, d * GLA_QK_WIDTH:(d + 1) * GLA_QK_WIDTH] = st.astype(BF16)
            st_s[d] = (st * dec_s[d, pl.ds(pl.multiple_of(cid * C, C), 1), :]
                       + upd_s[cid, :, d * GLA_QK_WIDTH:(d + 1) * GLA_QK_WIDTH])
        return carry

    lax.fori_loop(0, nc_lat + nc_ctx, scan, 0)

    RB = GLA_BLOCK_B
    k_rows = lax.broadcasted_iota(jnp.int32, (GLA_HEADS * RB, GLA_QK_WIDTH), 0) // RB
    k_lanes = lax.broadcasted_iota(jnp.int32, (GLA_HEADS * RB, GLA_QK_WIDTH), 1) // GLA_DK
    key_heads = k_rows == k_lanes
    v_rows = lax.broadcasted_iota(jnp.int32, (GLA_HEADS * RB, GLA_WIDTH), 0) // RB
    v_lanes = lax.broadcasted_iota(jnp.int32, (GLA_HEADS * RB, GLA_WIDTH), 1) // GLA_DV
    value_heads = v_rows == v_lanes
    qr = lax.broadcasted_iota(jnp.int32, (RB, GLA_HEADS * RB), 0)
    kc = lax.broadcasted_iota(jnp.int32, (RB, GLA_HEADS * RB), 1) % RB
    same_chunk = (qr // C) == (kc // C)
    forward = kc <= qr
    zero = jnp.zeros((), BF16)

    def outputs(i, carry):
        r0 = pl.multiple_of(i * RB, RB)
        rows = pl.ds(r0, RB)
        v = gv_ref[0, rows, :]
        v4 = jnp.where(value_heads, jnp.concatenate([v] * GLA_HEADS, axis=0), zero)
        qe = [qe_s[d, rows, :] for d in range(2)]
        att = []
        for d in range(2):
            ke4 = jnp.where(key_heads, jnp.concatenate([ke_s[d, rows, :]] * GLA_HEADS, axis=0), zero)
            att.append(lax.dot_general(qe[d], ke4, nt, preferred_element_type=F32))
        both = jnp.where(same_chunk, jnp.where(forward, att[0], att[1]), 0.0).astype(BF16)
        qe2 = jnp.concatenate(qe, axis=1)
        inter = [lax.dot_general(qe2[cc * C:(cc + 1) * C], prev_s[i * (RB // C) + cc], nt,
                                 preferred_element_type=F32) for cc in range(RB // C)]
        o_s[rows, :] = jnp.dot(both, v4, preferred_element_type=F32) + jnp.concatenate(inter, axis=0)
        return carry

    out_rows = T if with_ctx_out else n_lat
    n_out = out_rows // RB
    lax.fori_loop(0, n_out, outputs, 0, unroll=6 if n_out % 6 == 0 else 4)
    if not with_ctx_out:
        o_ref[0, n_lat:, :] = jnp.zeros((n_ctx, GLA_WIDTH + CONV_WIDTH), BF16)

    u_s[0:CONV_PAD, :] = jnp.zeros((CONV_PAD, CONV_WIDTH), F32)
    u_s[CONV_PAD + T:, :] = jnp.zeros((CONV_PAD, CONV_WIDTH), F32)
    u_s[CONV_PAD:CONV_PAD + T, :] = cc_ref[0].astype(F32) * ch_ref[0].astype(F32)
    w0 = cw_ref[0:1, :]
    w1 = cw_ref[1:2, :]
    w2 = cw_ref[2:3, :]
    for e in range(out_rows // EPI_ROWS):
        r0 = e * EPI_ROWS
        o = o_s[r0:r0 + EPI_ROWS, :]
        ss = jnp.dot((o * o).astype(BF16), bd_ref[...], preferred_element_type=F32) * (1.0 / GLA_DV)
        on = o * lax.rsqrt(ss + EPS) * ng_ref[...]
        r = gr_ref[0, r0:r0 + EPI_ROWS, :].astype(F32)
        o_ref[0, r0:r0 + EPI_ROWS, 0:GLA_WIDTH] = (on * _silu(r)).astype(BF16)
        t = r0 + lax.broadcasted_iota(jnp.int32, (EPI_ROWS, 1), 0)
        up = u_s[CONV_PAD + r0 - 1:CONV_PAD + r0 - 1 + EPI_ROWS, :]
        mid = u_s[CONV_PAD + r0:CONV_PAD + r0 + EPI_ROWS, :]
        dn = u_s[CONV_PAD + r0 + 1:CONV_PAD + r0 + 1 + EPI_ROWS, :]
        if r0 <= n_lat < r0 + EPI_ROWS:
            up = jnp.where(t == n_lat, 0.0, up)
        if r0 <= n_lat - 1 < r0 + EPI_ROWS:
            dn = jnp.where(t == n_lat - 1, 0.0, dn)
        conv = w0 * up + w1 * mid + w2 * dn
        o_ref[0, r0:r0 + EPI_ROWS, GLA_WIDTH:] = (cb_ref[0, r0:r0 + EPI_ROWS, :].astype(F32) * conv).astype(BF16)


def _gla_cum_matrices():
    i = np.arange(GLA_BLOCK_A)
    same = (i[:, None] // GLA_CHUNK) == (i[None, :] // GLA_CHUNK)
    fwd = same & (i[None, :] <= i[:, None])
    bwd = same & (i[None, :] >= i[:, None])
    mats = np.stack([np.concatenate([fwd, same], axis=0), np.concatenate([bwd, same], axis=0)])
    return jnp.asarray(mats.astype(np.float32), dtype=BF16)


def _gla_conv(p, wg, gbias, ng, cw, bd, cum, n_lat, with_ctx_out):
    B, T, _ = p.shape
    nc = T // GLA_CHUNK

    def col(width, start):
        return pl.BlockSpec((1, T, width), lambda b: (b, 0, start // width))

    def const(shape):
        return pl.BlockSpec(shape, lambda b: (0,) * len(shape))

    return pl.pallas_call(
        functools.partial(_gla_body, n_lat=n_lat, n_ctx=T - n_lat, with_ctx_out=with_ctx_out),
        grid=(B,),
        in_specs=[col(GLA_QK_WIDTH, COL_GQ), col(GLA_QK_WIDTH, COL_GK), col(GLA_WIDTH, COL_GV), col(GLA_WIDTH, COL_GR),
                  col(CONV_WIDTH, COL_CB), col(CONV_WIDTH, COL_CC), col(CONV_WIDTH, COL_CH), col(LANES, COL_GT),
                  const((LANES, 2 * GLA_QK_WIDTH)), const((8, 2 * GLA_QK_WIDTH)),
                  const((1, GLA_WIDTH)), const((8, CONV_WIDTH)), const((GLA_WIDTH, GLA_WIDTH)),
                  const((2, 2 * GLA_BLOCK_A, GLA_BLOCK_A))],
        out_specs=pl.BlockSpec((1, T, GLA_WIDTH + CONV_WIDTH), lambda b: (b, 0, 0)),
        out_shape=jax.ShapeDtypeStruct((B, T, GLA_WIDTH + CONV_WIDTH), BF16),
        scratch_shapes=[pltpu.VMEM((2, T, GLA_QK_WIDTH), BF16), pltpu.VMEM((2, T, GLA_QK_WIDTH), BF16),
                        pltpu.VMEM((2, T, GLA_QK_WIDTH), F32),
                        pltpu.VMEM((nc, GLA_WIDTH, 2 * GLA_QK_WIDTH), F32),
                        pltpu.VMEM((nc, GLA_WIDTH, 2 * GLA_QK_WIDTH), BF16),
                        pltpu.VMEM((2, GLA_WIDTH, GLA_QK_WIDTH), F32),
                        pltpu.VMEM((T, GLA_WIDTH), F32),
                        pltpu.VMEM((T + 2 * CONV_PAD, CONV_WIDTH), F32)],
        compiler_params=_cparams(1),
        name="gla_conv",
    )(p, p, p, p, p, p, p, p, wg, gbias, ng, cw, bd, cum)


def _route(logits_t):
    mx = jnp.max(logits_t, axis=0, keepdims=True)
    ex = jnp.exp(logits_t - mx)
    probs = ex / jnp.sum(ex, axis=0, keepdims=True)
    P = [probs[e:e + 1] for e in range(N_EXPERTS)]
    scores = []
    for g in range(N_GROUPS):
        a, b, c, d = P[4 * g:4 * g + 4]
        scores.append(jnp.maximum(jnp.maximum(jnp.maximum(a + b, a + c), jnp.maximum(a + d, b + c)),
                                  jnp.maximum(b + d, c + d)))
    best = jnp.maximum(jnp.maximum(scores[0], scores[1]), jnp.maximum(scores[2], scores[3]))
    taken = jnp.zeros_like(best, dtype=jnp.bool_)
    sel = []
    for g in range(N_GROUPS):
        s = (scores[g] == best) & jnp.logical_not(taken)
        sel.append(s)
        taken = taken | s
    gsel = jnp.where(sel[1], 1.0, 0.0) + jnp.where(sel[2], 2.0, 0.0) + jnp.where(sel[3], 3.0, 0.0)
    ig = [jnp.where(sel[0], P[j], jnp.where(sel[1], P[4 + j], jnp.where(sel[2], P[8 + j], P[12 + j])))
          for j in range(EXPERTS_PER_GROUP)]

    def first_max(vals):
        v = jnp.maximum(jnp.maximum(vals[0], vals[1]), jnp.maximum(vals[2], vals[3]))
        tk = jnp.zeros_like(v, dtype=jnp.bool_)
        hot = []
        for x in vals:
            s = (x == v) & jnp.logical_not(tk)
            hot.append(s)
            tk = tk | s
        idx = jnp.where(hot[1], 1.0, 0.0) + jnp.where(hot[2], 2.0, 0.0) + jnp.where(hot[3], 3.0, 0.0)
        return v, hot, idx

    _, hot1, i1 = first_max(ig)
    _, _, i2 = first_max([jnp.where(hot1[j], -1.0, ig[j]) for j in range(EXPERTS_PER_GROUP)])
    lo = jnp.minimum(i1, i2)
    hi = jnp.maximum(i1, i2)
    pair = jnp.where(lo == 0.0, hi - 1.0, jnp.where(lo == 1.0, hi + 1.0, 5.0))
    return gsel * N_PAIRS + pair


def _class_rank(cls, tri_ref, cnt_s):
    n = cls.shape[1]
    cid = lax.broadcasted_iota(jnp.int32, (CLS_ROWS, n), 0).astype(F32)
    onehot = jnp.where(cls == cid, 1.0, 0.0)
    segs = [onehot[:, k * LANES:(k + 1) * LANES] for k in range(n // LANES)]
    before = jnp.dot(jnp.concatenate(segs, axis=0).astype(BF16), tri_ref[...], preferred_element_type=F32)
    base = cnt_s[...]
    ranks = []
    for k, seg in enumerate(segs):
        ranks.append(jnp.sum(seg * (before[k * CLS_ROWS:(k + 1) * CLS_ROWS] + base), axis=0, keepdims=True))
        base = base + jnp.sum(seg, axis=1, keepdims=True)
    cnt_s[...] = base
    return jnp.concatenate(ranks, axis=1)


def _out_body(*refs, n_lat, tm, n_x):
    x_refs = refs[:n_x]
    ya_ref, yg_ref, mod_ref, wo_ref, g_ref, wr_ref, br_ref, tri_ref, xo_ref, rt_ref, cnt_ref, cnt_s, logit_s = refs[n_x:]
    j = pl.program_id(1)
    is_last_tile = j == pl.num_programs(1) - 1

    @pl.when((pl.program_id(0) == 0) & (j == 0))
    def _():
        cnt_s[...] = jnp.zeros_like(cnt_s)

    mod = mod_ref[0]
    subs = list(enumerate(range(0, tm, SUB_ROWS)))
    ys = [jnp.dot(ya_ref[0, r0:r0 + SUB_ROWS, :], wo_ref[0:ATT_WIDTH, :], preferred_element_type=F32)
          + jnp.dot(yg_ref[0, r0:r0 + SUB_ROWS, :], wo_ref[ATT_WIDTH:, :], preferred_element_type=F32) for _, r0 in subs]
    h2s = []
    for k, r0 in subs:
        row = j * tm + r0 + lax.broadcasted_iota(jnp.int32, (SUB_ROWS, 1), 0)
        is_ctx = row >= n_lat
        xn = _stream_piece(x_refs, k, tm // SUB_ROWS, is_last_tile) + _row_mod(mod, is_ctx, 2) * ys[k]
        xo_ref[0, r0:r0 + SUB_ROWS, :] = xn
        h2s.append(_norm_modulate(xn, g_ref[...], mod, is_ctx, 3, 4).astype(BF16))
    for k, r0 in subs:
        logit_s[r0:r0 + SUB_ROWS, :] = jnp.dot(h2s[k], wr_ref[...], preferred_element_type=F32) + br_ref[0:1, :]
    cls = _route(logit_s[...].T[0:N_EXPERTS, :])
    rank = _class_rank(cls, tri_ref, cnt_s)
    rt_ref[0] = jnp.concatenate([cls, rank, jnp.zeros((6, tm), F32)], axis=0).astype(jnp.int32)
    cnt_ref[...] = jnp.broadcast_to(cnt_s[...], (CLS_ROWS, LANES)).astype(jnp.int32)


def _out_proj(ya, yg, stream, ctx, modv, wo, gain, wr, br, tri, n_lat, rows, tm):
    B = stream.shape[0]
    nj = rows // tm
    x_specs, x_args = _stream_specs(stream, ctx, tm, lambda g: g[0], lambda g: g[1])
    return pl.pallas_call(
        functools.partial(_out_body, n_lat=n_lat, tm=tm, n_x=len(x_args)),
        grid=(B, nj),
        in_specs=x_specs + [
                  pl.BlockSpec((1, tm, ATT_WIDTH), lambda b, j: (b, j, 0)),
                  pl.BlockSpec((1, tm, GLA_WIDTH + CONV_WIDTH), lambda b, j: (b, j, 0)),
                  pl.BlockSpec((1, 16, D_MODEL), lambda b, j: (b, 0, 0)),
                  pl.BlockSpec((D_MODEL, D_MODEL), lambda b, j: (0, 0)),
                  pl.BlockSpec((1, D_MODEL), lambda b, j: (0, 0)),
                  pl.BlockSpec((D_MODEL, LANES), lambda b, j: (0, 0)),
                  pl.BlockSpec((8, LANES), lambda b, j: (0, 0)),
                  pl.BlockSpec((LANES, LANES), lambda b, j: (0, 0))],
        out_specs=[pl.BlockSpec((1, tm, D_MODEL), lambda b, j: (b, j, 0)),
                   pl.BlockSpec((1, 8, tm), lambda b, j: (b * nj + j, 0, 0)),
                   pl.BlockSpec((CLS_ROWS, LANES), lambda b, j: (0, 0))],
        out_shape=[jax.ShapeDtypeStruct((B, rows, D_MODEL), F32),
                   jax.ShapeDtypeStruct((B * nj, 8, tm), jnp.int32),
                   jax.ShapeDtypeStruct((CLS_ROWS, LANES), jnp.int32)],
        scratch_shapes=[pltpu.VMEM((CLS_ROWS, 1), F32), pltpu.VMEM((tm, LANES), F32)],
        compiler_params=_cparams(2),
        name="out_proj_router",
    )(*x_args, ya, yg, modv, wo, gain, wr, br, tri)


ROW_UNROLL = 8
IDX_STRIDE = 1024


def _idx_slot(idx, slot, tm):
    return idx.at[pl.ds(pl.multiple_of(slot * IDX_STRIDE, IDX_STRIDE), tm)]


def _issue_rows(tm, idx, slot, make_copy):
    base = slot * IDX_STRIDE

    def trip(i, c):
        for u in range(ROW_UNROLL):
            r = i * ROW_UNROLL + u
            make_copy(r, idx[base + r]).start(priority=u % 2)
        return c

    lax.fori_loop(0, tm // ROW_UNROLL, trip, 0)


PAD_SLOTS = 32
PAD_BITS = 9


def _zero_pad_rows(pad_ref, zeros_ref, hs_ref, sem):
    def pieces(c, fn):
        start = pad_ref[c]
        n = pad_ref[PAD_SLOTS + c]
        for bit in range(PAD_BITS):
            size = 1 << bit
            below = n & (size - 1)

            @pl.when((n & size) != 0)
            def _():
                fn(pltpu.make_async_copy(zeros_ref.at[pl.ds(0, size)], hs_ref.at[pl.ds(start + below, size)], sem))

        def block(i, carry):
            fn(pltpu.make_async_copy(zeros_ref, hs_ref.at[pl.ds(start + n + i * MOE_TILE, MOE_TILE)], sem))
            return carry

        lax.fori_loop(0, pad_ref[2 * PAD_SLOTS + c], block, 0)

    def start_all(c, carry):
        pieces(c, lambda cp: cp.start())
        return carry

    def wait_all(c, carry):
        pieces(c, lambda cp: cp.wait())
        return carry

    lax.fori_loop(0, N_CLASSES + 1, start_all, 0)
    lax.fori_loop(0, N_CLASSES + 1, wait_all, 0)


def _issue_rows_inline(tm, idx, slot, make_copy):
    base = slot * IDX_STRIDE
    for r in range(tm):
        make_copy(r, idx[base + r]).start(priority=r % 2)


def _disp_body(pad_ref, x_ref, mod_ref, g_ref, dest_ref, hs_ref, buf0, buf1, buf2, idx, sem_i, sem_d, sem_z, *, n_lat,
               tm, nj, n_steps):
    bufs = (buf0, buf1, buf2)
    j = pl.program_id(1)
    s = pl.program_id(0) * nj + j
    last = n_steps - 1

    def idx_fetch(step):
        return pltpu.make_async_copy(dest_ref.at[step], _idx_slot(idx, step % 3, tm), sem_i.at[step % 3])

    def drain(k):
        pltpu.make_async_copy(bufs[k], hs_ref.at[pl.ds(0, tm)], sem_d.at[k]).wait()

    def row_copy(k):
        return lambda r, d: pltpu.make_async_copy(bufs[k].at[r], hs_ref.at[d], sem_d.at[k])

    def normalise(k):
        row = j * tm + lax.broadcasted_iota(jnp.int32, (tm, 1), 0)
        h2 = _norm_modulate(x_ref[0], g_ref[...], mod_ref[0], row >= n_lat, 3, 4)
        bufs[k][...] = h2.reshape(tm, 8, LANES)

    @pl.when(s == 0)
    def _():
        idx_fetch(0).start()
        buf1[...] = jnp.zeros((tm, 8, LANES), F32)
        _zero_pad_rows(pad_ref, buf1.at[pl.ds(0, MOE_TILE)], hs_ref, sem_z)
        normalise(0)

    @pl.when(s < last)
    def _():
        idx_fetch(s + 1).start()

    idx_fetch(s).wait()

    for k in range(3):
        prev = (k + 2) % 3

        @pl.when((s > 0) & (s % 3 == k))
        def _():
            _issue_rows_inline(tm, idx, prev, row_copy(prev))
            normalise(k)

        @pl.when((s > 1) & (s % 3 == k))
        def _():
            drain((k + 1) % 3)

        @pl.when((s == last) & (s % 3 == k))
        def _():
            _issue_rows(tm, idx, k, row_copy(k))
            drain(k)
            if last > 0:
                drain(prev)


def _dispatch(pad, xx, modv, gain, dest, n_rows, n_lat, tm):
    B, rows, _ = xx.shape
    nj = rows // tm
    assert tm >= MOE_TILE
    return pl.pallas_call(
        functools.partial(_disp_body, n_lat=n_lat, tm=tm, nj=nj, n_steps=B * nj),
        grid=(B, nj),
        in_specs=[pl.BlockSpec(memory_space=pltpu.SMEM),
                  pl.BlockSpec((1, tm, D_MODEL), lambda b, j: (b, j, 0)),
                  pl.BlockSpec((1, 16, D_MODEL), lambda b, j: (b, 0, 0)),
                  pl.BlockSpec((1, D_MODEL), lambda b, j: (0, 0)),
                  pl.BlockSpec(memory_space=pl.ANY)],
        out_specs=pl.BlockSpec(memory_space=pl.ANY),
        scratch_shapes=[pltpu.VMEM((tm, 8, LANES), F32), pltpu.VMEM((tm, 8, LANES), F32),
                        pltpu.VMEM((tm, 8, LANES), F32), pltpu.SMEM((3 * IDX_STRIDE,), jnp.int32),
                        pltpu.SemaphoreType.DMA((3,)), pltpu.SemaphoreType.DMA((3,)), pltpu.SemaphoreType.DMA],
        out_shape=jax.ShapeDtypeStruct((n_rows, 8, LANES), F32),
        compiler_params=_cparams(2),
        name="dispatch",
    )(pad, xx, modv, gain, dest)


def _moe_body(tile_ref, e_lo_ref, e_hi_ref, valid_ref, hs_ref, wg1, wu1, wd1, wg2, wu2, wd2, wr_ref, br_ref, ys_ref):
    g = pl.program_id(0)
    tm = hs_ref.shape[0]

    @pl.when(valid_ref[g] == 1)
    def _():
        x = hs_ref[...].reshape(tm, D_MODEL)
        e_lo = e_lo_ref[g]
        e_hi = e_hi_ref[g]
        dw = wr_ref[pl.ds(e_lo, 1), :] - wr_ref[pl.ds(e_hi, 1), :]
        d = jnp.sum(x * dw, axis=-1, keepdims=True) + (br_ref[e_lo] - br_ref[e_hi])
        w_lo = jax.nn.sigmoid(d)
        w_hi = jax.nn.sigmoid(-d)
        h = x.astype(BF16)

        def act(wg, wu, w):
            a = _silu(jnp.dot(h, wg[0], preferred_element_type=F32)) * jnp.dot(h, wu[0], preferred_element_type=F32)
            return (a * w).astype(BF16)

        y = (jnp.dot(act(wg1, wu1, w_lo), wd1[0], preferred_element_type=F32)
             + jnp.dot(act(wg2, wu2, w_hi), wd2[0], preferred_element_type=F32))
        ys_ref[...] = y.reshape(tm, 8, LANES)

    @pl.when(valid_ref[g] == 0)
    def _():
        ys_ref[...] = jnp.zeros_like(ys_ref)


def _moe(hs, tile, e_lo, e_hi, valid, wg, wu, wd, wr_t, br, first_expert):
    n_tiles = tile.shape[0]
    tm = MOE_TILE

    def w_in(sel):
        return pl.BlockSpec((1, D_MODEL, D_EXPERT), lambda g, t, lo, hi, v: (first_expert + (lo, hi)[sel][g], 0, 0))

    def w_out(sel):
        return pl.BlockSpec((1, D_EXPERT, D_MODEL), lambda g, t, lo, hi, v: (first_expert + (lo, hi)[sel][g], 0, 0))

    return pl.pallas_call(
        _moe_body,
        grid_spec=pltpu.PrefetchScalarGridSpec(
            num_scalar_prefetch=4,
            grid=(n_tiles,),
            in_specs=[pl.BlockSpec((tm, 8, LANES), lambda g, t, lo, hi, v: (t[g], 0, 0)),
                      w_in(0), w_in(0), w_out(0), w_in(1), w_in(1), w_out(1),
                      pl.BlockSpec((N_EXPERTS, D_MODEL), lambda g, t, lo, hi, v: (0, 0)),
                      pl.BlockSpec(memory_space=pltpu.SMEM)],
            out_specs=pl.BlockSpec((tm, 8, LANES), lambda g, t, lo, hi, v: (g, 0, 0))),
        out_shape=jax.ShapeDtypeStruct((n_tiles * tm, 8, LANES), F32),
        compiler_params=_cparams(1),
        name="moe_pairs",
    )(tile, e_lo, e_hi, valid, hs, wg, wu, wd, wg, wu, wd, wr_t, br)


def _fin_body(x_ref, mod_ref, dest_ref, ys_ref, xo_ref, buf0, buf1, buf2, idx, sem_i, sem_d, *, n_lat, tm, nj,
              n_steps):
    bufs = (buf0, buf1, buf2)
    j = pl.program_id(1)
    s = pl.program_id(0) * nj + j
    last = n_steps - 1

    def idx_fetch(step):
        return pltpu.make_async_copy(dest_ref.at[step], _idx_slot(idx, step % 3, tm), sem_i.at[step % 3])

    def row_copy(k):
        return lambda r, d: pltpu.make_async_copy(ys_ref.at[d], bufs[k].at[r], sem_d.at[k])

    def add_rows(k):
        row = j * tm + lax.broadcasted_iota(jnp.int32, (tm, 1), 0)
        xo_ref[0] = x_ref[0] + _row_mod(mod_ref[0], row >= n_lat, 5) * bufs[k][...].reshape(tm, D_MODEL)

    @pl.when(s == 0)
    def _():
        for t in range(min(2, n_steps)):
            idx_fetch(t).start()
            idx_fetch(t).wait()
            _issue_rows(tm, idx, t, row_copy(t))
        if last >= 2:
            idx_fetch(2).start()

    for k in range(3):
        ahead = (k + 2) % 3

        @pl.when((s % 3 == k) & (s + 2 <= last))
        def _():
            pltpu.make_async_copy(ys_ref.at[pl.ds(0, tm)], bufs[k], sem_d.at[k]).wait()
            idx_fetch(s + 2).wait()
            _issue_rows_inline(tm, idx, ahead, row_copy(ahead))
            add_rows(k)

        @pl.when((s % 3 == k) & (s + 2 > last))
        def _():
            pltpu.make_async_copy(ys_ref.at[pl.ds(0, tm)], bufs[k], sem_d.at[k]).wait()
            add_rows(k)

    @pl.when(s + 3 <= last)
    def _():
        idx_fetch(s + 3).start()


def _combine(xx, modv, dest, ys, n_lat, tm):
    B, rows, _ = xx.shape
    nj = rows // tm
    return pl.pallas_call(
        functools.partial(_fin_body, n_lat=n_lat, tm=tm, nj=nj, n_steps=B * nj),
        grid=(B, nj),
        in_specs=[pl.BlockSpec((1, tm, D_MODEL), lambda b, j: (b, j, 0)),
                  pl.BlockSpec((1, 16, D_MODEL), lambda b, j: (b, 0, 0)),
                  pl.BlockSpec(memory_space=pl.ANY),
                  pl.BlockSpec(memory_space=pl.ANY)],
        out_specs=pl.BlockSpec((1, tm, D_MODEL), lambda b, j: (b, j, 0)),
        scratch_shapes=[pltpu.VMEM((tm, 8, LANES), F32), pltpu.VMEM((tm, 8, LANES), F32),
                        pltpu.VMEM((tm, 8, LANES), F32), pltpu.SMEM((3 * IDX_STRIDE,), jnp.int32),
                        pltpu.SemaphoreType.DMA((3,)), pltpu.SemaphoreType.DMA((3,))],
        out_shape=jax.ShapeDtypeStruct((B, rows, D_MODEL), F32),
        compiler_params=_cparams(2),
        name="combine",
    )(xx, modv, dest, ys)


def _rope_tables(n_lat, n_ctx):
    rows = n_lat // GRID_W
    row, col = jnp.meshgrid(jnp.arange(rows), jnp.arange(GRID_W), indexing="ij")
    n_freq = HEAD_DIM // 4
    inv_freq = ROPE_BASE ** (-jnp.arange(n_freq, dtype=F32) / n_freq)
    ang = jnp.concatenate([row.reshape(-1, 1).astype(F32) * inv_freq, col.reshape(-1, 1).astype(F32) * inv_freq],
                          axis=-1)
    cos = jnp.tile(jnp.cos(ang), (1, LANES // (HEAD_DIM // 2)))
    sin = jnp.tile(jnp.sin(ang), (1, LANES // (HEAD_DIM // 2)))
    sign = jnp.where((jnp.arange(LANES) % HEAD_DIM) < HEAD_DIM // 2, -1.0, 1.0).astype(F32)
    cos = jnp.concatenate([cos, jnp.ones((n_ctx, LANES), F32)], axis=0)
    sin = jnp.concatenate([sin * sign, jnp.zeros((n_ctx, LANES), F32)], axis=0)
    return cos, sin


def _block_diag_ones(n, blk):
    i = np.arange(n) // blk
    return jnp.asarray((i[:, None] == i[None, :]).astype(np.float32), dtype=BF16)


def _routing_tables(counts, n_tiles):
    tiles_c = (counts + MOE_TILE - 1) // MOE_TILE
    tile_end = jnp.cumsum(tiles_c)
    tile_start = tile_end - tiles_c
    off = tile_start * MOE_TILE
    total = tile_end[-1]
    g = jnp.arange(n_tiles, dtype=jnp.int32)
    valid = (g < total).astype(jnp.int32)
    g_eff = jnp.minimum(g, total - 1)
    c_of = jnp.sum((g_eff[:, None] >= tile_end[None, :]).astype(jnp.int32), axis=1)
    group = c_of // N_PAIRS
    pair = c_of % N_PAIRS
    lo = jnp.asarray(PAIR_LO, jnp.int32)
    hi = jnp.asarray(PAIR_HI, jnp.int32)
    e_lo = group * EXPERTS_PER_GROUP + jnp.sum((pair[:, None] == jnp.arange(N_PAIRS)[None, :]) * lo[None, :], axis=1)
    e_hi = group * EXPERTS_PER_GROUP + jnp.sum((pair[:, None] == jnp.arange(N_PAIRS)[None, :]) * hi[None, :], axis=1)
    fill = jnp.zeros((PAD_SLOTS - N_CLASSES - 1,), jnp.int32)
    pad = jnp.concatenate([off + counts, (total * MOE_TILE)[None], fill,
                           tiles_c * MOE_TILE - counts, jnp.zeros((1,), jnp.int32), fill,
                           jnp.zeros((N_CLASSES,), jnp.int32), (n_tiles - total)[None], fill])
    return off.astype(jnp.int32), pad.astype(jnp.int32), g_eff, e_lo.astype(jnp.int32), e_hi.astype(jnp.int32), valid


def kernel(x, c, ctx, c_ctx, w_ada, b_ada, norm_mix_g, norm_ffn_g, w_in, q_norm_g, k_norm_g, attn_sink, gla_gate_w,
           gla_gate_b, gla_norm_g, conv_w, w_out, w_router, b_router, w_gate_e, w_up_e, w_down_e):
    B, S, D = x.shape
    L = ctx.shape[1]
    T = S + L
    assert D == D_MODEL and T % TOKEN_TILE == 0 and S % LAT_TILE == 0 and S % GRID_W == 0
    assert S % ATT_BLOCK == 0 and L % ATT_BLOCK == 0 and S >= ATT_SPAN and T % EPI_ROWS == 0

    cond_rows = -(-(B + 1) // 8) * 8
    cond = jnp.zeros((cond_rows, D), F32).at[:B].set(c).at[B].set(c_ctx)
    mod_all = _modulation(cond, w_ada, b_ada)

    cos, sin = _rope_tables(S, L)
    bd_head = _block_diag_ones(LANES, HEAD_DIM)
    bd_gla = _block_diag_ones(GLA_WIDTH, GLA_DV)
    cum = _gla_cum_matrices()
    tri = jnp.asarray(np.triu(np.ones((LANES, LANES), np.float32), 1), dtype=BF16)
    stream, stream_ctx = x, ctx

    order = jnp.asarray(ATT_HEAD_ORDER)
    wg_all = w_gate_e.reshape(DEPTH * N_EXPERTS, D, D_EXPERT).astype(BF16)
    wu_all = w_up_e.reshape(DEPTH * N_EXPERTS, D, D_EXPERT).astype(BF16)
    wd_all = w_down_e.reshape(DEPTH * N_EXPERTS, D_EXPERT, D).astype(BF16)

    def mod_table(l):
        m_lat = mod_all[l, :B].reshape(B, 6, D)
        m_ctx = jnp.broadcast_to(mod_all[l, B].reshape(1, 6, D), (B, 6, D))
        return jnp.concatenate([m_lat, m_ctx, jnp.zeros((B, 4, D), F32)], axis=1)

    def in_proj_args(l):
        wl = w_in[l]
        wq = wl[:, :ATT_WIDTH].reshape(D, ATT_HEADS, HEAD_DIM)[:, order, :].reshape(D, ATT_WIDTH)
        w_perm = jnp.concatenate([wq, wl[:, ATT_WIDTH:SRC_GATES], wl[:, SRC_CONV:], wl[:, SRC_GATES:SRC_CONV],
                                  jnp.zeros((D, N_PROJ - wl.shape[1]), F32)], axis=1).astype(BF16)
        qg = jnp.tile(q_norm_g[l], LANES // HEAD_DIM) * (HEAD_DIM ** -0.5)
        kg = jnp.tile(k_norm_g[l], LANES // HEAD_DIM)
        qkg = jnp.stack([qg] * (ATT_WIDTH // LANES) + [kg] + [jnp.zeros_like(kg)] * 3)
        return mod_table(l), norm_mix_g[l].reshape(1, D), w_perm, cos, sin, qkg, bd_head

    p = _in_proj(stream, stream_ctx, *in_proj_args(0), S)
    for l in range(DEPTH):
        last = l == DEPTH - 1
        modv = mod_table(l)

        y_att = _attention(p, attn_sink[l], S, not last)

        pad_rows = jnp.zeros((LANES - 2 * GLA_GATE_RANK, GLA_QK_WIDTH), F32)
        zero_rank = jnp.zeros((GLA_GATE_RANK, GLA_QK_WIDTH), F32)
        wgf = jnp.concatenate([gla_gate_w[l, 0], zero_rank, pad_rows], axis=0)
        wgb = jnp.concatenate([zero_rank, gla_gate_w[l, 1], pad_rows], axis=0)
        wg = jnp.concatenate([wgf, wgb], axis=1).astype(BF16)
        gbias = jnp.concatenate([gla_gate_b[l].reshape(1, 2 * GLA_QK_WIDTH),
                                 jnp.zeros((7, 2 * GLA_QK_WIDTH), F32)], axis=0)
        ng = jnp.tile(gla_norm_g[l], GLA_HEADS).reshape(1, GLA_WIDTH)
        cw = jnp.concatenate([conv_w[l], jnp.zeros((5, CONV_WIDTH), F32)], axis=0)
        y_gc = _gla_conv(p, wg, gbias, ng, cw, bd_gla, cum, S, not last)

        rows, tm = (S, LAT_TILE) if last else (T, TOKEN_TILE)
        wr = jnp.concatenate([w_router, jnp.zeros((D, LANES - N_EXPERTS), F32)], axis=1).astype(BF16)
        br = jnp.zeros((8, LANES), F32).at[0, :N_EXPERTS].set(b_router)
        ffn_g = norm_ffn_g[l].reshape(1, D)
        wo_att = w_out[l, :ATT_WIDTH].reshape(ATT_HEADS, HEAD_DIM, D)[order].reshape(ATT_WIDTH, D)
        wo = jnp.concatenate([wo_att, w_out[l, ATT_WIDTH:]], axis=0).astype(BF16)
        xx_mid, route, counts = _out_proj(y_att, y_gc, stream, stream_ctx, modv, wo, ffn_g, wr, br, tri, S, rows, tm)

        n_tiles = -(-(B * rows) // MOE_TILE) + N_CLASSES
        off, pad, tile, e_lo, e_hi, valid = _routing_tables(counts[:N_CLASSES, 0], n_tiles)

        dest = route[:, 1, :]
        for cls_id in range(N_CLASSES):
            dest = dest + jnp.where(route[:, 0, :] == cls_id, off[cls_id], 0)
        hs = _dispatch(pad, xx_mid, modv, ffn_g, dest, n_tiles * MOE_TILE, S, tm)
        ys = _moe(hs, tile, e_lo, e_hi, valid, wg_all, wu_all, wd_all, w_router.T, b_router, l * N_EXPERTS)
        stream, stream_ctx = _combine(xx_mid, modv, dest, ys, S, tm), None
        if not last:
            p = _in_proj(stream, None, *in_proj_args(l + 1), S)
    return stream
```

```python
import functools

import numpy as np
import jax
import jax.numpy as jnp
from jax import lax
from jax.experimental import pallas as pl
from jax.experimental.pallas import tpu as pltpu

D_MODEL = 1024
DEPTH = 2
GRID_W = 64
EPS = 1e-6
HEAD_DIM = 64
ATT_HEADS = 8
ATT_KV_HEADS = 2
ATT_GROUP = ATT_HEADS // ATT_KV_HEADS
ATT_WIDTH = ATT_HEADS * HEAD_DIM
WINDOW = 128
ROPE_BASE = 10000.0
GLA_HEADS = 4
GLA_DV = 64
GLA_DK = 32
GLA_WIDTH = GLA_HEADS * GLA_DV
GLA_GATE_RANK = 16
GLA_GATE_NORM = 16.0
GLA_CHUNK = 64
CONV_WIDTH = 256
N_EXPERTS = 16
N_GROUPS = 4
EXPERTS_PER_GROUP = 4
D_EXPERT = D_MODEL // 2

LANES = 128
KV_WIDTH = ATT_KV_HEADS * HEAD_DIM
GLA_QK_WIDTH = GLA_HEADS * GLA_DK
COL_AQ, COL_AK, COL_AV = 0, 512, 640
COL_GQ, COL_GK, COL_GV, COL_GR = 768, 896, 1024, 1280
COL_CB, COL_CC, COL_CH, COL_GT = 1536, 1792, 2048, 2304
N_PROJ = 2432
SRC_GATES = ATT_WIDTH + 2 * KV_WIDTH + 2 * GLA_QK_WIDTH + 2 * GLA_WIDTH
SRC_CONV = SRC_GATES + 2 * GLA_GATE_RANK
QK_COLS = COL_AV
N_PAIRS = 6
N_CLASSES = N_GROUPS * N_PAIRS
PAIR_LO = (0, 0, 0, 1, 1, 2)
PAIR_HI = (1, 2, 3, 2, 3, 3)
CLS_ROWS = 32
NEG = -1e30

TOKEN_TILE = 768
LAT_TILE = 1024
MOE_TILE = 512
EXPERT_SEG = 256
SUB_ROWS = 256
VMEM_LIMIT = 56 * 1024 * 1024

F32 = jnp.float32
BF16 = jnp.bfloat16


def _cparams(n_axes):
    return pltpu.CompilerParams(dimension_semantics=("arbitrary",) * n_axes, vmem_limit_bytes=VMEM_LIMIT)


def _silu(x):
    return x * jax.nn.sigmoid(x)


def _mod_body(c_ref, w_ref, b_ref, o_ref):
    c = c_ref[...]
    a = _silu(c).astype(BF16)
    o_ref[0] = jnp.dot(a, w_ref[0].astype(BF16), preferred_element_type=F32) + b_ref[0]


def _modulation(cond, w_ada, b_ada):
    rows = cond.shape[0]
    nblk = w_ada.shape[2] // D_MODEL
    return pl.pallas_call(
        _mod_body,
        grid=(DEPTH, nblk),
        in_specs=[pl.BlockSpec((rows, D_MODEL), lambda l, n: (0, 0)),
                  pl.BlockSpec((1, D_MODEL, D_MODEL), lambda l, n: (l, 0, n)),
                  pl.BlockSpec((1, 1, D_MODEL), lambda l, n: (l, 0, n))],
        out_specs=pl.BlockSpec((1, rows, D_MODEL), lambda l, n: (l, 0, n)),
        out_shape=jax.ShapeDtypeStruct((DEPTH, rows, w_ada.shape[2]), F32),
        compiler_params=_cparams(2),
        name="modulation",
    )(cond, w_ada, b_ada.reshape(DEPTH, 1, -1))


def _row_mod(mod, is_ctx, i):
    return jnp.where(is_ctx, mod[6 + i:7 + i], mod[i:i + 1])


def _norm_modulate(x, gain, mod, is_ctx, i_shift, i_scale):
    ms = jnp.mean(x * x, axis=-1, keepdims=True)
    xn = x * lax.rsqrt(ms + EPS) * gain
    return xn * (1.0 + _row_mod(mod, is_ctx, i_scale)) + _row_mod(mod, is_ctx, i_shift)


def _stream_specs(stream, ctx, tm, b_of, j_of):
    n_sub = tm // SUB_ROWS
    last_piece = stream.shape[1] // SUB_ROWS - 1
    specs = [pl.BlockSpec((1, SUB_ROWS, D_MODEL),
                          lambda *g, k=k: (b_of(g), jnp.minimum(j_of(g) * n_sub + k, last_piece), 0))
             for k in range(n_sub)]
    args = [stream] * n_sub
    if ctx is not None:
        assert ctx.shape[1] == SUB_ROWS and (stream.shape[1] + SUB_ROWS) % tm == 0
        specs.append(pl.BlockSpec((1, SUB_ROWS, D_MODEL), lambda *g: (b_of(g), 0, 0)))
        args.append(ctx)
    return specs, args


def _stream_piece(x_refs, k, n_sub, is_last_tile):
    x = x_refs[k][0]
    if len(x_refs) > n_sub and k == n_sub - 1:
        x = jnp.where(is_last_tile, x_refs[n_sub][0], x)
    return x


def _in_tile(piece, j, mod_ref, g_ref, w_ref, cos_ref, sin_ref, qkg_ref, bd_ref, o_ref, n_lat, tm):
    lane = lax.broadcasted_iota(jnp.int32, (1, LANES), 1)
    first_half = (lane % HEAD_DIM) < (HEAD_DIM // 2)
    for k, r0 in enumerate(range(0, tm, SUB_ROWS)):
        rows = slice(r0, r0 + SUB_ROWS)
        row = j * tm + r0 + lax.broadcasted_iota(jnp.int32, (SUB_ROWS, 1), 0)
        is_ctx = row >= n_lat
        x = piece(k, is_ctx)
        h = _norm_modulate(x, g_ref[...], mod_ref[0], is_ctx, 0, 1).astype(BF16)
        qk = jnp.dot(h, w_ref[:, :QK_COLS], preferred_element_type=F32)
        cos = cos_ref[rows, :]
        sin = sin_ref[rows, :]
        for c in range(QK_COLS // LANES):
            xc = qk[:, c * LANES:(c + 1) * LANES]
            ss = jnp.dot((xc * xc).astype(BF16), bd_ref[...], preferred_element_type=F32) * (1.0 / HEAD_DIM)
            xc = xc * lax.rsqrt(ss + EPS) * qkg_ref[c:c + 1, :]
            rot = jnp.where(first_half, pltpu.roll(xc, LANES - HEAD_DIM // 2, 1), pltpu.roll(xc, HEAD_DIM // 2, 1))
            o_ref[0, rows, c * LANES:(c + 1) * LANES] = (xc * cos + rot * sin).astype(BF16)
        o_ref[0, rows, QK_COLS:] = jnp.dot(h, w_ref[:, QK_COLS:], preferred_element_type=F32).astype(BF16)


def _in_body(*refs, n_lat, tm, n_x):
    x_refs = refs[:n_x]
    j = pl.program_id(0)
    is_last_tile = j == pl.num_programs(0) - 1
    _in_tile(lambda k, is_ctx: _stream_piece(x_refs, k, tm // SUB_ROWS, is_last_tile), j, *refs[n_x:], n_lat, tm)


def _in_proj(stream, ctx, modv, gain, w, cos, sin, qkg, bd, n_lat):
    B = stream.shape[0]
    T = stream.shape[1] + (0 if ctx is None else ctx.shape[1])
    tm = TOKEN_TILE
    x_specs, x_args = _stream_specs(stream, ctx, tm, lambda g: g[1], lambda g: g[0])
    return pl.pallas_call(
        functools.partial(_in_body, n_lat=n_lat, tm=tm, n_x=len(x_args)),
        grid=(T // tm, B),
        in_specs=x_specs + [
                  pl.BlockSpec((1, 16, D_MODEL), lambda j, b: (b, 0, 0)),
                  pl.BlockSpec((1, D_MODEL), lambda j, b: (0, 0)),
                  pl.BlockSpec((D_MODEL, N_PROJ), lambda j, b: (0, 0)),
                  pl.BlockSpec((tm, LANES), lambda j, b: (j, 0)),
                  pl.BlockSpec((tm, LANES), lambda j, b: (j, 0)),
                  pl.BlockSpec((8, LANES), lambda j, b: (0, 0)),
                  pl.BlockSpec((LANES, LANES), lambda j, b: (0, 0))],
        out_specs=pl.BlockSpec((1, tm, N_PROJ), lambda j, b: (b, j, 0)),
        out_shape=jax.ShapeDtypeStruct((B, T, N_PROJ), BF16),
        compiler_params=_cparams(2),
        name="in_proj",
    )(*x_args, modv, gain, w, cos, sin, qkg, bd)


ATT_BLOCK = 128
ATT_SPAN = ATT_BLOCK + 2 * WINDOW


ATT_HEAD_ORDER = (0, 4, 1, 5, 2, 6, 3, 7)


def _attend(qblk, k_parts, v_parts, biases, sinks):
    rows = ATT_GROUP * ATT_BLOCK
    rowi = lax.broadcasted_iota(jnp.int32, (rows, 1), 0)
    lane = lax.broadcasted_iota(jnp.int32, (1, LANES), 1)
    lower = lane < HEAD_DIM
    nt = (((1,), (1,)), ((), ()))
    heads = range(ATT_KV_HEADS)
    keep = [lower, jnp.logical_not(lower)]
    sink, scores = [], []
    for h in heads:
        qs = jnp.concatenate([jnp.where(keep[h], qblk[:, g * LANES:(g + 1) * LANES], jnp.zeros((), BF16))
                              for g in range(ATT_GROUP)], axis=0)
        col = jnp.full((rows, 1), sinks[ATT_GROUP * h + ATT_GROUP - 1], F32)
        for g in range(ATT_GROUP - 2, -1, -1):
            col = jnp.where(rowi < (g + 1) * ATT_BLOCK, sinks[ATT_GROUP * h + g], col)
        sink.append(col)
        pieces = []
        for k in k_parts:
            s = lax.dot_general(qs, k, nt, preferred_element_type=F32)
            pieces += [s[:, c * LANES:(c + 1) * LANES] for c in range(k.shape[0] // LANES)]
        scores.append([s if b is None else s + b for s, b in zip(pieces, biases)])
    top, probs = [], []
    for h in heads:
        m = scores[h][0]
        for s in scores[h][1:]:
            m = jnp.maximum(m, s)
        m = jnp.maximum(jnp.max(m, axis=-1, keepdims=True), sink[h])
        top.append(m)
        probs.append([jnp.exp((s - m).astype(BF16)) for s in scores[h]])
    normed = []
    for h in heads:
        acc = jnp.where(keep[h], 0.0, jnp.exp(sink[h] - top[h]))
        c0 = 0
        for v in v_parts[h]:
            n = v.shape[0] // LANES
            acc = acc + jnp.dot(jnp.concatenate(probs[h][c0:c0 + n], axis=1), v, preferred_element_type=F32)
            c0 += n
        normed.append(acc / pltpu.roll(acc, HEAD_DIM, 1))
    return jnp.concatenate([jnp.where(lower, normed[0][g * ATT_BLOCK:(g + 1) * ATT_BLOCK],
                                      normed[1][g * ATT_BLOCK:(g + 1) * ATT_BLOCK]) for g in range(ATT_GROUP)], axis=1)


def _att_body(sink_ref, q_ref, k_ref, v_ref, o_ref, v1_s, *, n_lat, n_ctx, with_ctx_out):
    sinks = [sink_ref[i] for i in range(ATT_HEADS)]
    lane = lax.broadcasted_iota(jnp.int32, (1, LANES), 1)
    vv = v_ref[0]
    v1_s[0] = jnp.where(lane < HEAD_DIM, vv, jnp.ones((), BF16))
    v1_s[1] = jnp.where(lane < HEAD_DIM, jnp.ones((), BF16), vv)
    k_ctx = k_ref[0, n_lat:n_lat + n_ctx, :]
    v_ctx = [v1_s[h, n_lat:n_lat + n_ctx, :] for h in range(ATT_KV_HEADS)]
    no_bias = [None] * (n_ctx // LANES)
    qi = lax.broadcasted_iota(jnp.int32, (ATT_GROUP * ATT_BLOCK, LANES), 0) % ATT_BLOCK
    ki = lax.broadcasted_iota(jnp.int32, (ATT_GROUP * ATT_BLOCK, LANES), 1)
    past_ok = jnp.where(ki >= qi, 0.0, NEG)
    ahead_ok = jnp.where(ki <= qi, 0.0, NEG)

    def block(q0, k0, n_keys, biases):
        k_parts = [k_ref[0, pl.ds(k0, n_keys), :], k_ctx]
        v_parts = [[v1_s[h, pl.ds(k0, n_keys), :], v_ctx[h]] for h in range(ATT_KV_HEADS)]
        out = _attend(q_ref[0, pl.ds(q0, ATT_BLOCK), :], k_parts, v_parts, biases + no_bias, sinks)
        o_ref[0, pl.ds(q0, ATT_BLOCK), :] = out.astype(BF16)

    def interior(i, carry):
        q0 = pl.multiple_of(i * ATT_BLOCK, ATT_BLOCK)
        block(q0, pl.multiple_of(q0 - WINDOW, ATT_BLOCK), ATT_SPAN, [past_ok, None, ahead_ok])
        return carry

    nq = n_lat // ATT_BLOCK
    block(0, 0, 2 * ATT_BLOCK, [None, ahead_ok])
    lax.fori_loop(1, nq - 1, interior, 0, unroll=7)
    block(n_lat - ATT_BLOCK, n_lat - 2 * ATT_BLOCK, 2 * ATT_BLOCK, [past_ok, None])
    if with_ctx_out:
        for c in range(n_ctx // ATT_BLOCK):
            r0 = n_lat + c * ATT_BLOCK
            out = _attend(q_ref[0, r0:r0 + ATT_BLOCK, :], [k_ctx], [[v_ctx[h]] for h in range(ATT_KV_HEADS)],
                          no_bias, sinks)
            o_ref[0, r0:r0 + ATT_BLOCK, :] = out.astype(BF16)
    else:
        o_ref[0, n_lat:, :] = jnp.zeros((n_ctx, ATT_WIDTH), BF16)


def _attention(p, sinks, n_lat, with_ctx_out):
    B, T, _ = p.shape
    return pl.pallas_call(
        functools.partial(_att_body, n_lat=n_lat, n_ctx=T - n_lat, with_ctx_out=with_ctx_out),
        grid=(B,),
        in_specs=[pl.BlockSpec(memory_space=pltpu.SMEM),
                  pl.BlockSpec((1, T, ATT_WIDTH), lambda b: (b, 0, COL_AQ // ATT_WIDTH)),
                  pl.BlockSpec((1, T, KV_WIDTH), lambda b: (b, 0, COL_AK // KV_WIDTH)),
                  pl.BlockSpec((1, T, KV_WIDTH), lambda b: (b, 0, COL_AV // KV_WIDTH))],
        out_specs=pl.BlockSpec((1, T, ATT_WIDTH), lambda b: (b, 0, 0)),
        out_shape=jax.ShapeDtypeStruct((B, T, ATT_WIDTH), BF16),
        scratch_shapes=[pltpu.VMEM((ATT_KV_HEADS, T, KV_WIDTH), BF16)],
        compiler_params=_cparams(1),
        name="attention",
    )(sinks, p, p, p)


GLA_BLOCK_A = 256
GLA_BLOCK_B = 128
CONV_PAD = 8
EPI_ROWS = 256


def _log_sigmoid(z):
    return jnp.minimum(z, 0.0) - jnp.log1p(jnp.exp(-jnp.abs(z)))


def _gla_body(gq_ref, gk_ref, gv_ref, gr_ref, cb_ref, cc_ref, ch_ref, gt_ref, wg_ref, gbias_ref, ng_ref,
              cw_ref, bd_ref, cum_ref, o_ref, qe_s, ke_s, dec_s, upd_s, prev_s, st_s, o_s, u_s, *, n_lat, n_ctx,
              with_ctx_out):
    T = n_lat + n_ctx
    C = GLA_CHUNK
    nt = (((1,), (1,)), ((), ()))
    tn = (((0,), (0,)), ((), ()))

    sr = lax.broadcasted_iota(jnp.int32, (GLA_WIDTH, 2 * GLA_QK_WIDTH), 0) // GLA_DV
    sl = (lax.broadcasted_iota(jnp.int32, (GLA_WIDTH, 2 * GLA_QK_WIDTH), 1) % GLA_QK_WIDTH) // GLA_DK
    state_mask2 = sr == sl

    def factors(i, carry):
        r0 = pl.multiple_of(i * GLA_BLOCK_A, GLA_BLOCK_A)
        rows = pl.ds(r0, GLA_BLOCK_A)
        gt = gt_ref[0, rows, :]
        q = gq_ref[0, rows, :].astype(F32) * (GLA_DK ** -0.5)
        k = gk_ref[0, rows, :].astype(F32)
        v = gv_ref[0, rows, :]
        z2 = jnp.dot(gt, wg_ref[...], preferred_element_type=F32) + gbias_ref[0:1, :]
        g2 = _log_sigmoid(z2) * (1.0 / GLA_GATE_NORM)
        cum = []
        for d in range(2):
            g = g2[:, d * GLA_QK_WIDTH:(d + 1) * GLA_QK_WIDTH]
            g_hi = g.astype(BF16)
            g_lo = (g - g_hi.astype(F32)).astype(BF16)
            cum.append(jnp.dot(cum_ref[d], jnp.concatenate([g_hi, g_lo], axis=1), preferred_element_type=F32))
        kl = []
        for d in range(2):
            b = cum[d][:GLA_BLOCK_A, :GLA_QK_WIDTH] + cum[d][:GLA_BLOCK_A, GLA_QK_WIDTH:]
            tot = cum[d][GLA_BLOCK_A:, :GLA_QK_WIDTH] + cum[d][GLA_BLOCK_A:, GLA_QK_WIDTH:]
            dec = jnp.exp(tot)
            ke = k * jnp.exp(-b)
            qe_s[d, rows, :] = (q * jnp.exp(b)).astype(BF16)
            ke_s[d, rows, :] = ke.astype(BF16)
            dec_s[d, rows, :] = dec
            kl.append((ke * dec).astype(BF16))
        kl2 = jnp.concatenate(kl, axis=1)
        for cc in range(GLA_BLOCK_A // C):
            upd = lax.dot_general(v[cc * C:(cc + 1) * C], kl2[cc * C:(cc + 1) * C], tn, preferred_element_type=F32)
            upd_s[i * (GLA_BLOCK_A // C) + cc] = jnp.where(state_mask2, upd, 0.0)
        return carry

    lax.fori_loop(0, T // GLA_BLOCK_A, factors, 0, unroll=True)

    nc_lat = n_lat // C
    nc_ctx = n_ctx // C
    st_s[...] = jnp.zeros_like(st_s)

    def scan(i, carry):
        in_ctx = i < nc_ctx
        cf = jnp.where(in_ctx, nc_lat + i, i - nc_ctx)
        cb = jnp.where(in_ctx, nc_lat + nc_ctx - 1 - i, nc_lat - 1 - (i - nc_ctx))
        for d, cid in enumerate((cf, cb)):
            st = st_s[d]
            prev_s[cid, :, d * GLA_QK_WIDTH:(d + 1) * GLA_QK_WIDTH] = st.astype(BF16)
            st_s[d] = (st * dec_s[d, pl.ds(pl.multiple_of(cid * C, C), 1), :]
                       + upd_s[cid, :, d * GLA_QK_WIDTH:(d + 1) * GLA_QK_WIDTH])
        return carry

    lax.fori_loop(0, nc_lat + nc_ctx, scan, 0)

    RB = GLA_BLOCK_B
    k_rows = lax.broadcasted_iota(jnp.int32, (GLA_HEADS * RB, GLA_QK_WIDTH), 0) // RB
    k_lanes = lax.broadcasted_iota(jnp.int32, (GLA_HEADS * RB, GLA_QK_WIDTH), 1) // GLA_DK
    key_heads = k_rows == k_lanes
    v_rows = lax.broadcasted_iota(jnp.int32, (GLA_HEADS * RB, GLA_WIDTH), 0) // RB
    v_lanes = lax.broadcasted_iota(jnp.int32, (GLA_HEADS * RB, GLA_WIDTH), 1) // GLA_DV
    value_heads = v_rows == v_lanes
    qr = lax.broadcasted_iota(jnp.int32, (RB, GLA_HEADS * RB), 0)
    kc = lax.broadcasted_iota(jnp.int32, (RB, GLA_HEADS * RB), 1) % RB
    same_chunk = (qr // C) == (kc // C)
    forward = kc <= qr
    zero = jnp.zeros((), BF16)

    def outputs(i, carry):
        r0 = pl.multiple_of(i * RB, RB)
        rows = pl.ds(r0, RB)
        v = gv_ref[0, rows, :]
        v4 = jnp.where(value_heads, jnp.concatenate([v] * GLA_HEADS, axis=0), zero)
        qe = [qe_s[d, rows, :] for d in range(2)]
        att = []
        for d in range(2):
            ke4 = jnp.where(key_heads, jnp.concatenate([ke_s[d, rows, :]] * GLA_HEADS, axis=0), zero)
            att.append(lax.dot_general(qe[d], ke4, nt, preferred_element_type=F32))
        both = jnp.where(same_chunk, jnp.where(forward, att[0], att[1]), 0.0).astype(BF16)
        qe2 = jnp.concatenate(qe, axis=1)
        inter = [lax.dot_general(qe2[cc * C:(cc + 1) * C], prev_s[i * (RB // C) + cc], nt,
                                 preferred_element_type=F32) for cc in range(RB // C)]
        o_s[rows, :] = jnp.dot(both, v4, preferred_element_type=F32) + jnp.concatenate(inter, axis=0)
        return carry

    out_rows = T if with_ctx_out else n_lat
    n_out = out_rows // RB
    lax.fori_loop(0, n_out, outputs, 0, unroll=6 if n_out % 6 == 0 else 4)
    if not with_ctx_out:
        o_ref[0, n_lat:, :] = jnp.zeros((n_ctx, GLA_WIDTH + CONV_WIDTH), BF16)

    u_s[0:CONV_PAD, :] = jnp.zeros((CONV_PAD, CONV_WIDTH), F32)
    u_s[CONV_PAD + T:, :] = jnp.zeros((CONV_PAD, CONV_WIDTH), F32)
    u_s[CONV_PAD:CONV_PAD + T, :] = cc_ref[0].astype(F32) * ch_ref[0].astype(F32)
    w0 = cw_ref[0:1, :]
    w1 = cw_ref[1:2, :]
    w2 = cw_ref[2:3, :]
    for e in range(out_rows // EPI_ROWS):
        r0 = e * EPI_ROWS
        o = o_s[r0:r0 + EPI_ROWS, :]
        ss = jnp.dot((o * o).astype(BF16), bd_ref[...], preferred_element_type=F32) * (1.0 / GLA_DV)
        on = o * lax.rsqrt(ss + EPS) * ng_ref[...]
        r = gr_ref[0, r0:r0 + EPI_ROWS, :].astype(F32)
        o_ref[0, r0:r0 + EPI_ROWS, 0:GLA_WIDTH] = (on * _silu(r)).astype(BF16)
        t = r0 + lax.broadcasted_iota(jnp.int32, (EPI_ROWS, 1), 0)
        up = u_s[CONV_PAD + r0 - 1:CONV_PAD + r0 - 1 + EPI_ROWS, :]
        mid = u_s[CONV_PAD + r0:CONV_PAD + r0 + EPI_ROWS, :]
        dn = u_s[CONV_PAD + r0 + 1:CONV_PAD + r0 + 1 + EPI_ROWS, :]
        if r0 <= n_lat < r0 + EPI_ROWS:
            up = jnp.where(t == n_lat, 0.0, up)
        if r0 <= n_lat - 1 < r0 + EPI_ROWS:
            dn = jnp.where(t == n_lat - 1, 0.0, dn)
        conv = w0 * up + w1 * mid + w2 * dn
        o_ref[0, r0:r0 + EPI_ROWS, GLA_WIDTH:] = (cb_ref[0, r0:r0 + EPI_ROWS, :].astype(F32) * conv).astype(BF16)


def _gla_cum_matrices():
    i = np.arange(GLA_BLOCK_A)
    same = (i[:, None] // GLA_CHUNK) == (i[None, :] // GLA_CHUNK)
    fwd = same & (i[None, :] <= i[:, None])
    bwd = same & (i[None, :] >= i[:, None])
    mats = np.stack([np.concatenate([fwd, same], axis=0), np.concatenate([bwd, same], axis=0)])
    return jnp.asarray(mats.astype(np.float32), dtype=BF16)


def _gla_conv(p, wg, gbias, ng, cw, bd, cum, n_lat, with_ctx_out):
    B, T, _ = p.shape
    nc = T // GLA_CHUNK

    def col(width, start):
        return pl.BlockSpec((1, T, width), lambda b: (b, 0, start // width))

    def const(shape):
        return pl.BlockSpec(shape, lambda b: (0,) * len(shape))

    return pl.pallas_call(
        functools.partial(_gla_body, n_lat=n_lat, n_ctx=T - n_lat, with_ctx_out=with_ctx_out),
        grid=(B,),
        in_specs=[col(GLA_QK_WIDTH, COL_GQ), col(GLA_QK_WIDTH, COL_GK), col(GLA_WIDTH, COL_GV), col(GLA_WIDTH, COL_GR),
                  col(CONV_WIDTH, COL_CB), col(CONV_WIDTH, COL_CC), col(CONV_WIDTH, COL_CH), col(LANES, COL_GT),
                  const((LANES, 2 * GLA_QK_WIDTH)), const((8, 2 * GLA_QK_WIDTH)),
                  const((1, GLA_WIDTH)), const((8, CONV_WIDTH)), const((GLA_WIDTH, GLA_WIDTH)),
                  const((2, 2 * GLA_BLOCK_A, GLA_BLOCK_A))],
        out_specs=pl.BlockSpec((1, T, GLA_WIDTH + CONV_WIDTH), lambda b: (b, 0, 0)),
        out_shape=jax.ShapeDtypeStruct((B, T, GLA_WIDTH + CONV_WIDTH), BF16),
        scratch_shapes=[pltpu.VMEM((2, T, GLA_QK_WIDTH), BF16), pltpu.VMEM((2, T, GLA_QK_WIDTH), BF16),
                        pltpu.VMEM((2, T, GLA_QK_WIDTH), F32),
                        pltpu.VMEM((nc, GLA_WIDTH, 2 * GLA_QK_WIDTH), F32),
                        pltpu.VMEM((nc, GLA_WIDTH, 2 * GLA_QK_WIDTH), BF16),
                        pltpu.VMEM((2, GLA_WIDTH, GLA_QK_WIDTH), F32),
                        pltpu.VMEM((T, GLA_WIDTH), F32),
                        pltpu.VMEM((T + 2 * CONV_PAD, CONV_WIDTH), F32)],
        compiler_params=_cparams(1),
        name="gla_conv",
    )(p, p, p, p, p, p, p, p, wg, gbias, ng, cw, bd, cum)


def _route(logits_t):
    mx = jnp.max(logits_t, axis=0, keepdims=True)
    ex = jnp.exp(logits_t - mx)
    probs = ex / jnp.sum(ex, axis=0, keepdims=True)
    P = [probs[e:e + 1] for e in range(N_EXPERTS)]
    scores = []
    for g in range(N_GROUPS):
        a, b, c, d = P[4 * g:4 * g + 4]
        scores.append(jnp.maximum(jnp.maximum(jnp.maximum(a + b, a + c), jnp.maximum(a + d, b + c)),
                                  jnp.maximum(b + d, c + d)))
    best = jnp.maximum(jnp.maximum(scores[0], scores[1]), jnp.maximum(scores[2], scores[3]))
    taken = jnp.zeros_like(best, dtype=jnp.bool_)
    sel = []
    for g in range(N_GROUPS):
        s = (scores[g] == best) & jnp.logical_not(taken)
        sel.append(s)
        taken = taken | s
    gsel = jnp.where(sel[1], 1.0, 0.0) + jnp.where(sel[2], 2.0, 0.0) + jnp.where(sel[3], 3.0, 0.0)
    ig = [jnp.where(sel[0], P[j], jnp.where(sel[1], P[4 + j], jnp.where(sel[2], P[8 + j], P[12 + j])))
          for j in range(EXPERTS_PER_GROUP)]

    def first_max(vals):
        v = jnp.maximum(jnp.maximum(vals[0], vals[1]), jnp.maximum(vals[2], vals[3]))
        tk = jnp.zeros_like(v, dtype=jnp.bool_)
        hot = []
        for x in vals:
            s = (x == v) & jnp.logical_not(tk)
            hot.append(s)
            tk = tk | s
        idx = jnp.where(hot[1], 1.0, 0.0) + jnp.where(hot[2], 2.0, 0.0) + jnp.where(hot[3], 3.0, 0.0)
        return v, hot, idx

    _, hot1, i1 = first_max(ig)
    _, _, i2 = first_max([jnp.where(hot1[j], -1.0, ig[j]) for j in range(EXPERTS_PER_GROUP)])
    lo = jnp.minimum(i1, i2)
    hi = jnp.maximum(i1, i2)
    pair = jnp.where(lo == 0.0, hi - 1.0, jnp.where(lo == 1.0, hi + 1.0, 5.0))
    return gsel * N_PAIRS + pair


def _class_rank(cls, tri_ref, cnt_s):
    n = cls.shape[1]
    cid = lax.broadcasted_iota(jnp.int32, (CLS_ROWS, n), 0).astype(F32)
    onehot = jnp.where(cls == cid, 1.0, 0.0)
    segs = [onehot[:, k * LANES:(k + 1) * LANES] for k in range(n // LANES)]
    before = jnp.dot(jnp.concatenate(segs, axis=0).astype(BF16), tri_ref[...], preferred_element_type=F32)
    base = cnt_s[...]
    ranks = []
    for k, seg in enumerate(segs):
        ranks.append(jnp.sum(seg * (before[k * CLS_ROWS:(k + 1) * CLS_ROWS] + base), axis=0, keepdims=True))
        base = base + jnp.sum(seg, axis=1, keepdims=True)
    cnt_s[...] = base
    return jnp.concatenate(ranks, axis=1)


def _out_body(*refs, n_lat, tm, n_x):
    x_refs = refs[:n_x]
    ya_ref, yg_ref, mod_ref, wo_ref, g_ref, wr_ref, br_ref, tri_ref, xo_ref, rt_ref, cnt_ref, cnt_s, logit_s = refs[n_x:]
    j = pl.program_id(1)
    is_last_tile = j == pl.num_programs(1) - 1

    @pl.when((pl.program_id(0) == 0) & (j == 0))
    def _():
        cnt_s[...] = jnp.zeros_like(cnt_s)

    mod = mod_ref[0]
    subs = list(enumerate(range(0, tm, SUB_ROWS)))
    ys = [jnp.dot(ya_ref[0, r0:r0 + SUB_ROWS, :], wo_ref[0:ATT_WIDTH, :], preferred_element_type=F32)
          + jnp.dot(yg_ref[0, r0:r0 + SUB_ROWS, :], wo_ref[ATT_WIDTH:, :], preferred_element_type=F32) for _, r0 in subs]
    h2s = []
    for k, r0 in subs:
        row = j * tm + r0 + lax.broadcasted_iota(jnp.int32, (SUB_ROWS, 1), 0)
        is_ctx = row >= n_lat
        xn = _stream_piece(x_refs, k, tm // SUB_ROWS, is_last_tile) + _row_mod(mod, is_ctx, 2) * ys[k]
        xo_ref[0, r0:r0 + SUB_ROWS, :] = xn
        h2s.append(_norm_modulate(xn, g_ref[...], mod, is_ctx, 3, 4).astype(BF16))
    for k, r0 in subs:
        logit_s[r0:r0 + SUB_ROWS, :] = jnp.dot(h2s[k], wr_ref[...], preferred_element_type=F32) + br_ref[0:1, :]
    cls = _route(logit_s[...].T[0:N_EXPERTS, :])
    rank = _class_rank(cls, tri_ref, cnt_s)
    rt_ref[0] = jnp.concatenate([cls, rank, jnp.zeros((6, tm), F32)], axis=0).astype(jnp.int32)
    cnt_ref[...] = jnp.broadcast_to(cnt_s[...], (CLS_ROWS, LANES)).astype(jnp.int32)


def _out_proj(ya, yg, stream, ctx, modv, wo, gain, wr, br, tri, n_lat, rows, tm):
    B = stream.shape[0]
    nj = rows // tm
    x_specs, x_args = _stream_specs(stream, ctx, tm, lambda g: g[0], lambda g: g[1])
    return pl.pallas_call(
        functools.partial(_out_body, n_lat=n_lat, tm=tm, n_x=len(x_args)),
        grid=(B, nj),
        in_specs=x_specs + [
                  pl.BlockSpec((1, tm, ATT_WIDTH), lambda b, j: (b, j, 0)),
                  pl.BlockSpec((1, tm, GLA_WIDTH + CONV_WIDTH), lambda b, j: (b, j, 0)),
                  pl.BlockSpec((1, 16, D_MODEL), lambda b, j: (b, 0, 0)),
                  pl.BlockSpec((D_MODEL, D_MODEL), lambda b, j: (0, 0)),
                  pl.BlockSpec((1, D_MODEL), lambda b, j: (0, 0)),
                  pl.BlockSpec((D_MODEL, LANES), lambda b, j: (0, 0)),
                  pl.BlockSpec((8, LANES), lambda b, j: (0, 0)),
                  pl.BlockSpec((LANES, LANES), lambda b, j: (0, 0))],
        out_specs=[pl.BlockSpec((1, tm, D_MODEL), lambda b, j: (b, j, 0)),
                   pl.BlockSpec((1, 8, tm), lambda b, j: (b * nj + j, 0, 0)),
                   pl.BlockSpec((CLS_ROWS, LANES), lambda b, j: (0, 0))],
        out_shape=[jax.ShapeDtypeStruct((B, rows, D_MODEL), F32),
                   jax.ShapeDtypeStruct((B * nj, 8, tm), jnp.int32),
                   jax.ShapeDtypeStruct((CLS_ROWS, LANES), jnp.int32)],
        scratch_shapes=[pltpu.VMEM((CLS_ROWS, 1), F32), pltpu.VMEM((tm, LANES), F32)],
        compiler_params=_cparams(2),
        name="out_proj_router",
    )(*x_args, ya, yg, modv, wo, gain, wr, br, tri)


ROW_UNROLL = 8
IDX_STRIDE = 1024


def _idx_slot(idx, slot, tm):
    return idx.at[pl.ds(pl.multiple_of(slot * IDX_STRIDE, IDX_STRIDE), tm)]


def _issue_rows(tm, idx, slot, make_copy):
    base = slot * IDX_STRIDE

    def trip(i, c):
        for u in range(ROW_UNROLL):
            r = i * ROW_UNROLL + u
            make_copy(r, idx[base + r]).start(priority=u % 2)
        return c

    lax.fori_loop(0, tm // ROW_UNROLL, trip, 0)


PAD_SLOTS = 32
PAD_BITS = 9


def _zero_pad_rows(pad_ref, zeros_ref, hs_ref, sem):
    def pieces(c, fn):
        start = pad_ref[c]
        n = pad_ref[PAD_SLOTS + c]
        for bit in range(PAD_BITS):
            size = 1 << bit
            below = n & (size - 1)

            @pl.when((n & size) != 0)
            def _():
                fn(pltpu.make_async_copy(zeros_ref.at[pl.ds(0, size)], hs_ref.at[pl.ds(start + below, size)], sem))

        def block(i, carry):
            fn(pltpu.make_async_copy(zeros_ref, hs_ref.at[pl.ds(start + n + i * MOE_TILE, MOE_TILE)], sem))
            return carry

        lax.fori_loop(0, pad_ref[2 * PAD_SLOTS + c], block, 0)

    def start_all(c, carry):
        pieces(c, lambda cp: cp.start())
        return carry

    def wait_all(c, carry):
        pieces(c, lambda cp: cp.wait())
        return carry

    lax.fori_loop(0, N_CLASSES + 1, start_all, 0)
    lax.fori_loop(0, N_CLASSES + 1, wait_all, 0)


def _issue_rows_inline(tm, idx, slot, make_copy):
    base = slot * IDX_STRIDE
    for r in range(tm):
        make_copy(r, idx[base + r]).start(priority=r % 2)


def _disp_body(pad_ref, x_ref, mod_ref, g_ref, dest_ref, hs_ref, buf0, buf1, buf2, idx, sem_i, sem_d, sem_z, *, n_lat,
               tm, nj, n_steps):
    bufs = (buf0, buf1, buf2)
    j = pl.program_id(1)
    s = pl.program_id(0) * nj + j
    last = n_steps - 1

    def idx_fetch(step):
        return pltpu.make_async_copy(dest_ref.at[step], _idx_slot(idx, step % 3, tm), sem_i.at[step % 3])

    def drain(k):
        pltpu.make_async_copy(bufs[k], hs_ref.at[pl.ds(0, tm)], sem_d.at[k]).wait()

    def row_copy(k):
        return lambda r, d: pltpu.make_async_copy(bufs[k].at[r], hs_ref.at[d], sem_d.at[k])

    def normalise(k):
        row = j * tm + lax.broadcasted_iota(jnp.int32, (tm, 1), 0)
        h2 = _norm_modulate(x_ref[0], g_ref[...], mod_ref[0], row >= n_lat, 3, 4)
        bufs[k][...] = h2.reshape(tm, 8, LANES)

    @pl.when(s == 0)
    def _():
        idx_fetch(0).start()
        buf1[...] = jnp.zeros((tm, 8, LANES), F32)
        _zero_pad_rows(pad_ref, buf1.at[pl.ds(0, MOE_TILE)], hs_ref, sem_z)
        normalise(0)

    @pl.when(s < last)
    def _():
        idx_fetch(s + 1).start()

    idx_fetch(s).wait()

    for k in range(3):
        prev = (k + 2) % 3

        @pl.when((s > 0) & (s % 3 == k))
        def _():
            _issue_rows_inline(tm, idx, prev, row_copy(prev))
            normalise(k)

        @pl.when((s > 1) & (s % 3 == k))
        def _():
            drain((k + 1) % 3)

        @pl.when((s == last) & (s % 3 == k))
        def _():
            _issue_rows(tm, idx, k, row_copy(k))
            drain(k)
            if last > 0:
                drain(prev)


def _dispatch(pad, xx, modv, gain, dest, n_rows, n_lat, tm):
    B, rows, _ = xx.shape
    nj = rows // tm
    assert tm >= MOE_TILE
    return pl.pallas_call(
        functools.partial(_disp_body, n_lat=n_lat, tm=tm, nj=nj, n_steps=B * nj),
        grid=(B, nj),
        in_specs=[pl.BlockSpec(memory_space=pltpu.SMEM),
                  pl.BlockSpec((1, tm, D_MODEL), lambda b, j: (b, j, 0)),
                  pl.BlockSpec((1, 16, D_MODEL), lambda b, j: (b, 0, 0)),
                  pl.BlockSpec((1, D_MODEL), lambda b, j: (0, 0)),
                  pl.BlockSpec(memory_space=pl.ANY)],
        out_specs=pl.BlockSpec(memory_space=pl.ANY),
        scratch_shapes=[pltpu.VMEM((tm, 8, LANES), F32), pltpu.VMEM((tm, 8, LANES), F32),
                        pltpu.VMEM((tm, 8, LANES), F32), pltpu.SMEM((3 * IDX_STRIDE,), jnp.int32),
                        pltpu.SemaphoreType.DMA((3,)), pltpu.SemaphoreType.DMA((3,)), pltpu.SemaphoreType.DMA],
        out_shape=jax.ShapeDtypeStruct((n_rows, 8, LANES), F32),
        compiler_params=_cparams(2),
        name="dispatch",
    )(pad, xx, modv, gain, dest)


def _moe_body(tile_ref, e_lo_ref, e_hi_ref, valid_ref, hs_ref, wg1, wu1, wd1, wg2, wu2, wd2, wr_ref, br_ref, ys_ref):
    g = pl.program_id(0)
    tm = hs_ref.shape[0]

    @pl.when(valid_ref[g] == 1)
    def _():
        x = hs_ref[...].reshape(tm, D_MODEL)
        e_lo = e_lo_ref[g]
        e_hi = e_hi_ref[g]
        dw = wr_ref[pl.ds(e_lo, 1), :] - wr_ref[pl.ds(e_hi, 1), :]
        d = jnp.sum(x * dw, axis=-1, keepdims=True) + (br_ref[e_lo] - br_ref[e_hi])
        w_lo = jax.nn.sigmoid(d)
        w_hi = jax.nn.sigmoid(-d)
        h = x.astype(BF16)

        y = None
        for wg, wu, wd, w in ((wg1, wu1, wd1, w_lo), (wg2, wu2, wd2, w_hi)):
            for c0 in range(0, D_EXPERT, EXPERT_SEG):
                seg = slice(c0, c0 + EXPERT_SEG)
                a = _silu(jnp.dot(h, wg[0, :, seg], preferred_element_type=F32)) * jnp.dot(
                    h, wu[0, :, seg], preferred_element_type=F32)
                part = jnp.dot((a * w).astype(BF16), wd[0, seg, :], preferred_element_type=F32)
                y = part if y is None else y + part
        ys_ref[...] = y.reshape(tm, 8, LANES)

    @pl.when(valid_ref[g] == 0)
    def _():
        ys_ref[...] = jnp.zeros_like(ys_ref)


def _moe(hs, tile, e_lo, e_hi, valid, wg, wu, wd, wr_t, br, first_expert):
    n_tiles = tile.shape[0]
    tm = MOE_TILE

    def w_in(sel):
        return pl.BlockSpec((1, D_MODEL, D_EXPERT), lambda g, t, lo, hi, v: (first_expert + (lo, hi)[sel][g], 0, 0))

    def w_out(sel):
        return pl.BlockSpec((1, D_EXPERT, D_MODEL), lambda g, t, lo, hi, v: (first_expert + (lo, hi)[sel][g], 0, 0))

    return pl.pallas_call(
        _moe_body,
        grid_spec=pltpu.PrefetchScalarGridSpec(
            num_scalar_prefetch=4,
            grid=(n_tiles,),
            in_specs=[pl.BlockSpec((tm, 8, LANES), lambda g, t, lo, hi, v: (t[g], 0, 0)),
                      w_in(0), w_in(0), w_out(0), w_in(1), w_in(1), w_out(1),
                      pl.BlockSpec((N_EXPERTS, D_MODEL), lambda g, t, lo, hi, v: (0, 0)),
                      pl.BlockSpec(memory_space=pltpu.SMEM)],
            out_specs=pl.BlockSpec((tm, 8, LANES), lambda g, t, lo, hi, v: (g, 0, 0))),
        out_shape=jax.ShapeDtypeStruct((n_tiles * tm, 8, LANES), F32),
        compiler_params=_cparams(1),
        name="moe_pairs",
    )(tile, e_lo, e_hi, valid, hs, wg, wu, wd, wg, wu, wd, wr_t, br)


def _fin_body(x_ref, mod_ref, dest_ref, ys_ref, xo_ref, buf0, buf1, buf2, idx, sem_i, sem_d, *, n_lat, tm, nj,
              n_steps):
    bufs = (buf0, buf1, buf2)
    j = pl.program_id(1)
    s = pl.program_id(0) * nj + j
    last = n_steps - 1

    def idx_fetch(step):
        return pltpu.make_async_copy(dest_ref.at[step], _idx_slot(idx, step % 3, tm), sem_i.at[step % 3])

    def row_copy(k):
        return lambda r, d: pltpu.make_async_copy(ys_ref.at[d], bufs[k].at[r], sem_d.at[k])

    def add_rows(k):
        row = j * tm + lax.broadcasted_iota(jnp.int32, (tm, 1), 0)
        xo_ref[0] = x_ref[0] + _row_mod(mod_ref[0], row >= n_lat, 5) * bufs[k][...].reshape(tm, D_MODEL)

    @pl.when(s == 0)
    def _():
        for t in range(min(2, n_steps)):
            idx_fetch(t).start()
            idx_fetch(t).wait()
            _issue_rows(tm, idx, t, row_copy(t))
        if last >= 2:
            idx_fetch(2).start()

    for k in range(3):
        ahead = (k + 2) % 3

        @pl.when((s % 3 == k) & (s + 2 <= last))
        def _():
            pltpu.make_async_copy(ys_ref.at[pl.ds(0, tm)], bufs[k], sem_d.at[k]).wait()
            idx_fetch(s + 2).wait()
            _issue_rows_inline(tm, idx, ahead, row_copy(ahead))
            add_rows(k)

        @pl.when((s % 3 == k) & (s + 2 > last))
        def _():
            pltpu.make_async_copy(ys_ref.at[pl.ds(0, tm)], bufs[k], sem_d.at[k]).wait()
            add_rows(k)

    @pl.when(s + 3 <= last)
    def _():
        idx_fetch(s + 3).start()


def _combine(xx, modv, dest, ys, n_lat, tm):
    B, rows, _ = xx.shape
    nj = rows // tm
    return pl.pallas_call(
        functools.partial(_fin_body, n_lat=n_lat, tm=tm, nj=nj, n_steps=B * nj),
        grid=(B, nj),
        in_specs=[pl.BlockSpec((1, tm, D_MODEL), lambda b, j: (b, j, 0)),
                  pl.BlockSpec((1, 16, D_MODEL), lambda b, j: (b, 0, 0)),
                  pl.BlockSpec(memory_space=pl.ANY),
                  pl.BlockSpec(memory_space=pl.ANY)],
        out_specs=pl.BlockSpec((1, tm, D_MODEL), lambda b, j: (b, j, 0)),
        scratch_shapes=[pltpu.VMEM((tm, 8, LANES), F32), pltpu.VMEM((tm, 8, LANES), F32),
                        pltpu.VMEM((tm, 8, LANES), F32), pltpu.SMEM((3 * IDX_STRIDE,), jnp.int32),
                        pltpu.SemaphoreType.DMA((3,)), pltpu.SemaphoreType.DMA((3,))],
        out_shape=jax.ShapeDtypeStruct((B, rows, D_MODEL), F32),
        compiler_params=_cparams(2),
        name="combine",
    )(xx, modv, dest, ys)


def _rope_tables(n_lat, n_ctx):
    rows = n_lat // GRID_W
    row, col = jnp.meshgrid(jnp.arange(rows), jnp.arange(GRID_W), indexing="ij")
    n_freq = HEAD_DIM // 4
    inv_freq = ROPE_BASE ** (-jnp.arange(n_freq, dtype=F32) / n_freq)
    ang = jnp.concatenate([row.reshape(-1, 1).astype(F32) * inv_freq, col.reshape(-1, 1).astype(F32) * inv_freq],
                          axis=-1)
    cos = jnp.tile(jnp.cos(ang), (1, LANES // (HEAD_DIM // 2)))
    sin = jnp.tile(jnp.sin(ang), (1, LANES // (HEAD_DIM // 2)))
    sign = jnp.where((jnp.arange(LANES) % HEAD_DIM) < HEAD_DIM // 2, -1.0, 1.0).astype(F32)
    cos = jnp.concatenate([cos, jnp.ones((n_ctx, LANES), F32)], axis=0)
    sin = jnp.concatenate([sin * sign, jnp.zeros((n_ctx, LANES), F32)], axis=0)
    return cos, sin


def _block_diag_ones(n, blk):
    i = np.arange(n) // blk
    return jnp.asarray((i[:, None] == i[None, :]).astype(np.float32), dtype=BF16)


def _routing_tables(counts, n_tiles):
    tiles_c = (counts + MOE_TILE - 1) // MOE_TILE
    tile_end = jnp.cumsum(tiles_c)
    tile_start = tile_end - tiles_c
    off = tile_start * MOE_TILE
    total = tile_end[-1]
    g = jnp.arange(n_tiles, dtype=jnp.int32)
    valid = (g < total).astype(jnp.int32)
    g_eff = jnp.minimum(g, total - 1)
    c_of = jnp.sum((g_eff[:, None] >= tile_end[None, :]).astype(jnp.int32), axis=1)
    group = c_of // N_PAIRS
    pair = c_of % N_PAIRS
    lo = jnp.asarray(PAIR_LO, jnp.int32)
    hi = jnp.asarray(PAIR_HI, jnp.int32)
    e_lo = group * EXPERTS_PER_GROUP + jnp.sum((pair[:, None] == jnp.arange(N_PAIRS)[None, :]) * lo[None, :], axis=1)
    e_hi = group * EXPERTS_PER_GROUP + jnp.sum((pair[:, None] == jnp.arange(N_PAIRS)[None, :]) * hi[None, :], axis=1)
    fill = jnp.zeros((PAD_SLOTS - N_CLASSES - 1,), jnp.int32)
    pad = jnp.concatenate([off + counts, (total * MOE_TILE)[None], fill,
                           tiles_c * MOE_TILE - counts, jnp.zeros((1,), jnp.int32), fill,
                           jnp.zeros((N_CLASSES,), jnp.int32), (n_tiles - total)[None], fill])
    return off.astype(jnp.int32), pad.astype(jnp.int32), g_eff, e_lo.astype(jnp.int32), e_hi.astype(jnp.int32), valid


def kernel(x, c, ctx, c_ctx, w_ada, b_ada, norm_mix_g, norm_ffn_g, w_in, q_norm_g, k_norm_g, attn_sink, gla_gate_w,
           gla_gate_b, gla_norm_g, conv_w, w_out, w_router, b_router, w_gate_e, w_up_e, w_down_e):
    B, S, D = x.shape
    L = ctx.shape[1]
    T = S + L
    assert D == D_MODEL and T % TOKEN_TILE == 0 and S % LAT_TILE == 0 and S % GRID_W == 0
    assert S % ATT_BLOCK == 0 and L % ATT_BLOCK == 0 and S >= ATT_SPAN and T % EPI_ROWS == 0

    cond_rows = -(-(B + 1) // 8) * 8
    cond = jnp.zeros((cond_rows, D), F32).at[:B].set(c).at[B].set(c_ctx)
    mod_all = _modulation(cond, w_ada, b_ada)

    cos, sin = _rope_tables(S, L)
    bd_head = _block_diag_ones(LANES, HEAD_DIM)
    bd_gla = _block_diag_ones(GLA_WIDTH, GLA_DV)
    cum = _gla_cum_matrices()
    tri = jnp.asarray(np.triu(np.ones((LANES, LANES), np.float32), 1), dtype=BF16)
    stream, stream_ctx = x, ctx

    order = jnp.asarray(ATT_HEAD_ORDER)
    wg_all = w_gate_e.reshape(DEPTH * N_EXPERTS, D, D_EXPERT).astype(BF16)
    wu_all = w_up_e.reshape(DEPTH * N_EXPERTS, D, D_EXPERT).astype(BF16)
    wd_all = w_down_e.reshape(DEPTH * N_EXPERTS, D_EXPERT, D).astype(BF16)

    def mod_table(l):
        m_lat = mod_all[l, :B].reshape(B, 6, D)
        m_ctx = jnp.broadcast_to(mod_all[l, B].reshape(1, 6, D), (B, 6, D))
        return jnp.concatenate([m_lat, m_ctx, jnp.zeros((B, 4, D), F32)], axis=1)

    def in_proj_args(l):
        wl = w_in[l]
        wq = wl[:, :ATT_WIDTH].reshape(D, ATT_HEADS, HEAD_DIM)[:, order, :].reshape(D, ATT_WIDTH)
        w_perm = jnp.concatenate([wq, wl[:, ATT_WIDTH:SRC_GATES], wl[:, SRC_CONV:], wl[:, SRC_GATES:SRC_CONV],
                                  jnp.zeros((D, N_PROJ - wl.shape[1]), F32)], axis=1).astype(BF16)
        qg = jnp.tile(q_norm_g[l], LANES // HEAD_DIM) * (HEAD_DIM ** -0.5)
        kg = jnp.tile(k_norm_g[l], LANES // HEAD_DIM)
        qkg = jnp.stack([qg] * (ATT_WIDTH // LANES) + [kg] + [jnp.zeros_like(kg)] * 3)
        return mod_table(l), norm_mix_g[l].reshape(1, D), w_perm, cos, sin, qkg, bd_head

    p = _in_proj(stream, stream_ctx, *in_proj_args(0), S)
    for l in range(DEPTH):
        last = l == DEPTH - 1
        modv = mod_table(l)

        y_att = _attention(p, attn_sink[l], S, not last)

        pad_rows = jnp.zeros((LANES - 2 * GLA_GATE_RANK, GLA_QK_WIDTH), F32)
        zero_rank = jnp.zeros((GLA_GATE_RANK, GLA_QK_WIDTH), F32)
        wgf = jnp.concatenate([gla_gate_w[l, 0], zero_rank, pad_rows], axis=0)
        wgb = jnp.concatenate([zero_rank, gla_gate_w[l, 1], pad_rows], axis=0)
        wg = jnp.concatenate([wgf, wgb], axis=1).astype(BF16)
        gbias = jnp.concatenate([gla_gate_b[l].reshape(1, 2 * GLA_QK_WIDTH),
                                 jnp.zeros((7, 2 * GLA_QK_WIDTH), F32)], axis=0)
        ng = jnp.tile(gla_norm_g[l], GLA_HEADS).reshape(1, GLA_WIDTH)
        cw = jnp.concatenate([conv_w[l], jnp.zeros((5, CONV_WIDTH), F32)], axis=0)
        y_gc = _gla_conv(p, wg, gbias, ng, cw, bd_gla, cum, S, not last)

        rows, tm = (S, LAT_TILE) if last else (T, TOKEN_TILE)
        wr = jnp.concatenate([w_router, jnp.zeros((D, LANES - N_EXPERTS), F32)], axis=1).astype(BF16)
        br = jnp.zeros((8, LANES), F32).at[0, :N_EXPERTS].set(b_router)
        ffn_g = norm_ffn_g[l].reshape(1, D)
        wo_att = w_out[l, :ATT_WIDTH].reshape(ATT_HEADS, HEAD_DIM, D)[order].reshape(ATT_WIDTH, D)
        wo = jnp.concatenate([wo_att, w_out[l, ATT_WIDTH:]], axis=0).astype(BF16)
        xx_mid, route, counts = _out_proj(y_att, y_gc, stream, stream_ctx, modv, wo, ffn_g, wr, br, tri, S, rows, tm)

        n_tiles = -(-(B * rows) // MOE_TILE) + N_CLASSES
        off, pad, tile, e_lo, e_hi, valid = _routing_tables(counts[:N_CLASSES, 0], n_tiles)

        dest = route[:, 1, :]
        for cls_id in range(N_CLASSES):
            dest = dest + jnp.where(route[:, 0, :] == cls_id, off[cls_id], 0)
        hs = _dispatch(pad, xx_mid, modv, ffn_g, dest, n_tiles * MOE_TILE, S, tm)
        ys = _moe(hs, tile, e_lo, e_hi, valid, wg_all, wu_all, wd_all, w_router.T, b_router, l * N_EXPERTS)
        stream, stream_ctx = _combine(xx_mid, modv, dest, ys, S, tm), None
        if not last:
            p = _in_proj(stream, None, *in_proj_args(l + 1), S)
    return stream
```
